```python
import math
import jax, jax.numpy as jnp
from jax import lax
import numpy as np

D_MODEL = 1024
BATCH = 8
SEQ = 2048
DEPTH = 2

CHUNK = 64
Q_BLOCK = 128
HEAD_DIM = 64
ROPE_THETA = 10000.0
EPS = 1e-6
MAX_POS_OFFSET = 4096
A_HEADS = 8
A_KV_LORA = 256
IDX_HEADS = 4
IDX_DIM = 64
TOPK_MAX = 256
B_HEADS = 8
C_HEADS = 8
C_V_DIM = 128
A_WIDTH = A_HEADS * HEAD_DIM
B_WIDTH = B_HEADS * HEAD_DIM
C_QK_WIDTH = C_HEADS * HEAD_DIM
C_WIDTH = C_HEADS * C_V_DIM
N_BRANCH = 3
IN_SIZES = (A_WIDTH, A_KV_LORA, IDX_HEADS * IDX_DIM, IDX_DIM, IDX_HEADS,
            B_WIDTH, B_WIDTH, B_WIDTH,
            C_QK_WIDTH, C_QK_WIDTH, C_WIDTH, C_WIDTH,
            N_BRANCH * D_MODEL)
N_IN = sum(IN_SIZES)
N_EXPERTS = 64
TOP_K = 8
N_GROUPS = 8
TOPK_GROUPS = 4
EXPERT_DIM = 256
SHARED_DIM = 256
ROUTED_SCALE = 2.5
MOE_BLOCK = 128
N_ADA = 6

kernel_name = "chunk_causal_hybrid_dsa_stickbreak_retention_moe"


def rms_norm(x, g):
    xf = x.astype(jnp.float32)
    y = xf * lax.rsqrt(jnp.mean(xf * xf, axis=-1, keepdims=True) + EPS)
    return (y * g.astype(jnp.float32)).astype(x.dtype)


def rope(x, positions):
    half = x.shape[-1] // 2
    inv_freq = ROPE_THETA ** (-jnp.arange(half, dtype=jnp.float32) / half)
    ang = positions.astype(jnp.float32)[:, :, None, None] * inv_freq
    cos, sin = jnp.cos(ang), jnp.sin(ang)
    xf = x.astype(jnp.float32)
    x1, x2 = xf[..., :half], xf[..., half:]
    return jnp.concatenate([x1 * cos - x2 * sin, x2 * cos + x1 * sin], axis=-1).astype(x.dtype)


def dsa_mixer(q, c_kv, iq, ik, iw, g_kv, w_uk, w_uv, positions):
    b, s, h, d = q.shape
    topk = min(TOPK_MAX, s // 4)
    c_kv = rms_norm(c_kv, g_kv)
    k = rope((c_kv @ w_uk)[:, :, None, :], positions)[:, :, 0]
    v = c_kv @ w_uv
    q = rope(q, positions)
    iq = rope(iq, positions).astype(jnp.float32)
    ik = rope(ik[:, :, None, :], positions)[:, :, 0].astype(jnp.float32)
    iw = iw.astype(jnp.float32) * IDX_HEADS ** -0.5
    key_chunk = jnp.arange(s) // CHUNK

    def block(i):
        t0 = i * Q_BLOCK
        qb = lax.dynamic_slice_in_dim(q, t0, Q_BLOCK, axis=1)
        iqb = lax.dynamic_slice_in_dim(iq, t0, Q_BLOCK, axis=1)
        iwb = lax.dynamic_slice_in_dim(iw, t0, Q_BLOCK, axis=1)
        q_chunk = (t0 + jnp.arange(Q_BLOCK)) // CHUNK
        admissible = key_chunk[None, :] <= q_chunk[:, None]
        logits = jnp.einsum('bthd,bsd->bths', iqb, ik) * IDX_DIM ** -0.5
        score = jnp.einsum('bth,bths->bts', iwb, jax.nn.relu(logits))
        score = jnp.where(admissible[None], score, -jnp.inf)
        _, idx = lax.top_k(score, topk)
        valid = key_chunk[idx] <= q_chunk[None, :, None]
        k_sel = jax.vmap(lambda kb, ib: kb[ib])(k, idx)
        v_sel = jax.vmap(lambda vb, ib: vb[ib])(v, idx)
        sc = jnp.einsum('bthd,btkd->bhtk', qb, k_sel).astype(jnp.float32) * d ** -0.5
        sc = jnp.where(valid[:, None], sc, -jnp.inf)
        p = jax.nn.softmax(sc, axis=-1).astype(v.dtype)
        return jnp.einsum('bhtk,btkd->bthd', p, v_sel)

    out = lax.map(block, jnp.arange(s // Q_BLOCK))
    return jnp.moveaxis(out, 0, 1).reshape(b, s, h * d)


def stick_breaking_mixer(q, k, v):
    b, s, h, d = q.shape
    key_pos = jnp.arange(s)

    def block(i):
        t0 = i * Q_BLOCK
        qb = lax.dynamic_slice_in_dim(q, t0, Q_BLOCK, axis=1)
        z = jnp.einsum('bthd,bshd->bhts', qb, k).astype(jnp.float32) * d ** -0.5
        causal = key_pos[None, :] < (t0 + jnp.arange(Q_BLOCK))[:, None]
        log_beta = jax.nn.log_sigmoid(z)
        log_not = jnp.where(causal, jax.nn.log_sigmoid(-z), 0.0)
        tail = lax.cumsum(log_not, axis=3, reverse=True) - log_not
        a = jnp.where(causal, jnp.exp(log_beta + tail), 0.0)
        return jnp.einsum('bhts,bshd->bthd', a.astype(v.dtype), v)

    out = lax.map(block, jnp.arange(s // Q_BLOCK))
    return jnp.moveaxis(out, 0, 1).reshape(b, s, h * d)


def head_group_norm(o, g):
    b, s, h, dv = o.shape
    of = o.astype(jnp.float32)
    mu = jnp.mean(of, axis=-1, keepdims=True)
    var = jnp.mean(jnp.square(of - mu), axis=-1, keepdims=True)
    y = ((of - mu) * lax.rsqrt(var + EPS)).reshape(b, s, h * dv)
    return y * g.astype(jnp.float32)


def retention_mixer(q, k, v, gate, g_ret, positions):
    b, s, h, dk = q.shape
    dv = v.shape[-1]
    n = s // CHUNK
    log_g = jnp.log(1.0 - 2.0 ** (-5.0 - jnp.arange(h, dtype=jnp.float32)))
    q = rope(q, positions)
    k = rope(k, positions) * dk ** -0.5
    qc = q.reshape(b, n, CHUNK, h, dk)
    kc = k.reshape(b, n, CHUNK, h, dk)
    vc = v.reshape(b, n, CHUNK, h, dv)
    pos = jnp.arange(CHUNK, dtype=jnp.float32)
    decay_in = jnp.exp(log_g[:, None, None] * jnp.abs(pos[:, None] - pos[None, :]))
    sc = jnp.einsum('bnahd,bnchd->bnhac', qc, kc) * decay_in
    intra = jnp.einsum('bnhac,bnche->bnahe', sc, vc)
    zeta = jnp.exp(log_g[:, None] * (CHUNK - 1 - pos))
    kv = jnp.einsum('bnchd,hc,bnche->bnhde', kc, zeta, vc)
    g_chunk = jnp.exp(log_g * CHUNK)[None, :, None, None]

    def step(state, kv_i):
        return state * g_chunk + kv_i, state

    _, prev = lax.scan(step, jnp.zeros_like(kv[:, 0]), jnp.moveaxis(kv, 1, 0))
    prev = jnp.moveaxis(prev, 0, 1)
    xi = jnp.exp(log_g[:, None] * (pos + 1.0))
    cross = jnp.einsum('bnahd,bnhde,ha->bnahe', qc, prev, xi)
    o = (intra + cross).reshape(b, s, h, dv)
    o = head_group_norm(o, g_ret).astype(gate.dtype)
    return jax.nn.silu(gate.reshape(b, s, h * dv)) * o


def mixing_sublayer(h, positions, w_in, g_kv, w_uk, w_uv, g_ret, w_o_a, w_o_b, w_o_c, w_out):
    b, s, d_model = h.shape
    split_points = [int(p) for p in np.cumsum(IN_SIZES)[:-1]]
    z = h @ w_in
    (qa, ckv, iq, ik, iw, qb, kb, vb, qc, kc, vc, gc, gates) = jnp.split(z, split_points, axis=-1)
    ya = dsa_mixer(qa.reshape(b, s, A_HEADS, HEAD_DIM), ckv,
                   iq.reshape(b, s, IDX_HEADS, IDX_DIM), ik, iw, g_kv, w_uk, w_uv, positions) @ w_o_a
    yb = stick_breaking_mixer(qb.reshape(b, s, B_HEADS, HEAD_DIM), kb.reshape(b, s, B_HEADS, HEAD_DIM),
                              vb.reshape(b, s, B_HEADS, HEAD_DIM)) @ w_o_b
    yc = retention_mixer(qc.reshape(b, s, C_HEADS, HEAD_DIM), kc.reshape(b, s, C_HEADS, HEAD_DIM),
                         vc.reshape(b, s, C_HEADS, C_V_DIM), gc.reshape(b, s, C_HEADS, C_V_DIM),
                         g_ret, positions) @ w_o_c
    g = jax.nn.sigmoid(gates.astype(jnp.float32)).astype(h.dtype).reshape(b, s, N_BRANCH, d_model)
    merged = g[:, :, 0] * ya + g[:, :, 1] * yb + g[:, :, 2] * yc
    return merged @ w_out


def moe_sublayer(h, w_router, router_bias, w1, w3, w2, ws1, ws3, ws2):
    b, s, d_model = h.shape
    x = h.reshape(-1, d_model)
    t = x.shape[0]
    scores = jax.nn.sigmoid((x @ w_router).astype(jnp.float32))
    sel = scores + router_bias.astype(jnp.float32)
    grp_score = lax.top_k(sel.reshape(t, N_GROUPS, N_EXPERTS // N_GROUPS), 2)[0].sum(-1)
    _, top_groups = lax.top_k(grp_score, TOPK_GROUPS)
    gmask = jnp.any(top_groups[:, :, None] == jnp.arange(N_GROUPS)[None, None, :], axis=1)
    sel = jnp.where(jnp.repeat(gmask, N_EXPERTS // N_GROUPS, axis=1), sel, -jnp.inf)
    _, eidx = lax.top_k(sel, TOP_K)
    wts = jnp.take_along_axis(scores, eidx, axis=1)
    wts = wts / jnp.sum(wts, axis=-1, keepdims=True) * ROUTED_SCALE
    n_assign = t * TOP_K
    flat_e = eidx.reshape(-1)
    flat_tok = jnp.repeat(jnp.arange(t, dtype=jnp.int32), TOP_K)
    order = jnp.argsort(flat_e)
    se, stok, sw = flat_e[order], flat_tok[order], wts.reshape(-1)[order]
    counts = jnp.zeros((N_EXPERTS,), jnp.int32).at[flat_e].add(1)
    starts = jnp.cumsum(counts) - counts
    padded = (counts + MOE_BLOCK - 1) // MOE_BLOCK * MOE_BLOCK
    pad_ends = jnp.cumsum(padded)
    pad_starts = pad_ends - padded
    dest = pad_starts[se] + (jnp.arange(n_assign, dtype=jnp.int32) - starts[se])
    n_blocks = (n_assign + N_EXPERTS * (MOE_BLOCK - 1) + MOE_BLOCK - 1) // MOE_BLOCK
    n_rows = n_blocks * MOE_BLOCK
    row_tok = jnp.full((n_rows,), t, jnp.int32).at[dest].set(stok)
    row_w = jnp.zeros((n_rows,), jnp.float32).at[dest].set(sw)
    block_e = jnp.minimum(jnp.searchsorted(pad_ends, jnp.arange(n_blocks) * MOE_BLOCK, side='right'),
                          N_EXPERTS - 1)
    x_pad = jnp.concatenate([x, jnp.zeros((1, d_model), x.dtype)], axis=0)

    def run_block(args):
        tok, wt, e = args
        xb = x_pad[tok]
        hid = jax.nn.silu(xb @ w1[e]) * (xb @ w3[e])
        return (hid @ w2[e]) * wt[:, None].astype(x.dtype)

    out = lax.map(run_block, (row_tok.reshape(n_blocks, MOE_BLOCK), row_w.reshape(n_blocks, MOE_BLOCK), block_e))
    routed = jax.ops.segment_sum(out.reshape(n_rows, d_model), row_tok, num_segments=t + 1)[:t]
    shared = (jax.nn.silu(x @ ws1) * (x @ ws3)) @ ws2
    return (routed + shared).reshape(b, s, d_model)


def setup_inputs(seed: int = 0) -> dict:
    key = jax.random.key(seed)
    ks = jax.random.split(key, 32)
    f32 = jnp.float32
    nrm = lambda k, shape, fan_in: jax.random.normal(k, shape, f32) * fan_in ** -0.5
    gain = lambda k, shape: 1.0 + 0.05 * jax.random.normal(k, shape, f32)
    offsets = jax.random.randint(ks[2], (BATCH, 1), 0, MAX_POS_OFFSET, dtype=jnp.int32)
    return {
        "x": jax.random.normal(ks[0], (BATCH, SEQ, D_MODEL), f32),
        "c": jax.random.normal(ks[1], (BATCH, D_MODEL), f32),
        "positions": offsets + jnp.arange(SEQ, dtype=jnp.int32)[None, :],
        "ada_w": nrm(ks[3], (DEPTH, D_MODEL, N_ADA * D_MODEL), D_MODEL) * 0.5,
        "ada_b": 0.02 * jax.random.normal(ks[4], (DEPTH, N_ADA * D_MODEL), f32),
        "g_pre_mix": gain(ks[5], (DEPTH, D_MODEL)),
        "g_post_mix": gain(ks[6], (DEPTH, D_MODEL)),
        "g_pre_ffn": gain(ks[7], (DEPTH, D_MODEL)),
        "g_post_ffn": gain(ks[8], (DEPTH, D_MODEL)),
        "w_in": nrm(ks[9], (DEPTH, D_MODEL, N_IN), D_MODEL),
        "g_kv": gain(ks[10], (DEPTH, A_KV_LORA)),
        "w_uk": nrm(ks[11], (DEPTH, A_KV_LORA, HEAD_DIM), A_KV_LORA),
        "w_uv": nrm(ks[12], (DEPTH, A_KV_LORA, HEAD_DIM), A_KV_LORA),
        "g_ret": gain(ks[13], (DEPTH, C_WIDTH)),
        "w_o_a": nrm(ks[14], (DEPTH, A_WIDTH, D_MODEL), A_WIDTH),
        "w_o_b": nrm(ks[15], (DEPTH, B_WIDTH, D_MODEL), B_WIDTH),
        "w_o_c": nrm(ks[16], (DEPTH, C_WIDTH, D_MODEL), C_WIDTH),
        "w_out": nrm(ks[17], (DEPTH, D_MODEL, D_MODEL), D_MODEL),
        "w_router": nrm(ks[18], (DEPTH, D_MODEL, N_EXPERTS), D_MODEL),
        "router_bias": 0.01 * jax.random.normal(ks[19], (DEPTH, N_EXPERTS), f32),
        "w1": nrm(ks[20], (DEPTH, N_EXPERTS, D_MODEL, EXPERT_DIM), D_MODEL),
        "w3": nrm(ks[21], (DEPTH, N_EXPERTS, D_MODEL, EXPERT_DIM), D_MODEL),
        "w2": nrm(ks[22], (DEPTH, N_EXPERTS, EXPERT_DIM, D_MODEL), EXPERT_DIM),
        "ws1": nrm(ks[23], (DEPTH, D_MODEL, SHARED_DIM), D_MODEL),
        "ws3": nrm(ks[24], (DEPTH, D_MODEL, SHARED_DIM), D_MODEL),
        "ws2": nrm(ks[25], (DEPTH, SHARED_DIM, D_MODEL), SHARED_DIM),
    }


def reference(x, c, positions, ada_w, ada_b, g_pre_mix, g_post_mix, g_pre_ffn, g_post_ffn,
              w_in, g_kv, w_uk, w_uv, g_ret, w_o_a, w_o_b, w_o_c, w_out,
              w_router, router_bias, w1, w3, w2, ws1, ws3, ws2):
    for l in range(DEPTH):
        mod = (jax.nn.silu(c) @ ada_w[l] + ada_b[l])[:, None, :]
        sh1, sc1, gt1, sh2, sc2, gt2 = jnp.split(mod, N_ADA, axis=-1)
        h = rms_norm(x, g_pre_mix[l]) * (1.0 + sc1) + sh1
        y = mixing_sublayer(h, positions, w_in[l], g_kv[l], w_uk[l], w_uv[l], g_ret[l],
                            w_o_a[l], w_o_b[l], w_o_c[l], w_out[l])
        x = x + gt1 * rms_norm(y, g_post_mix[l])
        h = rms_norm(x, g_pre_ffn[l]) * (1.0 + sc2) + sh2
        y = moe_sublayer(h, w_router[l], router_bias[l], w1[l], w3[l], w2[l], ws1[l], ws3[l], ws2[l])
        x = x + gt2 * rms_norm(y, g_post_ffn[l])
    return x
```

```python
import functools
import math

import numpy as np
import jax
import jax.numpy as jnp
from jax import lax
from jax.experimental import pallas as pl
from jax.experimental.pallas import tpu as pltpu

F32 = jnp.float32
BF16 = jnp.bfloat16
I32 = jnp.int32

D_MODEL = 1024
BATCH = 8
SEQ = 2048
DEPTH = 2
N_TOK = BATCH * SEQ

CHUNK = 64
HEAD_DIM = 64
ROPE_THETA = 10000.0
EPS = 1e-6
A_HEADS = 8
A_KV_LORA = 256
IDX_HEADS = 4
IDX_DIM = 64
TOPK = 256
B_HEADS = 8
C_HEADS = 8
C_V_DIM = 128
N_EXPERTS = 64
TOP_K = 8
N_GROUPS = 8
TOPK_GROUPS = 4
EXPERT_DIM = 256
SHARED_DIM = 256
ROUTED_SCALE = 2.5
N_ADA = 6

LANES = 128
VMEM_LIMIT = 56 * 1024 * 1024

COL_GATES = 0
COL_VC = 3072
COL_GC = 4096
COL_QA = 5120
COL_QB = 5632
COL_KB = 6144
COL_VB = 6656
COL_QC = 7168
COL_KC = 7680
COL_CKV = 8192
COL_IQ = 8448
COL_IKIW = 8704
N_PROJ = 8832
PROJ_TN = 2944
PROJ_TM = 512

INT_MIN = -2147483648
NEG_INF_KEY = -2139095041

NT_DIMS = (((1,), (1,)), ((), ()))


def _params(sem):
    return pltpu.CompilerParams(dimension_semantics=sem, vmem_limit_bytes=VMEM_LIMIT)


def _sigmoid(x):
    return 1.0 / (1.0 + jnp.exp(-x))


def _rope_lanes(x, cos, sin_signed):
    n = x.shape[-1]
    lane = lax.broadcasted_iota(I32, (1, n), 1)
    first = (lane & 32) == 0
    ahead = pltpu.roll(x, n - 32, 1)
    behind = pltpu.roll(x, 32, 1)
    return x * cos + jnp.where(first, ahead, behind) * sin_signed


def _ada_kernel(c_ref, w_ref, b_ref, o_ref):
    c = c_ref[...]
    s = (c * _sigmoid(c)).astype(BF16)
    o_ref[...] = jnp.dot(s, w_ref[...].astype(BF16), preferred_element_type=F32) + b_ref[...]


def _ada(c, w, b):
    n = w.shape[1]
    return pl.pallas_call(
        _ada_kernel,
        grid=(n // D_MODEL,),
        in_specs=[pl.BlockSpec((BATCH, D_MODEL), lambda j: (0, 0)),
                  pl.BlockSpec((D_MODEL, D_MODEL), lambda j: (0, j)),
                  pl.BlockSpec((1, D_MODEL), lambda j: (0, j))],
        out_specs=pl.BlockSpec((BATCH, D_MODEL), lambda j: (0, j)),
        out_shape=jax.ShapeDtypeStruct((BATCH, n), F32),
        compiler_params=_params(("arbitrary",)),
        name="ada_mod",
    )(c, w, b.reshape(1, n))


def _inproj_kernel(x_ref, g_ref, sc_ref, sh_ref, w_ref, o_ref, h_ref):
    @pl.when(pl.program_id(1) == 0)
    def _():
        x = x_ref[...]
        ms = jnp.mean(x * x, axis=-1, keepdims=True)
        y = x * lax.rsqrt(ms + EPS) * g_ref[...]
        h_ref[...] = (y * (1.0 + sc_ref[0]) + sh_ref[0]).astype(BF16)

    o_ref[...] = jnp.dot(h_ref[...], w_ref[...], preferred_element_type=F32)


def _inproj(x, g, sc, sh, w_bf16):
    tiles_per_batch = SEQ // PROJ_TM
    return pl.pallas_call(
        _inproj_kernel,
        grid=(N_TOK // PROJ_TM, N_PROJ // PROJ_TN),
        in_specs=[pl.BlockSpec((PROJ_TM, D_MODEL), lambda i, j: (i, 0)),
                  pl.BlockSpec((1, D_MODEL), lambda i, j: (0, 0)),
                  pl.BlockSpec((1, 1, D_MODEL), lambda i, j: (i // tiles_per_batch, 0, 0)),
                  pl.BlockSpec((1, 1, D_MODEL), lambda i, j: (i // tiles_per_batch, 0, 0)),
                  pl.BlockSpec((D_MODEL, PROJ_TN), lambda i, j: (0, j))],
        out_specs=pl.BlockSpec((PROJ_TM, PROJ_TN), lambda i, j: (i, j)),
        out_shape=jax.ShapeDtypeStruct((N_TOK, N_PROJ), F32),
        scratch_shapes=[pltpu.VMEM((PROJ_TM, D_MODEL), BF16)],
        compiler_params=_params(("arbitrary", "arbitrary")),
        name="in_proj",
    )(x, g.reshape(1, D_MODEL), sc, sh, w_bf16)


DSA_QB = 128
DSA_KC = 256


def _dsa_kernel(ckv_ref, ikiw_ref, q_ref, iq_ref, cos_a, sin_a, cos_b, sin_b, gkv_ref, wukv_ref,
                o_ref, k_lo, k_hi, v_dup, ik_lo, ik_hi, keys_s, selt_s):
    qi = pl.program_id(1)
    lane = lax.broadcasted_iota(I32, (1, LANES), 1)
    lo_half = lane < 64
    n_kc = SEQ // DSA_KC

    @pl.when(qi == 0)
    def _prologue():
        def chunk(c, carry):
            r0 = pl.multiple_of(c * DSA_KC, DSA_KC)
            rows = pl.ds(r0, DSA_KC)
            ckv = ckv_ref[0, rows, :]
            ms = jnp.mean(ckv * ckv, axis=-1, keepdims=True)
            cn = (ckv * lax.rsqrt(ms + EPS) * gkv_ref[...]).astype(BF16)
            kv = jnp.dot(cn, wukv_ref[...], preferred_element_type=F32)
            cb, sb = cos_b[0, rows, :], sin_b[0, rows, :]
            kv = _rope_lanes(kv, cb, sb)
            swapped = pltpu.roll(kv, 64, 1)
            k_lo[rows, :] = jnp.where(lo_half, kv, 0.0).astype(BF16)
            k_hi[rows, :] = jnp.where(lo_half, 0.0, swapped).astype(BF16)
            v_dup[rows, :] = jnp.where(lo_half, swapped, kv).astype(BF16)
            ik = jnp.where(lo_half, _rope_lanes(ikiw_ref[0, rows, :], cb, sb), 0.0)
            ik_lo[rows, :] = ik.astype(BF16)
            ik_hi[rows, :] = pltpu.roll(ik, 64, 1).astype(BF16)
            return carry
        lax.fori_loop(0, n_kc, chunk, 0)

    q0 = pl.multiple_of(qi * DSA_QB, DSA_QB)
    qrows = pl.ds(q0, DSA_QB)
    cq, sq = cos_a[0, qrows, :], sin_a[0, qrows, :]

    iq = iq_ref[0]
    iq_heads = []
    for p in range(IDX_HEADS // 2):
        pair = _rope_lanes(iq[:, p * LANES:(p + 1) * LANES], cq, sq)
        iq_heads.append(jnp.where(lo_half, pair, 0.0).astype(BF16))
        iq_heads.append(jnp.where(lo_half, 0.0, pair).astype(BF16))
    iw_t = ikiw_ref[0, qrows, :].T * (IDX_HEADS ** -0.5)
    q_chunk = (q0 + lane) >> 6

    for c in range(n_kc):
        rows = pl.ds(c * DSA_KC, DSA_KC)
        k_chunk = (c * DSA_KC + lax.broadcasted_iota(I32, (DSA_KC, 1), 0)) >> 6
        score = jnp.zeros((DSA_KC, DSA_QB), F32)
        for h in range(IDX_HEADS):
            ik = ik_lo if h % 2 == 0 else ik_hi
            logit = lax.dot_general(ik[rows, :], iq_heads[h], NT_DIMS,
                                    preferred_element_type=F32) * (IDX_DIM ** -0.5)
            score = score + iw_t[64 + h:65 + h, :] * jnp.maximum(logit, 0.0)
        score = jnp.where(score == 0.0, 0.0, score)
        score = jnp.where(k_chunk <= q_chunk, score, -jnp.inf)
        bits = lax.bitcast_convert_type(score, I32)
        keys_s[rows, :] = bits ^ ((bits >> 31) & 0x7FFFFFFF)

    def descend(it, prefix):
        cand = prefix | lax.shift_left(jnp.int32(1), 31 - it)
        cnt = jnp.sum((keys_s[...] >= (cand ^ INT_MIN)).astype(I32), axis=0, keepdims=True)
        return jnp.where(cnt >= TOPK, cand, prefix)
    thr = lax.fori_loop(0, 32, descend, jnp.zeros((1, DSA_QB), I32)) ^ INT_MIN

    n_gt = jnp.sum((keys_s[...] > thr).astype(I32), axis=0, keepdims=True)
    need = (TOPK - n_gt).astype(F32)
    tri = (lax.broadcasted_iota(I32, (DSA_KC, DSA_KC), 0)
           >= lax.broadcasted_iota(I32, (DSA_KC, DSA_KC), 1)).astype(BF16)
    seen = jnp.zeros((1, DSA_QB), F32)
    for c in range(n_kc):
        rows = pl.ds(c * DSA_KC, DSA_KC)
        key = keys_s[rows, :]
        eq = key == thr
        rank = jnp.dot(tri, eq.astype(BF16), preferred_element_type=F32) + seen
        seen = rank[DSA_KC - 1:DSA_KC, :]
        sel = ((key > thr) | (eq & (rank <= need))) & (key > NEG_INF_KEY)
        selt_s[:, c * DSA_KC:(c + 1) * DSA_KC] = sel.astype(F32).T

    selected = selt_s[...] > 0.0
    q = q_ref[0]
    for p in range(A_HEADS // 2):
        pair = _rope_lanes(q[:, p * LANES:(p + 1) * LANES], cq, sq)
        outs = []
        for e in range(2):
            qm = (jnp.where(lo_half, pair, 0.0) if e == 0 else jnp.where(lo_half, 0.0, pair)).astype(BF16)
            kk = k_lo if e == 0 else k_hi
            s = lax.dot_general(qm, kk[...], NT_DIMS, preferred_element_type=F32) * (HEAD_DIM ** -0.5)
            s = jnp.where(selected, s, -jnp.inf)
            m = jnp.max(s, axis=1, keepdims=True)
            pe = jnp.exp(s - m)
            l = jnp.sum(pe, axis=1, keepdims=True)
            outs.append(jnp.dot(pe.astype(BF16), v_dup[...], preferred_element_type=F32) / l)
        o_ref[0, :, p * LANES:(p + 1) * LANES] = jnp.where(lo_half, outs[0], outs[1])


def _dsa(z3, cos_a, sin_a, cos_b, sin_b, g_kv, w_ukv_bf16):
    full = lambda width, col: pl.BlockSpec((1, SEQ, width), lambda b, i: (b, 0, col // width))
    blk = lambda width, col: pl.BlockSpec((1, DSA_QB, width), lambda b, i: (b, i, col // width))
    tab = pl.BlockSpec((1, SEQ, LANES), lambda b, i: (b, 0, 0))
    return pl.pallas_call(
        _dsa_kernel,
        grid=(BATCH, SEQ // DSA_QB),
        in_specs=[full(A_KV_LORA, COL_CKV), full(LANES, COL_IKIW),
                  blk(A_HEADS * HEAD_DIM, COL_QA), blk(IDX_HEADS * IDX_DIM, COL_IQ),
                  tab, tab, tab, tab,
                  pl.BlockSpec((1, A_KV_LORA), lambda b, i: (0, 0)),
                  pl.BlockSpec((A_KV_LORA, LANES), lambda b, i: (0, 0))],
        out_specs=pl.BlockSpec((1, DSA_QB, A_HEADS * HEAD_DIM), lambda b, i: (b, i, 0)),
        out_shape=jax.ShapeDtypeStruct((BATCH, SEQ, A_HEADS * HEAD_DIM), F32),
        scratch_shapes=[pltpu.VMEM((SEQ, LANES), BF16)] * 5
                       + [pltpu.VMEM((SEQ, DSA_QB), I32), pltpu.VMEM((DSA_QB, SEQ), F32)],
        compiler_params=_params(("arbitrary", "arbitrary")),
        name="dsa_mixer",
    )(z3, z3, z3, z3, cos_a, sin_a, cos_b, sin_b, g_kv.reshape(1, A_KV_LORA), w_ukv_bf16)


SB_BLK = 128


def _sb_kernel(q_ref, k_ref, v_ref, o_ref):
    qi = pl.program_id(2)
    lane = lax.broadcasted_iota(I32, (1, LANES), 1)
    lo_half = lane < 64
    row = lax.broadcasted_iota(I32, (SB_BLK, SB_BLK), 0)
    col = lax.broadcasted_iota(I32, (SB_BLK, SB_BLK), 1)
    r2 = lax.broadcasted_iota(I32, (2 * SB_BLK, 2 * SB_BLK), 0) & (SB_BLK - 1)
    c2 = lax.broadcasted_iota(I32, (2 * SB_BLK, 2 * SB_BLK), 1)
    suffix_mat = ((c2 >= SB_BLK) | (r2 > c2)).astype(BF16)

    q2 = q_ref[0]
    qm = (jnp.where(lo_half, q2, 0.0).astype(BF16), jnp.where(lo_half, 0.0, q2).astype(BF16))

    def step(jj, carry):
        kb = qi - jj
        r0 = pl.multiple_of(kb * SB_BLK, SB_BLK)
        k2 = k_ref[0, pl.ds(r0, SB_BLK), :].astype(BF16)
        v2 = v_ref[0, pl.ds(r0, SB_BLK), :].astype(BF16)
        valid = (kb < qi) | (col < row)
        new = []
        for e in range(2):
            tail_c, acc = carry[2 * e], carry[2 * e + 1]
            z = lax.dot_general(qm[e], k2, NT_DIMS, preferred_element_type=F32) * (HEAD_DIM ** -0.5)
            log_beta = jnp.minimum(z, 0.0) - jnp.log1p(jnp.exp(-jnp.abs(z)))
            log_not = jnp.where(valid, log_beta - z, 0.0)
            hi = log_not.astype(BF16)
            lo = (log_not - hi.astype(F32)).astype(BF16)
            t = jnp.dot(jnp.concatenate([hi, lo], axis=1), suffix_mat, preferred_element_type=F32)
            a = jnp.where(valid, jnp.exp(log_beta + t[:, :SB_BLK] + tail_c), 0.0)
            acc = acc + jnp.dot(a.astype(BF16), v2, preferred_element_type=F32)
            new += [tail_c + t[:, SB_BLK:], acc]
        return tuple(new)

    zero = jnp.zeros((SB_BLK, SB_BLK), F32)
    res = lax.fori_loop(0, qi + 1, step, (zero, zero, zero, zero))
    o_ref[0] = jnp.where(lo_half, res[1], res[3])


def _sb(z3):
    n_pairs = B_HEADS // 2
    return pl.pallas_call(
        _sb_kernel,
        grid=(BATCH, n_pairs, SEQ // SB_BLK),
        in_specs=[pl.BlockSpec((1, SB_BLK, LANES), lambda b, p, i: (b, i, COL_QB // LANES + p)),
                  pl.BlockSpec((1, SEQ, LANES), lambda b, p, i: (b, 0, COL_KB // LANES + p)),
                  pl.BlockSpec((1, SEQ, LANES), lambda b, p, i: (b, 0, COL_VB // LANES + p))],
        out_specs=pl.BlockSpec((1, SB_BLK, LANES), lambda b, p, i: (b, i, p)),
        out_shape=jax.ShapeDtypeStruct((BATCH, SEQ, B_HEADS * HEAD_DIM), F32),
        compiler_params=_params(("arbitrary", "arbitrary", "arbitrary")),
        name="stickbreak_mixer",
    )(z3, z3, z3)


RET_BLK = 256


def _ret_kernel(dec_ref, q_ref, k_ref, v_ref, gate_ref, cos_a, sin_a, e_off, e_diag, gret_ref, o_ref):
    p = pl.program_id(1)
    qi = pl.program_id(2)
    lane = lax.broadcasted_iota(I32, (1, LANES), 1)
    lo_half = lane < 64
    q0 = pl.multiple_of(qi * RET_BLK, RET_BLK)
    q2 = _rope_lanes(q_ref[0], cos_a[0, pl.ds(q0, RET_BLK), :], sin_a[0, pl.ds(q0, RET_BLK), :])
    qm = (jnp.where(lo_half, q2, 0.0).astype(BF16), jnp.where(lo_half, 0.0, q2).astype(BF16))

    def key_block(kb):
        r0 = pl.multiple_of(kb * RET_BLK, RET_BLK)
        rows = pl.ds(r0, RET_BLK)
        k2 = _rope_lanes(k_ref[0, rows, :], cos_a[0, rows, :], sin_a[0, rows, :]) * (HEAD_DIM ** -0.5)
        return k2.astype(BF16), v_ref[0, rows, :].astype(BF16)

    def off_diag(jj, accs):
        kb = qi - 1 - jj
        k2, v2 = key_block(kb)
        new = []
        for e in range(2):
            s = lax.dot_general(qm[e], k2, NT_DIMS, preferred_element_type=F32) * e_off[e]
            pv = jnp.dot(s.astype(BF16), v2[:, e * C_V_DIM:(e + 1) * C_V_DIM], preferred_element_type=F32)
            new.append(accs[e] + dec_ref[2 * p + e, jj + 1] * pv)
        return tuple(new)

    zero = jnp.zeros((RET_BLK, C_V_DIM), F32)
    accs = lax.fori_loop(0, qi, off_diag, (zero, zero))
    k2, v2 = key_block(qi)
    for e in range(2):
        s = lax.dot_general(qm[e], k2, NT_DIMS, preferred_element_type=F32) * e_diag[e]
        o = accs[e] + jnp.dot(s.astype(BF16), v2[:, e * C_V_DIM:(e + 1) * C_V_DIM], preferred_element_type=F32)
        mu = jnp.mean(o, axis=-1, keepdims=True)
        d = o - mu
        var = jnp.mean(d * d, axis=-1, keepdims=True)
        y = d * lax.rsqrt(var + EPS) * gret_ref[:, e * C_V_DIM:(e + 1) * C_V_DIM]
        g = gate_ref[0, :, e * C_V_DIM:(e + 1) * C_V_DIM]
        o_ref[0, :, e * C_V_DIM:(e + 1) * C_V_DIM] = g * _sigmoid(g) * y


def _ret(z3, cos_a, sin_a, dec, e_off, e_diag, g_ret):
    n_pairs = C_HEADS // 2
    w2 = 2 * C_V_DIM
    tab = pl.BlockSpec((1, SEQ, LANES), lambda b, p, i: (b, 0, 0))
    return pl.pallas_call(
        _ret_kernel,
        grid=(BATCH, n_pairs, SEQ // RET_BLK),
        in_specs=[pl.BlockSpec(memory_space=pltpu.SMEM),
                  pl.BlockSpec((1, RET_BLK, LANES), lambda b, p, i: (b, i, COL_QC // LANES + p)),
                  pl.BlockSpec((1, SEQ, LANES), lambda b, p, i: (b, 0, COL_KC // LANES + p)),
                  pl.BlockSpec((1, SEQ, w2), lambda b, p, i: (b, 0, COL_VC // w2 + p)),
                  pl.BlockSpec((1, RET_BLK, w2), lambda b, p, i: (b, i, COL_GC // w2 + p)),
                  tab, tab,
                  pl.BlockSpec((2, RET_BLK, RET_BLK), lambda b, p, i: (p, 0, 0)),
                  pl.BlockSpec((2, RET_BLK, RET_BLK), lambda b, p, i: (p, 0, 0)),
                  pl.BlockSpec((1, w2), lambda b, p, i: (0, p))],
        out_specs=pl.BlockSpec((1, RET_BLK, w2), lambda b, p, i: (b, i, p)),
        out_shape=jax.ShapeDtypeStruct((BATCH, SEQ, C_HEADS * C_V_DIM), F32),
        compiler_params=_params(("arbitrary", "arbitrary", "arbitrary")),
        name="retention_mixer",
    )(dec, z3, z3, z3, z3, cos_a, sin_a, e_off, e_diag, g_ret.reshape(1, C_HEADS * C_V_DIM))


MERGE_TM = 256


def _rms(y, g):
    ms = jnp.mean(y * y, axis=-1, keepdims=True)
    return y * lax.rsqrt(ms + EPS) * g


def _route(logits_t, bias_col):
    n_tok = logits_t.shape[1]
    per_group = N_EXPERTS // N_GROUPS
    scores = _sigmoid(logits_t)
    sel = scores + bias_col
    sel3 = sel.reshape(N_GROUPS, per_group, n_tok)
    member = lax.broadcasted_iota(I32, (N_GROUPS, per_group, n_tok), 1)
    m1 = jnp.max(sel3, axis=1, keepdims=True)
    first = jnp.min(jnp.where(sel3 == m1, member, per_group), axis=1, keepdims=True)
    m2 = jnp.max(jnp.where(member == first, -jnp.inf, sel3), axis=1, keepdims=True)
    grp = (m1 + m2).reshape(N_GROUPS, n_tok)
    gid = lax.broadcasted_iota(I32, (N_GROUPS, n_tok), 0)
    rank = jnp.zeros((N_GROUPS, n_tok), I32)
    for g in range(N_GROUPS):
        other = grp[g:g + 1, :]
        rank = rank + ((other > grp) | ((other == grp) & (g < gid))).astype(I32)
    keep = jnp.where(rank < TOPK_GROUPS, 1.0, 0.0).reshape(N_GROUPS, 1, n_tok)
    live = jnp.where(keep > 0.0, sel3, -jnp.inf).reshape(N_EXPERTS, n_tok)
    eid = lax.broadcasted_iota(I32, (N_EXPERTS, n_tok), 0)
    chosen = jnp.zeros((N_EXPERTS, n_tok), jnp.bool_)
    for _ in range(TOP_K):
        m = jnp.max(live, axis=0, keepdims=True)
        pick = eid == jnp.min(jnp.where(live == m, eid, N_EXPERTS), axis=0, keepdims=True)
        chosen = chosen | pick
        live = jnp.where(pick, -jnp.inf, live)
    w = jnp.where(chosen, scores, 0.0)
    return w / jnp.sum(w, axis=0, keepdims=True) * ROUTED_SCALE


def _merge_kernel(ya_ref, yb_ref, yc_ref, g_ref, x_ref, gt_ref, gpost_ref, gpre_ref, sc_ref, sh_ref,
                  woa, wob, woc, wout, wr_t, rb_ref, xo_ref, h_ref, gate_ref):
    ya = jnp.dot(ya_ref[...].astype(BF16), woa[...], preferred_element_type=F32)
    yb = jnp.dot(yb_ref[...].astype(BF16), wob[...], preferred_element_type=F32)
    yc = jnp.dot(yc_ref[...].astype(BF16), woc[...], preferred_element_type=F32)
    merged = (_sigmoid(g_ref[:, 0:D_MODEL]) * ya
              + _sigmoid(g_ref[:, D_MODEL:2 * D_MODEL]) * yb
              + _sigmoid(g_ref[:, 2 * D_MODEL:3 * D_MODEL]) * yc)
    y = jnp.dot(merged.astype(BF16), wout[...], preferred_element_type=F32)
    x = x_ref[...] + gt_ref[0] * _rms(y, gpost_ref[...])
    xo_ref[...] = x
    h = (_rms(x, gpre_ref[...]) * (1.0 + sc_ref[0]) + sh_ref[0]).astype(BF16)
    h_ref[...] = h
    logits_t = lax.dot_general(wr_t[...], h, NT_DIMS, preferred_element_type=F32)
    gates_t = _route(logits_t, rb_ref[...])
    gate_ref[...] = jnp.concatenate([gates_t, jnp.zeros_like(gates_t)], axis=0).T


def _merge(ya, yb, yc, z, x, gt1, g_post, g_pre, sc2, sh2, woa, wob, woc, wout, wr_t, rbias):
    tpb = SEQ // MERGE_TM
    row = lambda w: pl.BlockSpec((MERGE_TM, w), lambda i: (i, 0))
    vec = pl.BlockSpec((1, D_MODEL), lambda i: (0, 0))
    mod = pl.BlockSpec((1, 1, D_MODEL), lambda i: (i // tpb, 0, 0))
    whole = lambda a: pl.BlockSpec(a.shape, lambda i: (0, 0))
    return pl.pallas_call(
        _merge_kernel,
        grid=(N_TOK // MERGE_TM,),
        in_specs=[row(512), row(512), row(1024), pl.BlockSpec((MERGE_TM, 3 * D_MODEL), lambda i: (i, 0)),
                  row(D_MODEL), mod, vec, vec, mod, mod,
                  whole(woa), whole(wob), whole(woc), whole(wout), whole(wr_t), whole(rbias)],
        out_specs=[row(D_MODEL), row(D_MODEL), row(LANES)],
        out_shape=[jax.ShapeDtypeStruct((N_TOK, D_MODEL), F32),
                   jax.ShapeDtypeStruct((N_TOK, D_MODEL), BF16),
                   jax.ShapeDtypeStruct((N_TOK, LANES), F32)],
        compiler_params=_params(("arbitrary",)),
        name="merge_route",
    )(ya, yb, yc, z, x, gt1, g_post.reshape(1, D_MODEL), g_pre.reshape(1, D_MODEL), sc2, sh2,
      woa, wob, woc, wout, wr_t, rbias)


MOE_TM = 1024


def _moe_kernel(h_ref, gate_ref, x_ref, gt_ref, gpost_ref, w1, w3, w2, ws1, ws3, ws2, o_ref, acc):
    e = pl.program_id(1)
    h = h_ref[...]

    @pl.when(e == 0)
    def _():
        acc[...] = jnp.zeros_like(acc)

    g = gate_ref[...]
    g_hi = g.astype(BF16)
    g_lo = (g - g_hi.astype(F32)).astype(BF16)
    onehot = (lax.broadcasted_iota(I32, (LANES, LANES), 0) == e).astype(BF16)
    gcol = (jnp.dot(g_hi, onehot, preferred_element_type=F32)
            + jnp.dot(g_lo, onehot, preferred_element_type=F32))
    a = jnp.dot(h, w1[0], preferred_element_type=F32)
    hid = a * _sigmoid(a) * jnp.dot(h, w3[0], preferred_element_type=F32)
    y = jnp.dot(hid.astype(BF16), w2[0], preferred_element_type=F32)
    acc[...] += y * jnp.concatenate([gcol] * (D_MODEL // LANES), axis=1)

    @pl.when(e == N_EXPERTS - 1)
    def _():
        s = jnp.dot(h, ws1[...], preferred_element_type=F32)
        shid = s * _sigmoid(s) * jnp.dot(h, ws3[...], preferred_element_type=F32)
        y_all = acc[...] + jnp.dot(shid.astype(BF16), ws2[...], preferred_element_type=F32)
        o_ref[...] = x_ref[...] + gt_ref[0] * _rms(y_all, gpost_ref[...])


def _moe(h, gates, x, gt2, g_post, w1, w3, w2, ws1, ws3, ws2):
    tpb = SEQ // MOE_TM
    whole = lambda a: pl.BlockSpec(a.shape, lambda i, e: (0, 0))
    return pl.pallas_call(
        _moe_kernel,
        grid=(N_TOK // MOE_TM, N_EXPERTS),
        in_specs=[pl.BlockSpec((MOE_TM, D_MODEL), lambda i, e: (i, 0)),
                  pl.BlockSpec((MOE_TM, LANES), lambda i, e: (i, 0)),
                  pl.BlockSpec((MOE_TM, D_MODEL), lambda i, e: (i, 0)),
                  pl.BlockSpec((1, 1, D_MODEL), lambda i, e: (i // tpb, 0, 0)),
                  pl.BlockSpec((1, D_MODEL), lambda i, e: (0, 0)),
                  pl.BlockSpec((1, D_MODEL, EXPERT_DIM), lambda i, e: (e, 0, 0)),
                  pl.BlockSpec((1, D_MODEL, EXPERT_DIM), lambda i, e: (e, 0, 0)),
                  pl.BlockSpec((1, EXPERT_DIM, D_MODEL), lambda i, e: (e, 0, 0)),
                  whole(ws1), whole(ws3), whole(ws2)],
        out_specs=pl.BlockSpec((MOE_TM, D_MODEL), lambda i, e: (i, 0)),
        out_shape=jax.ShapeDtypeStruct((N_TOK, D_MODEL), F32),
        scratch_shapes=[pltpu.VMEM((MOE_TM, D_MODEL), F32)],
        compiler_params=_params(("arbitrary", "arbitrary")),
        name="moe_ffn",
    )(h, gates, x, gt2, g_post.reshape(1, D_MODEL), w1, w3, w2, ws1, ws3, ws2)


def _reorder_w_in(w):
    sizes = (512, 256, 256, 64, 4, 512, 512, 512, 512, 512, 1024, 1024, 3072)
    offs = np.concatenate([[0], np.cumsum(sizes)])
    seg = lambda k: w[:, offs[k]:offs[k + 1]]
    qa, ckv, iq, ik, iw, qb, kb, vb, qc, kc, vc, gc, gates = (seg(k) for k in range(13))
    pad = jnp.zeros((w.shape[0], LANES - 68), w.dtype)
    return jnp.concatenate([gates, vc, gc, qa, qb, kb, vb, qc, kc, ckv, iq, ik, iw, pad], axis=1).astype(BF16)


def _rope_tables(positions):
    half = HEAD_DIM // 2
    inv_freq = ROPE_THETA ** (-jnp.arange(half, dtype=F32) / half)
    ang = positions.astype(F32)[:, :, None] * inv_freq
    cos, sin = jnp.cos(ang), jnp.sin(ang)
    cos64 = jnp.concatenate([cos, cos], axis=-1)
    sin64 = jnp.concatenate([-sin, sin], axis=-1)
    cos_a = jnp.concatenate([cos64, cos64], axis=-1)
    sin_a = jnp.concatenate([sin64, sin64], axis=-1)
    cos_b = jnp.concatenate([cos64, jnp.ones_like(cos64)], axis=-1)
    sin_b = jnp.concatenate([sin64, jnp.zeros_like(sin64)], axis=-1)
    return cos_a, sin_a, cos_b, sin_b


def _decay_tables():
    log_g = jnp.log(1.0 - 2.0 ** (-5.0 - jnp.arange(C_HEADS, dtype=F32)))
    pos = jnp.arange(RET_BLK, dtype=F32)
    diff = pos[:, None] - pos[None, :]
    e_off = jnp.exp(log_g[:, None, None] * diff)
    chunk = jnp.arange(RET_BLK) // CHUNK
    visible = chunk[None, :] <= chunk[:, None]
    e_diag = jnp.where(visible, jnp.exp(log_g[:, None, None] * jnp.abs(diff)), 0.0)
    steps = jnp.arange(SEQ // RET_BLK, dtype=F32) * RET_BLK
    dec = jnp.exp(log_g[:, None] * steps[None, :])
    return dec, e_off, e_diag


def kernel(x, c, positions, ada_w, ada_b, g_pre_mix, g_post_mix, g_pre_ffn, g_post_ffn, w_in, g_kv, w_uk, w_uv,
           g_ret, w_o_a, w_o_b, w_o_c, w_out, w_router, router_bias, w1, w3, w2, ws1, ws3, ws2):
    cos_a, sin_a, cos_b, sin_b = _rope_tables(positions)
    dec, e_off, e_diag = _decay_tables()
    xf = x.reshape(N_TOK, D_MODEL)
    for l in range(DEPTH):
        mod = _ada(c, ada_w[l], ada_b[l]).reshape(BATCH, 1, N_ADA * D_MODEL)
        sh1, sc1, gt1, sh2, sc2, gt2 = (mod[:, :, k * D_MODEL:(k + 1) * D_MODEL] for k in range(N_ADA))
        z = _inproj(xf, g_pre_mix[l], sc1, sh1, _reorder_w_in(w_in[l]))
        z3 = z.reshape(BATCH, SEQ, N_PROJ)
        w_ukv = jnp.concatenate([w_uk[l], w_uv[l]], axis=1).astype(BF16)
        ya = _dsa(z3, cos_a, sin_a, cos_b, sin_b, g_kv[l], w_ukv).reshape(N_TOK, -1)
        yb = _sb(z3).reshape(N_TOK, -1)
        yc = _ret(z3, cos_a, sin_a, dec, e_off, e_diag, g_ret[l]).reshape(N_TOK, -1)
        xf, h, gates = _merge(ya, yb, yc, z, xf, gt1, g_post_mix[l], g_pre_ffn[l], sc2, sh2,
                              w_o_a[l].astype(BF16), w_o_b[l].astype(BF16), w_o_c[l].astype(BF16),
                              w_out[l].astype(BF16), w_router[l].T.astype(BF16),
                              router_bias[l].reshape(N_EXPERTS, 1))
        xf = _moe(h, gates, xf, gt2, g_post_ffn[l], w1[l].astype(BF16), w3[l].astype(BF16), w2[l].astype(BF16),
                  ws1[l].astype(BF16), ws3[l].astype(BF16), ws2[l].astype(BF16))
    return xf.reshape(BATCH, SEQ, D_MODEL)
```

```python
import functools
import math

import numpy as np
import jax
import jax.numpy as jnp
from jax import lax
from jax.experimental import pallas as pl
from jax.experimental.pallas import tpu as pltpu

F32 = jnp.float32
BF16 = jnp.bfloat16
I32 = jnp.int32

D_MODEL = 1024
BATCH = 8
SEQ = 2048
DEPTH = 2
N_TOK = BATCH * SEQ

CHUNK = 64
HEAD_DIM = 64
ROPE_THETA = 10000.0
EPS = 1e-6
A_HEADS = 8
A_KV_LORA = 256
IDX_HEADS = 4
IDX_DIM = 64
TOPK = 256
B_HEADS = 8
C_HEADS = 8
C_V_DIM = 128
N_EXPERTS = 64
TOP_K = 8
N_GROUPS = 8
TOPK_GROUPS = 4
EXPERT_DIM = 256
SHARED_DIM = 256
ROUTED_SCALE = 2.5
N_ADA = 6

LANES = 128
VMEM_LIMIT = 56 * 1024 * 1024

COL_GATES = 0
COL_VC = 3072
COL_GC = 4096
COL_QA = 5120
COL_QB = 5632
COL_KB = 6144
COL_VB = 6656
COL_QC = 7168
COL_KC = 7680
COL_CKV = 8192
COL_IQ = 8448
COL_IKIW = 8704
N_PROJ = 8832
PROJ_TN = 2944
PROJ_TM = 512

INT_MIN = -2147483648
NEG_INF_KEY = -2139095041

NT_DIMS = (((1,), (1,)), ((), ()))


def _params(sem):
    return pltpu.CompilerParams(dimension_semantics=sem, vmem_limit_bytes=VMEM_LIMIT)


def _sigmoid(x):
    return 1.0 / (1.0 + jnp.exp(-x))


def _rope_lanes(x, cos, sin_signed):
    n = x.shape[-1]
    lane = lax.broadcasted_iota(I32, (1, n), 1)
    first = (lane & 32) == 0
    ahead = pltpu.roll(x, n - 32, 1)
    behind = pltpu.roll(x, 32, 1)
    return x * cos + jnp.where(first, ahead, behind) * sin_signed


def _ada_kernel(c_ref, w_ref, b_ref, o_ref):
    c = c_ref[...]
    s = (c * _sigmoid(c)).astype(BF16)
    o_ref[...] = jnp.dot(s, w_ref[...].astype(BF16), preferred_element_type=F32) + b_ref[...]


def _ada(c, w, b):
    n = w.shape[1]
    return pl.pallas_call(
        _ada_kernel,
        grid=(n // D_MODEL,),
        in_specs=[pl.BlockSpec((BATCH, D_MODEL), lambda j: (0, 0)),
                  pl.BlockSpec((D_MODEL, D_MODEL), lambda j: (0, j)),
                  pl.BlockSpec((1, D_MODEL), lambda j: (0, j))],
        out_specs=pl.BlockSpec((BATCH, D_MODEL), lambda j: (0, j)),
        out_shape=jax.ShapeDtypeStruct((BATCH, n), F32),
        compiler_params=_params(("arbitrary",)),
        name="ada_mod",
    )(c, w, b.reshape(1, n))


def _inproj_kernel(x_ref, g_ref, sc_ref, sh_ref, w_ref, o_ref, h_ref):
    @pl.when(pl.program_id(1) == 0)
    def _():
        x = x_ref[...]
        ms = jnp.mean(x * x, axis=-1, keepdims=True)
        y = x * lax.rsqrt(ms + EPS) * g_ref[...]
        h_ref[...] = (y * (1.0 + sc_ref[0]) + sh_ref[0]).astype(BF16)

    o_ref[...] = lax.dot_general(h_ref[...], w_ref[...], NT_DIMS, preferred_element_type=F32)


def _inproj(x, g, sc, sh, w_t_bf16):
    tiles_per_batch = SEQ // PROJ_TM
    return pl.pallas_call(
        _inproj_kernel,
        grid=(N_TOK // PROJ_TM, N_PROJ // PROJ_TN),
        in_specs=[pl.BlockSpec((PROJ_TM, D_MODEL), lambda i, j: (i, 0)),
                  pl.BlockSpec((1, D_MODEL), lambda i, j: (0, 0)),
                  pl.BlockSpec((1, 1, D_MODEL), lambda i, j: (i // tiles_per_batch, 0, 0)),
                  pl.BlockSpec((1, 1, D_MODEL), lambda i, j: (i // tiles_per_batch, 0, 0)),
                  pl.BlockSpec((PROJ_TN, D_MODEL), lambda i, j: (j, 0))],
        out_specs=pl.BlockSpec((PROJ_TM, PROJ_TN), lambda i, j: (i, j)),
        out_shape=jax.ShapeDtypeStruct((N_TOK, N_PROJ), F32),
        scratch_shapes=[pltpu.VMEM((PROJ_TM, D_MODEL), BF16)],
        compiler_params=_params(("arbitrary", "arbitrary")),
        name="in_proj",
    )(x, g.reshape(1, D_MODEL), sc, sh, w_t_bf16)


DSA_QB = 128
DSA_KC = 256


def _dsa_kernel(ckv_ref, ikiw_ref, q_ref, iq_ref, cos_a, sin_a, cos_b, sin_b, gkv_ref, wukv_ref,
                o_ref, k_lo, k_hi, v_dup, ik_lo, ik_hi, keys_s, selt_s, acc_s):
    qi = pl.program_id(1)
    lane = lax.broadcasted_iota(I32, (1, LANES), 1)
    lo_half = lane < 64
    n_kc = SEQ // DSA_KC

    @pl.when(qi == 0)
    def _prologue():
        def chunk(c, carry):
            r0 = pl.multiple_of(c * DSA_KC, DSA_KC)
            rows = pl.ds(r0, DSA_KC)
            ckv = ckv_ref[0, rows, :]
            ms = jnp.mean(ckv * ckv, axis=-1, keepdims=True)
            cn = (ckv * lax.rsqrt(ms + EPS) * gkv_ref[...]).astype(BF16)
            kv = jnp.dot(cn, wukv_ref[...], preferred_element_type=F32)
            cb, sb = cos_b[0, rows, :], sin_b[0, rows, :]
            kv = _rope_lanes(kv, cb, sb)
            swapped = pltpu.roll(kv, 64, 1)
            k_lo[rows, :] = jnp.where(lo_half, kv, 0.0).astype(BF16)
            k_hi[rows, :] = jnp.where(lo_half, 0.0, swapped).astype(BF16)
            v_dup[rows, :] = jnp.where(lo_half, swapped, kv).astype(BF16)
            ik = jnp.where(lo_half, _rope_lanes(ikiw_ref[0, rows, :], cb, sb), 0.0)
            ik_lo[rows, :] = ik.astype(BF16)
            ik_hi[rows, :] = pltpu.roll(ik, 64, 1).astype(BF16)
            return carry
        lax.fori_loop(0, n_kc, chunk, 0)

    q0 = pl.multiple_of(qi * DSA_QB, DSA_QB)
    qrows = pl.ds(q0, DSA_QB)
    cq, sq = cos_a[0, qrows, :], sin_a[0, qrows, :]

    iq = iq_ref[0]
    iq_heads = []
    for p in range(IDX_HEADS // 2):
        pair = _rope_lanes(iq[:, p * LANES:(p + 1) * LANES], cq, sq)
        iq_heads.append(jnp.where(lo_half, pair, 0.0).astype(BF16))
        iq_heads.append(jnp.where(lo_half, 0.0, pair).astype(BF16))
    iw_t = ikiw_ref[0, qrows, :].T * (IDX_HEADS ** -0.5)
    q_chunk = (q0 + lane) >> 6
    n_live = lax.shift_right_logical(qi + 2, 1)

    def chunk_rows(c):
        return pl.ds(pl.multiple_of(c * DSA_KC, DSA_KC), DSA_KC)

    def score_chunk(c, carry):
        rows = chunk_rows(c)
        k_chunk = (c * DSA_KC + lax.broadcasted_iota(I32, (DSA_KC, 1), 0)) >> 6
        score = jnp.zeros((DSA_KC, DSA_QB), F32)
        for h in range(IDX_HEADS):
            ik = ik_lo if h % 2 == 0 else ik_hi
            logit = lax.dot_general(ik[rows, :], iq_heads[h], NT_DIMS,
                                    preferred_element_type=F32) * (IDX_DIM ** -0.5)
            score = score + iw_t[64 + h:65 + h, :] * jnp.maximum(logit, 0.0)
        score = jnp.where(score == 0.0, 0.0, score)
        score = jnp.where(k_chunk <= q_chunk, score, -jnp.inf)
        bits = lax.bitcast_convert_type(score, I32)
        keys_s[rows, :] = bits ^ ((bits >> 31) & 0x7FFFFFFF)
        return carry
    lax.fori_loop(0, n_live, score_chunk, 0)

    def count(pred):
        def chunk(c, acc):
            hit = pred(keys_s[chunk_rows(c), :]).astype(I32)
            return acc + jnp.sum(hit.reshape(4, DSA_KC // 4, DSA_QB), axis=0)
        acc = lax.fori_loop(0, n_live, chunk, jnp.zeros((DSA_KC // 4, DSA_QB), I32))
        return jnp.sum(acc, axis=0, keepdims=True)

    def descend(it, prefix):
        cand = prefix | lax.shift_left(jnp.int32(1), 31 - it)
        bound = cand ^ INT_MIN
        return jnp.where(count(lambda k: k >= bound) >= TOPK, cand, prefix)
    thr = lax.fori_loop(0, 32, descend, jnp.zeros((1, DSA_QB), I32)) ^ INT_MIN

    need = (TOPK - count(lambda k: k > thr)).astype(F32)
    tri = (lax.broadcasted_iota(I32, (DSA_KC, DSA_KC), 0)
           >= lax.broadcasted_iota(I32, (DSA_KC, DSA_KC), 1)).astype(BF16)

    def select_chunk(c, seen):
        key = keys_s[chunk_rows(c), :]
        eq = key == thr
        rank = jnp.dot(tri, eq.astype(BF16), preferred_element_type=F32) + seen
        sel = ((key > thr) | (eq & (rank <= need))) & (key > NEG_INF_KEY)
        selt_s[c] = sel.astype(F32).T
        return rank[DSA_KC - 1:DSA_KC, :]
    lax.fori_loop(0, n_live, select_chunk, jnp.zeros((1, DSA_QB), F32))

    q = q_ref[0]
    q_heads = []
    for p in range(A_HEADS // 2):
        pair = _rope_lanes(q[:, p * LANES:(p + 1) * LANES], cq, sq) * (HEAD_DIM ** -0.5)
        q_heads.append(jnp.where(lo_half, pair, 0.0).astype(BF16))
        q_heads.append(jnp.where(lo_half, 0.0, pair).astype(BF16))
    masked = -1e30

    def attend(c, carry):
        rows = chunk_rows(c)
        sel = selt_s[c] > 0.0
        k_ops = (k_lo[rows, :], k_hi[rows, :])
        vd = v_dup[rows, :]
        new = []
        for h in range(A_HEADS):
            m, l = carry[2 * h], carry[2 * h + 1]
            s = jnp.where(sel, lax.dot_general(q_heads[h], k_ops[h % 2], NT_DIMS,
                                               preferred_element_type=F32), masked)
            m_new = jnp.maximum(m, jnp.max(s, axis=1, keepdims=True))
            pe = jnp.where(sel, jnp.exp(s - m_new), 0.0)
            alpha = jnp.exp(m - m_new)
            acc_s[h] = alpha * acc_s[h] + jnp.dot(pe.astype(BF16), vd, preferred_element_type=F32)
            new += [m_new, alpha * l + jnp.sum(pe, axis=1, keepdims=True)]
        return tuple(new)

    acc_s[...] = jnp.zeros_like(acc_s)
    stats = lax.fori_loop(0, n_live, attend,
                          (jnp.full((DSA_QB, 1), masked, F32), jnp.zeros((DSA_QB, 1), F32)) * A_HEADS)
    for p in range(A_HEADS // 2):
        even = acc_s[2 * p] / stats[4 * p + 1]
        odd = acc_s[2 * p + 1] / stats[4 * p + 3]
        o_ref[0, :, p * LANES:(p + 1) * LANES] = jnp.where(lo_half, even, odd)


def _dsa(z3, cos_a, sin_a, cos_b, sin_b, g_kv, w_ukv_bf16):
    full = lambda width, col: pl.BlockSpec((1, SEQ, width), lambda b, i: (b, 0, col // width))
    blk = lambda width, col: pl.BlockSpec((1, DSA_QB, width), lambda b, i: (b, i, col // width))
    tab = pl.BlockSpec((1, SEQ, LANES), lambda b, i: (b, 0, 0))
    return pl.pallas_call(
        _dsa_kernel,
        grid=(BATCH, SEQ // DSA_QB),
        in_specs=[full(A_KV_LORA, COL_CKV), full(LANES, COL_IKIW),
                  blk(A_HEADS * HEAD_DIM, COL_QA), blk(IDX_HEADS * IDX_DIM, COL_IQ),
                  tab, tab, tab, tab,
                  pl.BlockSpec((1, A_KV_LORA), lambda b, i: (0, 0)),
                  pl.BlockSpec((A_KV_LORA, LANES), lambda b, i: (0, 0))],
        out_specs=pl.BlockSpec((1, DSA_QB, A_HEADS * HEAD_DIM), lambda b, i: (b, i, 0)),
        out_shape=jax.ShapeDtypeStruct((BATCH, SEQ, A_HEADS * HEAD_DIM), F32),
        scratch_shapes=[pltpu.VMEM((SEQ, LANES), BF16)] * 5
                       + [pltpu.VMEM((SEQ, DSA_QB), I32),
                          pltpu.VMEM((SEQ // DSA_KC, DSA_QB, DSA_KC), F32),
                          pltpu.VMEM((A_HEADS, DSA_QB, LANES), F32)],
        compiler_params=_params(("arbitrary", "arbitrary")),
        name="dsa_mixer",
    )(z3, z3, z3, z3, cos_a, sin_a, cos_b, sin_b, g_kv.reshape(1, A_KV_LORA), w_ukv_bf16)


SB_BLK = 128
SB_EXP_UNDERFLOW = -105.0


def _sb_kernel(q_ref, k_ref, v_ref, o_ref, tail_s, acc_s):
    qi = pl.program_id(1)
    lane = lax.broadcasted_iota(I32, (1, LANES), 1)
    lo_half = lane < 64
    row = lax.broadcasted_iota(I32, (SB_BLK, SB_BLK), 0)
    col = lax.broadcasted_iota(I32, (SB_BLK, SB_BLK), 1)
    causal = col < row
    r2 = lax.broadcasted_iota(I32, (2 * SB_BLK, 2 * SB_BLK), 0) & (SB_BLK - 1)
    c2 = lax.broadcasted_iota(I32, (2 * SB_BLK, 2 * SB_BLK), 1)
    suffix_mat = ((c2 >= SB_BLK) | (r2 > c2)).astype(BF16)

    def key_block(kb, diagonal):
        r0 = pl.multiple_of(kb * SB_BLK, SB_BLK)
        worst = None
        for p in range(B_HEADS // 2):
            cols = slice(p * LANES, (p + 1) * LANES)
            q2 = q_ref[0, :, cols] * (HEAD_DIM ** -0.5)
            k2 = k_ref[0, pl.ds(r0, SB_BLK), cols].astype(BF16)
            v2 = v_ref[0, pl.ds(r0, SB_BLK), cols].astype(BF16)
            for e in range(2):
                h = 2 * p + e
                qm = (jnp.where(lo_half, q2, 0.0) if e == 0 else jnp.where(lo_half, 0.0, q2)).astype(BF16)
                z = lax.dot_general(qm, k2, NT_DIMS, preferred_element_type=F32)
                log_beta = jnp.minimum(z, 0.0) - jnp.log1p(jnp.exp(-jnp.abs(z)))
                log_not = log_beta - z
                if diagonal:
                    log_not = jnp.where(causal, log_not, 0.0)
                hi = log_not.astype(BF16)
                lo = (log_not - hi.astype(F32)).astype(BF16)
                t = jnp.dot(jnp.concatenate([hi, lo], axis=1), suffix_mat, preferred_element_type=F32)
                if diagonal:
                    a = jnp.where(causal, jnp.exp(log_beta + t[:, :SB_BLK]), 0.0)
                    tail = t[:, SB_BLK:]
                    acc_s[h] = jnp.dot(a.astype(BF16), v2, preferred_element_type=F32)
                else:
                    a = jnp.exp(log_beta + t[:, :SB_BLK] + tail_s[h])
                    tail = tail_s[h] + t[:, SB_BLK:]
                    acc_s[h] += jnp.dot(a.astype(BF16), v2, preferred_element_type=F32)
                tail_s[h] = tail
                worst = tail if worst is None else jnp.maximum(worst, tail)
        return jnp.max(worst)

    first = key_block(qi, True)

    def more(state):
        jj, worst = state
        return (jj <= qi) & (worst > SB_EXP_UNDERFLOW)

    def step(state):
        jj, _ = state
        return jj + 1, key_block(qi - jj, False)

    lax.while_loop(more, step, (jnp.int32(1), first))
    for p in range(B_HEADS // 2):
        o_ref[0, :, p * LANES:(p + 1) * LANES] = jnp.where(lo_half, acc_s[2 * p], acc_s[2 * p + 1])


def _sb(z3):
    width = B_HEADS * HEAD_DIM
    return pl.pallas_call(
        _sb_kernel,
        grid=(BATCH, SEQ // SB_BLK),
        in_specs=[pl.BlockSpec((1, SB_BLK, width), lambda b, i: (b, i, COL_QB // width)),
                  pl.BlockSpec((1, SEQ, width), lambda b, i: (b, 0, COL_KB // width)),
                  pl.BlockSpec((1, SEQ, width), lambda b, i: (b, 0, COL_VB // width))],
        out_specs=pl.BlockSpec((1, SB_BLK, width), lambda b, i: (b, i, 0)),
        out_shape=jax.ShapeDtypeStruct((BATCH, SEQ, width), F32),
        scratch_shapes=[pltpu.VMEM((B_HEADS, SB_BLK, SB_BLK), F32)] * 2,
        compiler_params=_params(("arbitrary", "arbitrary")),
        name="stickbreak_mixer",
    )(z3, z3, z3)


RET_BLK = 256


def _ret_kernel(dec_ref, q_ref, k_ref, v_ref, gate_ref, cos_a, sin_a, qs_ref, ks_ref, e_diag, gret_ref,
                o_ref, state):
    lane = lax.broadcasted_iota(I32, (1, LANES), 1)
    lo_half = lane < 64

    @pl.when(pl.program_id(1) == 0)
    def _():
        state[...] = jnp.zeros_like(state)

    cq, sq = cos_a[0], sin_a[0]
    for p in range(C_HEADS // 2):
        cols = slice(p * LANES, (p + 1) * LANES)
        q2 = _rope_lanes(q_ref[0, :, cols], cq, sq)
        k2 = _rope_lanes(k_ref[0, :, cols], cq, sq) * (HEAD_DIM ** -0.5)
        q_hat = q2 * qs_ref[:, cols]
        k_hat = k2 * ks_ref[:, cols]
        for e in range(2):
            h = 2 * p + e
            pick = (lambda x: jnp.where(lo_half, x, 0.0)) if e == 0 else (lambda x: jnp.where(lo_half, 0.0, x))
            vcols = slice(h * C_V_DIM, (h + 1) * C_V_DIM)
            v2 = v_ref[0, :, vcols].astype(BF16)
            s = lax.dot_general(pick(q2).astype(BF16), k2.astype(BF16), NT_DIMS,
                                preferred_element_type=F32) * e_diag[h]
            o = jnp.dot(s.astype(BF16), v2, preferred_element_type=F32)
            o = o + jnp.dot(pick(q_hat).astype(BF16), state[h].astype(BF16), preferred_element_type=F32)
            kv = jnp.dot(pick(k_hat).T.astype(BF16), v2, preferred_element_type=F32)
            state[h] = dec_ref[h] * state[h] + kv
            mu = jnp.mean(o, axis=-1, keepdims=True)
            d = o - mu
            var = jnp.mean(d * d, axis=-1, keepdims=True)
            y = d * lax.rsqrt(var + EPS) * gret_ref[:, vcols]
            g = gate_ref[0, :, vcols]
            o_ref[0, :, vcols] = g * _sigmoid(g) * y


def _ret(z3, cos_a, sin_a, dec, q_scale, k_scale, e_diag, g_ret):
    qk_w = C_HEADS * HEAD_DIM
    v_w = C_HEADS * C_V_DIM
    tab = pl.BlockSpec((1, RET_BLK, LANES), lambda b, i: (b, i, 0))
    const2 = lambda a: pl.BlockSpec(a.shape, lambda b, i: (0, 0))
    return pl.pallas_call(
        _ret_kernel,
        grid=(BATCH, SEQ // RET_BLK),
        in_specs=[pl.BlockSpec(memory_space=pltpu.SMEM),
                  pl.BlockSpec((1, RET_BLK, qk_w), lambda b, i: (b, i, COL_QC // qk_w)),
                  pl.BlockSpec((1, RET_BLK, qk_w), lambda b, i: (b, i, COL_KC // qk_w)),
                  pl.BlockSpec((1, RET_BLK, v_w), lambda b, i: (b, i, COL_VC // v_w)),
                  pl.BlockSpec((1, RET_BLK, v_w), lambda b, i: (b, i, COL_GC // v_w)),
                  tab, tab, const2(q_scale), const2(k_scale),
                  pl.BlockSpec(e_diag.shape, lambda b, i: (0, 0, 0)),
                  pl.BlockSpec((1, v_w), lambda b, i: (0, 0))],
        out_specs=pl.BlockSpec((1, RET_BLK, v_w), lambda b, i: (b, i, 0)),
        out_shape=jax.ShapeDtypeStruct((BATCH, SEQ, v_w), F32),
        scratch_shapes=[pltpu.VMEM((C_HEADS, LANES, C_V_DIM), F32)],
        compiler_params=_params(("arbitrary", "arbitrary")),
        name="retention_mixer",
    )(dec, z3, z3, z3, z3, cos_a, sin_a, q_scale, k_scale, e_diag, g_ret.reshape(1, v_w))


MERGE_TM = 256


def _rms(y, g):
    ms = jnp.mean(y * y, axis=-1, keepdims=True)
    return y * lax.rsqrt(ms + EPS) * g


def _route(logits_t, bias_col):
    n_tok = logits_t.shape[1]
    per_group = N_EXPERTS // N_GROUPS
    scores = _sigmoid(logits_t)
    sel = scores + bias_col
    sel3 = sel.reshape(N_GROUPS, per_group, n_tok)
    member = lax.broadcasted_iota(I32, (N_GROUPS, per_group, n_tok), 1)
    m1 = jnp.max(sel3, axis=1, keepdims=True)
    first = jnp.min(jnp.where(sel3 == m1, member, per_group), axis=1, keepdims=True)
    m2 = jnp.max(jnp.where(member == first, -jnp.inf, sel3), axis=1, keepdims=True)
    grp = (m1 + m2).reshape(N_GROUPS, n_tok)
    gid = lax.broadcasted_iota(I32, (N_GROUPS, n_tok), 0)
    rank = jnp.zeros((N_GROUPS, n_tok), I32)
    for g in range(N_GROUPS):
        other = grp[g:g + 1, :]
        rank = rank + ((other > grp) | ((other == grp) & (g < gid))).astype(I32)
    keep = jnp.where(rank < TOPK_GROUPS, 1.0, 0.0).reshape(N_GROUPS, 1, n_tok)
    live = jnp.where(keep > 0.0, sel3, -jnp.inf).reshape(N_EXPERTS, n_tok)
    eid = lax.broadcasted_iota(I32, (N_EXPERTS, n_tok), 0)
    chosen = jnp.zeros((N_EXPERTS, n_tok), jnp.bool_)
    for _ in range(TOP_K):
        m = jnp.max(live, axis=0, keepdims=True)
        pick = eid == jnp.min(jnp.where(live == m, eid, N_EXPERTS), axis=0, keepdims=True)
        chosen = chosen | pick
        live = jnp.where(pick, -jnp.inf, live)
    w = jnp.where(chosen, scores, 0.0)
    return w / jnp.sum(w, axis=0, keepdims=True) * ROUTED_SCALE


def _merge_kernel(ya_ref, yb_ref, yc_ref, g_ref, x_ref, gt_ref, gpost_ref, gpre_ref, sc_ref, sh_ref,
                  woa, wob, woc, wout, wr_t, rb_ref, xo_ref, h_ref, gate_ref):
    ya = jnp.dot(ya_ref[...].astype(BF16), woa[...], preferred_element_type=F32)
    yb = jnp.dot(yb_ref[...].astype(BF16), wob[...], preferred_element_type=F32)
    yc = jnp.dot(yc_ref[...].astype(BF16), woc[...], preferred_element_type=F32)
    merged = (_sigmoid(g_ref[:, 0:D_MODEL]) * ya
              + _sigmoid(g_ref[:, D_MODEL:2 * D_MODEL]) * yb
              + _sigmoid(g_ref[:, 2 * D_MODEL:3 * D_MODEL]) * yc)
    y = jnp.dot(merged.astype(BF16), wout[...], preferred_element_type=F32)
    x = x_ref[...] + gt_ref[0] * _rms(y, gpost_ref[...])
    xo_ref[...] = x
    h = (_rms(x, gpre_ref[...]) * (1.0 + sc_ref[0]) + sh_ref[0]).astype(BF16)
    h_ref[...] = h
    logits_t = lax.dot_general(wr_t[...], h, NT_DIMS, preferred_element_type=F32)
    gates_t = _route(logits_t, rb_ref[...])
    gate_ref[...] = jnp.concatenate([gates_t, jnp.zeros_like(gates_t)], axis=0).T


def _merge(ya, yb, yc, z, x, gt1, g_post, g_pre, sc2, sh2, woa, wob, woc, wout, wr_t, rbias):
    tpb = SEQ // MERGE_TM
    row = lambda w: pl.BlockSpec((MERGE_TM, w), lambda i: (i, 0))
    vec = pl.BlockSpec((1, D_MODEL), lambda i: (0, 0))
    mod = pl.BlockSpec((1, 1, D_MODEL), lambda i: (i // tpb, 0, 0))
    whole = lambda a: pl.BlockSpec(a.shape, lambda i: (0, 0))
    return pl.pallas_call(
        _merge_kernel,
        grid=(N_TOK // MERGE_TM,),
        in_specs=[row(512), row(512), row(1024), pl.BlockSpec((MERGE_TM, 3 * D_MODEL), lambda i: (i, 0)),
                  row(D_MODEL), mod, vec, vec, mod, mod,
                  whole(woa), whole(wob), whole(woc), whole(wout), whole(wr_t), whole(rbias)],
        out_specs=[row(D_MODEL), row(D_MODEL), row(LANES)],
        out_shape=[jax.ShapeDtypeStruct((N_TOK, D_MODEL), F32),
                   jax.ShapeDtypeStruct((N_TOK, D_MODEL), BF16),
                   jax.ShapeDtypeStruct((N_TOK, LANES), F32)],
        compiler_params=_params(("arbitrary",)),
        name="merge_route",
    )(ya, yb, yc, z, x, gt1, g_post.reshape(1, D_MODEL), g_pre.reshape(1, D_MODEL), sc2, sh2,
      woa, wob, woc, wout, wr_t, rbias)


MOE_TM = 1024


def _moe_kernel(h_ref, gate_ref, x_ref, gt_ref, gpost_ref, w1, w3, w2, ws1, ws3, ws2, o_ref, acc):
    e = pl.program_id(1)
    h = h_ref[...]

    @pl.when(e == 0)
    def _():
        acc[...] = jnp.zeros_like(acc)

    g = gate_ref[...]
    g_hi = g.astype(BF16)
    g_lo = (g - g_hi.astype(F32)).astype(BF16)
    onehot = (lax.broadcasted_iota(I32, (LANES, LANES), 0) == e).astype(BF16)
    gcol = (jnp.dot(g_hi, onehot, preferred_element_type=F32)
            + jnp.dot(g_lo, onehot, preferred_element_type=F32))
    a = jnp.dot(h, w1[0], preferred_element_type=F32)
    hid = a * _sigmoid(a) * jnp.dot(h, w3[0], preferred_element_type=F32)
    y = jnp.dot(hid.astype(BF16), w2[0], preferred_element_type=F32)
    acc[...] += y * jnp.concatenate([gcol] * (D_MODEL // LANES), axis=1)

    @pl.when(e == N_EXPERTS - 1)
    def _():
        s = jnp.dot(h, ws1[...], preferred_element_type=F32)
        shid = s * _sigmoid(s) * jnp.dot(h, ws3[...], preferred_element_type=F32)
        y_all = acc[...] + jnp.dot(shid.astype(BF16), ws2[...], preferred_element_type=F32)
        o_ref[...] = x_ref[...] + gt_ref[0] * _rms(y_all, gpost_ref[...])


def _moe(h, gates, x, gt2, g_post, w1, w3, w2, ws1, ws3, ws2):
    tpb = SEQ // MOE_TM
    whole = lambda a: pl.BlockSpec(a.shape, lambda i, e: (0, 0))
    return pl.pallas_call(
        _moe_kernel,
        grid=(N_TOK // MOE_TM, N_EXPERTS),
        in_specs=[pl.BlockSpec((MOE_TM, D_MODEL), lambda i, e: (i, 0)),
                  pl.BlockSpec((MOE_TM, LANES), lambda i, e: (i, 0)),
                  pl.BlockSpec((MOE_TM, D_MODEL), lambda i, e: (i, 0)),
                  pl.BlockSpec((1, 1, D_MODEL), lambda i, e: (i // tpb, 0, 0)),
                  pl.BlockSpec((1, D_MODEL), lambda i, e: (0, 0)),
                  pl.BlockSpec((1, D_MODEL, EXPERT_DIM), lambda i, e: (e, 0, 0)),
                  pl.BlockSpec((1, D_MODEL, EXPERT_DIM), lambda i, e: (e, 0, 0)),
                  pl.BlockSpec((1, EXPERT_DIM, D_MODEL), lambda i, e: (e, 0, 0)),
                  whole(ws1), whole(ws3), whole(ws2)],
        out_specs=pl.BlockSpec((MOE_TM, D_MODEL), lambda i, e: (i, 0)),
        out_shape=jax.ShapeDtypeStruct((N_TOK, D_MODEL), F32),
        scratch_shapes=[pltpu.VMEM((MOE_TM, D_MODEL), F32)],
        compiler_params=_params(("arbitrary", "arbitrary")),
        name="moe_ffn",
    )(h, gates, x, gt2, g_post.reshape(1, D_MODEL), w1, w3, w2, ws1, ws3, ws2)


_IN_SIZES = (512, 256, 256, 64, 4, 512, 512, 512, 512, 512, 1024, 1024, 3072)
_IN_DEST = (COL_QA, COL_CKV, COL_IQ, COL_IKIW, COL_IKIW + 64, COL_QB, COL_KB, COL_VB, COL_QC, COL_KC,
            COL_VC, COL_GC, COL_GATES)
_IN_SRC = tuple(int(v) for v in np.cumsum((0,) + _IN_SIZES[:-1]))
N_IN = sum(_IN_SIZES)
PACK_TK = 256


def _pack_kernel(w_ref, o_ref):
    for src, width, dst in zip(_IN_SRC, _IN_SIZES, _IN_DEST):
        o_ref[dst:dst + width, :] = w_ref[src:src + width, :].astype(BF16)
    o_ref[COL_IKIW + 68:COL_IKIW + LANES, :] = jnp.zeros((LANES - 68, PACK_TK), BF16)


def _reorder_w_in(w_t):
    return pl.pallas_call(
        _pack_kernel,
        grid=(D_MODEL // PACK_TK,),
        in_specs=[pl.BlockSpec((N_IN, PACK_TK), lambda i: (0, i))],
        out_specs=pl.BlockSpec((N_PROJ, PACK_TK), lambda i: (0, i)),
        out_shape=jax.ShapeDtypeStruct((N_PROJ, D_MODEL), BF16),
        compiler_params=_params(("arbitrary",)),
        name="pack_w_in",
    )(w_t)


def _rope_tables(positions):
    half = HEAD_DIM // 2
    inv_freq = ROPE_THETA ** (-jnp.arange(half, dtype=F32) / half)
    ang = positions.astype(F32)[:, :, None] * inv_freq
    cos, sin = jnp.cos(ang), jnp.sin(ang)
    cos64 = jnp.concatenate([cos, cos], axis=-1)
    sin64 = jnp.concatenate([-sin, sin], axis=-1)
    cos_a = jnp.concatenate([cos64, cos64], axis=-1)
    sin_a = jnp.concatenate([sin64, sin64], axis=-1)
    cos_b = jnp.concatenate([cos64, jnp.ones_like(cos64)], axis=-1)
    sin_b = jnp.concatenate([sin64, jnp.zeros_like(sin64)], axis=-1)
    return cos_a, sin_a, cos_b, sin_b


def _decay_tables():
    log_g = jnp.log(1.0 - 2.0 ** (-5.0 - jnp.arange(C_HEADS, dtype=F32)))
    pos = jnp.arange(RET_BLK, dtype=F32)
    diff = pos[:, None] - pos[None, :]
    chunk = jnp.arange(RET_BLK) // CHUNK
    visible = chunk[None, :] <= chunk[:, None]
    e_diag = jnp.where(visible, jnp.exp(log_g[:, None, None] * jnp.abs(diff)), 0.0)
    dec = jnp.exp(log_g * RET_BLK)
    per_lane = jnp.repeat(log_g, HEAD_DIM)[None, :]
    q_scale = jnp.exp(per_lane * pos[:, None])
    k_scale = jnp.exp(per_lane * (RET_BLK - pos)[:, None])
    return dec, q_scale, k_scale, e_diag


def kernel(x, c, positions, ada_w, ada_b, g_pre_mix, g_post_mix, g_pre_ffn, g_post_ffn, w_in, g_kv, w_uk, w_uv,
           g_ret, w_o_a, w_o_b, w_o_c, w_out, w_router, router_bias, w1, w3, w2, ws1, ws3, ws2):
    cos_a, sin_a, cos_b, sin_b = _rope_tables(positions)
    dec, q_scale, k_scale, e_diag = _decay_tables()
    xf = x.reshape(N_TOK, D_MODEL)
    for l in range(DEPTH):
        mod = _ada(c, ada_w[l], ada_b[l]).reshape(BATCH, 1, N_ADA * D_MODEL)
        sh1, sc1, gt1, sh2, sc2, gt2 = (mod[:, :, k * D_MODEL:(k + 1) * D_MODEL] for k in range(N_ADA))
        z = _inproj(xf, g_pre_mix[l], sc1, sh1, _reorder_w_in(w_in[l].T))
        z3 = z.reshape(BATCH, SEQ, N_PROJ)
        w_ukv = jnp.concatenate([w_uk[l], w_uv[l]], axis=1).astype(BF16)
        ya = _dsa(z3, cos_a, sin_a, cos_b, sin_b, g_kv[l], w_ukv).reshape(N_TOK, -1)
        yb = _sb(z3).reshape(N_TOK, -1)
        yc = _ret(z3, cos_a, sin_a, dec, q_scale, k_scale, e_diag, g_ret[l]).reshape(N_TOK, -1)
        xf, h, gates = _merge(ya, yb, yc, z, xf, gt1, g_post_mix[l], g_pre_ffn[l], sc2, sh2,
                              w_o_a[l].astype(BF16), w_o_b[l].astype(BF16), w_o_c[l].astype(BF16),
                              w_out[l].astype(BF16), w_router[l].T.astype(BF16),
                              router_bias[l].reshape(N_EXPERTS, 1))
        xf = _moe(h, gates, xf, gt2, g_post_ffn[l], w1[l].astype(BF16), w3[l].astype(BF16), w2[l].astype(BF16),
                  ws1[l].astype(BF16), ws3[l].astype(BF16), ws2[l].astype(BF16))
    return xf.reshape(BATCH, SEQ, D_MODEL)
```

```python
import functools
import math

import numpy as np
import jax
import jax.numpy as jnp
from jax import lax
from jax.experimental import pallas as pl
from jax.experimental.pallas import tpu as pltpu

F32 = jnp.float32
BF16 = jnp.bfloat16
I32 = jnp.int32

D_MODEL = 1024
BATCH = 8
SEQ = 2048
DEPTH = 2
N_TOK = BATCH * SEQ

CHUNK = 64
HEAD_DIM = 64
ROPE_THETA = 10000.0
EPS = 1e-6
A_HEADS = 8
A_KV_LORA = 256
IDX_HEADS = 4
IDX_DIM = 64
TOPK = 256
B_HEADS = 8
C_HEADS = 8
C_V_DIM = 128
N_EXPERTS = 64
TOP_K = 8
N_GROUPS = 8
TOPK_GROUPS = 4
EXPERT_DIM = 256
SHARED_DIM = 256
ROUTED_SCALE = 2.5
N_ADA = 6

LANES = 128
VMEM_LIMIT = 56 * 1024 * 1024

COL_GATES = 0
COL_VC = 3072
COL_GC = 4096
COL_QA = 5120
COL_QB = 5632
COL_KB = 6144
COL_VB = 6656
COL_QC = 7168
COL_KC = 7680
COL_CKV = 8192
COL_IQ = 8448
COL_IKIW = 8704
N_PROJ = 8832
PROJ_TN = 2944
PROJ_TM = 512

INT_MIN = -2147483648
NEG_INF_KEY = -2139095041

NT_DIMS = (((1,), (1,)), ((), ()))


def _params(sem):
    return pltpu.CompilerParams(dimension_semantics=sem, vmem_limit_bytes=VMEM_LIMIT)


def _sigmoid(x):
    return 1.0 / (1.0 + jnp.exp(-x))


def _rope_lanes(x, cos, sin_signed):
    n = x.shape[-1]
    lane = lax.broadcasted_iota(I32, (1, n), 1)
    first = (lane & 32) == 0
    ahead = pltpu.roll(x, n - 32, 1)
    behind = pltpu.roll(x, 32, 1)
    return x * cos + jnp.where(first, ahead, behind) * sin_signed


def _ada_kernel(c_ref, w_ref, b_ref, o_ref):
    c = c_ref[...]
    s = (c * _sigmoid(c)).astype(BF16)
    o_ref[...] = jnp.dot(s, w_ref[...].astype(BF16), preferred_element_type=F32) + b_ref[...]


def _ada(c, w, b):
    n = w.shape[1]
    return pl.pallas_call(
        _ada_kernel,
        grid=(n // D_MODEL,),
        in_specs=[pl.BlockSpec((BATCH, D_MODEL), lambda j: (0, 0)),
                  pl.BlockSpec((D_MODEL, D_MODEL), lambda j: (0, j)),
                  pl.BlockSpec((1, D_MODEL), lambda j: (0, j))],
        out_specs=pl.BlockSpec((BATCH, D_MODEL), lambda j: (0, j)),
        out_shape=jax.ShapeDtypeStruct((BATCH, n), F32),
        compiler_params=_params(("arbitrary",)),
        name="ada_mod",
    )(c, w, b.reshape(1, n))


def _inproj_kernel(x_ref, g_ref, sc_ref, sh_ref, w_ref, o_ref, h_ref):
    @pl.when(pl.program_id(1) == 0)
    def _():
        x = x_ref[...]
        ms = jnp.mean(x * x, axis=-1, keepdims=True)
        y = x * lax.rsqrt(ms + EPS) * g_ref[...]
        h_ref[...] = (y * (1.0 + sc_ref[0]) + sh_ref[0]).astype(BF16)

    o_ref[...] = lax.dot_general(h_ref[...], w_ref[...], NT_DIMS, preferred_element_type=F32)


def _inproj(x, g, sc, sh, w_t_bf16):
    tiles_per_batch = SEQ // PROJ_TM
    return pl.pallas_call(
        _inproj_kernel,
        grid=(N_TOK // PROJ_TM, N_PROJ // PROJ_TN),
        in_specs=[pl.BlockSpec((PROJ_TM, D_MODEL), lambda i, j: (i, 0)),
                  pl.BlockSpec((1, D_MODEL), lambda i, j: (0, 0)),
                  pl.BlockSpec((1, 1, D_MODEL), lambda i, j: (i // tiles_per_batch, 0, 0)),
                  pl.BlockSpec((1, 1, D_MODEL), lambda i, j: (i // tiles_per_batch, 0, 0)),
                  pl.BlockSpec((PROJ_TN, D_MODEL), lambda i, j: (j, 0))],
        out_specs=pl.BlockSpec((PROJ_TM, PROJ_TN), lambda i, j: (i, j)),
        out_shape=jax.ShapeDtypeStruct((N_TOK, N_PROJ), F32),
        scratch_shapes=[pltpu.VMEM((PROJ_TM, D_MODEL), BF16)],
        compiler_params=_params(("arbitrary", "arbitrary")),
        name="in_proj",
    )(x, g.reshape(1, D_MODEL), sc, sh, w_t_bf16)


DSA_QB = 128
DSA_KC = 256
DESCENT_GROUP = 4


def _dsa_kernel(ckv_ref, ikiw_ref, q_ref, iq_ref, cos_a, sin_a, cos_b, sin_b, gkv_ref, wukv_ref,
                o_ref, k_lo, k_hi, v_dup, ik_lo, ik_hi, keys_s, selt_s, acc_s, q_s):
    qi = pl.program_id(1)
    lane = lax.broadcasted_iota(I32, (1, LANES), 1)
    lo_half = lane < 64
    n_kc = SEQ // DSA_KC

    @pl.when(qi == 0)
    def _prologue():
        def chunk(c, carry):
            r0 = pl.multiple_of(c * DSA_KC, DSA_KC)
            rows = pl.ds(r0, DSA_KC)
            ckv = ckv_ref[0, rows, :]
            ms = jnp.mean(ckv * ckv, axis=-1, keepdims=True)
            cn = (ckv * lax.rsqrt(ms + EPS) * gkv_ref[...]).astype(BF16)
            kv = jnp.dot(cn, wukv_ref[...], preferred_element_type=F32)
            cb, sb = cos_b[0, rows, :], sin_b[0, rows, :]
            kv = _rope_lanes(kv, cb, sb)
            swapped = pltpu.roll(kv, 64, 1)
            k_lo[rows, :] = jnp.where(lo_half, kv, 0.0).astype(BF16)
            k_hi[rows, :] = jnp.where(lo_half, 0.0, swapped).astype(BF16)
            v_dup[rows, :] = jnp.where(lo_half, swapped, kv).astype(BF16)
            ik = jnp.where(lo_half, _rope_lanes(ikiw_ref[0, rows, :], cb, sb), 0.0)
            ik_lo[rows, :] = ik.astype(BF16)
            ik_hi[rows, :] = pltpu.roll(ik, 64, 1).astype(BF16)
            return carry
        lax.fori_loop(0, n_kc, chunk, 0)

    q0 = pl.multiple_of(qi * DSA_QB, DSA_QB)
    qrows = pl.ds(q0, DSA_QB)
    cq, sq = cos_a[0, qrows, :], sin_a[0, qrows, :]

    iq = iq_ref[0]
    iq_heads = []
    for p in range(IDX_HEADS // 2):
        pair = _rope_lanes(iq[:, p * LANES:(p + 1) * LANES], cq, sq)
        iq_heads.append(jnp.where(lo_half, pair, 0.0).astype(BF16))
        iq_heads.append(jnp.where(lo_half, 0.0, pair).astype(BF16))
    iw_t = ikiw_ref[0, qrows, :].T * (IDX_HEADS ** -0.5)
    q_chunk = (q0 + lane) >> 6
    n_live = lax.shift_right_logical(qi + 2, 1)

    def chunk_rows(c):
        return pl.ds(pl.multiple_of(c * DSA_KC, DSA_KC), DSA_KC)

    def score_chunk(c, carry):
        rows = chunk_rows(c)
        k_chunk = (c * DSA_KC + lax.broadcasted_iota(I32, (DSA_KC, 1), 0)) >> 6
        score = jnp.zeros((DSA_KC, DSA_QB), F32)
        for h in range(IDX_HEADS):
            ik = ik_lo if h % 2 == 0 else ik_hi
            logit = lax.dot_general(ik[rows, :], iq_heads[h], NT_DIMS,
                                    preferred_element_type=F32) * (IDX_DIM ** -0.5)
            score = score + iw_t[64 + h:65 + h, :] * jnp.maximum(logit, 0.0)
        score = jnp.where(score == 0.0, 0.0, score)
        score = jnp.where(k_chunk <= q_chunk, score, -jnp.inf)
        bits = lax.bitcast_convert_type(score, I32)
        keys_s[rows, :] = bits ^ ((bits >> 31) & 0x7FFFFFFF)
        return carry
    lax.fori_loop(0, n_live, score_chunk, 0)

    def count(preds):
        def chunk(c, accs):
            key = keys_s[chunk_rows(c), :]
            return tuple(acc + jnp.sum(pred(key).astype(I32).reshape(4, DSA_KC // 4, DSA_QB), axis=0)
                         for acc, pred in zip(accs, preds))
        zero = jnp.zeros((DSA_KC // 4, DSA_QB), I32)
        accs = lax.fori_loop(0, n_live, chunk, (zero,) * len(preds))
        return [jnp.sum(acc, axis=0, keepdims=True) for acc in accs]

    def descend(state):
        group, prefix, n_at = state
        for j in range(DESCENT_GROUP):
            cand = prefix | lax.shift_left(jnp.int32(1), 31 - (group * DESCENT_GROUP + j))
            bound = cand ^ INT_MIN
            n_ge = count([lambda k: k >= bound])[0]
            prefix = jnp.where(n_ge >= TOPK, cand, prefix)
            n_at = jnp.where(n_ge >= TOPK, n_ge, n_at)
        return group + 1, prefix, n_at

    def undecided(state):
        group, _, n_at = state
        return (group < 32 // DESCENT_GROUP) & (jnp.max(jnp.abs(n_at - TOPK)) > 0)

    live_keys = jnp.zeros((1, DSA_QB), I32) + n_live * DSA_KC
    _, prefix, _ = lax.while_loop(undecided, descend,
                                  (jnp.int32(0), jnp.zeros((1, DSA_QB), I32), live_keys))
    thr = prefix ^ INT_MIN

    need = (TOPK - count([lambda k: k > thr])[0]).astype(F32)
    tri = (lax.broadcasted_iota(I32, (DSA_KC, DSA_KC), 0)
           >= lax.broadcasted_iota(I32, (DSA_KC, DSA_KC), 1)).astype(BF16)

    def select_chunk(c, seen):
        key = keys_s[chunk_rows(c), :]
        eq = key == thr
        rank = jnp.dot(tri, eq.astype(BF16), preferred_element_type=F32) + seen
        sel = ((key > thr) | (eq & (rank <= need))) & (key > NEG_INF_KEY)
        selt_s[c] = sel.astype(F32).T
        return rank[DSA_KC - 1:DSA_KC, :]
    lax.fori_loop(0, n_live, select_chunk, jnp.zeros((1, DSA_QB), F32))

    q = q_ref[0]
    n_pairs = A_HEADS // 2
    for p in range(n_pairs):
        pair = _rope_lanes(q[:, p * LANES:(p + 1) * LANES], cq, sq) * (HEAD_DIM ** -0.5)
        q_s[p * DSA_QB:(p + 1) * DSA_QB, :] = jnp.where(lo_half, pair, 0.0).astype(BF16)
        q_s[(n_pairs + p) * DSA_QB:(n_pairs + p + 1) * DSA_QB, :] = jnp.where(lo_half, 0.0, pair).astype(BF16)
    masked = -1e30
    half = n_pairs * DSA_QB

    def attend(c, carry):
        m, l = carry
        rows = chunk_rows(c)
        bias = jnp.where(selt_s[c] > 0.0, 0.0, masked)
        s = jnp.concatenate(
            [lax.dot_general(q_s[:half, :], k_lo[rows, :], NT_DIMS, preferred_element_type=F32),
             lax.dot_general(q_s[half:, :], k_hi[rows, :], NT_DIMS, preferred_element_type=F32)], axis=0)
        s = s.reshape(A_HEADS, DSA_QB, DSA_KC) + bias[None]
        m_new = jnp.maximum(m, jnp.max(s, axis=-1, keepdims=True))
        pe = jnp.exp(s - m_new)
        alpha = jnp.exp(m - m_new)
        pv = jnp.dot(pe.reshape(A_HEADS * DSA_QB, DSA_KC).astype(BF16), v_dup[rows, :],
                     preferred_element_type=F32)
        acc_s[...] = alpha * acc_s[...] + pv.reshape(A_HEADS, DSA_QB, LANES)
        return m_new, alpha * l + jnp.sum(pe, axis=-1, keepdims=True)

    acc_s[...] = jnp.zeros_like(acc_s)
    _, denom = lax.fori_loop(0, n_live, attend, (jnp.full((A_HEADS, DSA_QB, 1), masked, F32),
                                                 jnp.zeros((A_HEADS, DSA_QB, 1), F32)))
    for p in range(n_pairs):
        even = acc_s[p] / denom[p]
        odd = acc_s[n_pairs + p] / denom[n_pairs + p]
        o_ref[0, :, p * LANES:(p + 1) * LANES] = jnp.where(lo_half, even, odd)


def _dsa(z3, cos_a, sin_a, cos_b, sin_b, g_kv, w_ukv_bf16):
    full = lambda width, col: pl.BlockSpec((1, SEQ, width), lambda b, i: (b, 0, col // width))
    blk = lambda width, col: pl.BlockSpec((1, DSA_QB, width), lambda b, i: (b, i, col // width))
    tab = pl.BlockSpec((1, SEQ, LANES), lambda b, i: (b, 0, 0))
    return pl.pallas_call(
        _dsa_kernel,
        grid=(BATCH, SEQ // DSA_QB),
        in_specs=[full(A_KV_LORA, COL_CKV), full(LANES, COL_IKIW),
                  blk(A_HEADS * HEAD_DIM, COL_QA), blk(IDX_HEADS * IDX_DIM, COL_IQ),
                  tab, tab, tab, tab,
                  pl.BlockSpec((1, A_KV_LORA), lambda b, i: (0, 0)),
                  pl.BlockSpec((A_KV_LORA, LANES), lambda b, i: (0, 0))],
        out_specs=pl.BlockSpec((1, DSA_QB, A_HEADS * HEAD_DIM), lambda b, i: (b, i, 0)),
        out_shape=jax.ShapeDtypeStruct((BATCH, SEQ, A_HEADS * HEAD_DIM), F32),
        scratch_shapes=[pltpu.VMEM((SEQ, LANES), BF16)] * 5
                       + [pltpu.VMEM((SEQ, DSA_QB), I32),
                          pltpu.VMEM((SEQ // DSA_KC, DSA_QB, DSA_KC), F32),
                          pltpu.VMEM((A_HEADS, DSA_QB, LANES), F32),
                          pltpu.VMEM((A_HEADS * DSA_QB, LANES), BF16)],
        compiler_params=_params(("arbitrary", "arbitrary")),
        name="dsa_mixer",
    )(z3, z3, z3, z3, cos_a, sin_a, cos_b, sin_b, g_kv.reshape(1, A_KV_LORA), w_ukv_bf16)


SB_BLK = 128
SB_EXP_UNDERFLOW = -105.0


def _sb_kernel(q_ref, k_ref, v_ref, o_ref, q_s, tail_s, acc_s):
    qi = pl.program_id(1)
    n_pairs = B_HEADS // 2
    rows_all = B_HEADS * SB_BLK
    lane = lax.broadcasted_iota(I32, (1, LANES), 1)
    lo_half = lane < 64
    causal = (lax.broadcasted_iota(I32, (rows_all, SB_BLK), 1)
              < (lax.broadcasted_iota(I32, (rows_all, SB_BLK), 0) & (SB_BLK - 1)))
    r2 = lax.broadcasted_iota(I32, (2 * SB_BLK, 2 * SB_BLK), 0) & (SB_BLK - 1)
    c2 = lax.broadcasted_iota(I32, (2 * SB_BLK, 2 * SB_BLK), 1)
    suffix_mat = ((c2 >= SB_BLK) | (r2 > c2)).astype(BF16)

    for p in range(n_pairs):
        q2 = q_ref[0, :, p * LANES:(p + 1) * LANES] * (HEAD_DIM ** -0.5)
        q_s[2 * p * SB_BLK:(2 * p + 1) * SB_BLK, :] = jnp.where(lo_half, q2, 0.0).astype(BF16)
        q_s[(2 * p + 1) * SB_BLK:(2 * p + 2) * SB_BLK, :] = jnp.where(lo_half, 0.0, q2).astype(BF16)

    def key_block(kb, diagonal):
        r0 = pl.multiple_of(kb * SB_BLK, SB_BLK)
        pair_rows = lambda p: slice(2 * p * SB_BLK, (2 * p + 2) * SB_BLK)
        z = jnp.concatenate(
            [lax.dot_general(q_s[pair_rows(p), :], k_ref[0, pl.ds(r0, SB_BLK), p * LANES:(p + 1) * LANES].astype(BF16),
                             NT_DIMS, preferred_element_type=F32) for p in range(n_pairs)], axis=0)
        log_beta = jnp.minimum(z, 0.0) - jnp.log1p(jnp.exp(-jnp.abs(z)))
        log_not = log_beta - z
        if diagonal:
            log_not = jnp.where(causal, log_not, 0.0)
        hi = log_not.astype(BF16)
        lo = (log_not - hi.astype(F32)).astype(BF16)
        t = jnp.dot(jnp.concatenate([hi, lo], axis=1), suffix_mat, preferred_element_type=F32)
        if diagonal:
            a = jnp.where(causal, jnp.exp(log_beta + t[:, :SB_BLK]), 0.0)
            tail = t[:, SB_BLK:]
        else:
            a = jnp.exp(log_beta + t[:, :SB_BLK] + tail_s[...])
            tail = tail_s[...] + t[:, SB_BLK:]
        tail_s[...] = tail
        a = a.astype(BF16)
        for p in range(n_pairs):
            v2 = v_ref[0, pl.ds(r0, SB_BLK), p * LANES:(p + 1) * LANES].astype(BF16)
            pv = jnp.dot(a[pair_rows(p), :], v2, preferred_element_type=F32)
            if diagonal:
                acc_s[pair_rows(p), :] = pv
            else:
                acc_s[pair_rows(p), :] += pv
        return jnp.max(tail)

    first = key_block(qi, True)

    def more(state):
        jj, worst = state
        return (jj <= qi) & (worst > SB_EXP_UNDERFLOW)

    def step(state):
        jj, _ = state
        return jj + 1, key_block(qi - jj, False)

    lax.while_loop(more, step, (jnp.int32(1), first))
    for p in range(n_pairs):
        even = acc_s[2 * p * SB_BLK:(2 * p + 1) * SB_BLK, :]
        odd = acc_s[(2 * p + 1) * SB_BLK:(2 * p + 2) * SB_BLK, :]
        o_ref[0, :, p * LANES:(p + 1) * LANES] = jnp.where(lo_half, even, odd)


def _sb(z3):
    width = B_HEADS * HEAD_DIM
    return pl.pallas_call(
        _sb_kernel,
        grid=(BATCH, SEQ // SB_BLK),
        in_specs=[pl.BlockSpec((1, SB_BLK, width), lambda b, i: (b, i, COL_QB // width)),
                  pl.BlockSpec((1, SEQ, width), lambda b, i: (b, 0, COL_KB // width)),
                  pl.BlockSpec((1, SEQ, width), lambda b, i: (b, 0, COL_VB // width))],
        out_specs=pl.BlockSpec((1, SB_BLK, width), lambda b, i: (b, i, 0)),
        out_shape=jax.ShapeDtypeStruct((BATCH, SEQ, width), F32),
        scratch_shapes=[pltpu.VMEM((B_HEADS * SB_BLK, LANES), BF16),
                        pltpu.VMEM((B_HEADS * SB_BLK, SB_BLK), F32),
                        pltpu.VMEM((B_HEADS * SB_BLK, SB_BLK), F32)],
        compiler_params=_params(("arbitrary", "arbitrary")),
        name="stickbreak_mixer",
    )(z3, z3, z3)


RET_BLK = 256


def _ret_kernel(dec_ref, q_ref, k_ref, v_ref, gate_ref, cos_a, sin_a, qs_ref, ks_ref, e_diag, gret_ref,
                o_ref, state):
    lane = lax.broadcasted_iota(I32, (1, LANES), 1)
    lo_half = lane < 64

    @pl.when(pl.program_id(1) == 0)
    def _():
        state[...] = jnp.zeros_like(state)

    cq, sq = cos_a[0], sin_a[0]
    for p in range(C_HEADS // 2):
        cols = slice(p * LANES, (p + 1) * LANES)
        q2 = _rope_lanes(q_ref[0, :, cols], cq, sq)
        k2 = _rope_lanes(k_ref[0, :, cols], cq, sq) * (HEAD_DIM ** -0.5)
        q_hat = q2 * qs_ref[:, cols]
        k_hat = k2 * ks_ref[:, cols]
        for e in range(2):
            h = 2 * p + e
            pick = (lambda x: jnp.where(lo_half, x, 0.0)) if e == 0 else (lambda x: jnp.where(lo_half, 0.0, x))
            vcols = slice(h * C_V_DIM, (h + 1) * C_V_DIM)
            v2 = v_ref[0, :, vcols].astype(BF16)
            s = lax.dot_general(pick(q2).astype(BF16), k2.astype(BF16), NT_DIMS,
                                preferred_element_type=F32) * e_diag[h]
            o = jnp.dot(s.astype(BF16), v2, preferred_element_type=F32)
            o = o + jnp.dot(pick(q_hat).astype(BF16), state[h].astype(BF16), preferred_element_type=F32)
            kv = jnp.dot(pick(k_hat).T.astype(BF16), v2, preferred_element_type=F32)
            state[h] = dec_ref[h] * state[h] + kv
            mu = jnp.mean(o, axis=-1, keepdims=True)
            d = o - mu
            var = jnp.mean(d * d, axis=-1, keepdims=True)
            y = d * lax.rsqrt(var + EPS) * gret_ref[:, vcols]
            g = gate_ref[0, :, vcols]
            o_ref[0, :, vcols] = g * _sigmoid(g) * y


def _ret(z3, cos_a, sin_a, dec, q_scale, k_scale, e_diag, g_ret):
    qk_w = C_HEADS * HEAD_DIM
    v_w = C_HEADS * C_V_DIM
    tab = pl.BlockSpec((1, RET_BLK, LANES), lambda b, i: (b, i, 0))
    const2 = lambda a: pl.BlockSpec(a.shape, lambda b, i: (0, 0))
    return pl.pallas_call(
        _ret_kernel,
        grid=(BATCH, SEQ // RET_BLK),
        in_specs=[pl.BlockSpec(memory_space=pltpu.SMEM),
                  pl.BlockSpec((1, RET_BLK, qk_w), lambda b, i: (b, i, COL_QC // qk_w)),
                  pl.BlockSpec((1, RET_BLK, qk_w), lambda b, i: (b, i, COL_KC // qk_w)),
                  pl.BlockSpec((1, RET_BLK, v_w), lambda b, i: (b, i, COL_VC // v_w)),
                  pl.BlockSpec((1, RET_BLK, v_w), lambda b, i: (b, i, COL_GC // v_w)),
                  tab, tab, const2(q_scale), const2(k_scale),
                  pl.BlockSpec(e_diag.shape, lambda b, i: (0, 0, 0)),
                  pl.BlockSpec((1, v_w), lambda b, i: (0, 0))],
        out_specs=pl.BlockSpec((1, RET_BLK, v_w), lambda b, i: (b, i, 0)),
        out_shape=jax.ShapeDtypeStruct((BATCH, SEQ, v_w), F32),
        scratch_shapes=[pltpu.VMEM((C_HEADS, LANES, C_V_DIM), F32)],
        compiler_params=_params(("arbitrary", "arbitrary")),
        name="retention_mixer",
    )(dec, z3, z3, z3, z3, cos_a, sin_a, q_scale, k_scale, e_diag, g_ret.reshape(1, v_w))


MERGE_TM = 256


def _rms(y, g):
    ms = jnp.mean(y * y, axis=-1, keepdims=True)
    return y * lax.rsqrt(ms + EPS) * g


def _route(logits_t, bias_col):
    n_tok = logits_t.shape[1]
    per_group = N_EXPERTS // N_GROUPS
    scores = _sigmoid(logits_t)
    sel = scores + bias_col
    sel3 = sel.reshape(N_GROUPS, per_group, n_tok)
    member = lax.broadcasted_iota(I32, (N_GROUPS, per_group, n_tok), 1)
    m1 = jnp.max(sel3, axis=1, keepdims=True)
    first = jnp.min(jnp.where(sel3 == m1, member, per_group), axis=1, keepdims=True)
    m2 = jnp.max(jnp.where(member == first, -jnp.inf, sel3), axis=1, keepdims=True)
    grp = (m1 + m2).reshape(N_GROUPS, n_tok)
    gid = lax.broadcasted_iota(I32, (N_GROUPS, n_tok), 0)
    rank = jnp.zeros((N_GROUPS, n_tok), I32)
    for g in range(N_GROUPS):
        other = grp[g:g + 1, :]
        rank = rank + ((other > grp) | ((other == grp) & (g < gid))).astype(I32)
    keep = jnp.where(rank < TOPK_GROUPS, 1.0, 0.0).reshape(N_GROUPS, 1, n_tok)
    live = jnp.where(keep > 0.0, sel3, -jnp.inf).reshape(N_EXPERTS, n_tok)
    eid = lax.broadcasted_iota(I32, (N_EXPERTS, n_tok), 0)
    chosen = jnp.zeros((N_EXPERTS, n_tok), jnp.bool_)
    for _ in range(TOP_K):
        m = jnp.max(live, axis=0, keepdims=True)
        pick = eid == jnp.min(jnp.where(live == m, eid, N_EXPERTS), axis=0, keepdims=True)
        chosen = chosen | pick
        live = jnp.where(pick, -jnp.inf, live)
    w = jnp.where(chosen, scores, 0.0)
    return w / jnp.sum(w, axis=0, keepdims=True) * ROUTED_SCALE


def _merge_kernel(ya_ref, yb_ref, yc_ref, g_ref, x_ref, gt_ref, gpost_ref, gpre_ref, sc_ref, sh_ref,
                  woa, wob, woc, wout, wr_t, rb_ref, xo_ref, h_ref, gate_ref):
    ya = jnp.dot(ya_ref[...].astype(BF16), woa[...], preferred_element_type=F32)
    yb = jnp.dot(yb_ref[...].astype(BF16), wob[...], preferred_element_type=F32)
    yc = jnp.dot(yc_ref[...].astype(BF16), woc[...], preferred_element_type=F32)
    merged = (_sigmoid(g_ref[:, 0:D_MODEL]) * ya
              + _sigmoid(g_ref[:, D_MODEL:2 * D_MODEL]) * yb
              + _sigmoid(g_ref[:, 2 * D_MODEL:3 * D_MODEL]) * yc)
    y = jnp.dot(merged.astype(BF16), wout[...], preferred_element_type=F32)
    x = x_ref[...] + gt_ref[0] * _rms(y, gpost_ref[...])
    xo_ref[...] = x
    h = (_rms(x, gpre_ref[...]) * (1.0 + sc_ref[0]) + sh_ref[0]).astype(BF16)
    h_ref[...] = h
    logits_t = lax.dot_general(wr_t[...], h, NT_DIMS, preferred_element_type=F32)
    gates_t = _route(logits_t, rb_ref[...])
    gate_ref[...] = jnp.concatenate([gates_t, jnp.zeros_like(gates_t)], axis=0).T


def _merge(ya, yb, yc, z, x, gt1, g_post, g_pre, sc2, sh2, woa, wob, woc, wout, wr_t, rbias):
    tpb = SEQ // MERGE_TM
    row = lambda w: pl.BlockSpec((MERGE_TM, w), lambda i: (i, 0))
    vec = pl.BlockSpec((1, D_MODEL), lambda i: (0, 0))
    mod = pl.BlockSpec((1, 1, D_MODEL), lambda i: (i // tpb, 0, 0))
    whole = lambda a: pl.BlockSpec(a.shape, lambda i: (0, 0))
    return pl.pallas_call(
        _merge_kernel,
        grid=(N_TOK // MERGE_TM,),
        in_specs=[row(512), row(512), row(1024), pl.BlockSpec((MERGE_TM, 3 * D_MODEL), lambda i: (i, 0)),
                  row(D_MODEL), mod, vec, vec, mod, mod,
                  whole(woa), whole(wob), whole(woc), whole(wout), whole(wr_t), whole(rbias)],
        out_specs=[row(D_MODEL), row(D_MODEL), row(LANES)],
        out_shape=[jax.ShapeDtypeStruct((N_TOK, D_MODEL), F32),
                   jax.ShapeDtypeStruct((N_TOK, D_MODEL), BF16),
                   jax.ShapeDtypeStruct((N_TOK, LANES), F32)],
        compiler_params=_params(("arbitrary",)),
        name="merge_route",
    )(ya, yb, yc, z, x, gt1, g_post.reshape(1, D_MODEL), g_pre.reshape(1, D_MODEL), sc2, sh2,
      woa, wob, woc, wout, wr_t, rbias)


MOE_TM = 1024


def _moe_kernel(h_ref, gate_ref, x_ref, gt_ref, gpost_ref, w1, w3, w2, ws1, ws3, ws2, o_ref, acc):
    e = pl.program_id(1)
    h = h_ref[...]

    @pl.when(e == 0)
    def _():
        acc[...] = jnp.zeros_like(acc)

    g = gate_ref[...]
    g_hi = g.astype(BF16)
    g_lo = (g - g_hi.astype(F32)).astype(BF16)
    onehot = (lax.broadcasted_iota(I32, (LANES, LANES), 0) == e).astype(BF16)
    gcol = (jnp.dot(g_hi, onehot, preferred_element_type=F32)
            + jnp.dot(g_lo, onehot, preferred_element_type=F32))
    a = jnp.dot(h, w1[0], preferred_element_type=F32)
    hid = a * _sigmoid(a) * jnp.dot(h, w3[0], preferred_element_type=F32)
    y = jnp.dot(hid.astype(BF16), w2[0], preferred_element_type=F32)
    acc[...] += y * jnp.concatenate([gcol] * (D_MODEL // LANES), axis=1)

    @pl.when(e == N_EXPERTS - 1)
    def _():
        s = jnp.dot(h, ws1[...], preferred_element_type=F32)
        shid = s * _sigmoid(s) * jnp.dot(h, ws3[...], preferred_element_type=F32)
        y_all = acc[...] + jnp.dot(shid.astype(BF16), ws2[...], preferred_element_type=F32)
        o_ref[...] = x_ref[...] + gt_ref[0] * _rms(y_all, gpost_ref[...])


def _moe(h, gates, x, gt2, g_post, w1, w3, w2, ws1, ws3, ws2):
    tpb = SEQ // MOE_TM
    whole = lambda a: pl.BlockSpec(a.shape, lambda i, e: (0, 0))
    return pl.pallas_call(
        _moe_kernel,
        grid=(N_TOK // MOE_TM, N_EXPERTS),
        in_specs=[pl.BlockSpec((MOE_TM, D_MODEL), lambda i, e: (i, 0)),
                  pl.BlockSpec((MOE_TM, LANES), lambda i, e: (i, 0)),
                  pl.BlockSpec((MOE_TM, D_MODEL), lambda i, e: (i, 0)),
                  pl.BlockSpec((1, 1, D_MODEL), lambda i, e: (i // tpb, 0, 0)),
                  pl.BlockSpec((1, D_MODEL), lambda i, e: (0, 0)),
                  pl.BlockSpec((1, D_MODEL, EXPERT_DIM), lambda i, e: (e, 0, 0)),
                  pl.BlockSpec((1, D_MODEL, EXPERT_DIM), lambda i, e: (e, 0, 0)),
                  pl.BlockSpec((1, EXPERT_DIM, D_MODEL), lambda i, e: (e, 0, 0)),
                  whole(ws1), whole(ws3), whole(ws2)],
        out_specs=pl.BlockSpec((MOE_TM, D_MODEL), lambda i, e: (i, 0)),
        out_shape=jax.ShapeDtypeStruct((N_TOK, D_MODEL), F32),
        scratch_shapes=[pltpu.VMEM((MOE_TM, D_MODEL), F32)],
        compiler_params=_params(("arbitrary", "arbitrary")),
        name="moe_ffn",
    )(h, gates, x, gt2, g_post.reshape(1, D_MODEL), w1, w3, w2, ws1, ws3, ws2)


_IN_SIZES = (512, 256, 256, 64, 4, 512, 512, 512, 512, 512, 1024, 1024, 3072)
_IN_DEST = (COL_QA, COL_CKV, COL_IQ, COL_IKIW, COL_IKIW + 64, COL_QB, COL_KB, COL_VB, COL_QC, COL_KC,
            COL_VC, COL_GC, COL_GATES)
_IN_SRC = tuple(int(v) for v in np.cumsum((0,) + _IN_SIZES[:-1]))
N_IN = sum(_IN_SIZES)
PACK_TK = 256


def _pack_kernel(w_ref, o_ref):
    for src, width, dst in zip(_IN_SRC, _IN_SIZES, _IN_DEST):
        o_ref[dst:dst + width, :] = w_ref[src:src + width, :].astype(BF16)
    o_ref[COL_IKIW + 68:COL_IKIW + LANES, :] = jnp.zeros((LANES - 68, PACK_TK), BF16)


def _reorder_w_in(w_t):
    return pl.pallas_call(
        _pack_kernel,
        grid=(D_MODEL // PACK_TK,),
        in_specs=[pl.BlockSpec((N_IN, PACK_TK), lambda i: (0, i))],
        out_specs=pl.BlockSpec((N_PROJ, PACK_TK), lambda i: (0, i)),
        out_shape=jax.ShapeDtypeStruct((N_PROJ, D_MODEL), BF16),
        compiler_params=_params(("arbitrary",)),
        name="pack_w_in",
    )(w_t)


def _rope_tables(positions):
    half = HEAD_DIM // 2
    inv_freq = ROPE_THETA ** (-jnp.arange(half, dtype=F32) / half)
    ang = positions.astype(F32)[:, :, None] * inv_freq
    cos, sin = jnp.cos(ang), jnp.sin(ang)
    cos64 = jnp.concatenate([cos, cos], axis=-1)
    sin64 = jnp.concatenate([-sin, sin], axis=-1)
    cos_a = jnp.concatenate([cos64, cos64], axis=-1)
    sin_a = jnp.concatenate([sin64, sin64], axis=-1)
    cos_b = jnp.concatenate([cos64, jnp.ones_like(cos64)], axis=-1)
    sin_b = jnp.concatenate([sin64, jnp.zeros_like(sin64)], axis=-1)
    return cos_a, sin_a, cos_b, sin_b


def _decay_tables():
    log_g = jnp.log(1.0 - 2.0 ** (-5.0 - jnp.arange(C_HEADS, dtype=F32)))
    pos = jnp.arange(RET_BLK, dtype=F32)
    diff = pos[:, None] - pos[None, :]
    chunk = jnp.arange(RET_BLK) // CHUNK
    visible = chunk[None, :] <= chunk[:, None]
    e_diag = jnp.where(visible, jnp.exp(log_g[:, None, None] * jnp.abs(diff)), 0.0)
    dec = jnp.exp(log_g * RET_BLK)
    per_lane = jnp.repeat(log_g, HEAD_DIM)[None, :]
    q_scale = jnp.exp(per_lane * pos[:, None])
    k_scale = jnp.exp(per_lane * (RET_BLK - pos)[:, None])
    return dec, q_scale, k_scale, e_diag


def kernel(x, c, positions, ada_w, ada_b, g_pre_mix, g_post_mix, g_pre_ffn, g_post_ffn, w_in, g_kv, w_uk, w_uv,
           g_ret, w_o_a, w_o_b, w_o_c, w_out, w_router, router_bias, w1, w3, w2, ws1, ws3, ws2):
    cos_a, sin_a, cos_b, sin_b = _rope_tables(positions)
    dec, q_scale, k_scale, e_diag = _decay_tables()
    xf = x.reshape(N_TOK, D_MODEL)
    for l in range(DEPTH):
        mod = _ada(c, ada_w[l], ada_b[l]).reshape(BATCH, 1, N_ADA * D_MODEL)
        sh1, sc1, gt1, sh2, sc2, gt2 = (mod[:, :, k * D_MODEL:(k + 1) * D_MODEL] for k in range(N_ADA))
        z = _inproj(xf, g_pre_mix[l], sc1, sh1, _reorder_w_in(w_in[l].T))
        z3 = z.reshape(BATCH, SEQ, N_PROJ)
        w_ukv = jnp.concatenate([w_uk[l], w_uv[l]], axis=1).astype(BF16)
        ya = _dsa(z3, cos_a, sin_a, cos_b, sin_b, g_kv[l], w_ukv).reshape(N_TOK, -1)
        yb = _sb(z3).reshape(N_TOK, -1)
        yc = _ret(z3, cos_a, sin_a, dec, q_scale, k_scale, e_diag, g_ret[l]).reshape(N_TOK, -1)
        xf, h, gates = _merge(ya, yb, yc, z, xf, gt1, g_post_mix[l], g_pre_ffn[l], sc2, sh2,
                              w_o_a[l].astype(BF16), w_o_b[l].astype(BF16), w_o_c[l].astype(BF16),
                              w_out[l].astype(BF16), w_router[l].T.astype(BF16),
                              router_bias[l].reshape(N_EXPERTS, 1))
        xf = _moe(h, gates, xf, gt2, g_post_ffn[l], w1[l].astype(BF16), w3[l].astype(BF16), w2[l].astype(BF16),
                  ws1[l].astype(BF16), ws3[l].astype(BF16), ws2[l].astype(BF16))
    return xf.reshape(BATCH, SEQ, D_MODEL)
```

```python
import functools
import math

import numpy as np
import jax
import jax.numpy as jnp
from jax import lax
from jax.experimental import pallas as pl
from jax.experimental.pallas import tpu as pltpu

F32 = jnp.float32
BF16 = jnp.bfloat16
I32 = jnp.int32

D_MODEL = 1024
BATCH = 8
SEQ = 2048
DEPTH = 2
N_TOK = BATCH * SEQ

CHUNK = 64
HEAD_DIM = 64
ROPE_THETA = 10000.0
EPS = 1e-6
A_HEADS = 8
A_KV_LORA = 256
IDX_HEADS = 4
IDX_DIM = 64
TOPK = 256
B_HEADS = 8
C_HEADS = 8
C_V_DIM = 128
N_EXPERTS = 64
TOP_K = 8
N_GROUPS = 8
TOPK_GROUPS = 4
EXPERT_DIM = 256
SHARED_DIM = 256
ROUTED_SCALE = 2.5
N_ADA = 6

LANES = 128
VMEM_LIMIT = 56 * 1024 * 1024

COL_GATES = 0
COL_VC = 3072
COL_GC = 4096
COL_QA = 5120
COL_QB = 5632
COL_KB = 6144
COL_VB = 6656
COL_QC = 7168
COL_KC = 7680
COL_CKV = 8192
COL_IQ = 8448
COL_IKIW = 8704
N_PROJ = 8832
PROJ_TN = 2944
PROJ_TM = 512

INT_MIN = -2147483648
NEG_INF_KEY = -2139095041

NT_DIMS = (((1,), (1,)), ((), ()))


def _params(sem):
    return pltpu.CompilerParams(dimension_semantics=sem, vmem_limit_bytes=VMEM_LIMIT)


def _sigmoid(x):
    return 1.0 / (1.0 + jnp.exp(-x))


def _rope_lanes(x, cos, sin_signed):
    n = x.shape[-1]
    lane = lax.broadcasted_iota(I32, (1, n), 1)
    first = (lane & 32) == 0
    ahead = pltpu.roll(x, n - 32, 1)
    behind = pltpu.roll(x, 32, 1)
    return x * cos + jnp.where(first, ahead, behind) * sin_signed


def _ada_kernel(c_ref, w_ref, b_ref, o_ref):
    c = c_ref[...]
    s = (c * _sigmoid(c)).astype(BF16)
    o_ref[...] = jnp.dot(s, w_ref[...].astype(BF16), preferred_element_type=F32) + b_ref[...]


def _ada(c, w, b):
    n = w.shape[1]
    return pl.pallas_call(
        _ada_kernel,
        grid=(n // D_MODEL,),
        in_specs=[pl.BlockSpec((BATCH, D_MODEL), lambda j: (0, 0)),
                  pl.BlockSpec((D_MODEL, D_MODEL), lambda j: (0, j)),
                  pl.BlockSpec((1, D_MODEL), lambda j: (0, j))],
        out_specs=pl.BlockSpec((BATCH, D_MODEL), lambda j: (0, j)),
        out_shape=jax.ShapeDtypeStruct((BATCH, n), F32),
        compiler_params=_params(("arbitrary",)),
        name="ada_mod",
    )(c, w, b.reshape(1, n))


def _inproj_kernel(x_ref, g_ref, sc_ref, sh_ref, w_ref, o_ref, h_ref):
    @pl.when(pl.program_id(1) == 0)
    def _():
        x = x_ref[...]
        ms = jnp.mean(x * x, axis=-1, keepdims=True)
        y = x * lax.rsqrt(ms + EPS) * g_ref[...]
        h_ref[...] = (y * (1.0 + sc_ref[0]) + sh_ref[0]).astype(BF16)

    o_ref[...] = lax.dot_general(h_ref[...], w_ref[...], NT_DIMS, preferred_element_type=F32)


def _inproj(x, g, sc, sh, w_t_bf16):
    tiles_per_batch = SEQ // PROJ_TM
    return pl.pallas_call(
        _inproj_kernel,
        grid=(N_TOK // PROJ_TM, N_PROJ // PROJ_TN),
        in_specs=[pl.BlockSpec((PROJ_TM, D_MODEL), lambda i, j: (i, 0)),
                  pl.BlockSpec((1, D_MODEL), lambda i, j: (0, 0)),
                  pl.BlockSpec((1, 1, D_MODEL), lambda i, j: (i // tiles_per_batch, 0, 0)),
                  pl.BlockSpec((1, 1, D_MODEL), lambda i, j: (i // tiles_per_batch, 0, 0)),
                  pl.BlockSpec((PROJ_TN, D_MODEL), lambda i, j: (j, 0))],
        out_specs=pl.BlockSpec((PROJ_TM, PROJ_TN), lambda i, j: (i, j)),
        out_shape=jax.ShapeDtypeStruct((N_TOK, N_PROJ), F32),
        scratch_shapes=[pltpu.VMEM((PROJ_TM, D_MODEL), BF16)],
        compiler_params=_params(("arbitrary", "arbitrary")),
        name="in_proj",
    )(x, g.reshape(1, D_MODEL), sc, sh, w_t_bf16)


DSA_QB = 128
DSA_KC = 256
DESCENT_GROUP = 4


def _dsa_kernel(ckv_ref, ikiw_ref, q_ref, iq_ref, cos_a, sin_a, cos_b, sin_b, gkv_ref, wukv_ref,
                o_ref, k_lo, k_hi, v_dup, ik_lo, ik_hi, keys_s, selt_s, acc_s, q_s):
    qi = pl.program_id(1)
    lane = lax.broadcasted_iota(I32, (1, LANES), 1)
    lo_half = lane < 64
    n_kc = SEQ // DSA_KC

    @pl.when(qi == 0)
    def _prologue():
        def chunk(c, carry):
            r0 = pl.multiple_of(c * DSA_KC, DSA_KC)
            rows = pl.ds(r0, DSA_KC)
            ckv = ckv_ref[0, rows, :]
            ms = jnp.mean(ckv * ckv, axis=-1, keepdims=True)
            cn = (ckv * lax.rsqrt(ms + EPS) * gkv_ref[...]).astype(BF16)
            kv = jnp.dot(cn, wukv_ref[...], preferred_element_type=F32)
            cb, sb = cos_b[0, rows, :], sin_b[0, rows, :]
            kv = _rope_lanes(kv, cb, sb)
            swapped = pltpu.roll(kv, 64, 1)
            k_lo[rows, :] = jnp.where(lo_half, kv, 0.0).astype(BF16)
            k_hi[rows, :] = jnp.where(lo_half, 0.0, swapped).astype(BF16)
            v_dup[rows, :] = jnp.where(lo_half, swapped, kv).astype(BF16)
            ik = jnp.where(lo_half, _rope_lanes(ikiw_ref[0, rows, :], cb, sb), 0.0)
            ik_lo[rows, :] = ik.astype(BF16)
            ik_hi[rows, :] = pltpu.roll(ik, 64, 1).astype(BF16)
            return carry
        lax.fori_loop(0, n_kc, chunk, 0)

    q0 = pl.multiple_of(qi * DSA_QB, DSA_QB)
    qrows = pl.ds(q0, DSA_QB)
    cq, sq = cos_a[0, qrows, :], sin_a[0, qrows, :]

    iq = iq_ref[0]
    iq_heads = []
    for p in range(IDX_HEADS // 2):
        pair = _rope_lanes(iq[:, p * LANES:(p + 1) * LANES], cq, sq)
        iq_heads.append(jnp.where(lo_half, pair, 0.0).astype(BF16))
        iq_heads.append(jnp.where(lo_half, 0.0, pair).astype(BF16))
    iw_t = ikiw_ref[0, qrows, :].T * (IDX_HEADS ** -0.5)
    q_chunk = (q0 + lane) >> 6
    n_live = lax.shift_right_logical(qi + 2, 1)

    def chunk_rows(c):
        return pl.ds(pl.multiple_of(c * DSA_KC, DSA_KC), DSA_KC)

    def score_chunk(c, carry):
        rows = chunk_rows(c)
        k_chunk = (c * DSA_KC + lax.broadcasted_iota(I32, (DSA_KC, 1), 0)) >> 6
        score = jnp.zeros((DSA_KC, DSA_QB), F32)
        for h in range(IDX_HEADS):
            ik = ik_lo if h % 2 == 0 else ik_hi
            logit = lax.dot_general(ik[rows, :], iq_heads[h], NT_DIMS,
                                    preferred_element_type=F32) * (IDX_DIM ** -0.5)
            score = score + iw_t[64 + h:65 + h, :] * jnp.maximum(logit, 0.0)
        score = jnp.where(score == 0.0, 0.0, score)
        score = jnp.where(k_chunk <= q_chunk, score, -jnp.inf)
        bits = lax.bitcast_convert_type(score, I32)
        keys_s[rows, :] = bits ^ ((bits >> 31) & 0x7FFFFFFF)
        return carry
    lax.fori_loop(0, n_live, score_chunk, 0)

    def count(preds):
        def chunk(c, accs):
            key = keys_s[chunk_rows(c), :]
            return tuple(acc + jnp.sum(pred(key).astype(I32).reshape(4, DSA_KC // 4, DSA_QB), axis=0)
                         for acc, pred in zip(accs, preds))
        zero = jnp.zeros((DSA_KC // 4, DSA_QB), I32)
        accs = lax.fori_loop(0, n_live, chunk, (zero,) * len(preds))
        return [jnp.sum(acc, axis=0, keepdims=True) for acc in accs]

    def descend(state):
        group, prefix, n_at = state
        for j in range(DESCENT_GROUP):
            cand = prefix | lax.shift_left(jnp.int32(1), 31 - (group * DESCENT_GROUP + j))
            bound = cand ^ INT_MIN
            n_ge = count([lambda k: k >= bound])[0]
            prefix = jnp.where(n_ge >= TOPK, cand, prefix)
            n_at = jnp.where(n_ge >= TOPK, n_ge, n_at)
        return group + 1, prefix, n_at

    def undecided(state):
        group, _, n_at = state
        return (group < 32 // DESCENT_GROUP) & (jnp.max(jnp.abs(n_at - TOPK)) > 0)

    live_keys = jnp.zeros((1, DSA_QB), I32) + n_live * DSA_KC
    _, prefix, _ = lax.while_loop(undecided, descend,
                                  (jnp.int32(0), jnp.zeros((1, DSA_QB), I32), live_keys))
    thr = prefix ^ INT_MIN

    need = (TOPK - count([lambda k: k > thr])[0]).astype(F32)
    tri = (lax.broadcasted_iota(I32, (DSA_KC, DSA_KC), 0)
           >= lax.broadcasted_iota(I32, (DSA_KC, DSA_KC), 1)).astype(BF16)

    def select_chunk(c, seen):
        key = keys_s[chunk_rows(c), :]
        eq = key == thr
        rank = jnp.dot(tri, eq.astype(BF16), preferred_element_type=F32) + seen
        sel = ((key > thr) | (eq & (rank <= need))) & (key > NEG_INF_KEY)
        selt_s[c] = sel.astype(F32).T
        return rank[DSA_KC - 1:DSA_KC, :]
    lax.fori_loop(0, n_live, select_chunk, jnp.zeros((1, DSA_QB), F32))

    q = q_ref[0]
    n_pairs = A_HEADS // 2
    for p in range(n_pairs):
        pair = _rope_lanes(q[:, p * LANES:(p + 1) * LANES], cq, sq) * (HEAD_DIM ** -0.5)
        q_s[p * DSA_QB:(p + 1) * DSA_QB, :] = jnp.where(lo_half, pair, 0.0).astype(BF16)
        q_s[(n_pairs + p) * DSA_QB:(n_pairs + p + 1) * DSA_QB, :] = jnp.where(lo_half, 0.0, pair).astype(BF16)
    masked = -1e30
    half = n_pairs * DSA_QB

    def attend(c, carry):
        m, l = carry
        rows = chunk_rows(c)
        bias = jnp.where(selt_s[c] > 0.0, 0.0, masked)
        s = jnp.concatenate(
            [lax.dot_general(q_s[:half, :], k_lo[rows, :], NT_DIMS, preferred_element_type=F32),
             lax.dot_general(q_s[half:, :], k_hi[rows, :], NT_DIMS, preferred_element_type=F32)], axis=0)
        s = s.reshape(A_HEADS, DSA_QB, DSA_KC) + bias[None]
        m_new = jnp.maximum(m, jnp.max(s, axis=-1, keepdims=True))
        pe = jnp.exp(s - m_new)
        alpha = jnp.exp(m - m_new)
        pv = jnp.dot(pe.reshape(A_HEADS * DSA_QB, DSA_KC).astype(BF16), v_dup[rows, :],
                     preferred_element_type=F32)
        acc_s[...] = alpha * acc_s[...] + pv.reshape(A_HEADS, DSA_QB, LANES)
        return m_new, alpha * l + jnp.sum(pe, axis=-1, keepdims=True)

    acc_s[...] = jnp.zeros_like(acc_s)
    _, denom = lax.fori_loop(0, n_live, attend, (jnp.full((A_HEADS, DSA_QB, 1), masked, F32),
                                                 jnp.zeros((A_HEADS, DSA_QB, 1), F32)))
    for p in range(n_pairs):
        even = acc_s[p] / denom[p]
        odd = acc_s[n_pairs + p] / denom[n_pairs + p]
        o_ref[0, :, p * LANES:(p + 1) * LANES] = jnp.where(lo_half, even, odd)


def _dsa(z3, cos_a, sin_a, cos_b, sin_b, g_kv, w_ukv_bf16):
    full = lambda width, col: pl.BlockSpec((1, SEQ, width), lambda b, i: (b, 0, col // width))
    blk = lambda width, col: pl.BlockSpec((1, DSA_QB, width), lambda b, i: (b, i, col // width))
    tab = pl.BlockSpec((1, SEQ, LANES), lambda b, i: (b, 0, 0))
    return pl.pallas_call(
        _dsa_kernel,
        grid=(BATCH, SEQ // DSA_QB),
        in_specs=[full(A_KV_LORA, COL_CKV), full(LANES, COL_IKIW),
                  blk(A_HEADS * HEAD_DIM, COL_QA), blk(IDX_HEADS * IDX_DIM, COL_IQ),
                  tab, tab, tab, tab,
                  pl.BlockSpec((1, A_KV_LORA), lambda b, i: (0, 0)),
                  pl.BlockSpec((A_KV_LORA, LANES), lambda b, i: (0, 0))],
        out_specs=pl.BlockSpec((1, DSA_QB, A_HEADS * HEAD_DIM), lambda b, i: (b, i, 0)),
        out_shape=jax.ShapeDtypeStruct((BATCH, SEQ, A_HEADS * HEAD_DIM), F32),
        scratch_shapes=[pltpu.VMEM((SEQ, LANES), BF16)] * 5
                       + [pltpu.VMEM((SEQ, DSA_QB), I32),
                          pltpu.VMEM((SEQ // DSA_KC, DSA_QB, DSA_KC), F32),
                          pltpu.VMEM((A_HEADS, DSA_QB, LANES), F32),
                          pltpu.VMEM((A_HEADS * DSA_QB, LANES), BF16)],
        compiler_params=_params(("arbitrary", "arbitrary")),
        name="dsa_mixer",
    )(z3, z3, z3, z3, cos_a, sin_a, cos_b, sin_b, g_kv.reshape(1, A_KV_LORA), w_ukv_bf16)


SB_BLK = 128
SB_EXP_UNDERFLOW = -105.0


def _sb_kernel(q_ref, k_ref, v_ref, o_ref, q_s, tail_s, acc_s):
    qi = pl.program_id(1)
    n_pairs = B_HEADS // 2
    rows_all = B_HEADS * SB_BLK
    lane = lax.broadcasted_iota(I32, (1, LANES), 1)
    lo_half = lane < 64
    causal = (lax.broadcasted_iota(I32, (rows_all, SB_BLK), 1)
              < (lax.broadcasted_iota(I32, (rows_all, SB_BLK), 0) & (SB_BLK - 1)))
    r2 = lax.broadcasted_iota(I32, (2 * SB_BLK, 2 * SB_BLK), 0) & (SB_BLK - 1)
    c2 = lax.broadcasted_iota(I32, (2 * SB_BLK, 2 * SB_BLK), 1)
    suffix_mat = ((c2 >= SB_BLK) | (r2 > c2)).astype(BF16)

    for p in range(n_pairs):
        q2 = q_ref[0, :, p * LANES:(p + 1) * LANES] * (HEAD_DIM ** -0.5)
        q_s[2 * p * SB_BLK:(2 * p + 1) * SB_BLK, :] = jnp.where(lo_half, q2, 0.0).astype(BF16)
        q_s[(2 * p + 1) * SB_BLK:(2 * p + 2) * SB_BLK, :] = jnp.where(lo_half, 0.0, q2).astype(BF16)

    def key_block(kb, diagonal):
        r0 = pl.multiple_of(kb * SB_BLK, SB_BLK)
        pair_rows = lambda p: slice(2 * p * SB_BLK, (2 * p + 2) * SB_BLK)
        z = jnp.concatenate(
            [lax.dot_general(q_s[pair_rows(p), :], k_ref[0, pl.ds(r0, SB_BLK), p * LANES:(p + 1) * LANES].astype(BF16),
                             NT_DIMS, preferred_element_type=F32) for p in range(n_pairs)], axis=0)
        log_beta = jnp.minimum(z, 0.0) - jnp.log1p(jnp.exp(-jnp.abs(z)))
        log_not = log_beta - z
        if diagonal:
            log_not = jnp.where(causal, log_not, 0.0)
        hi = log_not.astype(BF16)
        lo = (log_not - hi.astype(F32)).astype(BF16)
        t = jnp.dot(jnp.concatenate([hi, lo], axis=1), suffix_mat, preferred_element_type=F32)
        if diagonal:
            a = jnp.where(causal, jnp.exp(log_beta + t[:, :SB_BLK]), 0.0)
            tail = t[:, SB_BLK:]
        else:
            a = jnp.exp(log_beta + t[:, :SB_BLK] + tail_s[...])
            tail = tail_s[...] + t[:, SB_BLK:]
        tail_s[...] = tail
        a = a.astype(BF16)
        for p in range(n_pairs):
            v2 = v_ref[0, pl.ds(r0, SB_BLK), p * LANES:(p + 1) * LANES].astype(BF16)
            pv = jnp.dot(a[pair_rows(p), :], v2, preferred_element_type=F32)
            if diagonal:
                acc_s[pair_rows(p), :] = pv
            else:
                acc_s[pair_rows(p), :] += pv
        return jnp.max(tail)

    first = key_block(qi, True)

    def more(state):
        jj, worst = state
        return (jj <= qi) & (worst > SB_EXP_UNDERFLOW)

    def step(state):
        jj, _ = state
        return jj + 1, key_block(qi - jj, False)

    lax.while_loop(more, step, (jnp.int32(1), first))
    for p in range(n_pairs):
        even = acc_s[2 * p * SB_BLK:(2 * p + 1) * SB_BLK, :]
        odd = acc_s[(2 * p + 1) * SB_BLK:(2 * p + 2) * SB_BLK, :]
        o_ref[0, :, p * LANES:(p + 1) * LANES] = jnp.where(lo_half, even, odd)


def _sb(z3):
    width = B_HEADS * HEAD_DIM
    return pl.pallas_call(
        _sb_kernel,
        grid=(BATCH, SEQ // SB_BLK),
        in_specs=[pl.BlockSpec((1, SB_BLK, width), lambda b, i: (b, i, COL_QB // width)),
                  pl.BlockSpec((1, SEQ, width), lambda b, i: (b, 0, COL_KB // width)),
                  pl.BlockSpec((1, SEQ, width), lambda b, i: (b, 0, COL_VB // width))],
        out_specs=pl.BlockSpec((1, SB_BLK, width), lambda b, i: (b, i, 0)),
        out_shape=jax.ShapeDtypeStruct((BATCH, SEQ, width), F32),
        scratch_shapes=[pltpu.VMEM((B_HEADS * SB_BLK, LANES), BF16),
                        pltpu.VMEM((B_HEADS * SB_BLK, SB_BLK), F32),
                        pltpu.VMEM((B_HEADS * SB_BLK, SB_BLK), F32)],
        compiler_params=_params(("arbitrary", "arbitrary")),
        name="stickbreak_mixer",
    )(z3, z3, z3)


RET_BLK = 256


def _ret_kernel(dec_ref, q_ref, k_ref, v_ref, gate_ref, cos_a, sin_a, qs_ref, ks_ref, e_diag, gret_ref,
                o_ref, state):
    lane = lax.broadcasted_iota(I32, (1, LANES), 1)
    lo_half = lane < 64

    @pl.when(pl.program_id(1) == 0)
    def _():
        state[...] = jnp.zeros_like(state)

    cq, sq = cos_a[0], sin_a[0]
    for p in range(C_HEADS // 2):
        cols = slice(p * LANES, (p + 1) * LANES)
        q2 = _rope_lanes(q_ref[0, :, cols], cq, sq)
        k2 = _rope_lanes(k_ref[0, :, cols], cq, sq) * (HEAD_DIM ** -0.5)
        q_hat = q2 * qs_ref[:, cols]
        k_hat = k2 * ks_ref[:, cols]
        for e in range(2):
            h = 2 * p + e
            pick = (lambda x: jnp.where(lo_half, x, 0.0)) if e == 0 else (lambda x: jnp.where(lo_half, 0.0, x))
            vcols = slice(h * C_V_DIM, (h + 1) * C_V_DIM)
            v2 = v_ref[0, :, vcols].astype(BF16)
            s = lax.dot_general(pick(q2).astype(BF16), k2.astype(BF16), NT_DIMS,
                                preferred_element_type=F32) * e_diag[h]
            o = jnp.dot(s.astype(BF16), v2, preferred_element_type=F32)
            o = o + jnp.dot(pick(q_hat).astype(BF16), state[h].astype(BF16), preferred_element_type=F32)
            kv = jnp.dot(pick(k_hat).T.astype(BF16), v2, preferred_element_type=F32)
            state[h] = dec_ref[h] * state[h] + kv
            mu = jnp.mean(o, axis=-1, keepdims=True)
            d = o - mu
            var = jnp.mean(d * d, axis=-1, keepdims=True)
            y = d * lax.rsqrt(var + EPS) * gret_ref[:, vcols]
            g = gate_ref[0, :, vcols]
            o_ref[0, :, vcols] = g * _sigmoid(g) * y


def _ret(z3, cos_a, sin_a, dec, q_scale, k_scale, e_diag, g_ret):
    qk_w = C_HEADS * HEAD_DIM
    v_w = C_HEADS * C_V_DIM
    tab = pl.BlockSpec((1, RET_BLK, LANES), lambda b, i: (b, i, 0))
    const2 = lambda a: pl.BlockSpec(a.shape, lambda b, i: (0, 0))
    return pl.pallas_call(
        _ret_kernel,
        grid=(BATCH, SEQ // RET_BLK),
        in_specs=[pl.BlockSpec(memory_space=pltpu.SMEM),
                  pl.BlockSpec((1, RET_BLK, qk_w), lambda b, i: (b, i, COL_QC // qk_w)),
                  pl.BlockSpec((1, RET_BLK, qk_w), lambda b, i: (b, i, COL_KC // qk_w)),
                  pl.BlockSpec((1, RET_BLK, v_w), lambda b, i: (b, i, COL_VC // v_w)),
                  pl.BlockSpec((1, RET_BLK, v_w), lambda b, i: (b, i, COL_GC // v_w)),
                  tab, tab, const2(q_scale), const2(k_scale),
                  pl.BlockSpec(e_diag.shape, lambda b, i: (0, 0, 0)),
                  pl.BlockSpec((1, v_w), lambda b, i: (0, 0))],
        out_specs=pl.BlockSpec((1, RET_BLK, v_w), lambda b, i: (b, i, 0)),
        out_shape=jax.ShapeDtypeStruct((BATCH, SEQ, v_w), F32),
        scratch_shapes=[pltpu.VMEM((C_HEADS, LANES, C_V_DIM), F32)],
        compiler_params=_params(("arbitrary", "arbitrary")),
        name="retention_mixer",
    )(dec, z3, z3, z3, z3, cos_a, sin_a, q_scale, k_scale, e_diag, g_ret.reshape(1, v_w))


MERGE_TM = 256


def _rms(y, g):
    ms = jnp.mean(y * y, axis=-1, keepdims=True)
    return y * lax.rsqrt(ms + EPS) * g


def _route(logits_t, bias_col):
    n_tok = logits_t.shape[1]
    per_group = N_EXPERTS // N_GROUPS
    scores = _sigmoid(logits_t)
    sel = scores + bias_col
    sel3 = sel.reshape(N_GROUPS, per_group, n_tok)
    member = lax.broadcasted_iota(I32, (N_GROUPS, per_group, n_tok), 1)
    m1 = jnp.max(sel3, axis=1, keepdims=True)
    first = jnp.min(jnp.where(sel3 == m1, member, per_group), axis=1, keepdims=True)
    m2 = jnp.max(jnp.where(member == first, -jnp.inf, sel3), axis=1, keepdims=True)
    grp = (m1 + m2).reshape(N_GROUPS, n_tok)
    gid = lax.broadcasted_iota(I32, (N_GROUPS, n_tok), 0)
    rank = jnp.zeros((N_GROUPS, n_tok), I32)
    for g in range(N_GROUPS):
        other = grp[g:g + 1, :]
        rank = rank + ((other > grp) | ((other == grp) & (g < gid))).astype(I32)
    keep = jnp.where(rank < TOPK_GROUPS, 1.0, 0.0).reshape(N_GROUPS, 1, n_tok)
    live = jnp.where(keep > 0.0, sel3, -jnp.inf).reshape(N_EXPERTS, n_tok)
    eid = lax.broadcasted_iota(I32, (N_EXPERTS, n_tok), 0)
    chosen = jnp.zeros((N_EXPERTS, n_tok), jnp.bool_)
    for _ in range(TOP_K):
        m = jnp.max(live, axis=0, keepdims=True)
        pick = eid == jnp.min(jnp.where(live == m, eid, N_EXPERTS), axis=0, keepdims=True)
        chosen = chosen | pick
        live = jnp.where(pick, -jnp.inf, live)
    w = jnp.where(chosen, scores, 0.0)
    return w / jnp.sum(w, axis=0, keepdims=True) * ROUTED_SCALE, chosen


def _merge_kernel(ya_ref, yb_ref, yc_ref, g_ref, x_ref, gt_ref, gpost_ref, gpre_ref, sc_ref, sh_ref,
                  woa, wob, woc, wout, wr_t, rb_ref, xo_ref, h_ref, gate_ref, rank_ref):
    ya = jnp.dot(ya_ref[...].astype(BF16), woa[...], preferred_element_type=F32)
    yb = jnp.dot(yb_ref[...].astype(BF16), wob[...], preferred_element_type=F32)
    yc = jnp.dot(yc_ref[...].astype(BF16), woc[...], preferred_element_type=F32)
    merged = (_sigmoid(g_ref[:, 0:D_MODEL]) * ya
              + _sigmoid(g_ref[:, D_MODEL:2 * D_MODEL]) * yb
              + _sigmoid(g_ref[:, 2 * D_MODEL:3 * D_MODEL]) * yc)
    y = jnp.dot(merged.astype(BF16), wout[...], preferred_element_type=F32)
    x = x_ref[...] + gt_ref[0] * _rms(y, gpost_ref[...])
    xo_ref[...] = x
    h = (_rms(x, gpre_ref[...]) * (1.0 + sc_ref[0]) + sh_ref[0]).astype(BF16)
    h_ref[...] = h
    logits_t = lax.dot_general(wr_t[...], h, NT_DIMS, preferred_element_type=F32)
    gates_t, chosen = _route(logits_t, rb_ref[...])
    gate_ref[...] = gates_t
    earlier = (lax.broadcasted_iota(I32, (MERGE_TM, MERGE_TM), 0)
               < lax.broadcasted_iota(I32, (MERGE_TM, MERGE_TM), 1)).astype(BF16)
    before = jnp.dot(jnp.where(chosen, 1.0, 0.0).astype(BF16), earlier, preferred_element_type=F32)
    rank_ref[...] = jnp.where(chosen, before, -1.0)


def _merge(ya, yb, yc, z, x, gt1, g_post, g_pre, sc2, sh2, woa, wob, woc, wout, wr_t, rbias):
    tpb = SEQ // MERGE_TM
    row = lambda w: pl.BlockSpec((MERGE_TM, w), lambda i: (i, 0))
    vec = pl.BlockSpec((1, D_MODEL), lambda i: (0, 0))
    mod = pl.BlockSpec((1, 1, D_MODEL), lambda i: (i // tpb, 0, 0))
    whole = lambda a: pl.BlockSpec(a.shape, lambda i: (0, 0))
    return pl.pallas_call(
        _merge_kernel,
        grid=(N_TOK // MERGE_TM,),
        in_specs=[row(512), row(512), row(1024), pl.BlockSpec((MERGE_TM, 3 * D_MODEL), lambda i: (i, 0)),
                  row(D_MODEL), mod, vec, vec, mod, mod,
                  whole(woa), whole(wob), whole(woc), whole(wout), whole(wr_t), whole(rbias)],
        out_specs=[row(D_MODEL), row(D_MODEL),
                   pl.BlockSpec((N_EXPERTS, MERGE_TM), lambda i: (0, i)),
                   pl.BlockSpec((N_EXPERTS, MERGE_TM), lambda i: (0, i))],
        out_shape=[jax.ShapeDtypeStruct((N_TOK, D_MODEL), F32),
                   jax.ShapeDtypeStruct((N_TOK, D_MODEL), BF16),
                   jax.ShapeDtypeStruct((N_EXPERTS, N_TOK), F32),
                   jax.ShapeDtypeStruct((N_EXPERTS, N_TOK), F32)],
        compiler_params=_params(("arbitrary",)),
        name="merge_route",
    )(ya, yb, yc, z, x, gt1, g_post.reshape(1, D_MODEL), g_pre.reshape(1, D_MODEL), sc2, sh2,
      woa, wob, woc, wout, wr_t, rbias)


MOE_TM = 1024
MOE_SUB = MERGE_TM
MOE_EG = 4
MOE_CAP = 64
MOE_NSUB = MOE_TM // MOE_SUB
MOE_SLAB = MOE_EG * MOE_CAP


def _moe_kernel(h_ref, gate_ref, rank_ref, x_ref, gt_ref, gpost_ref, w1, w3, w2, ws1, ws3, ws2,
                o_ref, acc, xg_s, yhi_s, ylo_s):
    g_idx = pl.program_id(1)

    @pl.when(g_idx == 0)
    def _():
        acc[...] = jnp.zeros_like(acc)

    rank = rank_ref[0]
    gate = gate_ref[0]
    n_rounds = (jnp.max(rank).astype(I32) + MOE_CAP) // MOE_CAP
    slot = lax.broadcasted_iota(I32, (MOE_CAP, 1), 0).astype(F32)

    def one_round(r, carry):
        base = (r * MOE_CAP).astype(F32)
        picks = []
        for s in range(MOE_NSUB):
            toks = slice(s * MOE_SUB, (s + 1) * MOE_SUB)
            pick = jnp.concatenate(
                [jnp.where(rank[e:e + 1, toks] == slot + base, 1.0, 0.0) for e in range(MOE_EG)], axis=0)
            picks.append(pick)
            xg_s[s] = jnp.dot(pick.astype(BF16), h_ref[toks, :], preferred_element_type=F32).astype(BF16)
        for e in range(MOE_EG):
            rows = slice(e * MOE_CAP, (e + 1) * MOE_CAP)
            x_e = jnp.concatenate([xg_s[s, rows, :] for s in range(MOE_NSUB)], axis=0)
            a = jnp.dot(x_e, w1[e], preferred_element_type=F32)
            hid = a * _sigmoid(a) * jnp.dot(x_e, w3[e], preferred_element_type=F32)
            y = jnp.dot(hid.astype(BF16), w2[e], preferred_element_type=F32)
            for s in range(MOE_NSUB):
                toks = slice(s * MOE_SUB, (s + 1) * MOE_SUB)
                w_row = jnp.sum(picks[s][rows, :] * gate[e:e + 1, toks], axis=1, keepdims=True)
                y_w = y[s * MOE_CAP:(s + 1) * MOE_CAP, :] * w_row
                y_hi = y_w.astype(BF16)
                yhi_s[s, rows, :] = y_hi
                ylo_s[s, rows, :] = (y_w - y_hi.astype(F32)).astype(BF16)
        for s in range(MOE_NSUB):
            toks = slice(s * MOE_SUB, (s + 1) * MOE_SUB)
            back = picks[s].T.astype(BF16)
            acc[toks, :] += (jnp.dot(back, yhi_s[s], preferred_element_type=F32)
                             + jnp.dot(back, ylo_s[s], preferred_element_type=F32))
        return carry

    lax.fori_loop(0, n_rounds, one_round, 0)

    @pl.when(g_idx == N_EXPERTS // MOE_EG - 1)
    def _():
        h = h_ref[...]
        s = jnp.dot(h, ws1[...], preferred_element_type=F32)
        shid = s * _sigmoid(s) * jnp.dot(h, ws3[...], preferred_element_type=F32)
        y_all = acc[...] + jnp.dot(shid.astype(BF16), ws2[...], preferred_element_type=F32)
        o_ref[...] = x_ref[...] + gt_ref[0] * _rms(y_all, gpost_ref[...])


def _moe(h, gates_t, ranks_t, x, gt2, g_post, w1, w3, w2, ws1, ws3, ws2):
    tpb = SEQ // MOE_TM
    n_groups = N_EXPERTS // MOE_EG
    whole = lambda a: pl.BlockSpec(a.shape, lambda i, g: (0, 0))
    per_group = pl.BlockSpec((1, MOE_EG, MOE_TM), lambda i, g: (g, 0, i))
    return pl.pallas_call(
        _moe_kernel,
        grid=(N_TOK // MOE_TM, n_groups),
        in_specs=[pl.BlockSpec((MOE_TM, D_MODEL), lambda i, g: (i, 0)),
                  per_group, per_group,
                  pl.BlockSpec((MOE_TM, D_MODEL), lambda i, g: (i, 0)),
                  pl.BlockSpec((1, 1, D_MODEL), lambda i, g: (i // tpb, 0, 0)),
                  pl.BlockSpec((1, D_MODEL), lambda i, g: (0, 0)),
                  pl.BlockSpec((MOE_EG, D_MODEL, EXPERT_DIM), lambda i, g: (g, 0, 0)),
                  pl.BlockSpec((MOE_EG, D_MODEL, EXPERT_DIM), lambda i, g: (g, 0, 0)),
                  pl.BlockSpec((MOE_EG, EXPERT_DIM, D_MODEL), lambda i, g: (g, 0, 0)),
                  whole(ws1), whole(ws3), whole(ws2)],
        out_specs=pl.BlockSpec((MOE_TM, D_MODEL), lambda i, g: (i, 0)),
        out_shape=jax.ShapeDtypeStruct((N_TOK, D_MODEL), F32),
        scratch_shapes=[pltpu.VMEM((MOE_TM, D_MODEL), F32),
                        pltpu.VMEM((MOE_NSUB, MOE_SLAB, D_MODEL), BF16),
                        pltpu.VMEM((MOE_NSUB, MOE_SLAB, D_MODEL), BF16),
                        pltpu.VMEM((MOE_NSUB, MOE_SLAB, D_MODEL), BF16)],
        compiler_params=_params(("arbitrary", "arbitrary")),
        name="moe_ffn",
    )(h, gates_t.reshape(n_groups, MOE_EG, N_TOK), ranks_t.reshape(n_groups, MOE_EG, N_TOK),
      x, gt2, g_post.reshape(1, D_MODEL), w1, w3, w2, ws1, ws3, ws2)


_IN_SIZES = (512, 256, 256, 64, 4, 512, 512, 512, 512, 512, 1024, 1024, 3072)
_IN_DEST = (COL_QA, COL_CKV, COL_IQ, COL_IKIW, COL_IKIW + 64, COL_QB, COL_KB, COL_VB, COL_QC, COL_KC,
            COL_VC, COL_GC, COL_GATES)
_IN_SRC = tuple(int(v) for v in np.cumsum((0,) + _IN_SIZES[:-1]))
N_IN = sum(_IN_SIZES)
PACK_TK = 256


def _pack_kernel(w_ref, o_ref):
    for src, width, dst in zip(_IN_SRC, _IN_SIZES, _IN_DEST):
        o_ref[dst:dst + width, :] = w_ref[src:src + width, :].astype(BF16)
    o_ref[COL_IKIW + 68:COL_IKIW + LANES, :] = jnp.zeros((LANES - 68, PACK_TK), BF16)


def _reorder_w_in(w_t):
    return pl.pallas_call(
        _pack_kernel,
        grid=(D_MODEL // PACK_TK,),
        in_specs=[pl.BlockSpec((N_IN, PACK_TK), lambda i: (0, i))],
        out_specs=pl.BlockSpec((N_PROJ, PACK_TK), lambda i: (0, i)),
        out_shape=jax.ShapeDtypeStruct((N_PROJ, D_MODEL), BF16),
        compiler_params=_params(("arbitrary",)),
        name="pack_w_in",
    )(w_t)


def _rope_tables(positions):
    half = HEAD_DIM // 2
    inv_freq = ROPE_THETA ** (-jnp.arange(half, dtype=F32) / half)
    ang = positions.astype(F32)[:, :, None] * inv_freq
    cos, sin = jnp.cos(ang), jnp.sin(ang)
    cos64 = jnp.concatenate([cos, cos], axis=-1)
    sin64 = jnp.concatenate([-sin, sin], axis=-1)
    cos_a = jnp.concatenate([cos64, cos64], axis=-1)
    sin_a = jnp.concatenate([sin64, sin64], axis=-1)
    cos_b = jnp.concatenate([cos64, jnp.ones_like(cos64)], axis=-1)
    sin_b = jnp.concatenate([sin64, jnp.zeros_like(sin64)], axis=-1)
    return cos_a, sin_a, cos_b, sin_b


def _decay_tables():
    log_g = jnp.log(1.0 - 2.0 ** (-5.0 - jnp.arange(C_HEADS, dtype=F32)))
    pos = jnp.arange(RET_BLK, dtype=F32)
    diff = pos[:, None] - pos[None, :]
    chunk = jnp.arange(RET_BLK) // CHUNK
    visible = chunk[None, :] <= chunk[:, None]
    e_diag = jnp.where(visible, jnp.exp(log_g[:, None, None] * jnp.abs(diff)), 0.0)
    dec = jnp.exp(log_g * RET_BLK)
    per_lane = jnp.repeat(log_g, HEAD_DIM)[None, :]
    q_scale = jnp.exp(per_lane * pos[:, None])
    k_scale = jnp.exp(per_lane * (RET_BLK - pos)[:, None])
    return dec, q_scale, k_scale, e_diag


def kernel(x, c, positions, ada_w, ada_b, g_pre_mix, g_post_mix, g_pre_ffn, g_post_ffn, w_in, g_kv, w_uk, w_uv,
           g_ret, w_o_a, w_o_b, w_o_c, w_out, w_router, router_bias, w1, w3, w2, ws1, ws3, ws2):
    cos_a, sin_a, cos_b, sin_b = _rope_tables(positions)
    dec, q_scale, k_scale, e_diag = _decay_tables()
    xf = x.reshape(N_TOK, D_MODEL)
    for l in range(DEPTH):
        mod = _ada(c, ada_w[l], ada_b[l]).reshape(BATCH, 1, N_ADA * D_MODEL)
        sh1, sc1, gt1, sh2, sc2, gt2 = (mod[:, :, k * D_MODEL:(k + 1) * D_MODEL] for k in range(N_ADA))
        z = _inproj(xf, g_pre_mix[l], sc1, sh1, _reorder_w_in(w_in[l].T))
        z3 = z.reshape(BATCH, SEQ, N_PROJ)
        w_ukv = jnp.concatenate([w_uk[l], w_uv[l]], axis=1).astype(BF16)
        ya = _dsa(z3, cos_a, sin_a, cos_b, sin_b, g_kv[l], w_ukv).reshape(N_TOK, -1)
        yb = _sb(z3).reshape(N_TOK, -1)
        yc = _ret(z3, cos_a, sin_a, dec, q_scale, k_scale, e_diag, g_ret[l]).reshape(N_TOK, -1)
        xf, h, gates, ranks = _merge(ya, yb, yc, z, xf, gt1, g_post_mix[l], g_pre_ffn[l], sc2, sh2,
                              w_o_a[l].astype(BF16), w_o_b[l].astype(BF16), w_o_c[l].astype(BF16),
                              w_out[l].astype(BF16), w_router[l].T.astype(BF16),
                              router_bias[l].reshape(N_EXPERTS, 1))
        xf = _moe(h, gates, ranks, xf, gt2, g_post_ffn[l], w1[l].astype(BF16), w3[l].astype(BF16), w2[l].astype(BF16),
                  ws1[l].astype(BF16), ws3[l].astype(BF16), ws2[l].astype(BF16))
    return xf.reshape(BATCH, SEQ, D_MODEL)
```

```python
import functools
import math

import numpy as np
import jax
import jax.numpy as jnp
from jax import lax
from jax.experimental import pallas as pl
from jax.experimental.pallas import tpu as pltpu

F32 = jnp.float32
BF16 = jnp.bfloat16
I32 = jnp.int32

D_MODEL = 1024
BATCH = 8
SEQ = 2048
DEPTH = 2
N_TOK = BATCH * SEQ

CHUNK = 64
HEAD_DIM = 64
ROPE_THETA = 10000.0
EPS = 1e-6
A_HEADS = 8
A_KV_LORA = 256
IDX_HEADS = 4
IDX_DIM = 64
TOPK = 256
B_HEADS = 8
C_HEADS = 8
C_V_DIM = 128
N_EXPERTS = 64
TOP_K = 8
N_GROUPS = 8
TOPK_GROUPS = 4
EXPERT_DIM = 256
SHARED_DIM = 256
ROUTED_SCALE = 2.5
N_ADA = 6

LANES = 128
VMEM_LIMIT = 56 * 1024 * 1024

COL_GATES = 0
COL_VC = 3072
COL_GC = 4096
COL_QA = 5120
COL_QB = 5632
COL_KB = 6144
COL_VB = 6656
COL_QC = 7168
COL_KC = 7680
COL_CKV = 8192
COL_IQ = 8448
COL_IKIW = 8704
N_PROJ = 8832
PROJ_TN = 2944
PROJ_TM = 512

INT_MIN = -2147483648
NEG_INF_KEY = -2139095041

NT_DIMS = (((1,), (1,)), ((), ()))


def _params(sem):
    return pltpu.CompilerParams(dimension_semantics=sem, vmem_limit_bytes=VMEM_LIMIT)


def _sigmoid(x):
    return 1.0 / (1.0 + jnp.exp(-x))


def _rope_lanes(x, cos, sin_signed):
    n = x.shape[-1]
    lane = lax.broadcasted_iota(I32, (1, n), 1)
    first = (lane & 32) == 0
    ahead = pltpu.roll(x, n - 32, 1)
    behind = pltpu.roll(x, 32, 1)
    return x * cos + jnp.where(first, ahead, behind) * sin_signed


def _ada_kernel(c_ref, w_ref, b_ref, o_ref):
    c = c_ref[...]
    s = (c * _sigmoid(c)).astype(BF16)
    o_ref[...] = jnp.dot(s, w_ref[...].astype(BF16), preferred_element_type=F32) + b_ref[...]


def _ada(c, w, b):
    n = w.shape[1]
    return pl.pallas_call(
        _ada_kernel,
        grid=(n // D_MODEL,),
        in_specs=[pl.BlockSpec((BATCH, D_MODEL), lambda j: (0, 0)),
                  pl.BlockSpec((D_MODEL, D_MODEL), lambda j: (0, j)),
                  pl.BlockSpec((1, D_MODEL), lambda j: (0, j))],
        out_specs=pl.BlockSpec((BATCH, D_MODEL), lambda j: (0, j)),
        out_shape=jax.ShapeDtypeStruct((BATCH, n), F32),
        compiler_params=_params(("arbitrary",)),
        name="ada_mod",
    )(c, w, b.reshape(1, n))


def _inproj_kernel(x_ref, g_ref, sc_ref, sh_ref, w_ref, o_ref, h_ref):
    @pl.when(pl.program_id(1) == 0)
    def _():
        x = x_ref[...]
        ms = jnp.mean(x * x, axis=-1, keepdims=True)
        y = x * lax.rsqrt(ms + EPS) * g_ref[...]
        h_ref[...] = (y * (1.0 + sc_ref[0]) + sh_ref[0]).astype(BF16)

    o_ref[...] = lax.dot_general(h_ref[...], w_ref[...], NT_DIMS, preferred_element_type=F32)


def _inproj(x, g, sc, sh, w_t_bf16):
    tiles_per_batch = SEQ // PROJ_TM
    return pl.pallas_call(
        _inproj_kernel,
        grid=(N_TOK // PROJ_TM, N_PROJ // PROJ_TN),
        in_specs=[pl.BlockSpec((PROJ_TM, D_MODEL), lambda i, j: (i, 0)),
                  pl.BlockSpec((1, D_MODEL), lambda i, j: (0, 0)),
                  pl.BlockSpec((1, 1, D_MODEL), lambda i, j: (i // tiles_per_batch, 0, 0)),
                  pl.BlockSpec((1, 1, D_MODEL), lambda i, j: (i // tiles_per_batch, 0, 0)),
                  pl.BlockSpec((PROJ_TN, D_MODEL), lambda i, j: (j, 0))],
        out_specs=pl.BlockSpec((PROJ_TM, PROJ_TN), lambda i, j: (i, j)),
        out_shape=jax.ShapeDtypeStruct((N_TOK, N_PROJ), F32),
        scratch_shapes=[pltpu.VMEM((PROJ_TM, D_MODEL), BF16)],
        compiler_params=_params(("arbitrary", "arbitrary")),
        name="in_proj",
    )(x, g.reshape(1, D_MODEL), sc, sh, w_t_bf16)


DSA_QB = 128
DSA_KC = 256


def _dsa_kernel(ckv_ref, ikiw_ref, q_ref, iq_ref, cos_a, sin_a, cos_b, sin_b, gkv_ref, wukv_ref,
                o_ref, k_lo, k_hi, v_dup, ik_lo, ik_hi, keys_s, selt_s, acc_s, q_s):
    qi = pl.program_id(1)
    lane = lax.broadcasted_iota(I32, (1, LANES), 1)
    lo_half = lane < 64
    n_kc = SEQ // DSA_KC

    @pl.when(qi == 0)
    def _prologue():
        def chunk(c, carry):
            r0 = pl.multiple_of(c * DSA_KC, DSA_KC)
            rows = pl.ds(r0, DSA_KC)
            ckv = ckv_ref[0, rows, :]
            ms = jnp.mean(ckv * ckv, axis=-1, keepdims=True)
            cn = (ckv * lax.rsqrt(ms + EPS) * gkv_ref[...]).astype(BF16)
            kv = jnp.dot(cn, wukv_ref[...], preferred_element_type=F32)
            cb, sb = cos_b[0, rows, :], sin_b[0, rows, :]
            kv = _rope_lanes(kv, cb, sb)
            swapped = pltpu.roll(kv, 64, 1)
            k_lo[rows, :] = jnp.where(lo_half, kv, 0.0).astype(BF16)
            k_hi[rows, :] = jnp.where(lo_half, 0.0, swapped).astype(BF16)
            v_dup[rows, :] = jnp.where(lo_half, swapped, kv).astype(BF16)
            ik = jnp.where(lo_half, _rope_lanes(ikiw_ref[0, rows, :], cb, sb), 0.0)
            ik_lo[rows, :] = ik.astype(BF16)
            ik_hi[rows, :] = pltpu.roll(ik, 64, 1).astype(BF16)
            return carry
        lax.fori_loop(0, n_kc, chunk, 0)

    q0 = pl.multiple_of(qi * DSA_QB, DSA_QB)
    qrows = pl.ds(q0, DSA_QB)
    cq, sq = cos_a[0, qrows, :], sin_a[0, qrows, :]

    iq = iq_ref[0]
    iq_heads = []
    for p in range(IDX_HEADS // 2):
        pair = _rope_lanes(iq[:, p * LANES:(p + 1) * LANES], cq, sq)
        iq_heads.append(jnp.where(lo_half, pair, 0.0).astype(BF16))
        iq_heads.append(jnp.where(lo_half, 0.0, pair).astype(BF16))
    iw_t = ikiw_ref[0, qrows, :].T * (IDX_HEADS ** -0.5)
    q_chunk = (q0 + lane) >> 6
    n_live = lax.shift_right_logical(qi + 2, 1)

    def chunk_rows(c):
        return pl.ds(pl.multiple_of(c * DSA_KC, DSA_KC), DSA_KC)

    def score_chunk(c, carry):
        rows = chunk_rows(c)
        k_chunk = (c * DSA_KC + lax.broadcasted_iota(I32, (DSA_KC, 1), 0)) >> 6
        score = jnp.zeros((DSA_KC, DSA_QB), F32)
        for h in range(IDX_HEADS):
            ik = ik_lo if h % 2 == 0 else ik_hi
            logit = lax.dot_general(ik[rows, :], iq_heads[h], NT_DIMS,
                                    preferred_element_type=F32) * (IDX_DIM ** -0.5)
            score = score + iw_t[64 + h:65 + h, :] * jnp.maximum(logit, 0.0)
        score = jnp.where(score == 0.0, 0.0, score)
        score = jnp.where(k_chunk <= q_chunk, score, -jnp.inf)
        bits = lax.bitcast_convert_type(score, I32)
        keys_s[rows, :] = bits ^ ((bits >> 31) & 0x7FFFFFFF)
        return carry
    lax.fori_loop(0, n_live, score_chunk, 0)

    def count(preds):
        def chunk(c, accs):
            key = keys_s[chunk_rows(c), :]
            return tuple(acc + jnp.sum(pred(key).astype(I32).reshape(4, DSA_KC // 4, DSA_QB), axis=0)
                         for acc, pred in zip(accs, preds))
        zero = jnp.zeros((DSA_KC // 4, DSA_QB), I32)
        accs = lax.fori_loop(0, n_live, chunk, (zero,) * len(preds))
        return [jnp.sum(acc, axis=0, keepdims=True) for acc in accs]

    def descend(it, prefix):
        cand = prefix | lax.shift_left(jnp.int32(1), 31 - it)
        bound = cand ^ INT_MIN
        return jnp.where(count([lambda k: k >= bound])[0] >= TOPK, cand, prefix)
    thr = lax.fori_loop(0, 32, descend, jnp.zeros((1, DSA_QB), I32)) ^ INT_MIN

    n_gt, n_eq = count([lambda k: k > thr, lambda k: k == thr])
    need = (TOPK - n_gt).astype(F32)
    tie_break = jnp.max(n_gt + n_eq) > TOPK

    @pl.when(tie_break)
    def _():
        tri = (lax.broadcasted_iota(I32, (DSA_KC, DSA_KC), 0)
               >= lax.broadcasted_iota(I32, (DSA_KC, DSA_KC), 1)).astype(BF16)

        def select_chunk(c, seen):
            key = keys_s[chunk_rows(c), :]
            eq = key == thr
            rank = jnp.dot(tri, eq.astype(BF16), preferred_element_type=F32) + seen
            sel = ((key > thr) | (eq & (rank <= need))) & (key > NEG_INF_KEY)
            selt_s[c] = sel.astype(F32).T
            return rank[DSA_KC - 1:DSA_KC, :]
        lax.fori_loop(0, n_live, select_chunk, jnp.zeros((1, DSA_QB), F32))

    @pl.when(jnp.logical_not(tie_break))
    def _():
        def select_chunk(c, carry):
            key = keys_s[chunk_rows(c), :]
            selt_s[c] = ((key >= thr) & (key > NEG_INF_KEY)).astype(F32).T
            return carry
        lax.fori_loop(0, n_live, select_chunk, 0)

    q = q_ref[0]
    n_pairs = A_HEADS // 2
    for p in range(n_pairs):
        pair = _rope_lanes(q[:, p * LANES:(p + 1) * LANES], cq, sq) * (HEAD_DIM ** -0.5)
        q_s[p * DSA_QB:(p + 1) * DSA_QB, :] = jnp.where(lo_half, pair, 0.0).astype(BF16)
        q_s[(n_pairs + p) * DSA_QB:(n_pairs + p + 1) * DSA_QB, :] = jnp.where(lo_half, 0.0, pair).astype(BF16)
    masked = -1e30
    half = n_pairs * DSA_QB

    def attend(c, carry):
        m, l = carry
        rows = chunk_rows(c)
        bias = jnp.where(selt_s[c] > 0.0, 0.0, masked)
        s = jnp.concatenate(
            [lax.dot_general(q_s[:half, :], k_lo[rows, :], NT_DIMS, preferred_element_type=F32),
             lax.dot_general(q_s[half:, :], k_hi[rows, :], NT_DIMS, preferred_element_type=F32)], axis=0)
        s = s.reshape(A_HEADS, DSA_QB, DSA_KC) + bias[None]
        m_new = jnp.maximum(m, jnp.max(s, axis=-1, keepdims=True))
        pe = jnp.exp(s - m_new)
        alpha = jnp.exp(m - m_new)
        pv = jnp.dot(pe.reshape(A_HEADS * DSA_QB, DSA_KC).astype(BF16), v_dup[rows, :],
                     preferred_element_type=F32)
        acc_s[...] = alpha * acc_s[...] + pv.reshape(A_HEADS, DSA_QB, LANES)
        return m_new, alpha * l + jnp.sum(pe, axis=-1, keepdims=True)

    acc_s[...] = jnp.zeros_like(acc_s)
    _, denom = lax.fori_loop(0, n_live, attend, (jnp.full((A_HEADS, DSA_QB, 1), masked, F32),
                                                 jnp.zeros((A_HEADS, DSA_QB, 1), F32)))
    for p in range(n_pairs):
        even = acc_s[p] / denom[p]
        odd = acc_s[n_pairs + p] / denom[n_pairs + p]
        o_ref[0, :, p * LANES:(p + 1) * LANES] = jnp.where(lo_half, even, odd)


def _dsa(z3, cos_a, sin_a, cos_b, sin_b, g_kv, w_ukv_bf16):
    full = lambda width, col: pl.BlockSpec((1, SEQ, width), lambda b, i: (b, 0, col // width))
    blk = lambda width, col: pl.BlockSpec((1, DSA_QB, width), lambda b, i: (b, i, col // width))
    tab = pl.BlockSpec((1, SEQ, LANES), lambda b, i: (b, 0, 0))
    return pl.pallas_call(
        _dsa_kernel,
        grid=(BATCH, SEQ // DSA_QB),
        in_specs=[full(A_KV_LORA, COL_CKV), full(LANES, COL_IKIW),
                  blk(A_HEADS * HEAD_DIM, COL_QA), blk(IDX_HEADS * IDX_DIM, COL_IQ),
                  tab, tab, tab, tab,
                  pl.BlockSpec((1, A_KV_LORA), lambda b, i: (0, 0)),
                  pl.BlockSpec((A_KV_LORA, LANES), lambda b, i: (0, 0))],
        out_specs=pl.BlockSpec((1, DSA_QB, A_HEADS * HEAD_DIM), lambda b, i: (b, i, 0)),
        out_shape=jax.ShapeDtypeStruct((BATCH, SEQ, A_HEADS * HEAD_DIM), F32),
        scratch_shapes=[pltpu.VMEM((SEQ, LANES), BF16)] * 5
                       + [pltpu.VMEM((SEQ, DSA_QB), I32),
                          pltpu.VMEM((SEQ // DSA_KC, DSA_QB, DSA_KC), F32),
                          pltpu.VMEM((A_HEADS, DSA_QB, LANES), F32),
                          pltpu.VMEM((A_HEADS * DSA_QB, LANES), BF16)],
        compiler_params=_params(("arbitrary", "arbitrary")),
        name="dsa_mixer",
    )(z3, z3, z3, z3, cos_a, sin_a, cos_b, sin_b, g_kv.reshape(1, A_KV_LORA), w_ukv_bf16)


SB_BLK = 128
SB_EXP_UNDERFLOW = -105.0


def _sb_kernel(q_ref, k_ref, v_ref, o_ref, q_s, tail_s, acc_s):
    qi = pl.program_id(1)
    n_pairs = B_HEADS // 2
    rows_all = B_HEADS * SB_BLK
    lane = lax.broadcasted_iota(I32, (1, LANES), 1)
    lo_half = lane < 64
    causal = (lax.broadcasted_iota(I32, (rows_all, SB_BLK), 1)
              < (lax.broadcasted_iota(I32, (rows_all, SB_BLK), 0) & (SB_BLK - 1)))
    r2 = lax.broadcasted_iota(I32, (2 * SB_BLK, 2 * SB_BLK), 0) & (SB_BLK - 1)
    c2 = lax.broadcasted_iota(I32, (2 * SB_BLK, 2 * SB_BLK), 1)
    suffix_mat = ((c2 >= SB_BLK) | (r2 > c2)).astype(BF16)

    for p in range(n_pairs):
        q2 = q_ref[0, :, p * LANES:(p + 1) * LANES] * (HEAD_DIM ** -0.5)
        q_s[2 * p * SB_BLK:(2 * p + 1) * SB_BLK, :] = jnp.where(lo_half, q2, 0.0).astype(BF16)
        q_s[(2 * p + 1) * SB_BLK:(2 * p + 2) * SB_BLK, :] = jnp.where(lo_half, 0.0, q2).astype(BF16)

    def key_block(kb, diagonal):
        r0 = pl.multiple_of(kb * SB_BLK, SB_BLK)
        pair_rows = lambda p: slice(2 * p * SB_BLK, (2 * p + 2) * SB_BLK)
        z = jnp.concatenate(
            [lax.dot_general(q_s[pair_rows(p), :], k_ref[0, pl.ds(r0, SB_BLK), p * LANES:(p + 1) * LANES].astype(BF16),
                             NT_DIMS, preferred_element_type=F32) for p in range(n_pairs)], axis=0)
        log_beta = jnp.minimum(z, 0.0) - jnp.log1p(jnp.exp(-jnp.abs(z)))
        log_not = log_beta - z
        if diagonal:
            log_not = jnp.where(causal, log_not, 0.0)
        hi = log_not.astype(BF16)
        lo = (log_not - hi.astype(F32)).astype(BF16)
        t = jnp.dot(jnp.concatenate([hi, lo], axis=1), suffix_mat, preferred_element_type=F32)
        if diagonal:
            a = jnp.where(causal, jnp.exp(log_beta + t[:, :SB_BLK]), 0.0)
            tail = t[:, SB_BLK:]
        else:
            a = jnp.exp(log_beta + t[:, :SB_BLK] + tail_s[...])
            tail = tail_s[...] + t[:, SB_BLK:]
        tail_s[...] = tail
        a = a.astype(BF16)
        for p in range(n_pairs):
            v2 = v_ref[0, pl.ds(r0, SB_BLK), p * LANES:(p + 1) * LANES].astype(BF16)
            pv = jnp.dot(a[pair_rows(p), :], v2, preferred_element_type=F32)
            if diagonal:
                acc_s[pair_rows(p), :] = pv
            else:
                acc_s[pair_rows(p), :] += pv
        return jnp.max(tail)

    first = key_block(qi, True)

    def more(state):
        jj, worst = state
        return (jj <= qi) & (worst > SB_EXP_UNDERFLOW)

    def step(state):
        jj, _ = state
        return jj + 1, key_block(qi - jj, False)

    lax.while_loop(more, step, (jnp.int32(1), first))
    for p in range(n_pairs):
        even = acc_s[2 * p * SB_BLK:(2 * p + 1) * SB_BLK, :]
        odd = acc_s[(2 * p + 1) * SB_BLK:(2 * p + 2) * SB_BLK, :]
        o_ref[0, :, p * LANES:(p + 1) * LANES] = jnp.where(lo_half, even, odd)


def _sb(z3):
    width = B_HEADS * HEAD_DIM
    return pl.pallas_call(
        _sb_kernel,
        grid=(BATCH, SEQ // SB_BLK),
        in_specs=[pl.BlockSpec((1, SB_BLK, width), lambda b, i: (b, i, COL_QB // width)),
                  pl.BlockSpec((1, SEQ, width), lambda b, i: (b, 0, COL_KB // width)),
                  pl.BlockSpec((1, SEQ, width), lambda b, i: (b, 0, COL_VB // width))],
        out_specs=pl.BlockSpec((1, SB_BLK, width), lambda b, i: (b, i, 0)),
        out_shape=jax.ShapeDtypeStruct((BATCH, SEQ, width), F32),
        scratch_shapes=[pltpu.VMEM((B_HEADS * SB_BLK, LANES), BF16),
                        pltpu.VMEM((B_HEADS * SB_BLK, SB_BLK), F32),
                        pltpu.VMEM((B_HEADS * SB_BLK, SB_BLK), F32)],
        compiler_params=_params(("arbitrary", "arbitrary")),
        name="stickbreak_mixer",
    )(z3, z3, z3)


RET_BLK = 256


def _ret_kernel(dec_ref, q_ref, k_ref, v_ref, gate_ref, cos_a, sin_a, qs_ref, ks_ref, e_diag, gret_ref,
                o_ref, state):
    lane = lax.broadcasted_iota(I32, (1, LANES), 1)
    lo_half = lane < 64

    @pl.when(pl.program_id(1) == 0)
    def _():
        state[...] = jnp.zeros_like(state)

    cq, sq = cos_a[0], sin_a[0]
    for p in range(C_HEADS // 2):
        cols = slice(p * LANES, (p + 1) * LANES)
        q2 = _rope_lanes(q_ref[0, :, cols], cq, sq)
        k2 = _rope_lanes(k_ref[0, :, cols], cq, sq) * (HEAD_DIM ** -0.5)
        q_hat = q2 * qs_ref[:, cols]
        k_hat = k2 * ks_ref[:, cols]
        for e in range(2):
            h = 2 * p + e
            pick = (lambda x: jnp.where(lo_half, x, 0.0)) if e == 0 else (lambda x: jnp.where(lo_half, 0.0, x))
            vcols = slice(h * C_V_DIM, (h + 1) * C_V_DIM)
            v2 = v_ref[0, :, vcols].astype(BF16)
            s = lax.dot_general(pick(q2).astype(BF16), k2.astype(BF16), NT_DIMS,
                                preferred_element_type=F32) * e_diag[h]
            o = jnp.dot(s.astype(BF16), v2, preferred_element_type=F32)
            o = o + jnp.dot(pick(q_hat).astype(BF16), state[h].astype(BF16), preferred_element_type=F32)
            kv = jnp.dot(pick(k_hat).T.astype(BF16), v2, preferred_element_type=F32)
            state[h] = dec_ref[h] * state[h] + kv
            mu = jnp.mean(o, axis=-1, keepdims=True)
            d = o - mu
            var = jnp.mean(d * d, axis=-1, keepdims=True)
            y = d * lax.rsqrt(var + EPS) * gret_ref[:, vcols]
            g = gate_ref[0, :, vcols]
            o_ref[0, :, vcols] = g * _sigmoid(g) * y


def _ret(z3, cos_a, sin_a, dec, q_scale, k_scale, e_diag, g_ret):
    qk_w = C_HEADS * HEAD_DIM
    v_w = C_HEADS * C_V_DIM
    tab = pl.BlockSpec((1, RET_BLK, LANES), lambda b, i: (b, i, 0))
    const2 = lambda a: pl.BlockSpec(a.shape, lambda b, i: (0, 0))
    return pl.pallas_call(
        _ret_kernel,
        grid=(BATCH, SEQ // RET_BLK),
        in_specs=[pl.BlockSpec(memory_space=pltpu.SMEM),
                  pl.BlockSpec((1, RET_BLK, qk_w), lambda b, i: (b, i, COL_QC // qk_w)),
                  pl.BlockSpec((1, RET_BLK, qk_w), lambda b, i: (b, i, COL_KC // qk_w)),
                  pl.BlockSpec((1, RET_BLK, v_w), lambda b, i: (b, i, COL_VC // v_w)),
                  pl.BlockSpec((1, RET_BLK, v_w), lambda b, i: (b, i, COL_GC // v_w)),
                  tab, tab, const2(q_scale), const2(k_scale),
                  pl.BlockSpec(e_diag.shape, lambda b, i: (0, 0, 0)),
                  pl.BlockSpec((1, v_w), lambda b, i: (0, 0))],
        out_specs=pl.BlockSpec((1, RET_BLK, v_w), lambda b, i: (b, i, 0)),
        out_shape=jax.ShapeDtypeStruct((BATCH, SEQ, v_w), F32),
        scratch_shapes=[pltpu.VMEM((C_HEADS, LANES, C_V_DIM), F32)],
        compiler_params=_params(("arbitrary", "arbitrary")),
        name="retention_mixer",
    )(dec, z3, z3, z3, z3, cos_a, sin_a, q_scale, k_scale, e_diag, g_ret.reshape(1, v_w))


MERGE_TM = 256


def _rms(y, g):
    ms = jnp.mean(y * y, axis=-1, keepdims=True)
    return y * lax.rsqrt(ms + EPS) * g


def _route(logits_t, bias_col):
    n_tok = logits_t.shape[1]
    per_group = N_EXPERTS // N_GROUPS
    scores = _sigmoid(logits_t)
    sel = scores + bias_col
    sel3 = sel.reshape(N_GROUPS, per_group, n_tok)
    member = lax.broadcasted_iota(I32, (N_GROUPS, per_group, n_tok), 1)
    m1 = jnp.max(sel3, axis=1, keepdims=True)
    first = jnp.min(jnp.where(sel3 == m1, member, per_group), axis=1, keepdims=True)
    m2 = jnp.max(jnp.where(member == first, -jnp.inf, sel3), axis=1, keepdims=True)
    grp = (m1 + m2).reshape(N_GROUPS, n_tok)
    gid = lax.broadcasted_iota(I32, (N_GROUPS, n_tok), 0)
    rank = jnp.zeros((N_GROUPS, n_tok), I32)
    for g in range(N_GROUPS):
        other = grp[g:g + 1, :]
        rank = rank + ((other > grp) | ((other == grp) & (g < gid))).astype(I32)
    keep = jnp.where(rank < TOPK_GROUPS, 1.0, 0.0).reshape(N_GROUPS, 1, n_tok)
    live = jnp.where(keep > 0.0, sel3, -jnp.inf).reshape(N_EXPERTS, n_tok)
    eid = lax.broadcasted_iota(I32, (N_EXPERTS, n_tok), 0)
    chosen = jnp.zeros((N_EXPERTS, n_tok), jnp.bool_)
    for _ in range(TOP_K):
        m = jnp.max(live, axis=0, keepdims=True)
        pick = eid == jnp.min(jnp.where(live == m, eid, N_EXPERTS), axis=0, keepdims=True)
        chosen = chosen | pick
        live = jnp.where(pick, -jnp.inf, live)
    w = jnp.where(chosen, scores, 0.0)
    return w / jnp.sum(w, axis=0, keepdims=True) * ROUTED_SCALE, chosen


def _merge_kernel(ya_ref, yb_ref, yc_ref, g_ref, x_ref, gt_ref, gpost_ref, gpre_ref, sc_ref, sh_ref,
                  woa, wob, woc, wout, wr_t, rb_ref, xo_ref, h_ref, gate_ref, rank_ref):
    ya = jnp.dot(ya_ref[...].astype(BF16), woa[...], preferred_element_type=F32)
    yb = jnp.dot(yb_ref[...].astype(BF16), wob[...], preferred_element_type=F32)
    yc = jnp.dot(yc_ref[...].astype(BF16), woc[...], preferred_element_type=F32)
    merged = (_sigmoid(g_ref[:, 0:D_MODEL]) * ya
              + _sigmoid(g_ref[:, D_MODEL:2 * D_MODEL]) * yb
              + _sigmoid(g_ref[:, 2 * D_MODEL:3 * D_MODEL]) * yc)
    y = jnp.dot(merged.astype(BF16), wout[...], preferred_element_type=F32)
    x = x_ref[...] + gt_ref[0] * _rms(y, gpost_ref[...])
    xo_ref[...] = x
    h = (_rms(x, gpre_ref[...]) * (1.0 + sc_ref[0]) + sh_ref[0]).astype(BF16)
    h_ref[...] = h
    logits_t = lax.dot_general(wr_t[...], h, NT_DIMS, preferred_element_type=F32)
    gates_t, chosen = _route(logits_t, rb_ref[...])
    gate_ref[...] = gates_t
    earlier = (lax.broadcasted_iota(I32, (MERGE_TM, MERGE_TM), 0)
               < lax.broadcasted_iota(I32, (MERGE_TM, MERGE_TM), 1)).astype(BF16)
    before = jnp.dot(jnp.where(chosen, 1.0, 0.0).astype(BF16), earlier, preferred_element_type=F32)
    rank_ref[...] = jnp.where(chosen, before, -1.0)


def _merge(ya, yb, yc, z, x, gt1, g_post, g_pre, sc2, sh2, woa, wob, woc, wout, wr_t, rbias):
    tpb = SEQ // MERGE_TM
    row = lambda w: pl.BlockSpec((MERGE_TM, w), lambda i: (i, 0))
    vec = pl.BlockSpec((1, D_MODEL), lambda i: (0, 0))
    mod = pl.BlockSpec((1, 1, D_MODEL), lambda i: (i // tpb, 0, 0))
    whole = lambda a: pl.BlockSpec(a.shape, lambda i: (0, 0))
    return pl.pallas_call(
        _merge_kernel,
        grid=(N_TOK // MERGE_TM,),
        in_specs=[row(512), row(512), row(1024), pl.BlockSpec((MERGE_TM, 3 * D_MODEL), lambda i: (i, 0)),
                  row(D_MODEL), mod, vec, vec, mod, mod,
                  whole(woa), whole(wob), whole(woc), whole(wout), whole(wr_t), whole(rbias)],
        out_specs=[row(D_MODEL), row(D_MODEL),
                   pl.BlockSpec((N_EXPERTS, MERGE_TM), lambda i: (0, i)),
                   pl.BlockSpec((N_EXPERTS, MERGE_TM), lambda i: (0, i))],
        out_shape=[jax.ShapeDtypeStruct((N_TOK, D_MODEL), F32),
                   jax.ShapeDtypeStruct((N_TOK, D_MODEL), BF16),
                   jax.ShapeDtypeStruct((N_EXPERTS, N_TOK), F32),
                   jax.ShapeDtypeStruct((N_EXPERTS, N_TOK), F32)],
        compiler_params=_params(("arbitrary",)),
        name="merge_route",
    )(ya, yb, yc, z, x, gt1, g_post.reshape(1, D_MODEL), g_pre.reshape(1, D_MODEL), sc2, sh2,
      woa, wob, woc, wout, wr_t, rbias)


MOE_TM = 1024
MOE_SUB = MERGE_TM
MOE_EG = 4
MOE_CAP = 64
MOE_NSUB = MOE_TM // MOE_SUB
MOE_SLAB = MOE_EG * MOE_CAP


def _moe_kernel(h_ref, gate_ref, rank_ref, x_ref, gt_ref, gpost_ref, w1, w3, w2, ws1, ws3, ws2,
                o_ref, acc, xg_s, yhi_s, ylo_s, pick_s, back_s):
    g_idx = pl.program_id(1)

    @pl.when(g_idx == 0)
    def _():
        acc[...] = jnp.zeros_like(acc)

    rank = rank_ref[0]
    gate = gate_ref[0]
    top_rank = jnp.max(rank)
    n_rounds = (top_rank.astype(I32) + MOE_CAP) // MOE_CAP
    slot = lax.broadcasted_iota(I32, (MOE_CAP, 1), 0).astype(F32)
    sub_top = [jnp.max(rank[:, s * MOE_SUB:(s + 1) * MOE_SUB]) for s in range(MOE_NSUB)]
    exp_top = [jnp.max(rank[e:e + 1, :]) for e in range(MOE_EG)]

    def one_round(r, carry):
        base = (r * MOE_CAP).astype(F32)
        first = r == 0
        for s in range(MOE_NSUB):
            toks = slice(s * MOE_SUB, (s + 1) * MOE_SUB)

            @pl.when(first | (sub_top[s] >= base))
            def _(s=s, toks=toks):
                pick = jnp.concatenate(
                    [jnp.where(rank[e:e + 1, toks] == slot + base, 1.0, 0.0) for e in range(MOE_EG)], axis=0)
                pick_s[s] = pick.astype(BF16)
                back_s[s] = pick.T.astype(BF16)
                xg_s[s] = jnp.dot(pick.astype(BF16), h_ref[toks, :], preferred_element_type=F32).astype(BF16)
        for e in range(MOE_EG):
            rows = slice(e * MOE_CAP, (e + 1) * MOE_CAP)

            @pl.when(first | (exp_top[e] >= base))
            def _(e=e, rows=rows):
                x_e = jnp.concatenate([xg_s[s, rows, :] for s in range(MOE_NSUB)], axis=0)
                a = jnp.dot(x_e, w1[e], preferred_element_type=F32)
                hid = a * _sigmoid(a) * jnp.dot(x_e, w3[e], preferred_element_type=F32)
                y = jnp.dot(hid.astype(BF16), w2[e], preferred_element_type=F32)
                for s in range(MOE_NSUB):
                    toks = slice(s * MOE_SUB, (s + 1) * MOE_SUB)
                    w_row = jnp.sum(pick_s[s, rows, :].astype(F32) * gate[e:e + 1, toks], axis=1, keepdims=True)
                    y_w = y[s * MOE_CAP:(s + 1) * MOE_CAP, :] * w_row
                    y_hi = y_w.astype(BF16)
                    yhi_s[s, rows, :] = y_hi
                    ylo_s[s, rows, :] = (y_w - y_hi.astype(F32)).astype(BF16)
        for s in range(MOE_NSUB):
            toks = slice(s * MOE_SUB, (s + 1) * MOE_SUB)

            @pl.when(first | (sub_top[s] >= base))
            def _(s=s, toks=toks):
                acc[toks, :] += (jnp.dot(back_s[s], yhi_s[s], preferred_element_type=F32)
                                 + jnp.dot(back_s[s], ylo_s[s], preferred_element_type=F32))
        return carry

    lax.fori_loop(0, n_rounds, one_round, 0)

    @pl.when(g_idx == N_EXPERTS // MOE_EG - 1)
    def _():
        h = h_ref[...]
        s = jnp.dot(h, ws1[...], preferred_element_type=F32)
        shid = s * _sigmoid(s) * jnp.dot(h, ws3[...], preferred_element_type=F32)
        y_all = acc[...] + jnp.dot(shid.astype(BF16), ws2[...], preferred_element_type=F32)
        o_ref[...] = x_ref[...] + gt_ref[0] * _rms(y_all, gpost_ref[...])


def _moe(h, gates_t, ranks_t, x, gt2, g_post, w1, w3, w2, ws1, ws3, ws2):
    tpb = SEQ // MOE_TM
    n_groups = N_EXPERTS // MOE_EG
    whole = lambda a: pl.BlockSpec(a.shape, lambda i, g: (0, 0))
    per_group = pl.BlockSpec((1, MOE_EG, MOE_TM), lambda i, g: (g, 0, i))
    return pl.pallas_call(
        _moe_kernel,
        grid=(N_TOK // MOE_TM, n_groups),
        in_specs=[pl.BlockSpec((MOE_TM, D_MODEL), lambda i, g: (i, 0)),
                  per_group, per_group,
                  pl.BlockSpec((MOE_TM, D_MODEL), lambda i, g: (i, 0)),
                  pl.BlockSpec((1, 1, D_MODEL), lambda i, g: (i // tpb, 0, 0)),
                  pl.BlockSpec((1, D_MODEL), lambda i, g: (0, 0)),
                  pl.BlockSpec((MOE_EG, D_MODEL, EXPERT_DIM), lambda i, g: (g, 0, 0)),
                  pl.BlockSpec((MOE_EG, D_MODEL, EXPERT_DIM), lambda i, g: (g, 0, 0)),
                  pl.BlockSpec((MOE_EG, EXPERT_DIM, D_MODEL), lambda i, g: (g, 0, 0)),
                  whole(ws1), whole(ws3), whole(ws2)],
        out_specs=pl.BlockSpec((MOE_TM, D_MODEL), lambda i, g: (i, 0)),
        out_shape=jax.ShapeDtypeStruct((N_TOK, D_MODEL), F32),
        scratch_shapes=[pltpu.VMEM((MOE_TM, D_MODEL), F32),
                        pltpu.VMEM((MOE_NSUB, MOE_SLAB, D_MODEL), BF16),
                        pltpu.VMEM((MOE_NSUB, MOE_SLAB, D_MODEL), BF16),
                        pltpu.VMEM((MOE_NSUB, MOE_SLAB, D_MODEL), BF16),
                        pltpu.VMEM((MOE_NSUB, MOE_SLAB, MOE_SUB), BF16),
                        pltpu.VMEM((MOE_NSUB, MOE_SUB, MOE_SLAB), BF16)],
        compiler_params=_params(("arbitrary", "arbitrary")),
        name="moe_ffn",
    )(h, gates_t.reshape(n_groups, MOE_EG, N_TOK), ranks_t.reshape(n_groups, MOE_EG, N_TOK),
      x, gt2, g_post.reshape(1, D_MODEL), w1, w3, w2, ws1, ws3, ws2)


_IN_SIZES = (512, 256, 256, 64, 4, 512, 512, 512, 512, 512, 1024, 1024, 3072)
_IN_DEST = (COL_QA, COL_CKV, COL_IQ, COL_IKIW, COL_IKIW + 64, COL_QB, COL_KB, COL_VB, COL_QC, COL_KC,
            COL_VC, COL_GC, COL_GATES)
_IN_SRC = tuple(int(v) for v in np.cumsum((0,) + _IN_SIZES[:-1]))
N_IN = sum(_IN_SIZES)
PACK_TK = 256


def _pack_kernel(w_ref, o_ref):
    for src, width, dst in zip(_IN_SRC, _IN_SIZES, _IN_DEST):
        o_ref[dst:dst + width, :] = w_ref[src:src + width, :].astype(BF16)
    o_ref[COL_IKIW + 68:COL_IKIW + LANES, :] = jnp.zeros((LANES - 68, PACK_TK), BF16)


def _reorder_w_in(w_t):
    return pl.pallas_call(
        _pack_kernel,
        grid=(D_MODEL // PACK_TK,),
        in_specs=[pl.BlockSpec((N_IN, PACK_TK), lambda i: (0, i))],
        out_specs=pl.BlockSpec((N_PROJ, PACK_TK), lambda i: (0, i)),
        out_shape=jax.ShapeDtypeStruct((N_PROJ, D_MODEL), BF16),
        compiler_params=_params(("arbitrary",)),
        name="pack_w_in",
    )(w_t)


def _rope_tables(positions):
    half = HEAD_DIM // 2
    inv_freq = ROPE_THETA ** (-jnp.arange(half, dtype=F32) / half)
    ang = positions.astype(F32)[:, :, None] * inv_freq
    cos, sin = jnp.cos(ang), jnp.sin(ang)
    cos64 = jnp.concatenate([cos, cos], axis=-1)
    sin64 = jnp.concatenate([-sin, sin], axis=-1)
    cos_a = jnp.concatenate([cos64, cos64], axis=-1)
    sin_a = jnp.concatenate([sin64, sin64], axis=-1)
    cos_b = jnp.concatenate([cos64, jnp.ones_like(cos64)], axis=-1)
    sin_b = jnp.concatenate([sin64, jnp.zeros_like(sin64)], axis=-1)
    return cos_a, sin_a, cos_b, sin_b


def _decay_tables():
    log_g = jnp.log(1.0 - 2.0 ** (-5.0 - jnp.arange(C_HEADS, dtype=F32)))
    pos = jnp.arange(RET_BLK, dtype=F32)
    diff = pos[:, None] - pos[None, :]
    chunk = jnp.arange(RET_BLK) // CHUNK
    visible = chunk[None, :] <= chunk[:, None]
    e_diag = jnp.where(visible, jnp.exp(log_g[:, None, None] * jnp.abs(diff)), 0.0)
    dec = jnp.exp(log_g * RET_BLK)
    per_lane = jnp.repeat(log_g, HEAD_DIM)[None, :]
    q_scale = jnp.exp(per_lane * pos[:, None])
    k_scale = jnp.exp(per_lane * (RET_BLK - pos)[:, None])
    return dec, q_scale, k_scale, e_diag


def kernel(x, c, positions, ada_w, ada_b, g_pre_mix, g_post_mix, g_pre_ffn, g_post_ffn, w_in, g_kv, w_uk, w_uv,
           g_ret, w_o_a, w_o_b, w_o_c, w_out, w_router, router_bias, w1, w3, w2, ws1, ws3, ws2):
    cos_a, sin_a, cos_b, sin_b = _rope_tables(positions)
    dec, q_scale, k_scale, e_diag = _decay_tables()
    xf = x.reshape(N_TOK, D_MODEL)
    for l in range(DEPTH):
        mod = _ada(c, ada_w[l], ada_b[l]).reshape(BATCH, 1, N_ADA * D_MODEL)
        sh1, sc1, gt1, sh2, sc2, gt2 = (mod[:, :, k * D_MODEL:(k + 1) * D_MODEL] for k in range(N_ADA))
        z = _inproj(xf, g_pre_mix[l], sc1, sh1, _reorder_w_in(w_in[l].T))
        z3 = z.reshape(BATCH, SEQ, N_PROJ)
        w_ukv = jnp.concatenate([w_uk[l], w_uv[l]], axis=1).astype(BF16)
        ya = _dsa(z3, cos_a, sin_a, cos_b, sin_b, g_kv[l], w_ukv).reshape(N_TOK, -1)
        yb = _sb(z3).reshape(N_TOK, -1)
        yc = _ret(z3, cos_a, sin_a, dec, q_scale, k_scale, e_diag, g_ret[l]).reshape(N_TOK, -1)
        xf, h, gates, ranks = _merge(ya, yb, yc, z, xf, gt1, g_post_mix[l], g_pre_ffn[l], sc2, sh2,
                              w_o_a[l].astype(BF16), w_o_b[l].astype(BF16), w_o_c[l].astype(BF16),
                              w_out[l].astype(BF16), w_router[l].T.astype(BF16),
                              router_bias[l].reshape(N_EXPERTS, 1))
        xf = _moe(h, gates, ranks, xf, gt2, g_post_ffn[l], w1[l].astype(BF16), w3[l].astype(BF16), w2[l].astype(BF16),
                  ws1[l].astype(BF16), ws3[l].astype(BF16), ws2[l].astype(BF16))
    return xf.reshape(BATCH, SEQ, D_MODEL)
```

```python
import functools
import math

import numpy as np
import jax
import jax.numpy as jnp
from jax import lax
from jax.experimental import pallas as pl
from jax.experimental.pallas import tpu as pltpu

F32 = jnp.float32
BF16 = jnp.bfloat16
I32 = jnp.int32

D_MODEL = 1024
BATCH = 8
SEQ = 2048
DEPTH = 2
N_TOK = BATCH * SEQ

CHUNK = 64
HEAD_DIM = 64
ROPE_THETA = 10000.0
EPS = 1e-6
A_HEADS = 8
A_KV_LORA = 256
IDX_HEADS = 4
IDX_DIM = 64
TOPK = 256
B_HEADS = 8
C_HEADS = 8
C_V_DIM = 128
N_EXPERTS = 64
TOP_K = 8
N_GROUPS = 8
TOPK_GROUPS = 4
EXPERT_DIM = 256
SHARED_DIM = 256
ROUTED_SCALE = 2.5
N_ADA = 6

LANES = 128
VMEM_LIMIT = 56 * 1024 * 1024

COL_GATES = 0
COL_VC = 3072
COL_GC = 4096
COL_QA = 5120
COL_QB = 5632
COL_KB = 6144
COL_VB = 6656
COL_QC = 7168
COL_KC = 7680
COL_CKV = 8192
COL_IQ = 8448
COL_IKIW = 8704
N_PROJ = 8832
PROJ_TN = 2944
PROJ_TM = 512

INT_MIN = -2147483648
NEG_INF_KEY = -2139095041

NT_DIMS = (((1,), (1,)), ((), ()))


def _params(sem):
    return pltpu.CompilerParams(dimension_semantics=sem, vmem_limit_bytes=VMEM_LIMIT)


def _sigmoid(x):
    return 1.0 / (1.0 + jnp.exp(-x))


def _rope_lanes(x, cos, sin_signed):
    n = x.shape[-1]
    lane = lax.broadcasted_iota(I32, (1, n), 1)
    first = (lane & 32) == 0
    ahead = pltpu.roll(x, n - 32, 1)
    behind = pltpu.roll(x, 32, 1)
    return x * cos + jnp.where(first, ahead, behind) * sin_signed


def _ada_kernel(c_ref, w_ref, b_ref, o_ref):
    c = c_ref[...]
    s = (c * _sigmoid(c)).astype(BF16)
    o_ref[...] = jnp.dot(s, w_ref[...].astype(BF16), preferred_element_type=F32) + b_ref[...]


def _ada(c, w, b):
    n = w.shape[1]
    return pl.pallas_call(
        _ada_kernel,
        grid=(n // D_MODEL,),
        in_specs=[pl.BlockSpec((BATCH, D_MODEL), lambda j: (0, 0)),
                  pl.BlockSpec((D_MODEL, D_MODEL), lambda j: (0, j)),
                  pl.BlockSpec((1, D_MODEL), lambda j: (0, j))],
        out_specs=pl.BlockSpec((BATCH, D_MODEL), lambda j: (0, j)),
        out_shape=jax.ShapeDtypeStruct((BATCH, n), F32),
        compiler_params=_params(("arbitrary",)),
        name="ada_mod",
    )(c, w, b.reshape(1, n))


def _inproj_kernel(x_ref, g_ref, sc_ref, sh_ref, w_ref, o_ref, h_ref):
    @pl.when(pl.program_id(1) == 0)
    def _():
        x = x_ref[...]
        ms = jnp.mean(x * x, axis=-1, keepdims=True)
        y = x * lax.rsqrt(ms + EPS) * g_ref[...]
        h_ref[...] = (y * (1.0 + sc_ref[0]) + sh_ref[0]).astype(BF16)

    o_ref[...] = lax.dot_general(h_ref[...], w_ref[...], NT_DIMS, preferred_element_type=F32)


def _inproj(x, g, sc, sh, w_t_bf16):
    tiles_per_batch = SEQ // PROJ_TM
    return pl.pallas_call(
        _inproj_kernel,
        grid=(N_TOK // PROJ_TM, N_PROJ // PROJ_TN),
        in_specs=[pl.BlockSpec((PROJ_TM, D_MODEL), lambda i, j: (i, 0)),
                  pl.BlockSpec((1, D_MODEL), lambda i, j: (0, 0)),
                  pl.BlockSpec((1, 1, D_MODEL), lambda i, j: (i // tiles_per_batch, 0, 0)),
                  pl.BlockSpec((1, 1, D_MODEL), lambda i, j: (i // tiles_per_batch, 0, 0)),
                  pl.BlockSpec((PROJ_TN, D_MODEL), lambda i, j: (j, 0))],
        out_specs=pl.BlockSpec((PROJ_TM, PROJ_TN), lambda i, j: (i, j)),
        out_shape=jax.ShapeDtypeStruct((N_TOK, N_PROJ), F32),
        scratch_shapes=[pltpu.VMEM((PROJ_TM, D_MODEL), BF16)],
        compiler_params=_params(("arbitrary", "arbitrary")),
        name="in_proj",
    )(x, g.reshape(1, D_MODEL), sc, sh, w_t_bf16)


DSA_QB = 128
DSA_KC = 256


def _dsa_kernel(ckv_ref, ikiw_ref, q_ref, iq_ref, cos_a, sin_a, cos_b, sin_b, gkv_ref, wukv_ref,
                o_ref, k_lo, k_hi, v_dup, ik_lo, ik_hi, keys_s, selt_s, acc_s, q_s):
    qi = pl.program_id(1)
    lane = lax.broadcasted_iota(I32, (1, LANES), 1)
    lo_half = lane < 64
    n_kc = SEQ // DSA_KC

    @pl.when(qi == 0)
    def _prologue():
        def chunk(c, carry):
            r0 = pl.multiple_of(c * DSA_KC, DSA_KC)
            rows = pl.ds(r0, DSA_KC)
            ckv = ckv_ref[0, rows, :]
            ms = jnp.mean(ckv * ckv, axis=-1, keepdims=True)
            cn = (ckv * lax.rsqrt(ms + EPS) * gkv_ref[...]).astype(BF16)
            kv = jnp.dot(cn, wukv_ref[...], preferred_element_type=F32)
            cb, sb = cos_b[0, rows, :], sin_b[0, rows, :]
            kv = _rope_lanes(kv, cb, sb)
            swapped = pltpu.roll(kv, 64, 1)
            k_lo[rows, :] = jnp.where(lo_half, kv, 0.0).astype(BF16)
            k_hi[rows, :] = jnp.where(lo_half, 0.0, swapped).astype(BF16)
            v_dup[rows, :] = jnp.where(lo_half, swapped, kv).astype(BF16)
            ik = jnp.where(lo_half, _rope_lanes(ikiw_ref[0, rows, :], cb, sb), 0.0)
            ik_lo[rows, :] = ik.astype(BF16)
            ik_hi[rows, :] = pltpu.roll(ik, 64, 1).astype(BF16)
            return carry
        lax.fori_loop(0, n_kc, chunk, 0)

    q0 = pl.multiple_of(qi * DSA_QB, DSA_QB)
    qrows = pl.ds(q0, DSA_QB)
    cq, sq = cos_a[0, qrows, :], sin_a[0, qrows, :]

    iq = iq_ref[0]
    iq_heads = []
    for p in range(IDX_HEADS // 2):
        pair = _rope_lanes(iq[:, p * LANES:(p + 1) * LANES], cq, sq)
        iq_heads.append(jnp.where(lo_half, pair, 0.0).astype(BF16))
        iq_heads.append(jnp.where(lo_half, 0.0, pair).astype(BF16))
    iw_t = ikiw_ref[0, qrows, :].T * (IDX_HEADS ** -0.5)
    q_chunk = (q0 + lane) >> 6
    n_live = lax.shift_right_logical(qi + 2, 1)

    def chunk_rows(c):
        return pl.ds(pl.multiple_of(c * DSA_KC, DSA_KC), DSA_KC)

    def score_chunk(c, carry):
        rows = chunk_rows(c)
        k_chunk = (c * DSA_KC + lax.broadcasted_iota(I32, (DSA_KC, 1), 0)) >> 6
        score = jnp.zeros((DSA_KC, DSA_QB), F32)
        for h in range(IDX_HEADS):
            ik = ik_lo if h % 2 == 0 else ik_hi
            logit = lax.dot_general(ik[rows, :], iq_heads[h], NT_DIMS,
                                    preferred_element_type=F32) * (IDX_DIM ** -0.5)
            score = score + iw_t[64 + h:65 + h, :] * jnp.maximum(logit, 0.0)
        score = jnp.where(score == 0.0, 0.0, score)
        score = jnp.where(k_chunk <= q_chunk, score, -jnp.inf)
        bits = lax.bitcast_convert_type(score, I32)
        keys_s[rows, :] = bits ^ ((bits >> 31) & 0x7FFFFFFF)
        return carry
    lax.fori_loop(0, n_live, score_chunk, 0)

    def count(preds):
        def chunk(c, accs):
            key = keys_s[chunk_rows(c), :]
            return tuple(acc + jnp.sum(pred(key).astype(I32).reshape(4, DSA_KC // 4, DSA_QB), axis=0)
                         for acc, pred in zip(accs, preds))
        zero = jnp.zeros((DSA_KC // 4, DSA_QB), I32)
        accs = lax.fori_loop(0, n_live, chunk, (zero,) * len(preds))
        return [jnp.sum(acc, axis=0, keepdims=True) for acc in accs]

    def descend(it, prefix):
        cand = prefix | lax.shift_left(jnp.int32(1), 31 - it)
        bound = cand ^ INT_MIN
        return jnp.where(count([lambda k: k >= bound])[0] >= TOPK, cand, prefix)
    thr = lax.fori_loop(0, 32, descend, jnp.zeros((1, DSA_QB), I32)) ^ INT_MIN

    n_gt, n_eq = count([lambda k: k > thr, lambda k: k == thr])
    need = (TOPK - n_gt).astype(F32)
    tie_break = jnp.max(n_gt + n_eq) > TOPK

    @pl.when(tie_break)
    def _():
        tri = (lax.broadcasted_iota(I32, (DSA_KC, DSA_KC), 0)
               >= lax.broadcasted_iota(I32, (DSA_KC, DSA_KC), 1)).astype(BF16)

        def select_chunk(c, seen):
            key = keys_s[chunk_rows(c), :]
            eq = key == thr
            rank = jnp.dot(tri, eq.astype(BF16), preferred_element_type=F32) + seen
            sel = ((key > thr) | (eq & (rank <= need))) & (key > NEG_INF_KEY)
            selt_s[c] = sel.astype(F32).T
            return rank[DSA_KC - 1:DSA_KC, :]
        lax.fori_loop(0, n_live, select_chunk, jnp.zeros((1, DSA_QB), F32))

    @pl.when(jnp.logical_not(tie_break))
    def _():
        def select_chunk(c, carry):
            key = keys_s[chunk_rows(c), :]
            selt_s[c] = ((key >= thr) & (key > NEG_INF_KEY)).astype(F32).T
            return carry
        lax.fori_loop(0, n_live, select_chunk, 0)

    q = q_ref[0]
    n_pairs = A_HEADS // 2
    for p in range(n_pairs):
        pair = _rope_lanes(q[:, p * LANES:(p + 1) * LANES], cq, sq) * (HEAD_DIM ** -0.5)
        q_s[p * DSA_QB:(p + 1) * DSA_QB, :] = jnp.where(lo_half, pair, 0.0).astype(BF16)
        q_s[(n_pairs + p) * DSA_QB:(n_pairs + p + 1) * DSA_QB, :] = jnp.where(lo_half, 0.0, pair).astype(BF16)
    masked = -1e30
    half = n_pairs * DSA_QB

    def attend(c, carry):
        m, l = carry
        rows = chunk_rows(c)
        bias = jnp.where(selt_s[c] > 0.0, 0.0, masked)
        s = jnp.concatenate(
            [lax.dot_general(q_s[:half, :], k_lo[rows, :], NT_DIMS, preferred_element_type=F32),
             lax.dot_general(q_s[half:, :], k_hi[rows, :], NT_DIMS, preferred_element_type=F32)], axis=0)
        s = s.reshape(A_HEADS, DSA_QB, DSA_KC) + bias[None]
        m_new = jnp.maximum(m, jnp.max(s, axis=-1, keepdims=True))
        pe = jnp.exp(s - m_new)
        alpha = jnp.exp(m - m_new)
        pv = jnp.dot(pe.reshape(A_HEADS * DSA_QB, DSA_KC).astype(BF16), v_dup[rows, :],
                     preferred_element_type=F32)
        acc_s[...] = alpha * acc_s[...] + pv.reshape(A_HEADS, DSA_QB, LANES)
        return m_new, alpha * l + jnp.sum(pe, axis=-1, keepdims=True)

    acc_s[...] = jnp.zeros_like(acc_s)
    _, denom = lax.fori_loop(0, n_live, attend, (jnp.full((A_HEADS, DSA_QB, 1), masked, F32),
                                                 jnp.zeros((A_HEADS, DSA_QB, 1), F32)))
    for p in range(n_pairs):
        even = acc_s[p] / denom[p]
        odd = acc_s[n_pairs + p] / denom[n_pairs + p]
        o_ref[0, :, p * LANES:(p + 1) * LANES] = jnp.where(lo_half, even, odd)


def _dsa(z3, cos_a, sin_a, cos_b, sin_b, g_kv, w_ukv_bf16):
    full = lambda width, col: pl.BlockSpec((1, SEQ, width), lambda b, i: (b, 0, col // width))
    blk = lambda width, col: pl.BlockSpec((1, DSA_QB, width), lambda b, i: (b, i, col // width))
    tab = pl.BlockSpec((1, SEQ, LANES), lambda b, i: (b, 0, 0))
    return pl.pallas_call(
        _dsa_kernel,
        grid=(BATCH, SEQ // DSA_QB),
        in_specs=[full(A_KV_LORA, COL_CKV), full(LANES, COL_IKIW),
                  blk(A_HEADS * HEAD_DIM, COL_QA), blk(IDX_HEADS * IDX_DIM, COL_IQ),
                  tab, tab, tab, tab,
                  pl.BlockSpec((1, A_KV_LORA), lambda b, i: (0, 0)),
                  pl.BlockSpec((A_KV_LORA, LANES), lambda b, i: (0, 0))],
        out_specs=pl.BlockSpec((1, DSA_QB, A_HEADS * HEAD_DIM), lambda b, i: (b, i, 0)),
        out_shape=jax.ShapeDtypeStruct((BATCH, SEQ, A_HEADS * HEAD_DIM), F32),
        scratch_shapes=[pltpu.VMEM((SEQ, LANES), BF16)] * 5
                       + [pltpu.VMEM((SEQ, DSA_QB), I32),
                          pltpu.VMEM((SEQ // DSA_KC, DSA_QB, DSA_KC), F32),
                          pltpu.VMEM((A_HEADS, DSA_QB, LANES), F32),
                          pltpu.VMEM((A_HEADS * DSA_QB, LANES), BF16)],
        compiler_params=_params(("arbitrary", "arbitrary")),
        name="dsa_mixer",
    )(z3, z3, z3, z3, cos_a, sin_a, cos_b, sin_b, g_kv.reshape(1, A_KV_LORA), w_ukv_bf16)


SB_BLK = 128
SB_EXP_UNDERFLOW = -105.0


def _sb_kernel(q_ref, k_ref, v_ref, o_ref, q_s, tail_s, acc_s):
    qi = pl.program_id(1)
    n_pairs = B_HEADS // 2
    rows_all = B_HEADS * SB_BLK
    lane = lax.broadcasted_iota(I32, (1, LANES), 1)
    lo_half = lane < 64
    causal = (lax.broadcasted_iota(I32, (rows_all, SB_BLK), 1)
              < (lax.broadcasted_iota(I32, (rows_all, SB_BLK), 0) & (SB_BLK - 1)))
    r2 = lax.broadcasted_iota(I32, (2 * SB_BLK, 2 * SB_BLK), 0) & (SB_BLK - 1)
    c2 = lax.broadcasted_iota(I32, (2 * SB_BLK, 2 * SB_BLK), 1)
    suffix_mat = ((c2 >= SB_BLK) | (r2 > c2)).astype(BF16)

    for p in range(n_pairs):
        q2 = q_ref[0, :, p * LANES:(p + 1) * LANES] * (HEAD_DIM ** -0.5)
        q_s[2 * p * SB_BLK:(2 * p + 1) * SB_BLK, :] = jnp.where(lo_half, q2, 0.0).astype(BF16)
        q_s[(2 * p + 1) * SB_BLK:(2 * p + 2) * SB_BLK, :] = jnp.where(lo_half, 0.0, q2).astype(BF16)

    def key_block(kb, diagonal):
        r0 = pl.multiple_of(kb * SB_BLK, SB_BLK)
        pair_rows = lambda p: slice(2 * p * SB_BLK, (2 * p + 2) * SB_BLK)
        z = jnp.concatenate(
            [lax.dot_general(q_s[pair_rows(p), :], k_ref[0, pl.ds(r0, SB_BLK), p * LANES:(p + 1) * LANES].astype(BF16),
                             NT_DIMS, preferred_element_type=F32) for p in range(n_pairs)], axis=0)
        log_beta = jnp.minimum(z, 0.0) - jnp.log1p(jnp.exp(-jnp.abs(z)))
        log_not = log_beta - z
        if diagonal:
            log_not = jnp.where(causal, log_not, 0.0)
        hi = log_not.astype(BF16)
        lo = (log_not - hi.astype(F32)).astype(BF16)
        t = jnp.dot(jnp.concatenate([hi, lo], axis=1), suffix_mat, preferred_element_type=F32)
        if diagonal:
            a = jnp.where(causal, jnp.exp(log_beta + t[:, :SB_BLK]), 0.0)
            tail = t[:, SB_BLK:]
        else:
            a = jnp.exp(log_beta + t[:, :SB_BLK] + tail_s[...])
            tail = tail_s[...] + t[:, SB_BLK:]
        tail_s[...] = tail
        a = a.astype(BF16)
        for p in range(n_pairs):
            v2 = v_ref[0, pl.ds(r0, SB_BLK), p * LANES:(p + 1) * LANES].astype(BF16)
            pv = jnp.dot(a[pair_rows(p), :], v2, preferred_element_type=F32)
            if diagonal:
                acc_s[pair_rows(p), :] = pv
            else:
                acc_s[pair_rows(p), :] += pv
        return jnp.max(tail)

    first = key_block(qi, True)

    def more(state):
        jj, worst = state
        return (jj <= qi) & (worst > SB_EXP_UNDERFLOW)

    def step(state):
        jj, _ = state
        return jj + 1, key_block(qi - jj, False)

    lax.while_loop(more, step, (jnp.int32(1), first))
    for p in range(n_pairs):
        even = acc_s[2 * p * SB_BLK:(2 * p + 1) * SB_BLK, :]
        odd = acc_s[(2 * p + 1) * SB_BLK:(2 * p + 2) * SB_BLK, :]
        o_ref[0, :, p * LANES:(p + 1) * LANES] = jnp.where(lo_half, even, odd)


def _sb(z3):
    width = B_HEADS * HEAD_DIM
    return pl.pallas_call(
        _sb_kernel,
        grid=(BATCH, SEQ // SB_BLK),
        in_specs=[pl.BlockSpec((1, SB_BLK, width), lambda b, i: (b, i, COL_QB // width)),
                  pl.BlockSpec((1, SEQ, width), lambda b, i: (b, 0, COL_KB // width)),
                  pl.BlockSpec((1, SEQ, width), lambda b, i: (b, 0, COL_VB // width))],
        out_specs=pl.BlockSpec((1, SB_BLK, width), lambda b, i: (b, i, 0)),
        out_shape=jax.ShapeDtypeStruct((BATCH, SEQ, width), F32),
        scratch_shapes=[pltpu.VMEM((B_HEADS * SB_BLK, LANES), BF16),
                        pltpu.VMEM((B_HEADS * SB_BLK, SB_BLK), F32),
                        pltpu.VMEM((B_HEADS * SB_BLK, SB_BLK), F32)],
        compiler_params=_params(("arbitrary", "arbitrary")),
        name="stickbreak_mixer",
    )(z3, z3, z3)


RET_BLK = 256


def _ret_kernel(dec_ref, q_ref, k_ref, v_ref, gate_ref, cos_a, sin_a, qs_ref, ks_ref, e_diag, gret_ref,
                o_ref, state):
    lane = lax.broadcasted_iota(I32, (1, LANES), 1)
    lo_half = lane < 64

    @pl.when(pl.program_id(1) == 0)
    def _():
        state[...] = jnp.zeros_like(state)

    cq, sq = cos_a[0], sin_a[0]
    for p in range(C_HEADS // 2):
        cols = slice(p * LANES, (p + 1) * LANES)
        q2 = _rope_lanes(q_ref[0, :, cols], cq, sq)
        k2 = _rope_lanes(k_ref[0, :, cols], cq, sq) * (HEAD_DIM ** -0.5)
        q_hat = q2 * qs_ref[:, cols]
        k_hat = k2 * ks_ref[:, cols]
        for e in range(2):
            h = 2 * p + e
            pick = (lambda x: jnp.where(lo_half, x, 0.0)) if e == 0 else (lambda x: jnp.where(lo_half, 0.0, x))
            vcols = slice(h * C_V_DIM, (h + 1) * C_V_DIM)
            v2 = v_ref[0, :, vcols].astype(BF16)
            s = lax.dot_general(pick(q2).astype(BF16), k2.astype(BF16), NT_DIMS,
                                preferred_element_type=F32) * e_diag[h]
            o = jnp.dot(s.astype(BF16), v2, preferred_element_type=F32)
            o = o + jnp.dot(pick(q_hat).astype(BF16), state[h].astype(BF16), preferred_element_type=F32)
            kv = jnp.dot(pick(k_hat).T.astype(BF16), v2, preferred_element_type=F32)
            state[h] = dec_ref[h] * state[h] + kv
            mu = jnp.mean(o, axis=-1, keepdims=True)
            d = o - mu
            var = jnp.mean(d * d, axis=-1, keepdims=True)
            y = d * lax.rsqrt(var + EPS) * gret_ref[:, vcols]
            g = gate_ref[0, :, vcols]
            o_ref[0, :, vcols] = g * _sigmoid(g) * y


def _ret(z3, cos_a, sin_a, dec, q_scale, k_scale, e_diag, g_ret):
    qk_w = C_HEADS * HEAD_DIM
    v_w = C_HEADS * C_V_DIM
    tab = pl.BlockSpec((1, RET_BLK, LANES), lambda b, i: (b, i, 0))
    const2 = lambda a: pl.BlockSpec(a.shape, lambda b, i: (0, 0))
    return pl.pallas_call(
        _ret_kernel,
        grid=(BATCH, SEQ // RET_BLK),
        in_specs=[pl.BlockSpec(memory_space=pltpu.SMEM),
                  pl.BlockSpec((1, RET_BLK, qk_w), lambda b, i: (b, i, COL_QC // qk_w)),
                  pl.BlockSpec((1, RET_BLK, qk_w), lambda b, i: (b, i, COL_KC // qk_w)),
                  pl.BlockSpec((1, RET_BLK, v_w), lambda b, i: (b, i, COL_VC // v_w)),
                  pl.BlockSpec((1, RET_BLK, v_w), lambda b, i: (b, i, COL_GC // v_w)),
                  tab, tab, const2(q_scale), const2(k_scale),
                  pl.BlockSpec(e_diag.shape, lambda b, i: (0, 0, 0)),
                  pl.BlockSpec((1, v_w), lambda b, i: (0, 0))],
        out_specs=pl.BlockSpec((1, RET_BLK, v_w), lambda b, i: (b, i, 0)),
        out_shape=jax.ShapeDtypeStruct((BATCH, SEQ, v_w), F32),
        scratch_shapes=[pltpu.VMEM((C_HEADS, LANES, C_V_DIM), F32)],
        compiler_params=_params(("arbitrary", "arbitrary")),
        name="retention_mixer",
    )(dec, z3, z3, z3, z3, cos_a, sin_a, q_scale, k_scale, e_diag, g_ret.reshape(1, v_w))


MERGE_TM = 256


def _rms(y, g):
    ms = jnp.mean(y * y, axis=-1, keepdims=True)
    return y * lax.rsqrt(ms + EPS) * g


def _route(logits_t, bias_col):
    n_tok = logits_t.shape[1]
    per_group = N_EXPERTS // N_GROUPS
    scores = _sigmoid(logits_t)
    sel = scores + bias_col
    sel3 = sel.reshape(N_GROUPS, per_group, n_tok)
    member = lax.broadcasted_iota(I32, (N_GROUPS, per_group, n_tok), 1)
    m1 = jnp.max(sel3, axis=1, keepdims=True)
    first = jnp.min(jnp.where(sel3 == m1, member, per_group), axis=1, keepdims=True)
    m2 = jnp.max(jnp.where(member == first, -jnp.inf, sel3), axis=1, keepdims=True)
    grp = (m1 + m2).reshape(N_GROUPS, n_tok)
    gid = lax.broadcasted_iota(I32, (N_GROUPS, n_tok), 0)
    rank = jnp.zeros((N_GROUPS, n_tok), I32)
    for g in range(N_GROUPS):
        other = grp[g:g + 1, :]
        rank = rank + ((other > grp) | ((other == grp) & (g < gid))).astype(I32)
    keep = jnp.where(rank < TOPK_GROUPS, 1.0, 0.0).reshape(N_GROUPS, 1, n_tok)
    live = jnp.where(keep > 0.0, sel3, -jnp.inf).reshape(N_EXPERTS, n_tok)
    eid = lax.broadcasted_iota(I32, (N_EXPERTS, n_tok), 0)
    chosen = jnp.zeros((N_EXPERTS, n_tok), jnp.bool_)
    for _ in range(TOP_K):
        m = jnp.max(live, axis=0, keepdims=True)
        pick = eid == jnp.min(jnp.where(live == m, eid, N_EXPERTS), axis=0, keepdims=True)
        chosen = chosen | pick
        live = jnp.where(pick, -jnp.inf, live)
    w = jnp.where(chosen, scores, 0.0)
    return w / jnp.sum(w, axis=0, keepdims=True) * ROUTED_SCALE, chosen


def _merge_kernel(ya_ref, yb_ref, yc_ref, g_ref, x_ref, gt_ref, gpost_ref, gpre_ref, sc_ref, sh_ref,
                  woa, wob, woc, wout, wr_t, rb_ref, xo_ref, h_ref, gate_ref, rank_ref):
    ya = jnp.dot(ya_ref[...].astype(BF16), woa[...], preferred_element_type=F32)
    yb = jnp.dot(yb_ref[...].astype(BF16), wob[...], preferred_element_type=F32)
    yc = jnp.dot(yc_ref[...].astype(BF16), woc[...], preferred_element_type=F32)
    merged = (_sigmoid(g_ref[:, 0:D_MODEL]) * ya
              + _sigmoid(g_ref[:, D_MODEL:2 * D_MODEL]) * yb
              + _sigmoid(g_ref[:, 2 * D_MODEL:3 * D_MODEL]) * yc)
    y = jnp.dot(merged.astype(BF16), wout[...], preferred_element_type=F32)
    x = x_ref[...] + gt_ref[0] * _rms(y, gpost_ref[...])
    xo_ref[...] = x
    h = (_rms(x, gpre_ref[...]) * (1.0 + sc_ref[0]) + sh_ref[0]).astype(BF16)
    h_ref[...] = h
    logits_t = lax.dot_general(wr_t[...], h, NT_DIMS, preferred_element_type=F32)
    gates_t, chosen = _route(logits_t, rb_ref[...])
    gate_ref[...] = gates_t
    earlier = (lax.broadcasted_iota(I32, (MERGE_TM, MERGE_TM), 0)
               < lax.broadcasted_iota(I32, (MERGE_TM, MERGE_TM), 1)).astype(BF16)
    before = jnp.dot(jnp.where(chosen, 1.0, 0.0).astype(BF16), earlier, preferred_element_type=F32)
    rank_ref[...] = jnp.where(chosen, before, -1.0)


def _merge(ya, yb, yc, z, x, gt1, g_post, g_pre, sc2, sh2, woa, wob, woc, wout, wr_t, rbias):
    tpb = SEQ // MERGE_TM
    row = lambda w: pl.BlockSpec((MERGE_TM, w), lambda i: (i, 0))
    vec = pl.BlockSpec((1, D_MODEL), lambda i: (0, 0))
    mod = pl.BlockSpec((1, 1, D_MODEL), lambda i: (i // tpb, 0, 0))
    whole = lambda a: pl.BlockSpec(a.shape, lambda i: (0, 0))
    return pl.pallas_call(
        _merge_kernel,
        grid=(N_TOK // MERGE_TM,),
        in_specs=[row(512), row(512), row(1024), pl.BlockSpec((MERGE_TM, 3 * D_MODEL), lambda i: (i, 0)),
                  row(D_MODEL), mod, vec, vec, mod, mod,
                  whole(woa), whole(wob), whole(woc), whole(wout), whole(wr_t), whole(rbias)],
        out_specs=[row(D_MODEL), row(D_MODEL),
                   pl.BlockSpec((N_EXPERTS, MERGE_TM), lambda i: (0, i)),
                   pl.BlockSpec((N_EXPERTS, MERGE_TM), lambda i: (0, i))],
        out_shape=[jax.ShapeDtypeStruct((N_TOK, D_MODEL), F32),
                   jax.ShapeDtypeStruct((N_TOK, D_MODEL), BF16),
                   jax.ShapeDtypeStruct((N_EXPERTS, N_TOK), F32),
                   jax.ShapeDtypeStruct((N_EXPERTS, N_TOK), F32)],
        compiler_params=_params(("arbitrary",)),
        name="merge_route",
    )(ya, yb, yc, z, x, gt1, g_post.reshape(1, D_MODEL), g_pre.reshape(1, D_MODEL), sc2, sh2,
      woa, wob, woc, wout, wr_t, rbias)


MOE_TM = 1024
MOE_SUB = MERGE_TM
MOE_EG = 4
MOE_CAP = 64
MOE_NSUB = MOE_TM // MOE_SUB
MOE_SLAB = MOE_EG * MOE_CAP


def _moe_kernel(h_ref, gate_ref, rank_ref, x_ref, gt_ref, gpost_ref, w1, w3, w2, ws1, ws3, ws2,
                o_ref, acc, xg_s, yhi_s, ylo_s, pick_s, back_s):
    g_idx = pl.program_id(1)

    @pl.when(g_idx == 0)
    def _():
        acc[...] = jnp.zeros_like(acc)

    rank = rank_ref[0]
    gate = gate_ref[0]
    top_rank = jnp.max(rank)
    n_rounds = (top_rank.astype(I32) + MOE_CAP) // MOE_CAP
    slot = lax.broadcasted_iota(I32, (MOE_CAP, 1), 0).astype(F32)
    sub_top = [jnp.max(rank[:, s * MOE_SUB:(s + 1) * MOE_SUB]) for s in range(MOE_NSUB)]
    exp_top = [jnp.max(rank[e:e + 1, :]) for e in range(MOE_EG)]

    def make_pick(s, base):
        toks = slice(s * MOE_SUB, (s + 1) * MOE_SUB)
        return jnp.concatenate(
            [jnp.where(rank[e:e + 1, toks] == slot + base, 1.0, 0.0) for e in range(MOE_EG)], axis=0)

    def expert_rows(e, pick_of):
        rows = slice(e * MOE_CAP, (e + 1) * MOE_CAP)
        x_e = jnp.concatenate([xg_s[s, rows, :] for s in range(MOE_NSUB)], axis=0)
        a = jnp.dot(x_e, w1[e], preferred_element_type=F32)
        hid = a * _sigmoid(a) * jnp.dot(x_e, w3[e], preferred_element_type=F32)
        y = jnp.dot(hid.astype(BF16), w2[e], preferred_element_type=F32)
        for s in range(MOE_NSUB):
            toks = slice(s * MOE_SUB, (s + 1) * MOE_SUB)
            w_row = jnp.sum(pick_of(s)[rows, :] * gate[e:e + 1, toks], axis=1, keepdims=True)
            y_w = y[s * MOE_CAP:(s + 1) * MOE_CAP, :] * w_row
            y_hi = y_w.astype(BF16)
            yhi_s[s, rows, :] = y_hi
            ylo_s[s, rows, :] = (y_w - y_hi.astype(F32)).astype(BF16)

    def scatter(s, back):
        toks = slice(s * MOE_SUB, (s + 1) * MOE_SUB)
        acc[toks, :] += (jnp.dot(back, yhi_s[s], preferred_element_type=F32)
                         + jnp.dot(back, ylo_s[s], preferred_element_type=F32))

    picks = [make_pick(s, 0.0) for s in range(MOE_NSUB)]
    for s in range(MOE_NSUB):
        toks = slice(s * MOE_SUB, (s + 1) * MOE_SUB)
        xg_s[s] = jnp.dot(picks[s].astype(BF16), h_ref[toks, :], preferred_element_type=F32).astype(BF16)
    for e in range(MOE_EG):
        expert_rows(e, lambda s: picks[s])
    for s in range(MOE_NSUB):
        scatter(s, picks[s].T.astype(BF16))

    @pl.when(n_rounds > 1)
    def _():
        for s in range(MOE_NSUB):
            pick_s[s] = picks[s].astype(BF16)

    def overflow_round(r, carry):
        base = (r * MOE_CAP).astype(F32)
        for s in range(MOE_NSUB):
            @pl.when(sub_top[s] >= base)
            def _(s=s):
                pick = make_pick(s, base)
                pick_s[s] = pick.astype(BF16)
                back_s[s] = pick.T.astype(BF16)
                toks = slice(s * MOE_SUB, (s + 1) * MOE_SUB)
                xg_s[s] = jnp.dot(pick.astype(BF16), h_ref[toks, :], preferred_element_type=F32).astype(BF16)
        for e in range(MOE_EG):
            @pl.when(exp_top[e] >= base)
            def _(e=e):
                expert_rows(e, lambda s: pick_s[s].astype(F32))
        for s in range(MOE_NSUB):
            @pl.when(sub_top[s] >= base)
            def _(s=s):
                scatter(s, back_s[s])
        return carry

    lax.fori_loop(1, n_rounds, overflow_round, 0)

    @pl.when(g_idx == N_EXPERTS // MOE_EG - 1)
    def _():
        h = h_ref[...]
        s = jnp.dot(h, ws1[...], preferred_element_type=F32)
        shid = s * _sigmoid(s) * jnp.dot(h, ws3[...], preferred_element_type=F32)
        y_all = acc[...] + jnp.dot(shid.astype(BF16), ws2[...], preferred_element_type=F32)
        o_ref[...] = x_ref[...] + gt_ref[0] * _rms(y_all, gpost_ref[...])


def _moe(h, gates_t, ranks_t, x, gt2, g_post, w1, w3, w2, ws1, ws3, ws2):
    tpb = SEQ // MOE_TM
    n_groups = N_EXPERTS // MOE_EG
    whole = lambda a: pl.BlockSpec(a.shape, lambda i, g: (0, 0))
    per_group = pl.BlockSpec((1, MOE_EG, MOE_TM), lambda i, g: (g, 0, i))
    return pl.pallas_call(
        _moe_kernel,
        grid=(N_TOK // MOE_TM, n_groups),
        in_specs=[pl.BlockSpec((MOE_TM, D_MODEL), lambda i, g: (i, 0)),
                  per_group, per_group,
                  pl.BlockSpec((MOE_TM, D_MODEL), lambda i, g: (i, 0)),
                  pl.BlockSpec((1, 1, D_MODEL), lambda i, g: (i // tpb, 0, 0)),
                  pl.BlockSpec((1, D_MODEL), lambda i, g: (0, 0)),
                  pl.BlockSpec((MOE_EG, D_MODEL, EXPERT_DIM), lambda i, g: (g, 0, 0)),
                  pl.BlockSpec((MOE_EG, D_MODEL, EXPERT_DIM), lambda i, g: (g, 0, 0)),
                  pl.BlockSpec((MOE_EG, EXPERT_DIM, D_MODEL), lambda i, g: (g, 0, 0)),
                  whole(ws1), whole(ws3), whole(ws2)],
        out_specs=pl.BlockSpec((MOE_TM, D_MODEL), lambda i, g: (i, 0)),
        out_shape=jax.ShapeDtypeStruct((N_TOK, D_MODEL), F32),
        scratch_shapes=[pltpu.VMEM((MOE_TM, D_MODEL), F32),
                        pltpu.VMEM((MOE_NSUB, MOE_SLAB, D_MODEL), BF16),
                        pltpu.VMEM((MOE_NSUB, MOE_SLAB, D_MODEL), BF16),
                        pltpu.VMEM((MOE_NSUB, MOE_SLAB, D_MODEL), BF16),
                        pltpu.VMEM((MOE_NSUB, MOE_SLAB, MOE_SUB), BF16),
                        pltpu.VMEM((MOE_NSUB, MOE_SUB, MOE_SLAB), BF16)],
        compiler_params=_params(("arbitrary", "arbitrary")),
        name="moe_ffn",
    )(h, gates_t.reshape(n_groups, MOE_EG, N_TOK), ranks_t.reshape(n_groups, MOE_EG, N_TOK),
      x, gt2, g_post.reshape(1, D_MODEL), w1, w3, w2, ws1, ws3, ws2)


_IN_SIZES = (512, 256, 256, 64, 4, 512, 512, 512, 512, 512, 1024, 1024, 3072)
_IN_DEST = (COL_QA, COL_CKV, COL_IQ, COL_IKIW, COL_IKIW + 64, COL_QB, COL_KB, COL_VB, COL_QC, COL_KC,
            COL_VC, COL_GC, COL_GATES)
_IN_SRC = tuple(int(v) for v in np.cumsum((0,) + _IN_SIZES[:-1]))
N_IN = sum(_IN_SIZES)
PACK_TK = 256


def _pack_kernel(w_ref, o_ref):
    for src, width, dst in zip(_IN_SRC, _IN_SIZES, _IN_DEST):
        o_ref[dst:dst + width, :] = w_ref[src:src + width, :].astype(BF16)
    o_ref[COL_IKIW + 68:COL_IKIW + LANES, :] = jnp.zeros((LANES - 68, PACK_TK), BF16)


def _reorder_w_in(w_t):
    return pl.pallas_call(
        _pack_kernel,
        grid=(D_MODEL // PACK_TK,),
        in_specs=[pl.BlockSpec((N_IN, PACK_TK), lambda i: (0, i))],
        out_specs=pl.BlockSpec((N_PROJ, PACK_TK), lambda i: (0, i)),
        out_shape=jax.ShapeDtypeStruct((N_PROJ, D_MODEL), BF16),
        compiler_params=_params(("arbitrary",)),
        name="pack_w_in",
    )(w_t)


def _rope_tables(positions):
    half = HEAD_DIM // 2
    inv_freq = ROPE_THETA ** (-jnp.arange(half, dtype=F32) / half)
    ang = positions.astype(F32)[:, :, None] * inv_freq
    cos, sin = jnp.cos(ang), jnp.sin(ang)
    cos64 = jnp.concatenate([cos, cos], axis=-1)
    sin64 = jnp.concatenate([-sin, sin], axis=-1)
    cos_a = jnp.concatenate([cos64, cos64], axis=-1)
    sin_a = jnp.concatenate([sin64, sin64], axis=-1)
    cos_b = jnp.concatenate([cos64, jnp.ones_like(cos64)], axis=-1)
    sin_b = jnp.concatenate([sin64, jnp.zeros_like(sin64)], axis=-1)
    return cos_a, sin_a, cos_b, sin_b


def _decay_tables():
    log_g = jnp.log(1.0 - 2.0 ** (-5.0 - jnp.arange(C_HEADS, dtype=F32)))
    pos = jnp.arange(RET_BLK, dtype=F32)
    diff = pos[:, None] - pos[None, :]
    chunk = jnp.arange(RET_BLK) // CHUNK
    visible = chunk[None, :] <= chunk[:, None]
    e_diag = jnp.where(visible, jnp.exp(log_g[:, None, None] * jnp.abs(diff)), 0.0)
    dec = jnp.exp(log_g * RET_BLK)
    per_lane = jnp.repeat(log_g, HEAD_DIM)[None, :]
    q_scale = jnp.exp(per_lane * pos[:, None])
    k_scale = jnp.exp(per_lane * (RET_BLK - pos)[:, None])
    return dec, q_scale, k_scale, e_diag


def kernel(x, c, positions, ada_w, ada_b, g_pre_mix, g_post_mix, g_pre_ffn, g_post_ffn, w_in, g_kv, w_uk, w_uv,
           g_ret, w_o_a, w_o_b, w_o_c, w_out, w_router, router_bias, w1, w3, w2, ws1, ws3, ws2):
    cos_a, sin_a, cos_b, sin_b = _rope_tables(positions)
    dec, q_scale, k_scale, e_diag = _decay_tables()
    xf = x.reshape(N_TOK, D_MODEL)
    for l in range(DEPTH):
        mod = _ada(c, ada_w[l], ada_b[l]).reshape(BATCH, 1, N_ADA * D_MODEL)
        sh1, sc1, gt1, sh2, sc2, gt2 = (mod[:, :, k * D_MODEL:(k + 1) * D_MODEL] for k in range(N_ADA))
        z = _inproj(xf, g_pre_mix[l], sc1, sh1, _reorder_w_in(w_in[l].T))
        z3 = z.reshape(BATCH, SEQ, N_PROJ)
        w_ukv = jnp.concatenate([w_uk[l], w_uv[l]], axis=1).astype(BF16)
        ya = _dsa(z3, cos_a, sin_a, cos_b, sin_b, g_kv[l], w_ukv).reshape(N_TOK, -1)
        yb = _sb(z3).reshape(N_TOK, -1)
        yc = _ret(z3, cos_a, sin_a, dec, q_scale, k_scale, e_diag, g_ret[l]).reshape(N_TOK, -1)
        xf, h, gates, ranks = _merge(ya, yb, yc, z, xf, gt1, g_post_mix[l], g_pre_ffn[l], sc2, sh2,
                              w_o_a[l].astype(BF16), w_o_b[l].astype(BF16), w_o_c[l].astype(BF16),
                              w_out[l].astype(BF16), w_router[l].T.astype(BF16),
                              router_bias[l].reshape(N_EXPERTS, 1))
        xf = _moe(h, gates, ranks, xf, gt2, g_post_ffn[l], w1[l].astype(BF16), w3[l].astype(BF16), w2[l].astype(BF16),
                  ws1[l].astype(BF16), ws3[l].astype(BF16), ws2[l].astype(BF16))
    return xf.reshape(BATCH, SEQ, D_MODEL)
```

```python
import functools
import math

import numpy as np
import jax
import jax.numpy as jnp
from jax import lax
from jax.experimental import pallas as pl
from jax.experimental.pallas import tpu as pltpu

F32 = jnp.float32
BF16 = jnp.bfloat16
I32 = jnp.int32

D_MODEL = 1024
BATCH = 8
SEQ = 2048
DEPTH = 2
N_TOK = BATCH * SEQ

CHUNK = 64
HEAD_DIM = 64
ROPE_THETA = 10000.0
EPS = 1e-6
A_HEADS = 8
A_KV_LORA = 256
IDX_HEADS = 4
IDX_DIM = 64
TOPK = 256
B_HEADS = 8
C_HEADS = 8
C_V_DIM = 128
N_EXPERTS = 64
TOP_K = 8
N_GROUPS = 8
TOPK_GROUPS = 4
EXPERT_DIM = 256
SHARED_DIM = 256
ROUTED_SCALE = 2.5
N_ADA = 6

LANES = 128
VMEM_LIMIT = 56 * 1024 * 1024

COL_GATES = 0
COL_VC = 3072
COL_GC = 4096
COL_QA = 5120
COL_QB = 5632
COL_KB = 6144
COL_VB = 6656
COL_QC = 7168
COL_KC = 7680
COL_CKV = 8192
COL_IQ = 8448
COL_IKIW = 8704
N_PROJ = 8832
PROJ_TN = 2944
PROJ_TM = 512

INT_MIN = -2147483648
NEG_INF_KEY = -2139095041

NT_DIMS = (((1,), (1,)), ((), ()))


def _params(sem):
    return pltpu.CompilerParams(dimension_semantics=sem, vmem_limit_bytes=VMEM_LIMIT)


def _sigmoid(x):
    return 1.0 / (1.0 + jnp.exp(-x))


def _rope_lanes(x, cos, sin_signed):
    n = x.shape[-1]
    lane = lax.broadcasted_iota(I32, (1, n), 1)
    first = (lane & 32) == 0
    ahead = pltpu.roll(x, n - 32, 1)
    behind = pltpu.roll(x, 32, 1)
    return x * cos + jnp.where(first, ahead, behind) * sin_signed


def _ada_kernel(c_ref, w_ref, b_ref, o_ref):
    c = c_ref[...]
    s = (c * _sigmoid(c)).astype(BF16)
    o_ref[...] = jnp.dot(s, w_ref[...].astype(BF16), preferred_element_type=F32) + b_ref[...]


def _ada(c, w, b, layer):
    n = w.shape[2]
    return pl.pallas_call(
        _ada_kernel,
        grid=(n // D_MODEL,),
        in_specs=[pl.BlockSpec((BATCH, D_MODEL), lambda j: (0, 0)),
                  pl.BlockSpec((None, D_MODEL, D_MODEL), lambda j: (layer, 0, j)),
                  pl.BlockSpec((None, 1, D_MODEL), lambda j: (layer, 0, j))],
        out_specs=pl.BlockSpec((BATCH, D_MODEL), lambda j: (0, j)),
        out_shape=jax.ShapeDtypeStruct((BATCH, n), F32),
        compiler_params=_params(("arbitrary",)),
        name="ada_mod",
    )(c, w, b.reshape(DEPTH, 1, n))


def _inproj_kernel(x_ref, g_ref, sc_ref, sh_ref, w_ref, o_ref, h_ref):
    @pl.when(pl.program_id(1) == 0)
    def _():
        x = x_ref[...]
        ms = jnp.mean(x * x, axis=-1, keepdims=True)
        y = x * lax.rsqrt(ms + EPS) * g_ref[...]
        h_ref[...] = (y * (1.0 + sc_ref[0]) + sh_ref[0]).astype(BF16)

    o_ref[...] = lax.dot_general(h_ref[...], w_ref[...], NT_DIMS, preferred_element_type=F32)


def _inproj(x, g, sc, sh, w_t_bf16):
    tiles_per_batch = SEQ // PROJ_TM
    return pl.pallas_call(
        _inproj_kernel,
        grid=(N_TOK // PROJ_TM, N_PROJ // PROJ_TN),
        in_specs=[pl.BlockSpec((PROJ_TM, D_MODEL), lambda i, j: (i, 0)),
                  pl.BlockSpec((1, D_MODEL), lambda i, j: (0, 0)),
                  pl.BlockSpec((1, 1, D_MODEL), lambda i, j: (i // tiles_per_batch, 0, 0)),
                  pl.BlockSpec((1, 1, D_MODEL), lambda i, j: (i // tiles_per_batch, 0, 0)),
                  pl.BlockSpec((PROJ_TN, D_MODEL), lambda i, j: (j, 0))],
        out_specs=pl.BlockSpec((PROJ_TM, PROJ_TN), lambda i, j: (i, j)),
        out_shape=jax.ShapeDtypeStruct((N_TOK, N_PROJ), F32),
        scratch_shapes=[pltpu.VMEM((PROJ_TM, D_MODEL), BF16)],
        compiler_params=_params(("arbitrary", "arbitrary")),
        name="in_proj",
    )(x, g.reshape(1, D_MODEL), sc, sh, w_t_bf16)


DSA_QB = 128
DSA_KC = 256


def _dsa_kernel(ckv_ref, ikiw_ref, q_ref, iq_ref, cos_a, sin_a, cos_b, sin_b, gkv_ref, wukv_ref,
                o_ref, k_lo, k_hi, v_dup, ik_lo, ik_hi, keys_s, selt_s, acc_s, q_s):
    qi = pl.program_id(1)
    lane = lax.broadcasted_iota(I32, (1, LANES), 1)
    lo_half = lane < 64
    n_kc = SEQ // DSA_KC

    @pl.when(qi == 0)
    def _prologue():
        def chunk(c, carry):
            r0 = pl.multiple_of(c * DSA_KC, DSA_KC)
            rows = pl.ds(r0, DSA_KC)
            ckv = ckv_ref[0, rows, :]
            ms = jnp.mean(ckv * ckv, axis=-1, keepdims=True)
            cn = (ckv * lax.rsqrt(ms + EPS) * gkv_ref[...]).astype(BF16)
            kv = jnp.dot(cn, wukv_ref[...], preferred_element_type=F32)
            cb, sb = cos_b[0, rows, :], sin_b[0, rows, :]
            kv = _rope_lanes(kv, cb, sb)
            swapped = pltpu.roll(kv, 64, 1)
            k_lo[rows, :] = jnp.where(lo_half, kv, 0.0).astype(BF16)
            k_hi[rows, :] = jnp.where(lo_half, 0.0, swapped).astype(BF16)
            v_dup[rows, :] = jnp.where(lo_half, swapped, kv).astype(BF16)
            ik = jnp.where(lo_half, _rope_lanes(ikiw_ref[0, rows, :], cb, sb), 0.0)
            ik_lo[rows, :] = ik.astype(BF16)
            ik_hi[rows, :] = pltpu.roll(ik, 64, 1).astype(BF16)
            return carry
        lax.fori_loop(0, n_kc, chunk, 0)

    q0 = pl.multiple_of(qi * DSA_QB, DSA_QB)
    qrows = pl.ds(q0, DSA_QB)
    cq, sq = cos_a[0, qrows, :], sin_a[0, qrows, :]

    iq = iq_ref[0]
    iq_heads = []
    for p in range(IDX_HEADS // 2):
        pair = _rope_lanes(iq[:, p * LANES:(p + 1) * LANES], cq, sq)
        iq_heads.append(jnp.where(lo_half, pair, 0.0).astype(BF16))
        iq_heads.append(jnp.where(lo_half, 0.0, pair).astype(BF16))
    iq_stacked = [jnp.concatenate(iq_heads[parity::2], axis=0) for parity in range(2)]
    iw_t = ikiw_ref[0, qrows, :].T * (IDX_HEADS ** -0.5)
    q_chunk = (q0 + lane) >> 6
    n_live = lax.shift_right_logical(qi + 2, 1)

    def chunk_rows(c):
        return pl.ds(pl.multiple_of(c * DSA_KC, DSA_KC), DSA_KC)

    def score_chunk(c, carry):
        rows = chunk_rows(c)
        k_chunk = (c * DSA_KC + lax.broadcasted_iota(I32, (DSA_KC, 1), 0)) >> 6
        score = jnp.zeros((DSA_KC, DSA_QB), F32)
        for parity, ik in enumerate((ik_lo, ik_hi)):
            logits = lax.dot_general(ik[rows, :], iq_stacked[parity], NT_DIMS,
                                     preferred_element_type=F32) * (IDX_DIM ** -0.5)
            for j in range(IDX_HEADS // 2):
                h = parity + 2 * j
                score = score + iw_t[64 + h:65 + h, :] * jnp.maximum(logits[:, j * DSA_QB:(j + 1) * DSA_QB], 0.0)
        score = jnp.where(score == 0.0, 0.0, score)
        score = jnp.where(k_chunk <= q_chunk, score, -jnp.inf)
        bits = lax.bitcast_convert_type(score, I32)
        keys_s[rows, :] = bits ^ ((bits >> 31) & 0x7FFFFFFF)
        return carry
    lax.fori_loop(0, n_live, score_chunk, 0)

    def count(preds):
        def chunk(c, accs):
            key = keys_s[chunk_rows(c), :]
            return tuple(acc + jnp.sum(pred(key).astype(I32).reshape(4, DSA_KC // 4, DSA_QB), axis=0)
                         for acc, pred in zip(accs, preds))
        zero = jnp.zeros((DSA_KC // 4, DSA_QB), I32)
        accs = lax.fori_loop(0, n_live, chunk, (zero,) * len(preds))
        return [jnp.sum(acc, axis=0, keepdims=True) for acc in accs]

    def descend(it, prefix):
        cand = prefix | lax.shift_left(jnp.int32(1), 31 - it)
        bound = cand ^ INT_MIN
        return jnp.where(count([lambda k: k >= bound])[0] >= TOPK, cand, prefix)
    thr = lax.fori_loop(0, 32, descend, jnp.zeros((1, DSA_QB), I32)) ^ INT_MIN

    n_gt, n_eq = count([lambda k: k > thr, lambda k: k == thr])
    need = (TOPK - n_gt).astype(F32)
    tie_break = jnp.max(n_gt + n_eq) > TOPK

    @pl.when(tie_break)
    def _():
        tri = (lax.broadcasted_iota(I32, (DSA_KC, DSA_KC), 0)
               >= lax.broadcasted_iota(I32, (DSA_KC, DSA_KC), 1)).astype(BF16)

        def select_chunk(c, seen):
            key = keys_s[chunk_rows(c), :]
            eq = key == thr
            rank = jnp.dot(tri, eq.astype(BF16), preferred_element_type=F32) + seen
            sel = ((key > thr) | (eq & (rank <= need))) & (key > NEG_INF_KEY)
            selt_s[c] = sel.astype(F32).T
            return rank[DSA_KC - 1:DSA_KC, :]
        lax.fori_loop(0, n_live, select_chunk, jnp.zeros((1, DSA_QB), F32))

    @pl.when(jnp.logical_not(tie_break))
    def _():
        def select_chunk(c, carry):
            key = keys_s[chunk_rows(c), :]
            selt_s[c] = ((key >= thr) & (key > NEG_INF_KEY)).astype(F32).T
            return carry
        lax.fori_loop(0, n_live, select_chunk, 0)

    q = q_ref[0]
    n_pairs = A_HEADS // 2
    for p in range(n_pairs):
        pair = _rope_lanes(q[:, p * LANES:(p + 1) * LANES], cq, sq) * (HEAD_DIM ** -0.5)
        q_s[p * DSA_QB:(p + 1) * DSA_QB, :] = jnp.where(lo_half, pair, 0.0).astype(BF16)
        q_s[(n_pairs + p) * DSA_QB:(n_pairs + p + 1) * DSA_QB, :] = jnp.where(lo_half, 0.0, pair).astype(BF16)
    masked = -1e30
    half = n_pairs * DSA_QB

    def attend(c, carry):
        m, l = carry
        rows = chunk_rows(c)
        bias = jnp.where(selt_s[c] > 0.0, 0.0, masked)
        s = jnp.concatenate(
            [lax.dot_general(q_s[:half, :], k_lo[rows, :], NT_DIMS, preferred_element_type=F32),
             lax.dot_general(q_s[half:, :], k_hi[rows, :], NT_DIMS, preferred_element_type=F32)], axis=0)
        s = s.reshape(A_HEADS, DSA_QB, DSA_KC) + bias[None]
        m_new = jnp.maximum(m, jnp.max(s, axis=-1, keepdims=True))
        pe = jnp.exp(s - m_new)
        alpha = jnp.exp(m - m_new)
        pv = jnp.dot(pe.reshape(A_HEADS * DSA_QB, DSA_KC).astype(BF16), v_dup[rows, :],
                     preferred_element_type=F32)
        acc_s[...] = alpha * acc_s[...] + pv.reshape(A_HEADS, DSA_QB, LANES)
        return m_new, alpha * l + jnp.sum(pe, axis=-1, keepdims=True)

    acc_s[...] = jnp.zeros_like(acc_s)
    _, denom = lax.fori_loop(0, n_live, attend, (jnp.full((A_HEADS, DSA_QB, 1), masked, F32),
                                                 jnp.zeros((A_HEADS, DSA_QB, 1), F32)))
    for p in range(n_pairs):
        even = acc_s[p] / denom[p]
        odd = acc_s[n_pairs + p] / denom[n_pairs + p]
        o_ref[0, :, p * LANES:(p + 1) * LANES] = jnp.where(lo_half, even, odd)


def _dsa(z3, cos_a, sin_a, cos_b, sin_b, g_kv, w_ukv_bf16):
    full = lambda width, col: pl.BlockSpec((1, SEQ, width), lambda b, i: (b, 0, col // width))
    blk = lambda width, col: pl.BlockSpec((1, DSA_QB, width), lambda b, i: (b, i, col // width))
    tab = pl.BlockSpec((1, SEQ, LANES), lambda b, i: (b, 0, 0))
    return pl.pallas_call(
        _dsa_kernel,
        grid=(BATCH, SEQ // DSA_QB),
        in_specs=[full(A_KV_LORA, COL_CKV), full(LANES, COL_IKIW),
                  blk(A_HEADS * HEAD_DIM, COL_QA), blk(IDX_HEADS * IDX_DIM, COL_IQ),
                  tab, tab, tab, tab,
                  pl.BlockSpec((1, A_KV_LORA), lambda b, i: (0, 0)),
                  pl.BlockSpec((A_KV_LORA, LANES), lambda b, i: (0, 0))],
        out_specs=pl.BlockSpec((1, DSA_QB, A_HEADS * HEAD_DIM), lambda b, i: (b, i, 0)),
        out_shape=jax.ShapeDtypeStruct((BATCH, SEQ, A_HEADS * HEAD_DIM), F32),
        scratch_shapes=[pltpu.VMEM((SEQ, LANES), BF16)] * 5
                       + [pltpu.VMEM((SEQ, DSA_QB), I32),
                          pltpu.VMEM((SEQ // DSA_KC, DSA_QB, DSA_KC), F32),
                          pltpu.VMEM((A_HEADS, DSA_QB, LANES), F32),
                          pltpu.VMEM((A_HEADS * DSA_QB, LANES), BF16)],
        compiler_params=_params(("arbitrary", "arbitrary")),
        name="dsa_mixer",
    )(z3, z3, z3, z3, cos_a, sin_a, cos_b, sin_b, g_kv.reshape(1, A_KV_LORA), w_ukv_bf16)


SB_BLK = 128
SB_EXP_UNDERFLOW = -105.0


def _sb_kernel(q_ref, k_ref, v_ref, o_ref, q_s, tail_s, acc_s):
    qi = pl.program_id(1)
    n_pairs = B_HEADS // 2
    rows_all = B_HEADS * SB_BLK
    lane = lax.broadcasted_iota(I32, (1, LANES), 1)
    lo_half = lane < 64
    causal = (lax.broadcasted_iota(I32, (rows_all, SB_BLK), 1)
              < (lax.broadcasted_iota(I32, (rows_all, SB_BLK), 0) & (SB_BLK - 1)))
    r2 = lax.broadcasted_iota(I32, (2 * SB_BLK, 2 * SB_BLK), 0) & (SB_BLK - 1)
    c2 = lax.broadcasted_iota(I32, (2 * SB_BLK, 2 * SB_BLK), 1)
    suffix_mat = ((c2 >= SB_BLK) | (r2 > c2)).astype(BF16)

    for p in range(n_pairs):
        q2 = q_ref[0, :, p * LANES:(p + 1) * LANES] * (HEAD_DIM ** -0.5)
        q_s[2 * p * SB_BLK:(2 * p + 1) * SB_BLK, :] = jnp.where(lo_half, q2, 0.0).astype(BF16)
        q_s[(2 * p + 1) * SB_BLK:(2 * p + 2) * SB_BLK, :] = jnp.where(lo_half, 0.0, q2).astype(BF16)

    def key_block(kb, diagonal):
        r0 = pl.multiple_of(kb * SB_BLK, SB_BLK)
        pair_rows = lambda p: slice(2 * p * SB_BLK, (2 * p + 2) * SB_BLK)
        z = jnp.concatenate(
            [lax.dot_general(q_s[pair_rows(p), :], k_ref[0, pl.ds(r0, SB_BLK), p * LANES:(p + 1) * LANES].astype(BF16),
                             NT_DIMS, preferred_element_type=F32) for p in range(n_pairs)], axis=0)
        log_beta = jnp.minimum(z, 0.0) - jnp.log(1.0 + jnp.exp(-jnp.abs(z)))
        log_not = log_beta - z
        if diagonal:
            log_not = jnp.where(causal, log_not, 0.0)
        hi = log_not.astype(BF16)
        lo = (log_not - hi.astype(F32)).astype(BF16)
        t = jnp.dot(jnp.concatenate([hi, lo], axis=1), suffix_mat, preferred_element_type=F32)
        if diagonal:
            a = jnp.where(causal, jnp.exp(log_beta + t[:, :SB_BLK]), 0.0)
            tail = t[:, SB_BLK:]
        else:
            a = jnp.exp(log_beta + t[:, :SB_BLK] + tail_s[...])
            tail = tail_s[...] + t[:, SB_BLK:]
        tail_s[...] = tail
        a = a.astype(BF16)
        for p in range(n_pairs):
            v2 = v_ref[0, pl.ds(r0, SB_BLK), p * LANES:(p + 1) * LANES].astype(BF16)
            pv = jnp.dot(a[pair_rows(p), :], v2, preferred_element_type=F32)
            if diagonal:
                acc_s[pair_rows(p), :] = pv
            else:
                acc_s[pair_rows(p), :] += pv
        return jnp.max(tail)

    first = key_block(qi, True)

    def more(state):
        jj, worst = state
        return (jj <= qi) & (worst > SB_EXP_UNDERFLOW)

    def step(state):
        jj, _ = state
        return jj + 1, key_block(qi - jj, False)

    lax.while_loop(more, step, (jnp.int32(1), first))
    for p in range(n_pairs):
        even = acc_s[2 * p * SB_BLK:(2 * p + 1) * SB_BLK, :]
        odd = acc_s[(2 * p + 1) * SB_BLK:(2 * p + 2) * SB_BLK, :]
        o_ref[0, :, p * LANES:(p + 1) * LANES] = jnp.where(lo_half, even, odd)


def _sb(z3):
    width = B_HEADS * HEAD_DIM
    return pl.pallas_call(
        _sb_kernel,
        grid=(BATCH, SEQ // SB_BLK),
        in_specs=[pl.BlockSpec((1, SB_BLK, width), lambda b, i: (b, i, COL_QB // width)),
                  pl.BlockSpec((1, SEQ, width), lambda b, i: (b, 0, COL_KB // width)),
                  pl.BlockSpec((1, SEQ, width), lambda b, i: (b, 0, COL_VB // width))],
        out_specs=pl.BlockSpec((1, SB_BLK, width), lambda b, i: (b, i, 0)),
        out_shape=jax.ShapeDtypeStruct((BATCH, SEQ, width), F32),
        scratch_shapes=[pltpu.VMEM((B_HEADS * SB_BLK, LANES), BF16),
                        pltpu.VMEM((B_HEADS * SB_BLK, SB_BLK), F32),
                        pltpu.VMEM((B_HEADS * SB_BLK, SB_BLK), F32)],
        compiler_params=_params(("arbitrary", "arbitrary")),
        name="stickbreak_mixer",
    )(z3, z3, z3)


RET_BLK = 256


def _ret_kernel(dec_ref, q_ref, k_ref, v_ref, gate_ref, cos_a, sin_a, qs_ref, ks_ref, e_diag, gret_ref,
                o_ref, state):
    lane = lax.broadcasted_iota(I32, (1, LANES), 1)
    lo_half = lane < 64

    @pl.when(pl.program_id(1) == 0)
    def _():
        state[...] = jnp.zeros_like(state)

    cq, sq = cos_a[0], sin_a[0]
    for p in range(C_HEADS // 2):
        cols = slice(p * LANES, (p + 1) * LANES)
        q2 = _rope_lanes(q_ref[0, :, cols], cq, sq)
        k2 = _rope_lanes(k_ref[0, :, cols], cq, sq) * (HEAD_DIM ** -0.5)
        q_hat = q2 * qs_ref[:, cols]
        k_hat = k2 * ks_ref[:, cols]
        for e in range(2):
            h = 2 * p + e
            pick = (lambda x: jnp.where(lo_half, x, 0.0)) if e == 0 else (lambda x: jnp.where(lo_half, 0.0, x))
            vcols = slice(h * C_V_DIM, (h + 1) * C_V_DIM)
            v2 = v_ref[0, :, vcols].astype(BF16)
            s = lax.dot_general(pick(q2).astype(BF16), k2.astype(BF16), NT_DIMS,
                                preferred_element_type=F32) * e_diag[h]
            o = jnp.dot(s.astype(BF16), v2, preferred_element_type=F32)
            o = o + jnp.dot(pick(q_hat).astype(BF16), state[h].astype(BF16), preferred_element_type=F32)
            kv = jnp.dot(pick(k_hat).T.astype(BF16), v2, preferred_element_type=F32)
            state[h] = dec_ref[h] * state[h] + kv
            mu = jnp.mean(o, axis=-1, keepdims=True)
            d = o - mu
            var = jnp.mean(d * d, axis=-1, keepdims=True)
            y = d * lax.rsqrt(var + EPS) * gret_ref[:, vcols]
            g = gate_ref[0, :, vcols]
            o_ref[0, :, vcols] = g * _sigmoid(g) * y


def _ret(z3, cos_a, sin_a, dec, q_scale, k_scale, e_diag, g_ret):
    qk_w = C_HEADS * HEAD_DIM
    v_w = C_HEADS * C_V_DIM
    tab = pl.BlockSpec((1, RET_BLK, LANES), lambda b, i: (b, i, 0))
    const2 = lambda a: pl.BlockSpec(a.shape, lambda b, i: (0, 0))
    return pl.pallas_call(
        _ret_kernel,
        grid=(BATCH, SEQ // RET_BLK),
        in_specs=[pl.BlockSpec(memory_space=pltpu.SMEM),
                  pl.BlockSpec((1, RET_BLK, qk_w), lambda b, i: (b, i, COL_QC // qk_w)),
                  pl.BlockSpec((1, RET_BLK, qk_w), lambda b, i: (b, i, COL_KC // qk_w)),
                  pl.BlockSpec((1, RET_BLK, v_w), lambda b, i: (b, i, COL_VC // v_w)),
                  pl.BlockSpec((1, RET_BLK, v_w), lambda b, i: (b, i, COL_GC // v_w)),
                  tab, tab, const2(q_scale), const2(k_scale),
                  pl.BlockSpec(e_diag.shape, lambda b, i: (0, 0, 0)),
                  pl.BlockSpec((1, v_w), lambda b, i: (0, 0))],
        out_specs=pl.BlockSpec((1, RET_BLK, v_w), lambda b, i: (b, i, 0)),
        out_shape=jax.ShapeDtypeStruct((BATCH, SEQ, v_w), F32),
        scratch_shapes=[pltpu.VMEM((C_HEADS, LANES, C_V_DIM), F32)],
        compiler_params=_params(("arbitrary", "arbitrary")),
        name="retention_mixer",
    )(dec, z3, z3, z3, z3, cos_a, sin_a, q_scale, k_scale, e_diag, g_ret.reshape(1, v_w))


MERGE_TM = 256


def _rms(y, g):
    ms = jnp.mean(y * y, axis=-1, keepdims=True)
    return y * lax.rsqrt(ms + EPS) * g


def _route(logits_t, bias_col):
    n_tok = logits_t.shape[1]
    per_group = N_EXPERTS // N_GROUPS
    scores = _sigmoid(logits_t)
    sel = scores + bias_col
    sel3 = sel.reshape(N_GROUPS, per_group, n_tok)
    member = lax.broadcasted_iota(I32, (N_GROUPS, per_group, n_tok), 1)
    m1 = jnp.max(sel3, axis=1, keepdims=True)
    first = jnp.min(jnp.where(sel3 == m1, member, per_group), axis=1, keepdims=True)
    m2 = jnp.max(jnp.where(member == first, -jnp.inf, sel3), axis=1, keepdims=True)
    grp = (m1 + m2).reshape(N_GROUPS, n_tok)
    gid = lax.broadcasted_iota(I32, (N_GROUPS, n_tok), 0)
    rank = jnp.zeros((N_GROUPS, n_tok), I32)
    for g in range(N_GROUPS):
        other = grp[g:g + 1, :]
        rank = rank + ((other > grp) | ((other == grp) & (g < gid))).astype(I32)
    keep = jnp.where(rank < TOPK_GROUPS, 1.0, 0.0).reshape(N_GROUPS, 1, n_tok)
    live = jnp.where(keep > 0.0, sel3, -jnp.inf).reshape(N_EXPERTS, n_tok)
    eid = lax.broadcasted_iota(I32, (N_EXPERTS, n_tok), 0)
    chosen = jnp.zeros((N_EXPERTS, n_tok), jnp.bool_)
    for _ in range(TOP_K):
        m = jnp.max(live, axis=0, keepdims=True)
        pick = eid == jnp.min(jnp.where(live == m, eid, N_EXPERTS), axis=0, keepdims=True)
        chosen = chosen | pick
        live = jnp.where(pick, -jnp.inf, live)
    w = jnp.where(chosen, scores, 0.0)
    return w / jnp.sum(w, axis=0, keepdims=True) * ROUTED_SCALE, chosen


def _merge_kernel(ya_ref, yb_ref, yc_ref, g_ref, x_ref, gt_ref, gpost_ref, gpre_ref, sc_ref, sh_ref,
                  woa, wob, woc, wout, wr_t, rb_ref, xo_ref, h_ref, gate_ref, rank_ref):
    ya = jnp.dot(ya_ref[...].astype(BF16), woa[...], preferred_element_type=F32)
    yb = jnp.dot(yb_ref[...].astype(BF16), wob[...], preferred_element_type=F32)
    yc = jnp.dot(yc_ref[...].astype(BF16), woc[...], preferred_element_type=F32)
    merged = (_sigmoid(g_ref[:, 0:D_MODEL]) * ya
              + _sigmoid(g_ref[:, D_MODEL:2 * D_MODEL]) * yb
              + _sigmoid(g_ref[:, 2 * D_MODEL:3 * D_MODEL]) * yc)
    y = jnp.dot(merged.astype(BF16), wout[...], preferred_element_type=F32)
    x = x_ref[...] + gt_ref[0] * _rms(y, gpost_ref[...])
    xo_ref[...] = x
    h = (_rms(x, gpre_ref[...]) * (1.0 + sc_ref[0]) + sh_ref[0]).astype(BF16)
    h_ref[...] = h
    logits_t = lax.dot_general(wr_t[...], h, NT_DIMS, preferred_element_type=F32)
    gates_t, chosen = _route(logits_t, rb_ref[...])
    gate_ref[...] = gates_t
    earlier = (lax.broadcasted_iota(I32, (MERGE_TM, MERGE_TM), 0)
               < lax.broadcasted_iota(I32, (MERGE_TM, MERGE_TM), 1)).astype(BF16)
    before = jnp.dot(jnp.where(chosen, 1.0, 0.0).astype(BF16), earlier, preferred_element_type=F32)
    rank_ref[...] = jnp.where(chosen, before, -1.0)


def _merge(ya, yb, yc, z, x, gt1, g_post, g_pre, sc2, sh2, woa, wob, woc, wout, wr_t, rbias):
    tpb = SEQ // MERGE_TM
    row = lambda w: pl.BlockSpec((MERGE_TM, w), lambda i: (i, 0))
    vec = pl.BlockSpec((1, D_MODEL), lambda i: (0, 0))
    mod = pl.BlockSpec((1, 1, D_MODEL), lambda i: (i // tpb, 0, 0))
    whole = lambda a: pl.BlockSpec(a.shape, lambda i: (0, 0))
    return pl.pallas_call(
        _merge_kernel,
        grid=(N_TOK // MERGE_TM,),
        in_specs=[row(512), row(512), row(1024), pl.BlockSpec((MERGE_TM, 3 * D_MODEL), lambda i: (i, 0)),
                  row(D_MODEL), mod, vec, vec, mod, mod,
                  whole(woa), whole(wob), whole(woc), whole(wout), whole(wr_t), whole(rbias)],
        out_specs=[row(D_MODEL), row(D_MODEL),
                   pl.BlockSpec((N_EXPERTS, MERGE_TM), lambda i: (0, i)),
                   pl.BlockSpec((N_EXPERTS, MERGE_TM), lambda i: (0, i))],
        out_shape=[jax.ShapeDtypeStruct((N_TOK, D_MODEL), F32),
                   jax.ShapeDtypeStruct((N_TOK, D_MODEL), BF16),
                   jax.ShapeDtypeStruct((N_EXPERTS, N_TOK), F32),
                   jax.ShapeDtypeStruct((N_EXPERTS, N_TOK), F32)],
        compiler_params=_params(("arbitrary",)),
        name="merge_route",
    )(ya, yb, yc, z, x, gt1, g_post.reshape(1, D_MODEL), g_pre.reshape(1, D_MODEL), sc2, sh2,
      woa, wob, woc, wout, wr_t, rbias)


MOE_TM = 1024
MOE_SUB = MERGE_TM
MOE_EG = 4
MOE_CAP = 64
MOE_NSUB = MOE_TM // MOE_SUB
MOE_SLAB = MOE_EG * MOE_CAP


def _moe_kernel(h_ref, gate_ref, rank_ref, x_ref, gt_ref, gpost_ref, w1, w3, w2, ws1, ws3, ws2,
                o_ref, acc, xg_s, yw_s, pick_s, back_s):
    g_idx = pl.program_id(1)

    @pl.when(g_idx == 0)
    def _():
        acc[...] = jnp.zeros_like(acc)

    rank = rank_ref[0]
    gate = gate_ref[0]
    top_rank = jnp.max(rank)
    n_rounds = (top_rank.astype(I32) + MOE_CAP) // MOE_CAP
    slot = lax.broadcasted_iota(I32, (MOE_CAP, 1), 0).astype(F32)
    sub_top = [jnp.max(rank[:, s * MOE_SUB:(s + 1) * MOE_SUB]) for s in range(MOE_NSUB)]
    exp_top = [jnp.max(rank[e:e + 1, :]) for e in range(MOE_EG)]

    def make_pick(s, base):
        toks = slice(s * MOE_SUB, (s + 1) * MOE_SUB)
        return jnp.concatenate(
            [jnp.where(rank[e:e + 1, toks] == slot + base, 1.0, 0.0) for e in range(MOE_EG)], axis=0)

    def expert_rows(e, pick_of):
        rows = slice(e * MOE_CAP, (e + 1) * MOE_CAP)
        x_e = jnp.concatenate([xg_s[s, rows, :] for s in range(MOE_NSUB)], axis=0)
        a = jnp.dot(x_e, w1[e], preferred_element_type=F32)
        hid = a * _sigmoid(a) * jnp.dot(x_e, w3[e], preferred_element_type=F32)
        y = jnp.dot(hid.astype(BF16), w2[e], preferred_element_type=F32)
        for s in range(MOE_NSUB):
            toks = slice(s * MOE_SUB, (s + 1) * MOE_SUB)
            w_row = jnp.sum(pick_of(s)[rows, :] * gate[e:e + 1, toks], axis=1, keepdims=True)
            yw_s[s, rows, :] = (y[s * MOE_CAP:(s + 1) * MOE_CAP, :] * w_row).astype(BF16)

    def scatter(s, back):
        toks = slice(s * MOE_SUB, (s + 1) * MOE_SUB)
        acc[toks, :] += jnp.dot(back, yw_s[s], preferred_element_type=F32)

    picks = [make_pick(s, 0.0) for s in range(MOE_NSUB)]
    for s in range(MOE_NSUB):
        toks = slice(s * MOE_SUB, (s + 1) * MOE_SUB)
        xg_s[s] = jnp.dot(picks[s].astype(BF16), h_ref[toks, :], preferred_element_type=F32).astype(BF16)
    for e in range(MOE_EG):
        expert_rows(e, lambda s: picks[s])
    for s in range(MOE_NSUB):
        scatter(s, picks[s].T.astype(BF16))

    @pl.when(n_rounds > 1)
    def _():
        for s in range(MOE_NSUB):
            pick_s[s] = picks[s].astype(BF16)

    def overflow_round(r, carry):
        base = (r * MOE_CAP).astype(F32)
        for s in range(MOE_NSUB):
            @pl.when(sub_top[s] >= base)
            def _(s=s):
                pick = make_pick(s, base)
                pick_s[s] = pick.astype(BF16)
                back_s[s] = pick.T.astype(BF16)
                toks = slice(s * MOE_SUB, (s + 1) * MOE_SUB)
                xg_s[s] = jnp.dot(pick.astype(BF16), h_ref[toks, :], preferred_element_type=F32).astype(BF16)
        for e in range(MOE_EG):
            @pl.when(exp_top[e] >= base)
            def _(e=e):
                expert_rows(e, lambda s: pick_s[s].astype(F32))
        for s in range(MOE_NSUB):
            @pl.when(sub_top[s] >= base)
            def _(s=s):
                scatter(s, back_s[s])
        return carry

    lax.fori_loop(1, n_rounds, overflow_round, 0)

    @pl.when(g_idx == N_EXPERTS // MOE_EG - 1)
    def _():
        h = h_ref[...]
        s = jnp.dot(h, ws1[...], preferred_element_type=F32)
        shid = s * _sigmoid(s) * jnp.dot(h, ws3[...], preferred_element_type=F32)
        y_all = acc[...] + jnp.dot(shid.astype(BF16), ws2[...], preferred_element_type=F32)
        o_ref[...] = x_ref[...] + gt_ref[0] * _rms(y_all, gpost_ref[...])


def _moe(h, gates_t, ranks_t, x, gt2, g_post, w1, w3, w2, ws1, ws3, ws2, layer):
    tpb = SEQ // MOE_TM
    n_groups = N_EXPERTS // MOE_EG
    whole = lambda a: pl.BlockSpec(a.shape, lambda i, g: (0, 0))
    per_group = pl.BlockSpec((1, MOE_EG, MOE_TM), lambda i, g: (g, 0, i))
    return pl.pallas_call(
        _moe_kernel,
        grid=(N_TOK // MOE_TM, n_groups),
        in_specs=[pl.BlockSpec((MOE_TM, D_MODEL), lambda i, g: (i, 0)),
                  per_group, per_group,
                  pl.BlockSpec((MOE_TM, D_MODEL), lambda i, g: (i, 0)),
                  pl.BlockSpec((1, 1, D_MODEL), lambda i, g: (i // tpb, 0, 0)),
                  pl.BlockSpec((1, D_MODEL), lambda i, g: (0, 0)),
                  pl.BlockSpec((None, MOE_EG, D_MODEL, EXPERT_DIM), lambda i, g: (layer, g, 0, 0)),
                  pl.BlockSpec((None, MOE_EG, D_MODEL, EXPERT_DIM), lambda i, g: (layer, g, 0, 0)),
                  pl.BlockSpec((None, MOE_EG, EXPERT_DIM, D_MODEL), lambda i, g: (layer, g, 0, 0)),
                  whole(ws1), whole(ws3), whole(ws2)],
        out_specs=pl.BlockSpec((MOE_TM, D_MODEL), lambda i, g: (i, 0)),
        out_shape=jax.ShapeDtypeStruct((N_TOK, D_MODEL), F32),
        scratch_shapes=[pltpu.VMEM((MOE_TM, D_MODEL), F32),
                        pltpu.VMEM((MOE_NSUB, MOE_SLAB, D_MODEL), BF16),
                        pltpu.VMEM((MOE_NSUB, MOE_SLAB, D_MODEL), BF16),
                        pltpu.VMEM((MOE_NSUB, MOE_SLAB, MOE_SUB), BF16),
                        pltpu.VMEM((MOE_NSUB, MOE_SUB, MOE_SLAB), BF16)],
        compiler_params=_params(("arbitrary", "arbitrary")),
        name="moe_ffn",
    )(h, gates_t.reshape(n_groups, MOE_EG, N_TOK), ranks_t.reshape(n_groups, MOE_EG, N_TOK),
      x, gt2, g_post.reshape(1, D_MODEL), w1, w3, w2, ws1, ws3, ws2)


_IN_SIZES = (512, 256, 256, 64, 4, 512, 512, 512, 512, 512, 1024, 1024, 3072)
_IN_DEST = (COL_QA, COL_CKV, COL_IQ, COL_IKIW, COL_IKIW + 64, COL_QB, COL_KB, COL_VB, COL_QC, COL_KC,
            COL_VC, COL_GC, COL_GATES)
_IN_SRC = tuple(int(v) for v in np.cumsum((0,) + _IN_SIZES[:-1]))
N_IN = sum(_IN_SIZES)
PACK_TK = 128


def _pack_kernel(w_ref, o_ref, *, layer):
    for src, width, dst in zip(_IN_SRC, _IN_SIZES, _IN_DEST):
        o_ref[dst:dst + width, :] = w_ref[src:src + width, layer, :].astype(BF16)
    o_ref[COL_IKIW + 68:COL_IKIW + LANES, :] = jnp.zeros((LANES - 68, PACK_TK), BF16)


def _reorder_w_in(w_t, layer):
    return pl.pallas_call(
        functools.partial(_pack_kernel, layer=layer),
        grid=(D_MODEL // PACK_TK,),
        in_specs=[pl.BlockSpec((N_IN, DEPTH, PACK_TK), lambda i: (0, 0, i))],
        out_specs=pl.BlockSpec((N_PROJ, PACK_TK), lambda i: (0, i)),
        out_shape=jax.ShapeDtypeStruct((N_PROJ, D_MODEL), BF16),
        compiler_params=_params(("arbitrary",)),
        name="pack_w_in",
    )(w_t)


def _rope_tables(positions):
    half = HEAD_DIM // 2
    inv_freq = ROPE_THETA ** (-jnp.arange(half, dtype=F32) / half)
    ang = positions.astype(F32)[:, :, None] * inv_freq
    cos, sin = jnp.cos(ang), jnp.sin(ang)
    cos64 = jnp.concatenate([cos, cos], axis=-1)
    sin64 = jnp.concatenate([-sin, sin], axis=-1)
    cos_a = jnp.concatenate([cos64, cos64], axis=-1)
    sin_a = jnp.concatenate([sin64, sin64], axis=-1)
    cos_b = jnp.concatenate([cos64, jnp.ones_like(cos64)], axis=-1)
    sin_b = jnp.concatenate([sin64, jnp.zeros_like(sin64)], axis=-1)
    return cos_a, sin_a, cos_b, sin_b


def _decay_tables():
    log_g = jnp.log(1.0 - 2.0 ** (-5.0 - jnp.arange(C_HEADS, dtype=F32)))
    pos = jnp.arange(RET_BLK, dtype=F32)
    diff = pos[:, None] - pos[None, :]
    chunk = jnp.arange(RET_BLK) // CHUNK
    visible = chunk[None, :] <= chunk[:, None]
    e_diag = jnp.where(visible, jnp.exp(log_g[:, None, None] * jnp.abs(diff)), 0.0)
    dec = jnp.exp(log_g * RET_BLK)
    per_lane = jnp.repeat(log_g, HEAD_DIM)[None, :]
    q_scale = jnp.exp(per_lane * pos[:, None])
    k_scale = jnp.exp(per_lane * (RET_BLK - pos)[:, None])
    return dec, q_scale, k_scale, e_diag


def kernel(x, c, positions, ada_w, ada_b, g_pre_mix, g_post_mix, g_pre_ffn, g_post_ffn, w_in, g_kv, w_uk, w_uv,
           g_ret, w_o_a, w_o_b, w_o_c, w_out, w_router, router_bias, w1, w3, w2, ws1, ws3, ws2):
    cos_a, sin_a, cos_b, sin_b = _rope_tables(positions)
    dec, q_scale, k_scale, e_diag = _decay_tables()
    xf = x.reshape(N_TOK, D_MODEL)
    w_in_t = jnp.transpose(w_in, (2, 0, 1))
    w1_b, w3_b, w2_b = w1.astype(BF16), w3.astype(BF16), w2.astype(BF16)
    for l in range(DEPTH):
        mod = _ada(c, ada_w, ada_b, l).reshape(BATCH, 1, N_ADA * D_MODEL)
        sh1, sc1, gt1, sh2, sc2, gt2 = (mod[:, :, k * D_MODEL:(k + 1) * D_MODEL] for k in range(N_ADA))
        z = _inproj(xf, g_pre_mix[l], sc1, sh1, _reorder_w_in(w_in_t, l))
        z3 = z.reshape(BATCH, SEQ, N_PROJ)
        w_ukv = jnp.concatenate([w_uk[l], w_uv[l]], axis=1).astype(BF16)
        ya = _dsa(z3, cos_a, sin_a, cos_b, sin_b, g_kv[l], w_ukv).reshape(N_TOK, -1)
        yb = _sb(z3).reshape(N_TOK, -1)
        yc = _ret(z3, cos_a, sin_a, dec, q_scale, k_scale, e_diag, g_ret[l]).reshape(N_TOK, -1)
        xf, h, gates, ranks = _merge(ya, yb, yc, z, xf, gt1, g_post_mix[l], g_pre_ffn[l], sc2, sh2,
                              w_o_a[l].astype(BF16), w_o_b[l].astype(BF16), w_o_c[l].astype(BF16),
                              w_out[l].astype(BF16), w_router[l].T.astype(BF16),
                              router_bias[l].reshape(N_EXPERTS, 1))
        xf = _moe(h, gates, ranks, xf, gt2, g_post_ffn[l], w1_b, w3_b, w2_b,
                  ws1[l].astype(BF16), ws3[l].astype(BF16), ws2[l].astype(BF16), l)
    return xf.reshape(BATCH, SEQ, D_MODEL)
```

```python
import functools
import math

import numpy as np
import jax
import jax.numpy as jnp
from jax import lax
from jax.experimental import pallas as pl
from jax.experimental.pallas import tpu as pltpu

F32 = jnp.float32
BF16 = jnp.bfloat16
I32 = jnp.int32

D_MODEL = 1024
BATCH = 8
SEQ = 2048
DEPTH = 2
N_TOK = BATCH * SEQ

CHUNK = 64
HEAD_DIM = 64
ROPE_THETA = 10000.0
EPS = 1e-6
A_HEADS = 8
A_KV_LORA = 256
IDX_HEADS = 4
IDX_DIM = 64
TOPK = 256
B_HEADS = 8
C_HEADS = 8
C_V_DIM = 128
N_EXPERTS = 64
TOP_K = 8
N_GROUPS = 8
TOPK_GROUPS = 4
EXPERT_DIM = 256
SHARED_DIM = 256
ROUTED_SCALE = 2.5
N_ADA = 6

LANES = 128
VMEM_LIMIT = 56 * 1024 * 1024

COL_GATES = 0
COL_VC = 3072
COL_GC = 4096
COL_QA = 5120
COL_QB = 5632
COL_KB = 6144
COL_VB = 6656
COL_QC = 7168
COL_KC = 7680
COL_CKV = 8192
COL_IQ = 8448
COL_IKIW = 8704
N_PROJ = 8832
PROJ_TN = 2944
PROJ_TM = 1024

INT_MIN = -2147483648
NEG_INF_KEY = -2139095041

NT_DIMS = (((1,), (1,)), ((), ()))
LOG2_E = math.log2(math.e)


def _params(sem):
    return pltpu.CompilerParams(dimension_semantics=sem, vmem_limit_bytes=VMEM_LIMIT)


def _sigmoid(x):
    return 1.0 / (1.0 + jnp.exp(-x))


def _rope_lanes(x, cos, sin_signed):
    n = x.shape[-1]
    lane = lax.broadcasted_iota(I32, (1, n), 1)
    first = (lane & 32) == 0
    ahead = pltpu.roll(x, n - 32, 1)
    behind = pltpu.roll(x, 32, 1)
    return x * cos + jnp.where(first, ahead, behind) * sin_signed


def _ada_kernel(c_ref, w_ref, b_ref, o_ref):
    c = c_ref[...]
    s = (c * _sigmoid(c)).astype(BF16)
    o_ref[...] = jnp.dot(s, w_ref[...].astype(BF16), preferred_element_type=F32) + b_ref[...]


def _ada(c, w, b, layer):
    n = w.shape[2]
    return pl.pallas_call(
        _ada_kernel,
        grid=(n // D_MODEL,),
        in_specs=[pl.BlockSpec((BATCH, D_MODEL), lambda j: (0, 0)),
                  pl.BlockSpec((None, D_MODEL, D_MODEL), lambda j: (layer, 0, j)),
                  pl.BlockSpec((None, 1, D_MODEL), lambda j: (layer, 0, j))],
        out_specs=pl.BlockSpec((BATCH, D_MODEL), lambda j: (0, j)),
        out_shape=jax.ShapeDtypeStruct((BATCH, n), F32),
        compiler_params=_params(("arbitrary",)),
        name="ada_mod",
    )(c, w, b.reshape(DEPTH, 1, n))


def _inproj_kernel(x_ref, g_ref, sc_ref, sh_ref, w_ref, o_ref, h_ref):
    @pl.when(pl.program_id(1) == 0)
    def _():
        x = x_ref[...]
        ms = jnp.mean(x * x, axis=-1, keepdims=True)
        y = x * lax.rsqrt(ms + EPS) * g_ref[...]
        h_ref[...] = (y * (1.0 + sc_ref[0]) + sh_ref[0]).astype(BF16)

    o_ref[...] = lax.dot_general(h_ref[...], w_ref[...], NT_DIMS, preferred_element_type=F32)


def _inproj(x, g, sc, sh, w_t_bf16):
    tiles_per_batch = SEQ // PROJ_TM
    return pl.pallas_call(
        _inproj_kernel,
        grid=(N_TOK // PROJ_TM, N_PROJ // PROJ_TN),
        in_specs=[pl.BlockSpec((PROJ_TM, D_MODEL), lambda i, j: (i, 0)),
                  pl.BlockSpec((1, D_MODEL), lambda i, j: (0, 0)),
                  pl.BlockSpec((1, 1, D_MODEL), lambda i, j: (i // tiles_per_batch, 0, 0)),
                  pl.BlockSpec((1, 1, D_MODEL), lambda i, j: (i // tiles_per_batch, 0, 0)),
                  pl.BlockSpec((PROJ_TN, D_MODEL), lambda i, j: (j, 0))],
        out_specs=pl.BlockSpec((PROJ_TM, PROJ_TN), lambda i, j: (i, j)),
        out_shape=jax.ShapeDtypeStruct((N_TOK, N_PROJ), F32),
        scratch_shapes=[pltpu.VMEM((PROJ_TM, D_MODEL), BF16)],
        compiler_params=_params(("arbitrary", "arbitrary")),
        name="in_proj",
    )(x, g.reshape(1, D_MODEL), sc, sh, w_t_bf16)


DSA_QB = 128
DSA_KC = 256


def _dsa_kernel(ckv_ref, ikiw_ref, q_ref, iq_ref, cos_a, sin_a, cos_b, sin_b, gkv_ref, wukv_ref,
                o_ref, k_lo, k_hi, v_one, ik_lo, ik_hi, keys_s, selt_s, acc_s, q_s):
    qi = pl.program_id(1)
    lane = lax.broadcasted_iota(I32, (1, LANES), 1)
    lo_half = lane < 64
    n_kc = SEQ // DSA_KC

    @pl.when(qi == 0)
    def _prologue():
        def chunk(c, carry):
            r0 = pl.multiple_of(c * DSA_KC, DSA_KC)
            rows = pl.ds(r0, DSA_KC)
            ckv = ckv_ref[0, rows, :]
            ms = jnp.mean(ckv * ckv, axis=-1, keepdims=True)
            cn = (ckv * lax.rsqrt(ms + EPS) * gkv_ref[...]).astype(BF16)
            kv = jnp.dot(cn, wukv_ref[...], preferred_element_type=F32)
            cb, sb = cos_b[0, rows, :], sin_b[0, rows, :]
            kv = _rope_lanes(kv, cb, sb)
            swapped = pltpu.roll(kv, 64, 1)
            k_lo[rows, :] = jnp.where(lo_half, kv, 0.0).astype(BF16)
            k_hi[rows, :] = jnp.where(lo_half, 0.0, swapped).astype(BF16)
            v_one[rows, :] = jnp.where(lo_half, swapped, 1.0).astype(BF16)
            ik = jnp.where(lo_half, _rope_lanes(ikiw_ref[0, rows, :], cb, sb), 0.0)
            ik_lo[rows, :] = ik.astype(BF16)
            ik_hi[rows, :] = pltpu.roll(ik, 64, 1).astype(BF16)
            return carry
        lax.fori_loop(0, n_kc, chunk, 0)

    q0 = pl.multiple_of(qi * DSA_QB, DSA_QB)
    qrows = pl.ds(q0, DSA_QB)
    cq, sq = cos_a[0, qrows, :], sin_a[0, qrows, :]

    iq = iq_ref[0]
    iq_heads = []
    for p in range(IDX_HEADS // 2):
        pair = _rope_lanes(iq[:, p * LANES:(p + 1) * LANES], cq, sq)
        iq_heads.append(jnp.where(lo_half, pair, 0.0).astype(BF16))
        iq_heads.append(jnp.where(lo_half, 0.0, pair).astype(BF16))
    iq_stacked = [jnp.concatenate(iq_heads[parity::2], axis=0) for parity in range(2)]
    iw_t = ikiw_ref[0, qrows, :].T * (IDX_HEADS ** -0.5)
    q_chunk = (q0 + lane) >> 6
    n_live = lax.shift_right_logical(qi + 2, 1)

    def chunk_rows(c):
        return pl.ds(pl.multiple_of(c * DSA_KC, DSA_KC), DSA_KC)

    def score_chunk(c, carry):
        rows = chunk_rows(c)
        k_chunk = (c * DSA_KC + lax.broadcasted_iota(I32, (DSA_KC, 1), 0)) >> 6
        score = jnp.zeros((DSA_KC, DSA_QB), F32)
        for parity, ik in enumerate((ik_lo, ik_hi)):
            logits = lax.dot_general(ik[rows, :], iq_stacked[parity], NT_DIMS,
                                     preferred_element_type=F32) * (IDX_DIM ** -0.5)
            for j in range(IDX_HEADS // 2):
                h = parity + 2 * j
                score = score + iw_t[64 + h:65 + h, :] * jnp.maximum(logits[:, j * DSA_QB:(j + 1) * DSA_QB], 0.0)
        score = jnp.where(score == 0.0, 0.0, score)
        score = jnp.where(k_chunk <= q_chunk, score, -jnp.inf)
        bits = lax.bitcast_convert_type(score, I32)
        keys_s[rows, :] = bits ^ ((bits >> 31) & 0x7FFFFFFF)
        return carry
    lax.fori_loop(0, n_live, score_chunk, 0)

    def count(preds):
        def chunk(c, accs):
            key = keys_s[chunk_rows(c), :]
            return tuple(acc + jnp.sum(pred(key).astype(I32).reshape(4, DSA_KC // 4, DSA_QB), axis=0)
                         for acc, pred in zip(accs, preds))
        zero = jnp.zeros((DSA_KC // 4, DSA_QB), I32)
        accs = lax.fori_loop(0, n_live, chunk, (zero,) * len(preds))
        return [jnp.sum(acc, axis=0, keepdims=True) for acc in accs]

    def descend(it, prefix):
        cand = prefix | lax.shift_left(jnp.int32(1), 31 - it)
        bound = cand ^ INT_MIN
        return jnp.where(count([lambda k: k >= bound])[0] >= TOPK, cand, prefix)
    thr = lax.fori_loop(0, 32, descend, jnp.zeros((1, DSA_QB), I32)) ^ INT_MIN

    n_gt, n_eq = count([lambda k: k > thr, lambda k: k == thr])
    need = (TOPK - n_gt).astype(F32)
    tie_break = jnp.max(n_gt + n_eq) > TOPK

    @pl.when(tie_break)
    def _():
        tri = (lax.broadcasted_iota(I32, (DSA_KC, DSA_KC), 0)
               >= lax.broadcasted_iota(I32, (DSA_KC, DSA_KC), 1)).astype(BF16)

        def select_chunk(c, seen):
            key = keys_s[chunk_rows(c), :]
            eq = key == thr
            rank = jnp.dot(tri, eq.astype(BF16), preferred_element_type=F32) + seen
            sel = ((key > thr) | (eq & (rank <= need))) & (key > NEG_INF_KEY)
            selt_s[c] = sel.astype(F32).T
            return rank[DSA_KC - 1:DSA_KC, :]
        lax.fori_loop(0, n_live, select_chunk, jnp.zeros((1, DSA_QB), F32))

    @pl.when(jnp.logical_not(tie_break))
    def _():
        def select_chunk(c, carry):
            key = keys_s[chunk_rows(c), :]
            selt_s[c] = ((key >= thr) & (key > NEG_INF_KEY)).astype(F32).T
            return carry
        lax.fori_loop(0, n_live, select_chunk, 0)

    q = q_ref[0]
    n_pairs = A_HEADS // 2
    for p in range(n_pairs):
        pair = _rope_lanes(q[:, p * LANES:(p + 1) * LANES], cq, sq) * (HEAD_DIM ** -0.5 * LOG2_E)
        q_s[p * DSA_QB:(p + 1) * DSA_QB, :] = jnp.where(lo_half, pair, 0.0).astype(BF16)
        q_s[(n_pairs + p) * DSA_QB:(n_pairs + p + 1) * DSA_QB, :] = jnp.where(lo_half, 0.0, pair).astype(BF16)
    masked = -1e30
    half = n_pairs * DSA_QB

    def attend(c, m):
        rows = chunk_rows(c)
        bias = jnp.where(selt_s[c] > 0.0, 0.0, masked)
        s = jnp.concatenate(
            [lax.dot_general(q_s[:half, :], k_lo[rows, :], NT_DIMS, preferred_element_type=F32),
             lax.dot_general(q_s[half:, :], k_hi[rows, :], NT_DIMS, preferred_element_type=F32)], axis=0)
        s = s.reshape(A_HEADS, DSA_QB, DSA_KC) + bias[None]
        m_new = jnp.maximum(m, jnp.max(s, axis=-1, keepdims=True))
        pe = jnp.exp2(s - m_new)
        pv = jnp.dot(pe.reshape(A_HEADS * DSA_QB, DSA_KC).astype(BF16), v_one[rows, :],
                     preferred_element_type=F32)
        acc_s[...] = jnp.exp2(m - m_new) * acc_s[...] + pv.reshape(A_HEADS, DSA_QB, LANES)
        return m_new

    acc_s[...] = jnp.zeros_like(acc_s)
    lax.fori_loop(0, n_live, attend, jnp.full((A_HEADS, DSA_QB, 1), masked, F32))
    for p in range(n_pairs):
        even, odd = acc_s[p], acc_s[n_pairs + p]
        even_swapped = pltpu.roll(even, HEAD_DIM, 1)
        odd_swapped = pltpu.roll(odd, HEAD_DIM, 1)
        o_ref[0, :, p * LANES:(p + 1) * LANES] = jnp.where(lo_half, even / even_swapped, odd_swapped / odd)


def _dsa(z3, cos_a, sin_a, cos_b, sin_b, g_kv, w_ukv_bf16):
    full = lambda width, col: pl.BlockSpec((1, SEQ, width), lambda b, i: (b, 0, col // width))
    blk = lambda width, col: pl.BlockSpec((1, DSA_QB, width), lambda b, i: (b, i, col // width))
    tab = pl.BlockSpec((1, SEQ, LANES), lambda b, i: (b, 0, 0))
    return pl.pallas_call(
        _dsa_kernel,
        grid=(BATCH, SEQ // DSA_QB),
        in_specs=[full(A_KV_LORA, COL_CKV), full(LANES, COL_IKIW),
                  blk(A_HEADS * HEAD_DIM, COL_QA), blk(IDX_HEADS * IDX_DIM, COL_IQ),
                  tab, tab, tab, tab,
                  pl.BlockSpec((1, A_KV_LORA), lambda b, i: (0, 0)),
                  pl.BlockSpec((A_KV_LORA, LANES), lambda b, i: (0, 0))],
        out_specs=pl.BlockSpec((1, DSA_QB, A_HEADS * HEAD_DIM), lambda b, i: (b, i, 0)),
        out_shape=jax.ShapeDtypeStruct((BATCH, SEQ, A_HEADS * HEAD_DIM), F32),
        scratch_shapes=[pltpu.VMEM((SEQ, LANES), BF16)] * 5
                       + [pltpu.VMEM((SEQ, DSA_QB), I32),
                          pltpu.VMEM((SEQ // DSA_KC, DSA_QB, DSA_KC), F32),
                          pltpu.VMEM((A_HEADS, DSA_QB, LANES), F32),
                          pltpu.VMEM((A_HEADS * DSA_QB, LANES), BF16)],
        compiler_params=_params(("arbitrary", "arbitrary")),
        name="dsa_mixer",
    )(z3, z3, z3, z3, cos_a, sin_a, cos_b, sin_b, g_kv.reshape(1, A_KV_LORA), w_ukv_bf16)


SB_BLK = 128
SB_EXP_UNDERFLOW = -105.0


def _sb_kernel(q_ref, k_ref, v_ref, o_ref, q_s, tail_s, acc_s):
    qi = pl.program_id(1)
    n_pairs = B_HEADS // 2
    rows_all = B_HEADS * SB_BLK
    lane = lax.broadcasted_iota(I32, (1, LANES), 1)
    lo_half = lane < 64
    causal = (lax.broadcasted_iota(I32, (rows_all, SB_BLK), 1)
              < (lax.broadcasted_iota(I32, (rows_all, SB_BLK), 0) & (SB_BLK - 1)))
    r2 = lax.broadcasted_iota(I32, (2 * SB_BLK, 2 * SB_BLK), 0) & (SB_BLK - 1)
    c2 = lax.broadcasted_iota(I32, (2 * SB_BLK, 2 * SB_BLK), 1)
    suffix_mat = ((c2 >= SB_BLK) | (r2 > c2)).astype(BF16)

    for p in range(n_pairs):
        q2 = q_ref[0, :, p * LANES:(p + 1) * LANES] * (HEAD_DIM ** -0.5)
        q_s[2 * p * SB_BLK:(2 * p + 1) * SB_BLK, :] = jnp.where(lo_half, q2, 0.0).astype(BF16)
        q_s[(2 * p + 1) * SB_BLK:(2 * p + 2) * SB_BLK, :] = jnp.where(lo_half, 0.0, q2).astype(BF16)

    def key_block(kb, diagonal):
        r0 = pl.multiple_of(kb * SB_BLK, SB_BLK)
        pair_rows = lambda p: slice(2 * p * SB_BLK, (2 * p + 2) * SB_BLK)
        z = jnp.concatenate(
            [lax.dot_general(q_s[pair_rows(p), :], k_ref[0, pl.ds(r0, SB_BLK), p * LANES:(p + 1) * LANES].astype(BF16),
                             NT_DIMS, preferred_element_type=F32) for p in range(n_pairs)], axis=0)
        log_beta = jnp.minimum(z, 0.0) - jnp.log(1.0 + jnp.exp(-jnp.abs(z)))
        log_not = log_beta - z
        if diagonal:
            log_not = jnp.where(causal, log_not, 0.0)
        hi = log_not.astype(BF16)
        lo = (log_not - hi.astype(F32)).astype(BF16)
        t = jnp.dot(jnp.concatenate([hi, lo], axis=1), suffix_mat, preferred_element_type=F32)
        if diagonal:
            a = jnp.where(causal, jnp.exp(log_beta + t[:, :SB_BLK]), 0.0)
            tail = t[:, SB_BLK:]
        else:
            a = jnp.exp(log_beta + t[:, :SB_BLK] + tail_s[...])
            tail = tail_s[...] + t[:, SB_BLK:]
        tail_s[...] = tail
        a = a.astype(BF16)
        for p in range(n_pairs):
            v2 = v_ref[0, pl.ds(r0, SB_BLK), p * LANES:(p + 1) * LANES].astype(BF16)
            pv = jnp.dot(a[pair_rows(p), :], v2, preferred_element_type=F32)
            if diagonal:
                acc_s[pair_rows(p), :] = pv
            else:
                acc_s[pair_rows(p), :] += pv
        return jnp.max(tail)

    first = key_block(qi, True)

    def more(state):
        jj, worst = state
        return (jj <= qi) & (worst > SB_EXP_UNDERFLOW)

    def step(state):
        jj, _ = state
        return jj + 1, key_block(qi - jj, False)

    lax.while_loop(more, step, (jnp.int32(1), first))
    for p in range(n_pairs):
        even = acc_s[2 * p * SB_BLK:(2 * p + 1) * SB_BLK, :]
        odd = acc_s[(2 * p + 1) * SB_BLK:(2 * p + 2) * SB_BLK, :]
        o_ref[0, :, p * LANES:(p + 1) * LANES] = jnp.where(lo_half, even, odd)


def _sb(z3):
    width = B_HEADS * HEAD_DIM
    return pl.pallas_call(
        _sb_kernel,
        grid=(BATCH, SEQ // SB_BLK),
        in_specs=[pl.BlockSpec((1, SB_BLK, width), lambda b, i: (b, i, COL_QB // width)),
                  pl.BlockSpec((1, SEQ, width), lambda b, i: (b, 0, COL_KB // width)),
                  pl.BlockSpec((1, SEQ, width), lambda b, i: (b, 0, COL_VB // width))],
        out_specs=pl.BlockSpec((1, SB_BLK, width), lambda b, i: (b, i, 0)),
        out_shape=jax.ShapeDtypeStruct((BATCH, SEQ, width), F32),
        scratch_shapes=[pltpu.VMEM((B_HEADS * SB_BLK, LANES), BF16),
                        pltpu.VMEM((B_HEADS * SB_BLK, SB_BLK), F32),
                        pltpu.VMEM((B_HEADS * SB_BLK, SB_BLK), F32)],
        compiler_params=_params(("arbitrary", "arbitrary")),
        name="stickbreak_mixer",
    )(z3, z3, z3)


RET_BLK = 256


def _ret_kernel(dec_ref, q_ref, k_ref, v_ref, gate_ref, cos_a, sin_a, qs_ref, ks_ref, e_diag, gret_ref,
                o_ref, state):
    lane = lax.broadcasted_iota(I32, (1, LANES), 1)
    lo_half = lane < 64

    @pl.when(pl.program_id(1) == 0)
    def _():
        state[...] = jnp.zeros_like(state)

    cq, sq = cos_a[0], sin_a[0]
    for p in range(C_HEADS // 2):
        cols = slice(p * LANES, (p + 1) * LANES)
        q2 = _rope_lanes(q_ref[0, :, cols], cq, sq)
        k2 = _rope_lanes(k_ref[0, :, cols], cq, sq) * (HEAD_DIM ** -0.5)
        q_hat = q2 * qs_ref[:, cols]
        k_hat = k2 * ks_ref[:, cols]
        for e in range(2):
            h = 2 * p + e
            pick = (lambda x: jnp.where(lo_half, x, 0.0)) if e == 0 else (lambda x: jnp.where(lo_half, 0.0, x))
            vcols = slice(h * C_V_DIM, (h + 1) * C_V_DIM)
            v2 = v_ref[0, :, vcols].astype(BF16)
            s = lax.dot_general(pick(q2).astype(BF16), k2.astype(BF16), NT_DIMS,
                                preferred_element_type=F32) * e_diag[h]
            o = jnp.dot(s.astype(BF16), v2, preferred_element_type=F32)
            o = o + jnp.dot(pick(q_hat).astype(BF16), state[h].astype(BF16), preferred_element_type=F32)
            kv = jnp.dot(pick(k_hat).T.astype(BF16), v2, preferred_element_type=F32)
            state[h] = dec_ref[h] * state[h] + kv
            mu = jnp.mean(o, axis=-1, keepdims=True)
            d = o - mu
            var = jnp.mean(d * d, axis=-1, keepdims=True)
            y = d * lax.rsqrt(var + EPS) * gret_ref[:, vcols]
            g = gate_ref[0, :, vcols]
            o_ref[0, :, vcols] = g * _sigmoid(g) * y


def _ret(z3, cos_a, sin_a, dec, q_scale, k_scale, e_diag, g_ret):
    qk_w = C_HEADS * HEAD_DIM
    v_w = C_HEADS * C_V_DIM
    tab = pl.BlockSpec((1, RET_BLK, LANES), lambda b, i: (b, i, 0))
    const2 = lambda a: pl.BlockSpec(a.shape, lambda b, i: (0, 0))
    return pl.pallas_call(
        _ret_kernel,
        grid=(BATCH, SEQ // RET_BLK),
        in_specs=[pl.BlockSpec(memory_space=pltpu.SMEM),
                  pl.BlockSpec((1, RET_BLK, qk_w), lambda b, i: (b, i, COL_QC // qk_w)),
                  pl.BlockSpec((1, RET_BLK, qk_w), lambda b, i: (b, i, COL_KC // qk_w)),
                  pl.BlockSpec((1, RET_BLK, v_w), lambda b, i: (b, i, COL_VC // v_w)),
                  pl.BlockSpec((1, RET_BLK, v_w), lambda b, i: (b, i, COL_GC // v_w)),
                  tab, tab, const2(q_scale), const2(k_scale),
                  pl.BlockSpec(e_diag.shape, lambda b, i: (0, 0, 0)),
                  pl.BlockSpec((1, v_w), lambda b, i: (0, 0))],
        out_specs=pl.BlockSpec((1, RET_BLK, v_w), lambda b, i: (b, i, 0)),
        out_shape=jax.ShapeDtypeStruct((BATCH, SEQ, v_w), F32),
        scratch_shapes=[pltpu.VMEM((C_HEADS, LANES, C_V_DIM), F32)],
        compiler_params=_params(("arbitrary", "arbitrary")),
        name="retention_mixer",
    )(dec, z3, z3, z3, z3, cos_a, sin_a, q_scale, k_scale, e_diag, g_ret.reshape(1, v_w))


MERGE_TM = 256


def _rms(y, g):
    ms = jnp.mean(y * y, axis=-1, keepdims=True)
    return y * lax.rsqrt(ms + EPS) * g


def _route(logits_t, bias_col):
    n_tok = logits_t.shape[1]
    per_group = N_EXPERTS // N_GROUPS
    scores = _sigmoid(logits_t)
    sel = scores + bias_col
    sel3 = sel.reshape(N_GROUPS, per_group, n_tok)
    member = lax.broadcasted_iota(I32, (N_GROUPS, per_group, n_tok), 1)
    m1 = jnp.max(sel3, axis=1, keepdims=True)
    first = jnp.min(jnp.where(sel3 == m1, member, per_group), axis=1, keepdims=True)
    m2 = jnp.max(jnp.where(member == first, -jnp.inf, sel3), axis=1, keepdims=True)
    grp = (m1 + m2).reshape(N_GROUPS, n_tok)
    gid = lax.broadcasted_iota(I32, (N_GROUPS, n_tok), 0)
    rank = jnp.zeros((N_GROUPS, n_tok), I32)
    for g in range(N_GROUPS):
        other = grp[g:g + 1, :]
        rank = rank + ((other > grp) | ((other == grp) & (g < gid))).astype(I32)
    keep = jnp.where(rank < TOPK_GROUPS, 1.0, 0.0).reshape(N_GROUPS, 1, n_tok)
    live = jnp.where(keep > 0.0, sel3, -jnp.inf).reshape(N_EXPERTS, n_tok)
    eid = lax.broadcasted_iota(I32, (N_EXPERTS, n_tok), 0)
    chosen = jnp.zeros((N_EXPERTS, n_tok), jnp.bool_)
    for _ in range(TOP_K):
        m = jnp.max(live, axis=0, keepdims=True)
        pick = eid == jnp.min(jnp.where(live == m, eid, N_EXPERTS), axis=0, keepdims=True)
        chosen = chosen | pick
        live = jnp.where(pick, -jnp.inf, live)
    w = jnp.where(chosen, scores, 0.0)
    return w / jnp.sum(w, axis=0, keepdims=True) * ROUTED_SCALE, chosen


def _merge_kernel(ya_ref, yb_ref, yc_ref, g_ref, x_ref, gt_ref, gpost_ref, gpre_ref, sc_ref, sh_ref,
                  woa, wob, woc, wout, wr_t, rb_ref, xo_ref, h_ref, gate_ref, rank_ref):
    ya = jnp.dot(ya_ref[...].astype(BF16), woa[...], preferred_element_type=F32)
    yb = jnp.dot(yb_ref[...].astype(BF16), wob[...], preferred_element_type=F32)
    yc = jnp.dot(yc_ref[...].astype(BF16), woc[...], preferred_element_type=F32)
    merged = (_sigmoid(g_ref[:, 0:D_MODEL]) * ya
              + _sigmoid(g_ref[:, D_MODEL:2 * D_MODEL]) * yb
              + _sigmoid(g_ref[:, 2 * D_MODEL:3 * D_MODEL]) * yc)
    y = jnp.dot(merged.astype(BF16), wout[...], preferred_element_type=F32)
    x = x_ref[...] + gt_ref[0] * _rms(y, gpost_ref[...])
    xo_ref[...] = x
    h = (_rms(x, gpre_ref[...]) * (1.0 + sc_ref[0]) + sh_ref[0]).astype(BF16)
    h_ref[...] = h
    logits_t = lax.dot_general(wr_t[...], h, NT_DIMS, preferred_element_type=F32)
    gates_t, chosen = _route(logits_t, rb_ref[...])
    gate_ref[...] = gates_t
    earlier = (lax.broadcasted_iota(I32, (MERGE_TM, MERGE_TM), 0)
               < lax.broadcasted_iota(I32, (MERGE_TM, MERGE_TM), 1)).astype(BF16)
    before = jnp.dot(jnp.where(chosen, 1.0, 0.0).astype(BF16), earlier, preferred_element_type=F32)
    rank_ref[...] = jnp.where(chosen, before, -1.0)


def _merge(ya, yb, yc, z, x, gt1, g_post, g_pre, sc2, sh2, woa, wob, woc, wout, wr_t, rbias):
    tpb = SEQ // MERGE_TM
    row = lambda w: pl.BlockSpec((MERGE_TM, w), lambda i: (i, 0))
    vec = pl.BlockSpec((1, D_MODEL), lambda i: (0, 0))
    mod = pl.BlockSpec((1, 1, D_MODEL), lambda i: (i // tpb, 0, 0))
    whole = lambda a: pl.BlockSpec(a.shape, lambda i: (0, 0))
    return pl.pallas_call(
        _merge_kernel,
        grid=(N_TOK // MERGE_TM,),
        in_specs=[row(512), row(512), row(1024), pl.BlockSpec((MERGE_TM, 3 * D_MODEL), lambda i: (i, 0)),
                  row(D_MODEL), mod, vec, vec, mod, mod,
                  whole(woa), whole(wob), whole(woc), whole(wout), whole(wr_t), whole(rbias)],
        out_specs=[row(D_MODEL), row(D_MODEL),
                   pl.BlockSpec((N_EXPERTS, MERGE_TM), lambda i: (0, i)),
                   pl.BlockSpec((N_EXPERTS, MERGE_TM), lambda i: (0, i))],
        out_shape=[jax.ShapeDtypeStruct((N_TOK, D_MODEL), F32),
                   jax.ShapeDtypeStruct((N_TOK, D_MODEL), BF16),
                   jax.ShapeDtypeStruct((N_EXPERTS, N_TOK), F32),
                   jax.ShapeDtypeStruct((N_EXPERTS, N_TOK), F32)],
        compiler_params=_params(("arbitrary",)),
        name="merge_route",
    )(ya, yb, yc, z, x, gt1, g_post.reshape(1, D_MODEL), g_pre.reshape(1, D_MODEL), sc2, sh2,
      woa, wob, woc, wout, wr_t, rbias)


MOE_TM = 1024
MOE_SUB = MERGE_TM
MOE_EG = 4
MOE_CAP = 64
MOE_NSUB = MOE_TM // MOE_SUB
MOE_SLAB = MOE_EG * MOE_CAP


def _moe_kernel(h_ref, gate_ref, rank_ref, x_ref, gt_ref, gpost_ref, w1, w3, w2, ws1, ws3, ws2,
                o_ref, acc, xg_s, yw_s, pick_s, back_s):
    g_idx = pl.program_id(1)

    @pl.when(g_idx == 0)
    def _():
        acc[...] = jnp.zeros_like(acc)

    rank = rank_ref[0]
    gate = gate_ref[0]
    top_rank = jnp.max(rank)
    n_rounds = (top_rank.astype(I32) + MOE_CAP) // MOE_CAP
    slot = lax.broadcasted_iota(I32, (MOE_CAP, 1), 0).astype(F32)
    sub_top = [jnp.max(rank[:, s * MOE_SUB:(s + 1) * MOE_SUB]) for s in range(MOE_NSUB)]
    exp_top = [jnp.max(rank[e:e + 1, :]) for e in range(MOE_EG)]

    def make_pick(s, base):
        toks = slice(s * MOE_SUB, (s + 1) * MOE_SUB)
        return jnp.concatenate(
            [jnp.where(rank[e:e + 1, toks] == slot + base, 1.0, 0.0) for e in range(MOE_EG)], axis=0)

    def expert_rows(e, pick_of):
        rows = slice(e * MOE_CAP, (e + 1) * MOE_CAP)
        x_e = jnp.concatenate([xg_s[s, rows, :] for s in range(MOE_NSUB)], axis=0)
        a = jnp.dot(x_e, w1[e], preferred_element_type=F32)
        hid = a * _sigmoid(a) * jnp.dot(x_e, w3[e], preferred_element_type=F32)
        y = jnp.dot(hid.astype(BF16), w2[e], preferred_element_type=F32)
        for s in range(MOE_NSUB):
            toks = slice(s * MOE_SUB, (s + 1) * MOE_SUB)
            w_row = jnp.sum(pick_of(s)[rows, :] * gate[e:e + 1, toks], axis=1, keepdims=True)
            yw_s[s, rows, :] = (y[s * MOE_CAP:(s + 1) * MOE_CAP, :] * w_row).astype(BF16)

    def scatter(s, back):
        toks = slice(s * MOE_SUB, (s + 1) * MOE_SUB)
        acc[toks, :] += jnp.dot(back, yw_s[s], preferred_element_type=F32)

    picks = [make_pick(s, 0.0) for s in range(MOE_NSUB)]
    for s in range(MOE_NSUB):
        toks = slice(s * MOE_SUB, (s + 1) * MOE_SUB)
        xg_s[s] = jnp.dot(picks[s].astype(BF16), h_ref[toks, :], preferred_element_type=F32).astype(BF16)
    for e in range(MOE_EG):
        expert_rows(e, lambda s: picks[s])
    for s in range(MOE_NSUB):
        scatter(s, picks[s].T.astype(BF16))

    @pl.when(n_rounds > 1)
    def _():
        for s in range(MOE_NSUB):
            pick_s[s] = picks[s].astype(BF16)

    def overflow_round(r, carry):
        base = (r * MOE_CAP).astype(F32)
        for s in range(MOE_NSUB):
            @pl.when(sub_top[s] >= base)
            def _(s=s):
                pick = make_pick(s, base)
                pick_s[s] = pick.astype(BF16)
                back_s[s] = pick.T.astype(BF16)
                toks = slice(s * MOE_SUB, (s + 1) * MOE_SUB)
                xg_s[s] = jnp.dot(pick.astype(BF16), h_ref[toks, :], preferred_element_type=F32).astype(BF16)
        for e in range(MOE_EG):
            @pl.when(exp_top[e] >= base)
            def _(e=e):
                expert_rows(e, lambda s: pick_s[s].astype(F32))
        for s in range(MOE_NSUB):
            @pl.when(sub_top[s] >= base)
            def _(s=s):
                scatter(s, back_s[s])
        return carry

    lax.fori_loop(1, n_rounds, overflow_round, 0)

    @pl.when(g_idx == N_EXPERTS // MOE_EG - 1)
    def _():
        h = h_ref[...]
        s = jnp.dot(h, ws1[...], preferred_element_type=F32)
        shid = s * _sigmoid(s) * jnp.dot(h, ws3[...], preferred_element_type=F32)
        y_all = acc[...] + jnp.dot(shid.astype(BF16), ws2[...], preferred_element_type=F32)
        o_ref[...] = x_ref[...] + gt_ref[0] * _rms(y_all, gpost_ref[...])


def _moe(h, gates_t, ranks_t, x, gt2, g_post, w1, w3, w2, ws1, ws3, ws2, layer):
    tpb = SEQ // MOE_TM
    n_groups = N_EXPERTS // MOE_EG
    whole = lambda a: pl.BlockSpec(a.shape, lambda i, g: (0, 0))
    per_group = pl.BlockSpec((1, MOE_EG, MOE_TM), lambda i, g: (g, 0, i))
    return pl.pallas_call(
        _moe_kernel,
        grid=(N_TOK // MOE_TM, n_groups),
        in_specs=[pl.BlockSpec((MOE_TM, D_MODEL), lambda i, g: (i, 0)),
                  per_group, per_group,
                  pl.BlockSpec((MOE_TM, D_MODEL), lambda i, g: (i, 0)),
                  pl.BlockSpec((1, 1, D_MODEL), lambda i, g: (i // tpb, 0, 0)),
                  pl.BlockSpec((1, D_MODEL), lambda i, g: (0, 0)),
                  pl.BlockSpec((None, MOE_EG, D_MODEL, EXPERT_DIM), lambda i, g: (layer, g, 0, 0)),
                  pl.BlockSpec((None, MOE_EG, D_MODEL, EXPERT_DIM), lambda i, g: (layer, g, 0, 0)),
                  pl.BlockSpec((None, MOE_EG, EXPERT_DIM, D_MODEL), lambda i, g: (layer, g, 0, 0)),
                  whole(ws1), whole(ws3), whole(ws2)],
        out_specs=pl.BlockSpec((MOE_TM, D_MODEL), lambda i, g: (i, 0)),
        out_shape=jax.ShapeDtypeStruct((N_TOK, D_MODEL), F32),
        scratch_shapes=[pltpu.VMEM((MOE_TM, D_MODEL), F32),
                        pltpu.VMEM((MOE_NSUB, MOE_SLAB, D_MODEL), BF16),
                        pltpu.VMEM((MOE_NSUB, MOE_SLAB, D_MODEL), BF16),
                        pltpu.VMEM((MOE_NSUB, MOE_SLAB, MOE_SUB), BF16),
                        pltpu.VMEM((MOE_NSUB, MOE_SUB, MOE_SLAB), BF16)],
        compiler_params=_params(("arbitrary", "arbitrary")),
        name="moe_ffn",
    )(h, gates_t.reshape(n_groups, MOE_EG, N_TOK), ranks_t.reshape(n_groups, MOE_EG, N_TOK),
      x, gt2, g_post.reshape(1, D_MODEL), w1, w3, w2, ws1, ws3, ws2)


_IN_SIZES = (512, 256, 256, 64, 4, 512, 512, 512, 512, 512, 1024, 1024, 3072)
_IN_DEST = (COL_QA, COL_CKV, COL_IQ, COL_IKIW, COL_IKIW + 64, COL_QB, COL_KB, COL_VB, COL_QC, COL_KC,
            COL_VC, COL_GC, COL_GATES)
_IN_SRC = tuple(int(v) for v in np.cumsum((0,) + _IN_SIZES[:-1]))
N_IN = sum(_IN_SIZES)
PACK_TK = 128


def _pack_kernel(w_ref, o_ref, *, layer):
    for src, width, dst in zip(_IN_SRC, _IN_SIZES, _IN_DEST):
        o_ref[dst:dst + width, :] = w_ref[src:src + width, layer, :].astype(BF16)
    o_ref[COL_IKIW + 68:COL_IKIW + LANES, :] = jnp.zeros((LANES - 68, PACK_TK), BF16)


def _reorder_w_in(w_t, layer):
    return pl.pallas_call(
        functools.partial(_pack_kernel, layer=layer),
        grid=(D_MODEL // PACK_TK,),
        in_specs=[pl.BlockSpec((N_IN, DEPTH, PACK_TK), lambda i: (0, 0, i))],
        out_specs=pl.BlockSpec((N_PROJ, PACK_TK), lambda i: (0, i)),
        out_shape=jax.ShapeDtypeStruct((N_PROJ, D_MODEL), BF16),
        compiler_params=_params(("arbitrary",)),
        name="pack_w_in",
    )(w_t)


def _rope_tables(positions):
    half = HEAD_DIM // 2
    inv_freq = ROPE_THETA ** (-jnp.arange(half, dtype=F32) / half)
    ang = positions.astype(F32)[:, :, None] * inv_freq
    cos, sin = jnp.cos(ang), jnp.sin(ang)
    cos64 = jnp.concatenate([cos, cos], axis=-1)
    sin64 = jnp.concatenate([-sin, sin], axis=-1)
    cos_a = jnp.concatenate([cos64, cos64], axis=-1)
    sin_a = jnp.concatenate([sin64, sin64], axis=-1)
    cos_b = jnp.concatenate([cos64, jnp.ones_like(cos64)], axis=-1)
    sin_b = jnp.concatenate([sin64, jnp.zeros_like(sin64)], axis=-1)
    return cos_a, sin_a, cos_b, sin_b


def _decay_tables():
    log_g = jnp.log(1.0 - 2.0 ** (-5.0 - jnp.arange(C_HEADS, dtype=F32)))
    pos = jnp.arange(RET_BLK, dtype=F32)
    diff = pos[:, None] - pos[None, :]
    chunk = jnp.arange(RET_BLK) // CHUNK
    visible = chunk[None, :] <= chunk[:, None]
    e_diag = jnp.where(visible, jnp.exp(log_g[:, None, None] * jnp.abs(diff)), 0.0)
    dec = jnp.exp(log_g * RET_BLK)
    per_lane = jnp.repeat(log_g, HEAD_DIM)[None, :]
    q_scale = jnp.exp(per_lane * pos[:, None])
    k_scale = jnp.exp(per_lane * (RET_BLK - pos)[:, None])
    return dec, q_scale, k_scale, e_diag


def kernel(x, c, positions, ada_w, ada_b, g_pre_mix, g_post_mix, g_pre_ffn, g_post_ffn, w_in, g_kv, w_uk, w_uv,
           g_ret, w_o_a, w_o_b, w_o_c, w_out, w_router, router_bias, w1, w3, w2, ws1, ws3, ws2):
    cos_a, sin_a, cos_b, sin_b = _rope_tables(positions)
    dec, q_scale, k_scale, e_diag = _decay_tables()
    xf = x.reshape(N_TOK, D_MODEL)
    w_in_t = jnp.transpose(w_in, (2, 0, 1))
    w1_b, w3_b, w2_b = w1.astype(BF16), w3.astype(BF16), w2.astype(BF16)
    for l in range(DEPTH):
        mod = _ada(c, ada_w, ada_b, l).reshape(BATCH, 1, N_ADA * D_MODEL)
        sh1, sc1, gt1, sh2, sc2, gt2 = (mod[:, :, k * D_MODEL:(k + 1) * D_MODEL] for k in range(N_ADA))
        z = _inproj(xf, g_pre_mix[l], sc1, sh1, _reorder_w_in(w_in_t, l))
        z3 = z.reshape(BATCH, SEQ, N_PROJ)
        w_ukv = jnp.concatenate([w_uk[l], w_uv[l]], axis=1).astype(BF16)
        ya = _dsa(z3, cos_a, sin_a, cos_b, sin_b, g_kv[l], w_ukv).reshape(N_TOK, -1)
        yb = _sb(z3).reshape(N_TOK, -1)
        yc = _ret(z3, cos_a, sin_a, dec, q_scale, k_scale, e_diag, g_ret[l]).reshape(N_TOK, -1)
        xf, h, gates, ranks = _merge(ya, yb, yc, z, xf, gt1, g_post_mix[l], g_pre_ffn[l], sc2, sh2,
                              w_o_a[l].astype(BF16), w_o_b[l].astype(BF16), w_o_c[l].astype(BF16),
                              w_out[l].astype(BF16), w_router[l].T.astype(BF16),
                              router_bias[l].reshape(N_EXPERTS, 1))
        xf = _moe(h, gates, ranks, xf, gt2, g_post_ffn[l], w1_b, w3_b, w2_b,
                  ws1[l].astype(BF16), ws3[l].astype(BF16), ws2[l].astype(BF16), l)
    return xf.reshape(BATCH, SEQ, D_MODEL)
```

```python
import functools
import math

import numpy as np
import jax
import jax.numpy as jnp
from jax import lax
from jax.experimental import pallas as pl
from jax.experimental.pallas import tpu as pltpu

F32 = jnp.float32
BF16 = jnp.bfloat16
I32 = jnp.int32

D_MODEL = 1024
BATCH = 8
SEQ = 2048
DEPTH = 2
N_TOK = BATCH * SEQ

CHUNK = 64
HEAD_DIM = 64
ROPE_THETA = 10000.0
EPS = 1e-6
A_HEADS = 8
A_KV_LORA = 256
IDX_HEADS = 4
IDX_DIM = 64
TOPK = 256
B_HEADS = 8
C_HEADS = 8
C_V_DIM = 128
N_EXPERTS = 64
TOP_K = 8
N_GROUPS = 8
TOPK_GROUPS = 4
EXPERT_DIM = 256
SHARED_DIM = 256
ROUTED_SCALE = 2.5
N_ADA = 6

LANES = 128
VMEM_LIMIT = 56 * 1024 * 1024

COL_GATES = 0
COL_VC = 3072
COL_GC = 4096
COL_QA = 5120
COL_QB = 5632
COL_KB = 6144
COL_VB = 6656
COL_QC = 7168
COL_KC = 7680
COL_CKV = 8192
COL_IQ = 8448
COL_IKIW = 8704
N_PROJ = 8832
PROJ_TN = 2944
PROJ_TM = 1024

INT_MIN = -2147483648
NEG_INF_KEY = -2139095041

NT_DIMS = (((1,), (1,)), ((), ()))
LOG2_E = math.log2(math.e)


def _params(sem):
    return pltpu.CompilerParams(dimension_semantics=sem, vmem_limit_bytes=VMEM_LIMIT)


def _sigmoid(x):
    return 1.0 / (1.0 + jnp.exp(-x))


def _rope_lanes(x, cos, sin_signed):
    n = x.shape[-1]
    lane = lax.broadcasted_iota(I32, (1, n), 1)
    first = (lane & 32) == 0
    ahead = pltpu.roll(x, n - 32, 1)
    behind = pltpu.roll(x, 32, 1)
    return x * cos + jnp.where(first, ahead, behind) * sin_signed


def _ada_kernel(c_ref, w_ref, b_ref, o_ref):
    c = c_ref[...]
    s = (c * _sigmoid(c)).astype(BF16)
    o_ref[...] = jnp.dot(s, w_ref[...].astype(BF16), preferred_element_type=F32) + b_ref[...]


def _ada(c, w, b, layer):
    n = w.shape[2]
    return pl.pallas_call(
        _ada_kernel,
        grid=(n // D_MODEL,),
        in_specs=[pl.BlockSpec((BATCH, D_MODEL), lambda j: (0, 0)),
                  pl.BlockSpec((None, D_MODEL, D_MODEL), lambda j: (layer, 0, j)),
                  pl.BlockSpec((None, 1, D_MODEL), lambda j: (layer, 0, j))],
        out_specs=pl.BlockSpec((BATCH, D_MODEL), lambda j: (0, j)),
        out_shape=jax.ShapeDtypeStruct((BATCH, n), F32),
        compiler_params=_params(("arbitrary",)),
        name="ada_mod",
    )(c, w, b.reshape(DEPTH, 1, n))


def _inproj_kernel(x_ref, g_ref, sc_ref, sh_ref, w_ref, o_ref, h_ref):
    @pl.when(pl.program_id(1) == 0)
    def _():
        x = x_ref[...]
        ms = jnp.mean(x * x, axis=-1, keepdims=True)
        y = x * lax.rsqrt(ms + EPS) * g_ref[...]
        h_ref[...] = (y * (1.0 + sc_ref[0]) + sh_ref[0]).astype(BF16)

    o_ref[...] = lax.dot_general(h_ref[...], w_ref[...], NT_DIMS, preferred_element_type=F32)


def _inproj(x, g, sc, sh, w_t_bf16):
    tiles_per_batch = SEQ // PROJ_TM
    return pl.pallas_call(
        _inproj_kernel,
        grid=(N_TOK // PROJ_TM, N_PROJ // PROJ_TN),
        in_specs=[pl.BlockSpec((PROJ_TM, D_MODEL), lambda i, j: (i, 0)),
                  pl.BlockSpec((1, D_MODEL), lambda i, j: (0, 0)),
                  pl.BlockSpec((1, 1, D_MODEL), lambda i, j: (i // tiles_per_batch, 0, 0)),
                  pl.BlockSpec((1, 1, D_MODEL), lambda i, j: (i // tiles_per_batch, 0, 0)),
                  pl.BlockSpec((PROJ_TN, D_MODEL), lambda i, j: (j, 0))],
        out_specs=pl.BlockSpec((PROJ_TM, PROJ_TN), lambda i, j: (i, j)),
        out_shape=jax.ShapeDtypeStruct((N_TOK, N_PROJ), F32),
        scratch_shapes=[pltpu.VMEM((PROJ_TM, D_MODEL), BF16)],
        compiler_params=_params(("arbitrary", "arbitrary")),
        name="in_proj",
    )(x, g.reshape(1, D_MODEL), sc, sh, w_t_bf16)


DSA_QB = 128
DSA_KC = 256


def _dsa_kernel(ckv_ref, ikiw_ref, q_ref, iq_ref, cos_a, sin_a, cos_b, sin_b, gkv_ref, wukv_ref,
                o_ref, k_lo, k_hi, v_one, ik_lo, ik_hi, keys_s, selt_s, acc_s, q_s):
    qi = pl.program_id(1)
    lane = lax.broadcasted_iota(I32, (1, LANES), 1)
    lo_half = lane < 64
    n_kc = SEQ // DSA_KC

    @pl.when(qi == 0)
    def _prologue():
        def chunk(c, carry):
            r0 = pl.multiple_of(c * DSA_KC, DSA_KC)
            rows = pl.ds(r0, DSA_KC)
            ckv = ckv_ref[0, rows, :]
            ms = jnp.mean(ckv * ckv, axis=-1, keepdims=True)
            cn = (ckv * lax.rsqrt(ms + EPS) * gkv_ref[...]).astype(BF16)
            kv = jnp.dot(cn, wukv_ref[...], preferred_element_type=F32)
            cb, sb = cos_b[0, rows, :], sin_b[0, rows, :]
            kv = _rope_lanes(kv, cb, sb)
            swapped = pltpu.roll(kv, 64, 1)
            k_lo[rows, :] = jnp.where(lo_half, kv, 0.0).astype(BF16)
            k_hi[rows, :] = jnp.where(lo_half, 0.0, swapped).astype(BF16)
            v_one[rows, :] = jnp.where(lo_half, swapped, 1.0).astype(BF16)
            ik = jnp.where(lo_half, _rope_lanes(ikiw_ref[0, rows, :], cb, sb), 0.0)
            ik_lo[rows, :] = ik.astype(BF16)
            ik_hi[rows, :] = pltpu.roll(ik, 64, 1).astype(BF16)
            return carry
        lax.fori_loop(0, n_kc, chunk, 0)

    q0 = pl.multiple_of(qi * DSA_QB, DSA_QB)
    qrows = pl.ds(q0, DSA_QB)
    cq, sq = cos_a[0, qrows, :], sin_a[0, qrows, :]

    iq = iq_ref[0]
    iq_heads = []
    for p in range(IDX_HEADS // 2):
        pair = _rope_lanes(iq[:, p * LANES:(p + 1) * LANES], cq, sq)
        iq_heads.append(jnp.where(lo_half, pair, 0.0).astype(BF16))
        iq_heads.append(jnp.where(lo_half, 0.0, pair).astype(BF16))
    iq_stacked = [jnp.concatenate(iq_heads[parity::2], axis=0) for parity in range(2)]
    iw_t = ikiw_ref[0, qrows, :].T * (IDX_HEADS ** -0.5)
    q_chunk = (q0 + lane) >> 6
    n_live = lax.shift_right_logical(qi + 2, 1)

    def chunk_rows(c):
        return pl.ds(pl.multiple_of(c * DSA_KC, DSA_KC), DSA_KC)

    def score_chunk(c, carry):
        rows = chunk_rows(c)
        k_chunk = (c * DSA_KC + lax.broadcasted_iota(I32, (DSA_KC, 1), 0)) >> 6
        score = jnp.zeros((DSA_KC, DSA_QB), F32)
        for parity, ik in enumerate((ik_lo, ik_hi)):
            logits = lax.dot_general(ik[rows, :], iq_stacked[parity], NT_DIMS,
                                     preferred_element_type=F32) * (IDX_DIM ** -0.5)
            for j in range(IDX_HEADS // 2):
                h = parity + 2 * j
                score = score + iw_t[64 + h:65 + h, :] * jnp.maximum(logits[:, j * DSA_QB:(j + 1) * DSA_QB], 0.0)
        score = jnp.where(score == 0.0, 0.0, score)
        score = jnp.where(k_chunk <= q_chunk, score, -jnp.inf)
        bits = lax.bitcast_convert_type(score, I32)
        keys_s[rows, :] = bits ^ ((bits >> 31) & 0x7FFFFFFF)
        return carry
    lax.fori_loop(0, n_live, score_chunk, 0)

    def count(preds):
        def chunk(c, accs):
            key = keys_s[chunk_rows(c), :]
            return tuple(acc + jnp.sum(pred(key).astype(I32).reshape(4, DSA_KC // 4, DSA_QB), axis=0)
                         for acc, pred in zip(accs, preds))
        zero = jnp.zeros((DSA_KC // 4, DSA_QB), I32)
        accs = lax.fori_loop(0, n_live, chunk, (zero,) * len(preds))
        return [jnp.sum(acc, axis=0, keepdims=True) for acc in accs]

    def descend(it, prefix):
        cand = prefix | lax.shift_left(jnp.int32(1), 31 - it)
        bound = cand ^ INT_MIN
        return jnp.where(count([lambda k: k >= bound])[0] >= TOPK, cand, prefix)
    thr = lax.fori_loop(0, 32, descend, jnp.zeros((1, DSA_QB), I32)) ^ INT_MIN

    n_gt, n_eq = count([lambda k: k > thr, lambda k: k == thr])
    need = (TOPK - n_gt).astype(F32)
    tie_break = jnp.max(n_gt + n_eq) > TOPK

    @pl.when(tie_break)
    def _():
        tri = (lax.broadcasted_iota(I32, (DSA_KC, DSA_KC), 0)
               >= lax.broadcasted_iota(I32, (DSA_KC, DSA_KC), 1)).astype(BF16)

        def select_chunk(c, seen):
            key = keys_s[chunk_rows(c), :]
            eq = key == thr
            rank = jnp.dot(tri, eq.astype(BF16), preferred_element_type=F32) + seen
            sel = ((key > thr) | (eq & (rank <= need))) & (key > NEG_INF_KEY)
            selt_s[c] = sel.astype(F32).T
            return rank[DSA_KC - 1:DSA_KC, :]
        lax.fori_loop(0, n_live, select_chunk, jnp.zeros((1, DSA_QB), F32))

    @pl.when(jnp.logical_not(tie_break))
    def _():
        def select_chunk(c, carry):
            key = keys_s[chunk_rows(c), :]
            selt_s[c] = ((key >= thr) & (key > NEG_INF_KEY)).astype(F32).T
            return carry
        lax.fori_loop(0, n_live, select_chunk, 0)

    q = q_ref[0]
    n_pairs = A_HEADS // 2
    for p in range(n_pairs):
        pair = _rope_lanes(q[:, p * LANES:(p + 1) * LANES], cq, sq) * (HEAD_DIM ** -0.5 * LOG2_E)
        q_s[p * DSA_QB:(p + 1) * DSA_QB, :] = jnp.where(lo_half, pair, 0.0).astype(BF16)
        q_s[(n_pairs + p) * DSA_QB:(n_pairs + p + 1) * DSA_QB, :] = jnp.where(lo_half, 0.0, pair).astype(BF16)
    masked = -1e30
    half = n_pairs * DSA_QB

    def attend(c, m):
        rows = chunk_rows(c)
        bias = jnp.where(selt_s[c] > 0.0, 0.0, masked)
        s = jnp.concatenate(
            [lax.dot_general(q_s[:half, :], k_lo[rows, :], NT_DIMS, preferred_element_type=F32),
             lax.dot_general(q_s[half:, :], k_hi[rows, :], NT_DIMS, preferred_element_type=F32)], axis=0)
        s = s.reshape(A_HEADS, DSA_QB, DSA_KC) + bias[None]
        m_new = jnp.maximum(m, jnp.max(s, axis=-1, keepdims=True))
        pe = jnp.exp2(s - m_new).astype(BF16)
        alpha = jnp.exp2(m - m_new)
        for part in range(2):
            slabs = slice(part * n_pairs, (part + 1) * n_pairs)
            pv = jnp.dot(pe[slabs].reshape(half, DSA_KC), v_one[rows, :], preferred_element_type=F32)
            acc_s[slabs] = alpha[slabs] * acc_s[slabs] + pv.reshape(n_pairs, DSA_QB, LANES)
        return m_new

    acc_s[...] = jnp.zeros_like(acc_s)
    lax.fori_loop(0, n_live, attend, jnp.full((A_HEADS, DSA_QB, 1), masked, F32))
    for p in range(n_pairs):
        even, odd = acc_s[p], acc_s[n_pairs + p]
        even_swapped = pltpu.roll(even, HEAD_DIM, 1)
        odd_swapped = pltpu.roll(odd, HEAD_DIM, 1)
        o_ref[0, :, p * LANES:(p + 1) * LANES] = jnp.where(lo_half, even / even_swapped, odd_swapped / odd)


def _dsa(z3, cos_a, sin_a, cos_b, sin_b, g_kv, w_ukv_bf16):
    full = lambda width, col: pl.BlockSpec((1, SEQ, width), lambda b, i: (b, 0, col // width))
    blk = lambda width, col: pl.BlockSpec((1, DSA_QB, width), lambda b, i: (b, i, col // width))
    tab = pl.BlockSpec((1, SEQ, LANES), lambda b, i: (b, 0, 0))
    return pl.pallas_call(
        _dsa_kernel,
        grid=(BATCH, SEQ // DSA_QB),
        in_specs=[full(A_KV_LORA, COL_CKV), full(LANES, COL_IKIW),
                  blk(A_HEADS * HEAD_DIM, COL_QA), blk(IDX_HEADS * IDX_DIM, COL_IQ),
                  tab, tab, tab, tab,
                  pl.BlockSpec((1, A_KV_LORA), lambda b, i: (0, 0)),
                  pl.BlockSpec((A_KV_LORA, LANES), lambda b, i: (0, 0))],
        out_specs=pl.BlockSpec((1, DSA_QB, A_HEADS * HEAD_DIM), lambda b, i: (b, i, 0)),
        out_shape=jax.ShapeDtypeStruct((BATCH, SEQ, A_HEADS * HEAD_DIM), F32),
        scratch_shapes=[pltpu.VMEM((SEQ, LANES), BF16)] * 5
                       + [pltpu.VMEM((SEQ, DSA_QB), I32),
                          pltpu.VMEM((SEQ // DSA_KC, DSA_QB, DSA_KC), F32),
                          pltpu.VMEM((A_HEADS, DSA_QB, LANES), F32),
                          pltpu.VMEM((A_HEADS * DSA_QB, LANES), BF16)],
        compiler_params=_params(("arbitrary", "arbitrary")),
        name="dsa_mixer",
    )(z3, z3, z3, z3, cos_a, sin_a, cos_b, sin_b, g_kv.reshape(1, A_KV_LORA), w_ukv_bf16)


SB_BLK = 128
SB_EXP_UNDERFLOW = -105.0


def _sb_kernel(q_ref, k_ref, v_ref, o_ref, q_s, tail_s, acc_s):
    qi = pl.program_id(1)
    n_pairs = B_HEADS // 2
    rows_all = B_HEADS * SB_BLK
    lane = lax.broadcasted_iota(I32, (1, LANES), 1)
    lo_half = lane < 64
    causal = (lax.broadcasted_iota(I32, (rows_all, SB_BLK), 1)
              < (lax.broadcasted_iota(I32, (rows_all, SB_BLK), 0) & (SB_BLK - 1)))
    r2 = lax.broadcasted_iota(I32, (2 * SB_BLK, 2 * SB_BLK), 0) & (SB_BLK - 1)
    c2 = lax.broadcasted_iota(I32, (2 * SB_BLK, 2 * SB_BLK), 1)
    suffix_mat = ((c2 >= SB_BLK) | (r2 > c2)).astype(BF16)

    for p in range(n_pairs):
        q2 = q_ref[0, :, p * LANES:(p + 1) * LANES] * (HEAD_DIM ** -0.5)
        q_s[2 * p * SB_BLK:(2 * p + 1) * SB_BLK, :] = jnp.where(lo_half, q2, 0.0).astype(BF16)
        q_s[(2 * p + 1) * SB_BLK:(2 * p + 2) * SB_BLK, :] = jnp.where(lo_half, 0.0, q2).astype(BF16)

    def key_block(kb, diagonal):
        r0 = pl.multiple_of(kb * SB_BLK, SB_BLK)
        pair_rows = lambda p: slice(2 * p * SB_BLK, (2 * p + 2) * SB_BLK)
        z = jnp.concatenate(
            [lax.dot_general(q_s[pair_rows(p), :], k_ref[0, pl.ds(r0, SB_BLK), p * LANES:(p + 1) * LANES].astype(BF16),
                             NT_DIMS, preferred_element_type=F32) for p in range(n_pairs)], axis=0)
        log_beta = jnp.minimum(z, 0.0) - jnp.log(1.0 + jnp.exp(-jnp.abs(z)))
        log_not = log_beta - z
        if diagonal:
            log_not = jnp.where(causal, log_not, 0.0)
        hi = log_not.astype(BF16)
        lo = (log_not - hi.astype(F32)).astype(BF16)
        hi_lo = jnp.concatenate([hi, lo], axis=1)
        t = jnp.concatenate([jnp.dot(hi_lo[pair_rows(p), :], suffix_mat, preferred_element_type=F32)
                             for p in range(n_pairs)], axis=0)
        if diagonal:
            a = jnp.where(causal, jnp.exp(log_beta + t[:, :SB_BLK]), 0.0)
            tail = t[:, SB_BLK:]
        else:
            a = jnp.exp(log_beta + t[:, :SB_BLK] + tail_s[...])
            tail = tail_s[...] + t[:, SB_BLK:]
        tail_s[...] = tail
        a = a.astype(BF16)
        for p in range(n_pairs):
            v2 = v_ref[0, pl.ds(r0, SB_BLK), p * LANES:(p + 1) * LANES].astype(BF16)
            pv = jnp.dot(a[pair_rows(p), :], v2, preferred_element_type=F32)
            if diagonal:
                acc_s[pair_rows(p), :] = pv
            else:
                acc_s[pair_rows(p), :] += pv
        return jnp.max(tail)

    first = key_block(qi, True)

    def more(state):
        jj, worst = state
        return (jj <= qi) & (worst > SB_EXP_UNDERFLOW)

    def step(state):
        jj, _ = state
        return jj + 1, key_block(qi - jj, False)

    lax.while_loop(more, step, (jnp.int32(1), first))
    for p in range(n_pairs):
        even = acc_s[2 * p * SB_BLK:(2 * p + 1) * SB_BLK, :]
        odd = acc_s[(2 * p + 1) * SB_BLK:(2 * p + 2) * SB_BLK, :]
        o_ref[0, :, p * LANES:(p + 1) * LANES] = jnp.where(lo_half, even, odd)


def _sb(z3):
    width = B_HEADS * HEAD_DIM
    return pl.pallas_call(
        _sb_kernel,
        grid=(BATCH, SEQ // SB_BLK),
        in_specs=[pl.BlockSpec((1, SB_BLK, width), lambda b, i: (b, i, COL_QB // width)),
                  pl.BlockSpec((1, SEQ, width), lambda b, i: (b, 0, COL_KB // width)),
                  pl.BlockSpec((1, SEQ, width), lambda b, i: (b, 0, COL_VB // width))],
        out_specs=pl.BlockSpec((1, SB_BLK, width), lambda b, i: (b, i, 0)),
        out_shape=jax.ShapeDtypeStruct((BATCH, SEQ, width), F32),
        scratch_shapes=[pltpu.VMEM((B_HEADS * SB_BLK, LANES), BF16),
                        pltpu.VMEM((B_HEADS * SB_BLK, SB_BLK), F32),
                        pltpu.VMEM((B_HEADS * SB_BLK, SB_BLK), F32)],
        compiler_params=_params(("arbitrary", "arbitrary")),
        name="stickbreak_mixer",
    )(z3, z3, z3)


RET_BLK = 256


def _ret_kernel(dec_ref, q_ref, k_ref, v_ref, gate_ref, cos_a, sin_a, qs_ref, ks_ref, e_diag, gret_ref,
                o_ref, state):
    lane = lax.broadcasted_iota(I32, (1, LANES), 1)
    lo_half = lane < 64

    @pl.when(pl.program_id(1) == 0)
    def _():
        state[...] = jnp.zeros_like(state)

    cq, sq = cos_a[0], sin_a[0]
    for p in range(C_HEADS // 2):
        cols = slice(p * LANES, (p + 1) * LANES)
        q2 = _rope_lanes(q_ref[0, :, cols], cq, sq)
        k2 = _rope_lanes(k_ref[0, :, cols], cq, sq) * (HEAD_DIM ** -0.5)
        q_hat = q2 * qs_ref[:, cols]
        k_hat = k2 * ks_ref[:, cols]
        for e in range(2):
            h = 2 * p + e
            pick = (lambda x: jnp.where(lo_half, x, 0.0)) if e == 0 else (lambda x: jnp.where(lo_half, 0.0, x))
            vcols = slice(h * C_V_DIM, (h + 1) * C_V_DIM)
            v2 = v_ref[0, :, vcols].astype(BF16)
            s = lax.dot_general(pick(q2).astype(BF16), k2.astype(BF16), NT_DIMS,
                                preferred_element_type=F32) * e_diag[h]
            o = jnp.dot(s.astype(BF16), v2, preferred_element_type=F32)
            o = o + jnp.dot(pick(q_hat).astype(BF16), state[h].astype(BF16), preferred_element_type=F32)
            kv = jnp.dot(pick(k_hat).T.astype(BF16), v2, preferred_element_type=F32)
            state[h] = dec_ref[h] * state[h] + kv
            mu = jnp.mean(o, axis=-1, keepdims=True)
            d = o - mu
            var = jnp.mean(d * d, axis=-1, keepdims=True)
            y = d * lax.rsqrt(var + EPS) * gret_ref[:, vcols]
            g = gate_ref[0, :, vcols]
            o_ref[0, :, vcols] = g * _sigmoid(g) * y


def _ret(z3, cos_a, sin_a, dec, q_scale, k_scale, e_diag, g_ret):
    qk_w = C_HEADS * HEAD_DIM
    v_w = C_HEADS * C_V_DIM
    tab = pl.BlockSpec((1, RET_BLK, LANES), lambda b, i: (b, i, 0))
    const2 = lambda a: pl.BlockSpec(a.shape, lambda b, i: (0, 0))
    return pl.pallas_call(
        _ret_kernel,
        grid=(BATCH, SEQ // RET_BLK),
        in_specs=[pl.BlockSpec(memory_space=pltpu.SMEM),
                  pl.BlockSpec((1, RET_BLK, qk_w), lambda b, i: (b, i, COL_QC // qk_w)),
                  pl.BlockSpec((1, RET_BLK, qk_w), lambda b, i: (b, i, COL_KC // qk_w)),
                  pl.BlockSpec((1, RET_BLK, v_w), lambda b, i: (b, i, COL_VC // v_w)),
                  pl.BlockSpec((1, RET_BLK, v_w), lambda b, i: (b, i, COL_GC // v_w)),
                  tab, tab, const2(q_scale), const2(k_scale),
                  pl.BlockSpec(e_diag.shape, lambda b, i: (0, 0, 0)),
                  pl.BlockSpec((1, v_w), lambda b, i: (0, 0))],
        out_specs=pl.BlockSpec((1, RET_BLK, v_w), lambda b, i: (b, i, 0)),
        out_shape=jax.ShapeDtypeStruct((BATCH, SEQ, v_w), F32),
        scratch_shapes=[pltpu.VMEM((C_HEADS, LANES, C_V_DIM), F32)],
        compiler_params=_params(("arbitrary", "arbitrary")),
        name="retention_mixer",
    )(dec, z3, z3, z3, z3, cos_a, sin_a, q_scale, k_scale, e_diag, g_ret.reshape(1, v_w))


MERGE_TM = 256


def _rms(y, g):
    ms = jnp.mean(y * y, axis=-1, keepdims=True)
    return y * lax.rsqrt(ms + EPS) * g


def _route(logits_t, bias_col):
    n_tok = logits_t.shape[1]
    per_group = N_EXPERTS // N_GROUPS
    scores = _sigmoid(logits_t)
    sel = scores + bias_col
    sel3 = sel.reshape(N_GROUPS, per_group, n_tok)
    member = lax.broadcasted_iota(I32, (N_GROUPS, per_group, n_tok), 1)
    m1 = jnp.max(sel3, axis=1, keepdims=True)
    first = jnp.min(jnp.where(sel3 == m1, member, per_group), axis=1, keepdims=True)
    m2 = jnp.max(jnp.where(member == first, -jnp.inf, sel3), axis=1, keepdims=True)
    grp = (m1 + m2).reshape(N_GROUPS, n_tok)
    gid = lax.broadcasted_iota(I32, (N_GROUPS, n_tok), 0)
    rank = jnp.zeros((N_GROUPS, n_tok), I32)
    for g in range(N_GROUPS):
        other = grp[g:g + 1, :]
        rank = rank + ((other > grp) | ((other == grp) & (g < gid))).astype(I32)
    keep = jnp.where(rank < TOPK_GROUPS, 1.0, 0.0).reshape(N_GROUPS, 1, n_tok)
    live = jnp.where(keep > 0.0, sel3, -jnp.inf).reshape(N_EXPERTS, n_tok)
    eid = lax.broadcasted_iota(I32, (N_EXPERTS, n_tok), 0)
    chosen = jnp.zeros((N_EXPERTS, n_tok), jnp.bool_)
    for _ in range(TOP_K):
        m = jnp.max(live, axis=0, keepdims=True)
        pick = eid == jnp.min(jnp.where(live == m, eid, N_EXPERTS), axis=0, keepdims=True)
        chosen = chosen | pick
        live = jnp.where(pick, -jnp.inf, live)
    w = jnp.where(chosen, scores, 0.0)
    return w / jnp.sum(w, axis=0, keepdims=True) * ROUTED_SCALE, chosen


def _merge_kernel(ya_ref, yb_ref, yc_ref, g_ref, x_ref, gt_ref, gpost_ref, gpre_ref, sc_ref, sh_ref,
                  woa, wob, woc, wout, wr_t, rb_ref, xo_ref, h_ref, gate_ref, rank_ref):
    ya = jnp.dot(ya_ref[...].astype(BF16), woa[...], preferred_element_type=F32)
    yb = jnp.dot(yb_ref[...].astype(BF16), wob[...], preferred_element_type=F32)
    yc = jnp.dot(yc_ref[...].astype(BF16), woc[...], preferred_element_type=F32)
    merged = (_sigmoid(g_ref[:, 0:D_MODEL]) * ya
              + _sigmoid(g_ref[:, D_MODEL:2 * D_MODEL]) * yb
              + _sigmoid(g_ref[:, 2 * D_MODEL:3 * D_MODEL]) * yc)
    y = jnp.dot(merged.astype(BF16), wout[...], preferred_element_type=F32)
    x = x_ref[...] + gt_ref[0] * _rms(y, gpost_ref[...])
    xo_ref[...] = x
    h = (_rms(x, gpre_ref[...]) * (1.0 + sc_ref[0]) + sh_ref[0]).astype(BF16)
    h_ref[...] = h
    logits_t = lax.dot_general(wr_t[...], h, NT_DIMS, preferred_element_type=F32)
    gates_t, chosen = _route(logits_t, rb_ref[...])
    gate_ref[...] = gates_t
    earlier = (lax.broadcasted_iota(I32, (MERGE_TM, MERGE_TM), 0)
               < lax.broadcasted_iota(I32, (MERGE_TM, MERGE_TM), 1)).astype(BF16)
    before = jnp.dot(jnp.where(chosen, 1.0, 0.0).astype(BF16), earlier, preferred_element_type=F32)
    rank_ref[...] = jnp.where(chosen, before, -1.0)


def _merge(ya, yb, yc, z, x, gt1, g_post, g_pre, sc2, sh2, woa, wob, woc, wout, wr_t, rbias):
    tpb = SEQ // MERGE_TM
    row = lambda w: pl.BlockSpec((MERGE_TM, w), lambda i: (i, 0))
    vec = pl.BlockSpec((1, D_MODEL), lambda i: (0, 0))
    mod = pl.BlockSpec((1, 1, D_MODEL), lambda i: (i // tpb, 0, 0))
    whole = lambda a: pl.BlockSpec(a.shape, lambda i: (0, 0))
    return pl.pallas_call(
        _merge_kernel,
        grid=(N_TOK // MERGE_TM,),
        in_specs=[row(512), row(512), row(1024), pl.BlockSpec((MERGE_TM, 3 * D_MODEL), lambda i: (i, 0)),
                  row(D_MODEL), mod, vec, vec, mod, mod,
                  whole(woa), whole(wob), whole(woc), whole(wout), whole(wr_t), whole(rbias)],
        out_specs=[row(D_MODEL), row(D_MODEL),
                   pl.BlockSpec((N_EXPERTS, MERGE_TM), lambda i: (0, i)),
                   pl.BlockSpec((N_EXPERTS, MERGE_TM), lambda i: (0, i))],
        out_shape=[jax.ShapeDtypeStruct((N_TOK, D_MODEL), F32),
                   jax.ShapeDtypeStruct((N_TOK, D_MODEL), BF16),
                   jax.ShapeDtypeStruct((N_EXPERTS, N_TOK), F32),
                   jax.ShapeDtypeStruct((N_EXPERTS, N_TOK), F32)],
        compiler_params=_params(("arbitrary",)),
        name="merge_route",
    )(ya, yb, yc, z, x, gt1, g_post.reshape(1, D_MODEL), g_pre.reshape(1, D_MODEL), sc2, sh2,
      woa, wob, woc, wout, wr_t, rbias)


MOE_TM = 1024
MOE_SUB = MERGE_TM
MOE_EG = 4
MOE_CAP = 64
MOE_NSUB = MOE_TM // MOE_SUB
MOE_SLAB = MOE_EG * MOE_CAP


def _moe_kernel(h_ref, gate_ref, rank_ref, x_ref, gt_ref, gpost_ref, w1, w3, w2, ws1, ws3, ws2,
                o_ref, acc, xg_s, yw_s, pick_s, back_s):
    g_idx = pl.program_id(1)

    @pl.when(g_idx == 0)
    def _():
        acc[...] = jnp.zeros_like(acc)

    rank = rank_ref[0]
    gate = gate_ref[0]
    top_rank = jnp.max(rank)
    n_rounds = (top_rank.astype(I32) + MOE_CAP) // MOE_CAP
    slot = lax.broadcasted_iota(I32, (MOE_CAP, 1), 0).astype(F32)
    sub_top = [jnp.max(rank[:, s * MOE_SUB:(s + 1) * MOE_SUB]) for s in range(MOE_NSUB)]
    exp_top = [jnp.max(rank[e:e + 1, :]) for e in range(MOE_EG)]

    def make_pick(s, base):
        toks = slice(s * MOE_SUB, (s + 1) * MOE_SUB)
        return jnp.concatenate(
            [jnp.where(rank[e:e + 1, toks] == slot + base, 1.0, 0.0) for e in range(MOE_EG)], axis=0)

    def expert_rows(e, pick_of):
        rows = slice(e * MOE_CAP, (e + 1) * MOE_CAP)
        x_e = jnp.concatenate([xg_s[s, rows, :] for s in range(MOE_NSUB)], axis=0)
        a = jnp.dot(x_e, w1[e], preferred_element_type=F32)
        hid = a * _sigmoid(a) * jnp.dot(x_e, w3[e], preferred_element_type=F32)
        y = jnp.dot(hid.astype(BF16), w2[e], preferred_element_type=F32)
        for s in range(MOE_NSUB):
            toks = slice(s * MOE_SUB, (s + 1) * MOE_SUB)
            w_row = jnp.sum(pick_of(s)[rows, :] * gate[e:e + 1, toks], axis=1, keepdims=True)
            yw_s[s, rows, :] = (y[s * MOE_CAP:(s + 1) * MOE_CAP, :] * w_row).astype(BF16)

    def scatter(s, back):
        toks = slice(s * MOE_SUB, (s + 1) * MOE_SUB)
        acc[toks, :] += jnp.dot(back, yw_s[s], preferred_element_type=F32)

    picks = [make_pick(s, 0.0) for s in range(MOE_NSUB)]
    for s in range(MOE_NSUB):
        toks = slice(s * MOE_SUB, (s + 1) * MOE_SUB)
        xg_s[s] = jnp.dot(picks[s].astype(BF16), h_ref[toks, :], preferred_element_type=F32).astype(BF16)
    for e in range(MOE_EG):
        expert_rows(e, lambda s: picks[s])
    for s in range(MOE_NSUB):
        scatter(s, picks[s].T.astype(BF16))

    @pl.when(n_rounds > 1)
    def _():
        for s in range(MOE_NSUB):
            pick_s[s] = picks[s].astype(BF16)

    def overflow_round(r, carry):
        base = (r * MOE_CAP).astype(F32)
        for s in range(MOE_NSUB):
            @pl.when(sub_top[s] >= base)
            def _(s=s):
                pick = make_pick(s, base)
                pick_s[s] = pick.astype(BF16)
                back_s[s] = pick.T.astype(BF16)
                toks = slice(s * MOE_SUB, (s + 1) * MOE_SUB)
                xg_s[s] = jnp.dot(pick.astype(BF16), h_ref[toks, :], preferred_element_type=F32).astype(BF16)
        for e in range(MOE_EG):
            @pl.when(exp_top[e] >= base)
            def _(e=e):
                expert_rows(e, lambda s: pick_s[s].astype(F32))
        for s in range(MOE_NSUB):
            @pl.when(sub_top[s] >= base)
            def _(s=s):
                scatter(s, back_s[s])
        return carry

    lax.fori_loop(1, n_rounds, overflow_round, 0)

    @pl.when(g_idx == N_EXPERTS // MOE_EG - 1)
    def _():
        h = h_ref[...]
        s = jnp.dot(h, ws1[...], preferred_element_type=F32)
        shid = s * _sigmoid(s) * jnp.dot(h, ws3[...], preferred_element_type=F32)
        y_all = acc[...] + jnp.dot(shid.astype(BF16), ws2[...], preferred_element_type=F32)
        o_ref[...] = x_ref[...] + gt_ref[0] * _rms(y_all, gpost_ref[...])


def _moe(h, gates_t, ranks_t, x, gt2, g_post, w1, w3, w2, ws1, ws3, ws2, layer):
    tpb = SEQ // MOE_TM
    n_groups = N_EXPERTS // MOE_EG
    whole = lambda a: pl.BlockSpec(a.shape, lambda i, g: (0, 0))
    per_group = pl.BlockSpec((1, MOE_EG, MOE_TM), lambda i, g: (g, 0, i))
    return pl.pallas_call(
        _moe_kernel,
        grid=(N_TOK // MOE_TM, n_groups),
        in_specs=[pl.BlockSpec((MOE_TM, D_MODEL), lambda i, g: (i, 0)),
                  per_group, per_group,
                  pl.BlockSpec((MOE_TM, D_MODEL), lambda i, g: (i, 0)),
                  pl.BlockSpec((1, 1, D_MODEL), lambda i, g: (i // tpb, 0, 0)),
                  pl.BlockSpec((1, D_MODEL), lambda i, g: (0, 0)),
                  pl.BlockSpec((None, MOE_EG, D_MODEL, EXPERT_DIM), lambda i, g: (layer, g, 0, 0)),
                  pl.BlockSpec((None, MOE_EG, D_MODEL, EXPERT_DIM), lambda i, g: (layer, g, 0, 0)),
                  pl.BlockSpec((None, MOE_EG, EXPERT_DIM, D_MODEL), lambda i, g: (layer, g, 0, 0)),
                  whole(ws1), whole(ws3), whole(ws2)],
        out_specs=pl.BlockSpec((MOE_TM, D_MODEL), lambda i, g: (i, 0)),
        out_shape=jax.ShapeDtypeStruct((N_TOK, D_MODEL), F32),
        scratch_shapes=[pltpu.VMEM((MOE_TM, D_MODEL), F32),
                        pltpu.VMEM((MOE_NSUB, MOE_SLAB, D_MODEL), BF16),
                        pltpu.VMEM((MOE_NSUB, MOE_SLAB, D_MODEL), BF16),
                        pltpu.VMEM((MOE_NSUB, MOE_SLAB, MOE_SUB), BF16),
                        pltpu.VMEM((MOE_NSUB, MOE_SUB, MOE_SLAB), BF16)],
        compiler_params=_params(("arbitrary", "arbitrary")),
        name="moe_ffn",
    )(h, gates_t.reshape(n_groups, MOE_EG, N_TOK), ranks_t.reshape(n_groups, MOE_EG, N_TOK),
      x, gt2, g_post.reshape(1, D_MODEL), w1, w3, w2, ws1, ws3, ws2)


_IN_SIZES = (512, 256, 256, 64, 4, 512, 512, 512, 512, 512, 1024, 1024, 3072)
_IN_DEST = (COL_QA, COL_CKV, COL_IQ, COL_IKIW, COL_IKIW + 64, COL_QB, COL_KB, COL_VB, COL_QC, COL_KC,
            COL_VC, COL_GC, COL_GATES)
_IN_SRC = tuple(int(v) for v in np.cumsum((0,) + _IN_SIZES[:-1]))
N_IN = sum(_IN_SIZES)
PACK_TK = 128


def _pack_kernel(w_ref, o_ref, *, layer):
    for src, width, dst in zip(_IN_SRC, _IN_SIZES, _IN_DEST):
        o_ref[dst:dst + width, :] = w_ref[src:src + width, layer, :].astype(BF16)
    o_ref[COL_IKIW + 68:COL_IKIW + LANES, :] = jnp.zeros((LANES - 68, PACK_TK), BF16)


def _reorder_w_in(w_t, layer):
    return pl.pallas_call(
        functools.partial(_pack_kernel, layer=layer),
        grid=(D_MODEL // PACK_TK,),
        in_specs=[pl.BlockSpec((N_IN, DEPTH, PACK_TK), lambda i: (0, 0, i))],
        out_specs=pl.BlockSpec((N_PROJ, PACK_TK), lambda i: (0, i)),
        out_shape=jax.ShapeDtypeStruct((N_PROJ, D_MODEL), BF16),
        compiler_params=_params(("arbitrary",)),
        name="pack_w_in",
    )(w_t)


def _rope_tables(positions):
    half = HEAD_DIM // 2
    inv_freq = ROPE_THETA ** (-jnp.arange(half, dtype=F32) / half)
    ang = positions.astype(F32)[:, :, None] * inv_freq
    cos, sin = jnp.cos(ang), jnp.sin(ang)
    cos64 = jnp.concatenate([cos, cos], axis=-1)
    sin64 = jnp.concatenate([-sin, sin], axis=-1)
    cos_a = jnp.concatenate([cos64, cos64], axis=-1)
    sin_a = jnp.concatenate([sin64, sin64], axis=-1)
    cos_b = jnp.concatenate([cos64, jnp.ones_like(cos64)], axis=-1)
    sin_b = jnp.concatenate([sin64, jnp.zeros_like(sin64)], axis=-1)
    return cos_a, sin_a, cos_b, sin_b


def _decay_tables():
    log_g = jnp.log(1.0 - 2.0 ** (-5.0 - jnp.arange(C_HEADS, dtype=F32)))
    pos = jnp.arange(RET_BLK, dtype=F32)
    diff = pos[:, None] - pos[None, :]
    chunk = jnp.arange(RET_BLK) // CHUNK
    visible = chunk[None, :] <= chunk[:, None]
    e_diag = jnp.where(visible, jnp.exp(log_g[:, None, None] * jnp.abs(diff)), 0.0)
    dec = jnp.exp(log_g * RET_BLK)
    per_lane = jnp.repeat(log_g, HEAD_DIM)[None, :]
    q_scale = jnp.exp(per_lane * pos[:, None])
    k_scale = jnp.exp(per_lane * (RET_BLK - pos)[:, None])
    return dec, q_scale, k_scale, e_diag


def kernel(x, c, positions, ada_w, ada_b, g_pre_mix, g_post_mix, g_pre_ffn, g_post_ffn, w_in, g_kv, w_uk, w_uv,
           g_ret, w_o_a, w_o_b, w_o_c, w_out, w_router, router_bias, w1, w3, w2, ws1, ws3, ws2):
    cos_a, sin_a, cos_b, sin_b = _rope_tables(positions)
    dec, q_scale, k_scale, e_diag = _decay_tables()
    xf = x.reshape(N_TOK, D_MODEL)
    w_in_t = jnp.transpose(w_in, (2, 0, 1))
    w1_b, w3_b, w2_b = w1.astype(BF16), w3.astype(BF16), w2.astype(BF16)
    for l in range(DEPTH):
        mod = _ada(c, ada_w, ada_b, l).reshape(BATCH, 1, N_ADA * D_MODEL)
        sh1, sc1, gt1, sh2, sc2, gt2 = (mod[:, :, k * D_MODEL:(k + 1) * D_MODEL] for k in range(N_ADA))
        z = _inproj(xf, g_pre_mix[l], sc1, sh1, _reorder_w_in(w_in_t, l))
        z3 = z.reshape(BATCH, SEQ, N_PROJ)
        w_ukv = jnp.concatenate([w_uk[l], w_uv[l]], axis=1).astype(BF16)
        ya = _dsa(z3, cos_a, sin_a, cos_b, sin_b, g_kv[l], w_ukv).reshape(N_TOK, -1)
        yb = _sb(z3).reshape(N_TOK, -1)
        yc = _ret(z3, cos_a, sin_a, dec, q_scale, k_scale, e_diag, g_ret[l]).reshape(N_TOK, -1)
        xf, h, gates, ranks = _merge(ya, yb, yc, z, xf, gt1, g_post_mix[l], g_pre_ffn[l], sc2, sh2,
                              w_o_a[l].astype(BF16), w_o_b[l].astype(BF16), w_o_c[l].astype(BF16),
                              w_out[l].astype(BF16), w_router[l].T.astype(BF16),
                              router_bias[l].reshape(N_EXPERTS, 1))
        xf = _moe(h, gates, ranks, xf, gt2, g_post_ffn[l], w1_b, w3_b, w2_b,
                  ws1[l].astype(BF16), ws3[l].astype(BF16), ws2[l].astype(BF16), l)
    return xf.reshape(BATCH, SEQ, D_MODEL)
```

```python
import functools
import math

import numpy as np
import jax
import jax.numpy as jnp
from jax import lax
from jax.experimental import pallas as pl
from jax.experimental.pallas import tpu as pltpu

F32 = jnp.float32
BF16 = jnp.bfloat16
I32 = jnp.int32

D_MODEL = 1024
BATCH = 8
SEQ = 2048
DEPTH = 2
N_TOK = BATCH * SEQ

CHUNK = 64
HEAD_DIM = 64
ROPE_THETA = 10000.0
EPS = 1e-6
A_HEADS = 8
A_KV_LORA = 256
IDX_HEADS = 4
IDX_DIM = 64
TOPK = 256
B_HEADS = 8
C_HEADS = 8
C_V_DIM = 128
N_EXPERTS = 64
TOP_K = 8
N_GROUPS = 8
TOPK_GROUPS = 4
EXPERT_DIM = 256
SHARED_DIM = 256
ROUTED_SCALE = 2.5
N_ADA = 6

LANES = 128
VMEM_LIMIT = 56 * 1024 * 1024

COL_GATES = 0
COL_VC = 3072
COL_GC = 4096
COL_QA = 5120
COL_QB = 5632
COL_KB = 6144
COL_VB = 6656
COL_QC = 7168
COL_KC = 7680
COL_CKV = 8192
COL_IQ = 8448
COL_IKIW = 8704
N_PROJ = 8832
PROJ_TN = 2944
PROJ_TM = 1024

INT_MIN = -2147483648
NEG_INF_KEY = -2139095041

NT_DIMS = (((1,), (1,)), ((), ()))
LOG2_E = math.log2(math.e)


def _params(sem):
    return pltpu.CompilerParams(dimension_semantics=sem, vmem_limit_bytes=VMEM_LIMIT)


def _sigmoid(x):
    return 1.0 / (1.0 + jnp.exp(-x))


def _rope_lanes(x, cos, sin_signed):
    n = x.shape[-1]
    lane = lax.broadcasted_iota(I32, (1, n), 1)
    first = (lane & 32) == 0
    ahead = pltpu.roll(x, n - 32, 1)
    behind = pltpu.roll(x, 32, 1)
    return x * cos + jnp.where(first, ahead, behind) * sin_signed


def _ada_kernel(c_ref, w_ref, b_ref, o_ref):
    c = c_ref[...]
    s = (c * _sigmoid(c)).astype(BF16)
    o_ref[...] = jnp.dot(s, w_ref[...].astype(BF16), preferred_element_type=F32) + b_ref[...]


def _ada(c, w, b, layer):
    n = w.shape[2]
    return pl.pallas_call(
        _ada_kernel,
        grid=(n // D_MODEL,),
        in_specs=[pl.BlockSpec((BATCH, D_MODEL), lambda j: (0, 0)),
                  pl.BlockSpec((None, D_MODEL, D_MODEL), lambda j: (layer, 0, j)),
                  pl.BlockSpec((None, 1, D_MODEL), lambda j: (layer, 0, j))],
        out_specs=pl.BlockSpec((BATCH, D_MODEL), lambda j: (0, j)),
        out_shape=jax.ShapeDtypeStruct((BATCH, n), F32),
        compiler_params=_params(("arbitrary",)),
        name="ada_mod",
    )(c, w, b.reshape(DEPTH, 1, n))


def _inproj_kernel(x_ref, g_ref, sc_ref, sh_ref, w_ref, o_ref, h_ref):
    @pl.when(pl.program_id(1) == 0)
    def _():
        x = x_ref[...]
        ms = jnp.mean(x * x, axis=-1, keepdims=True)
        y = x * lax.rsqrt(ms + EPS) * g_ref[...]
        h_ref[...] = (y * (1.0 + sc_ref[0]) + sh_ref[0]).astype(BF16)

    o_ref[...] = lax.dot_general(h_ref[...], w_ref[...], NT_DIMS, preferred_element_type=F32)


def _inproj(x, g, sc, sh, w_t_bf16):
    tiles_per_batch = SEQ // PROJ_TM
    return pl.pallas_call(
        _inproj_kernel,
        grid=(N_TOK // PROJ_TM, N_PROJ // PROJ_TN),
        in_specs=[pl.BlockSpec((PROJ_TM, D_MODEL), lambda i, j: (i, 0)),
                  pl.BlockSpec((1, D_MODEL), lambda i, j: (0, 0)),
                  pl.BlockSpec((1, 1, D_MODEL), lambda i, j: (i // tiles_per_batch, 0, 0)),
                  pl.BlockSpec((1, 1, D_MODEL), lambda i, j: (i // tiles_per_batch, 0, 0)),
                  pl.BlockSpec((PROJ_TN, D_MODEL), lambda i, j: (j, 0))],
        out_specs=pl.BlockSpec((PROJ_TM, PROJ_TN), lambda i, j: (i, j)),
        out_shape=jax.ShapeDtypeStruct((N_TOK, N_PROJ), F32),
        scratch_shapes=[pltpu.VMEM((PROJ_TM, D_MODEL), BF16)],
        compiler_params=_params(("arbitrary", "arbitrary")),
        name="in_proj",
    )(x, g.reshape(1, D_MODEL), sc, sh, w_t_bf16)


DSA_QB = 128
DSA_KC = 256


def _dsa_kernel(ckv_ref, ikiw_ref, q_ref, iq_ref, cos_a, sin_a, gkv_ref, wukv_ref,
                o_ref, k_lo, k_hi, v_one, ik_lo, ik_hi, keys_s, selt_s, acc_s, q_s):
    qi = pl.program_id(1)
    lane = lax.broadcasted_iota(I32, (1, LANES), 1)
    lo_half = lane < 64
    n_kc = SEQ // DSA_KC

    @pl.when(qi == 0)
    def _prologue():
        def chunk(c, carry):
            r0 = pl.multiple_of(c * DSA_KC, DSA_KC)
            rows = pl.ds(r0, DSA_KC)
            ckv = ckv_ref[0, rows, :]
            ms = jnp.mean(ckv * ckv, axis=-1, keepdims=True)
            cn = (ckv * lax.rsqrt(ms + EPS) * gkv_ref[...]).astype(BF16)
            kv = jnp.dot(cn, wukv_ref[...], preferred_element_type=F32)
            cb = jnp.where(lo_half, cos_a[0, rows, :], 1.0)
            sb = jnp.where(lo_half, sin_a[0, rows, :], 0.0)
            kv = _rope_lanes(kv, cb, sb)
            swapped = pltpu.roll(kv, 64, 1)
            k_lo[rows, :] = jnp.where(lo_half, kv, 0.0).astype(BF16)
            k_hi[rows, :] = jnp.where(lo_half, 0.0, swapped).astype(BF16)
            v_one[rows, :] = jnp.where(lo_half, swapped, 1.0).astype(BF16)
            ik = jnp.where(lo_half, _rope_lanes(ikiw_ref[0, rows, :], cb, sb), 0.0)
            ik_lo[rows, :] = ik.astype(BF16)
            ik_hi[rows, :] = pltpu.roll(ik, 64, 1).astype(BF16)
            return carry
        lax.fori_loop(0, n_kc, chunk, 0)

    q0 = pl.multiple_of(qi * DSA_QB, DSA_QB)
    qrows = pl.ds(q0, DSA_QB)
    cq, sq = cos_a[0, qrows, :], sin_a[0, qrows, :]

    iq = iq_ref[0]
    iq_heads = []
    for p in range(IDX_HEADS // 2):
        pair = _rope_lanes(iq[:, p * LANES:(p + 1) * LANES], cq, sq)
        iq_heads.append(jnp.where(lo_half, pair, 0.0).astype(BF16))
        iq_heads.append(jnp.where(lo_half, 0.0, pair).astype(BF16))
    iq_stacked = [jnp.concatenate(iq_heads[parity::2], axis=0) for parity in range(2)]
    iw_t = ikiw_ref[0, qrows, :].T * (IDX_HEADS ** -0.5)
    q_chunk = (q0 + lane) >> 6
    n_live = lax.shift_right_logical(qi + 2, 1)

    def chunk_rows(c):
        return pl.ds(pl.multiple_of(c * DSA_KC, DSA_KC), DSA_KC)

    def score_chunk(c, carry):
        rows = chunk_rows(c)
        k_chunk = (c * DSA_KC + lax.broadcasted_iota(I32, (DSA_KC, 1), 0)) >> 6
        score = jnp.zeros((DSA_KC, DSA_QB), F32)
        for parity, ik in enumerate((ik_lo, ik_hi)):
            logits = lax.dot_general(ik[rows, :], iq_stacked[parity], NT_DIMS,
                                     preferred_element_type=F32) * (IDX_DIM ** -0.5)
            for j in range(IDX_HEADS // 2):
                h = parity + 2 * j
                score = score + iw_t[64 + h:65 + h, :] * jnp.maximum(logits[:, j * DSA_QB:(j + 1) * DSA_QB], 0.0)
        score = jnp.where(score == 0.0, 0.0, score)
        score = jnp.where(k_chunk <= q_chunk, score, -jnp.inf)
        bits = lax.bitcast_convert_type(score, I32)
        keys_s[rows, :] = bits ^ ((bits >> 31) & 0x7FFFFFFF)
        return carry
    lax.fori_loop(0, n_live, score_chunk, 0)

    def count(preds):
        def chunk(c, accs):
            key = keys_s[chunk_rows(c), :]
            return tuple(acc + jnp.sum(pred(key).astype(I32).reshape(4, DSA_KC // 4, DSA_QB), axis=0)
                         for acc, pred in zip(accs, preds))
        zero = jnp.zeros((DSA_KC // 4, DSA_QB), I32)
        accs = lax.fori_loop(0, n_live, chunk, (zero,) * len(preds))
        return [jnp.sum(acc, axis=0, keepdims=True) for acc in accs]

    def descend(it, prefix):
        cand = prefix | lax.shift_left(jnp.int32(1), 31 - it)
        bound = cand ^ INT_MIN
        return jnp.where(count([lambda k: k >= bound])[0] >= TOPK, cand, prefix)
    thr = lax.fori_loop(0, 32, descend, jnp.zeros((1, DSA_QB), I32)) ^ INT_MIN

    n_gt, n_eq = count([lambda k: k > thr, lambda k: k == thr])
    need = (TOPK - n_gt).astype(F32)
    tie_break = jnp.max(n_gt + n_eq) > TOPK

    @pl.when(tie_break)
    def _():
        tri = (lax.broadcasted_iota(I32, (DSA_KC, DSA_KC), 0)
               >= lax.broadcasted_iota(I32, (DSA_KC, DSA_KC), 1)).astype(BF16)

        def select_chunk(c, seen):
            key = keys_s[chunk_rows(c), :]
            eq = key == thr
            rank = jnp.dot(tri, eq.astype(BF16), preferred_element_type=F32) + seen
            sel = ((key > thr) | (eq & (rank <= need))) & (key > NEG_INF_KEY)
            selt_s[c] = sel.astype(F32).T
            return rank[DSA_KC - 1:DSA_KC, :]
        lax.fori_loop(0, n_live, select_chunk, jnp.zeros((1, DSA_QB), F32))

    @pl.when(jnp.logical_not(tie_break))
    def _():
        def select_chunk(c, carry):
            key = keys_s[chunk_rows(c), :]
            selt_s[c] = ((key >= thr) & (key > NEG_INF_KEY)).astype(F32).T
            return carry
        lax.fori_loop(0, n_live, select_chunk, 0)

    q = q_ref[0]
    n_pairs = A_HEADS // 2
    for p in range(n_pairs):
        pair = _rope_lanes(q[:, p * LANES:(p + 1) * LANES], cq, sq) * (HEAD_DIM ** -0.5 * LOG2_E)
        q_s[p * DSA_QB:(p + 1) * DSA_QB, :] = jnp.where(lo_half, pair, 0.0).astype(BF16)
        q_s[(n_pairs + p) * DSA_QB:(n_pairs + p + 1) * DSA_QB, :] = jnp.where(lo_half, 0.0, pair).astype(BF16)
    masked = -1e30
    half = n_pairs * DSA_QB

    def attend(c, m):
        rows = chunk_rows(c)
        bias = jnp.where(selt_s[c] > 0.0, 0.0, masked)
        s = jnp.concatenate(
            [lax.dot_general(q_s[:half, :], k_lo[rows, :], NT_DIMS, preferred_element_type=F32),
             lax.dot_general(q_s[half:, :], k_hi[rows, :], NT_DIMS, preferred_element_type=F32)], axis=0)
        s = s.reshape(A_HEADS, DSA_QB, DSA_KC) + bias[None]
        m_new = jnp.maximum(m, jnp.max(s, axis=-1, keepdims=True))
        pe = jnp.exp2(s - m_new).astype(BF16)
        alpha = jnp.exp2(m - m_new)
        for part in range(2):
            slabs = slice(part * n_pairs, (part + 1) * n_pairs)
            pv = jnp.dot(pe[slabs].reshape(half, DSA_KC), v_one[rows, :], preferred_element_type=F32)
            acc_s[slabs] = alpha[slabs] * acc_s[slabs] + pv.reshape(n_pairs, DSA_QB, LANES)
        return m_new

    acc_s[...] = jnp.zeros_like(acc_s)
    lax.fori_loop(0, n_live, attend, jnp.full((A_HEADS, DSA_QB, 1), masked, F32))
    for p in range(n_pairs):
        even, odd = acc_s[p], acc_s[n_pairs + p]
        even_swapped = pltpu.roll(even, HEAD_DIM, 1)
        odd_swapped = pltpu.roll(odd, HEAD_DIM, 1)
        o_ref[0, :, p * LANES:(p + 1) * LANES] = jnp.where(lo_half, even / even_swapped, odd_swapped / odd)


def _dsa(z3, cos_a, sin_a, g_kv, w_ukv_bf16):
    full = lambda width, col: pl.BlockSpec((1, SEQ, width), lambda b, i: (b, 0, col // width))
    blk = lambda width, col: pl.BlockSpec((1, DSA_QB, width), lambda b, i: (b, i, col // width))
    tab = pl.BlockSpec((1, SEQ, LANES), lambda b, i: (b, 0, 0))
    return pl.pallas_call(
        _dsa_kernel,
        grid=(BATCH, SEQ // DSA_QB),
        in_specs=[full(A_KV_LORA, COL_CKV), full(LANES, COL_IKIW),
                  blk(A_HEADS * HEAD_DIM, COL_QA), blk(IDX_HEADS * IDX_DIM, COL_IQ),
                  tab, tab,
                  pl.BlockSpec((1, A_KV_LORA), lambda b, i: (0, 0)),
                  pl.BlockSpec((A_KV_LORA, LANES), lambda b, i: (0, 0))],
        out_specs=pl.BlockSpec((1, DSA_QB, A_HEADS * HEAD_DIM), lambda b, i: (b, i, 0)),
        out_shape=jax.ShapeDtypeStruct((BATCH, SEQ, A_HEADS * HEAD_DIM), F32),
        scratch_shapes=[pltpu.VMEM((SEQ, LANES), BF16)] * 5
                       + [pltpu.VMEM((SEQ, DSA_QB), I32),
                          pltpu.VMEM((SEQ // DSA_KC, DSA_QB, DSA_KC), F32),
                          pltpu.VMEM((A_HEADS, DSA_QB, LANES), F32),
                          pltpu.VMEM((A_HEADS * DSA_QB, LANES), BF16)],
        compiler_params=_params(("arbitrary", "arbitrary")),
        name="dsa_mixer",
    )(z3, z3, z3, z3, cos_a, sin_a, g_kv.reshape(1, A_KV_LORA), w_ukv_bf16)


SB_BLK = 128
SB_EXP_UNDERFLOW = -105.0


def _sb_kernel(q_ref, k_ref, v_ref, o_ref, q_s, tail_s, acc_s):
    qi = pl.program_id(1)
    n_pairs = B_HEADS // 2
    rows_all = B_HEADS * SB_BLK
    lane = lax.broadcasted_iota(I32, (1, LANES), 1)
    lo_half = lane < 64
    causal = (lax.broadcasted_iota(I32, (rows_all, SB_BLK), 1)
              < (lax.broadcasted_iota(I32, (rows_all, SB_BLK), 0) & (SB_BLK - 1)))
    r2 = lax.broadcasted_iota(I32, (2 * SB_BLK, 2 * SB_BLK), 0) & (SB_BLK - 1)
    c2 = lax.broadcasted_iota(I32, (2 * SB_BLK, 2 * SB_BLK), 1)
    suffix_mat = ((c2 >= SB_BLK) | (r2 > c2)).astype(BF16)

    for p in range(n_pairs):
        q2 = q_ref[0, :, p * LANES:(p + 1) * LANES] * (HEAD_DIM ** -0.5)
        q_s[2 * p * SB_BLK:(2 * p + 1) * SB_BLK, :] = jnp.where(lo_half, q2, 0.0).astype(BF16)
        q_s[(2 * p + 1) * SB_BLK:(2 * p + 2) * SB_BLK, :] = jnp.where(lo_half, 0.0, q2).astype(BF16)

    def key_block(kb, diagonal):
        r0 = pl.multiple_of(kb * SB_BLK, SB_BLK)
        pair_rows = lambda p: slice(2 * p * SB_BLK, (2 * p + 2) * SB_BLK)
        z = jnp.concatenate(
            [lax.dot_general(q_s[pair_rows(p), :], k_ref[0, pl.ds(r0, SB_BLK), p * LANES:(p + 1) * LANES].astype(BF16),
                             NT_DIMS, preferred_element_type=F32) for p in range(n_pairs)], axis=0)
        log_beta = jnp.minimum(z, 0.0) - jnp.log(1.0 + jnp.exp(-jnp.abs(z)))
        log_not = log_beta - z
        if diagonal:
            log_not = jnp.where(causal, log_not, 0.0)
        hi = log_not.astype(BF16)
        lo = (log_not - hi.astype(F32)).astype(BF16)
        hi_lo = jnp.concatenate([hi, lo], axis=1)
        t = jnp.concatenate([jnp.dot(hi_lo[pair_rows(p), :], suffix_mat, preferred_element_type=F32)
                             for p in range(n_pairs)], axis=0)
        if diagonal:
            a = jnp.where(causal, jnp.exp(log_beta + t[:, :SB_BLK]), 0.0)
            tail = t[:, SB_BLK:]
        else:
            a = jnp.exp(log_beta + t[:, :SB_BLK] + tail_s[...])
            tail = tail_s[...] + t[:, SB_BLK:]
        tail_s[...] = tail
        a = a.astype(BF16)
        for p in range(n_pairs):
            v2 = v_ref[0, pl.ds(r0, SB_BLK), p * LANES:(p + 1) * LANES].astype(BF16)
            pv = jnp.dot(a[pair_rows(p), :], v2, preferred_element_type=F32)
            if diagonal:
                acc_s[pair_rows(p), :] = pv
            else:
                acc_s[pair_rows(p), :] += pv
        return jnp.max(tail)

    first = key_block(qi, True)

    def more(state):
        jj, worst = state
        return (jj <= qi) & (worst > SB_EXP_UNDERFLOW)

    def step(state):
        jj, _ = state
        return jj + 1, key_block(qi - jj, False)

    lax.while_loop(more, step, (jnp.int32(1), first))
    for p in range(n_pairs):
        even = acc_s[2 * p * SB_BLK:(2 * p + 1) * SB_BLK, :]
        odd = acc_s[(2 * p + 1) * SB_BLK:(2 * p + 2) * SB_BLK, :]
        o_ref[0, :, p * LANES:(p + 1) * LANES] = jnp.where(lo_half, even, odd)


def _sb(z3):
    width = B_HEADS * HEAD_DIM
    return pl.pallas_call(
        _sb_kernel,
        grid=(BATCH, SEQ // SB_BLK),
        in_specs=[pl.BlockSpec((1, SB_BLK, width), lambda b, i: (b, i, COL_QB // width)),
                  pl.BlockSpec((1, SEQ, width), lambda b, i: (b, 0, COL_KB // width)),
                  pl.BlockSpec((1, SEQ, width), lambda b, i: (b, 0, COL_VB // width))],
        out_specs=pl.BlockSpec((1, SB_BLK, width), lambda b, i: (b, i, 0)),
        out_shape=jax.ShapeDtypeStruct((BATCH, SEQ, width), F32),
        scratch_shapes=[pltpu.VMEM((B_HEADS * SB_BLK, LANES), BF16),
                        pltpu.VMEM((B_HEADS * SB_BLK, SB_BLK), F32),
                        pltpu.VMEM((B_HEADS * SB_BLK, SB_BLK), F32)],
        compiler_params=_params(("arbitrary", "arbitrary")),
        name="stickbreak_mixer",
    )(z3, z3, z3)


RET_BLK = 256


def _ret_kernel(dec_ref, q_ref, k_ref, v_ref, gate_ref, cos_a, sin_a, qs_ref, ks_ref, e_diag, gret_ref,
                o_ref, state):
    lane = lax.broadcasted_iota(I32, (1, LANES), 1)
    lo_half = lane < 64
    row_lo = lax.broadcasted_iota(I32, (LANES, 1), 0) < 64

    @pl.when(pl.program_id(1) == 0)
    def _():
        state[...] = jnp.zeros_like(state)

    cq, sq = cos_a[0], sin_a[0]
    for p in range(C_HEADS // 2):
        cols = slice(p * LANES, (p + 1) * LANES)
        q2 = _rope_lanes(q_ref[0, :, cols], cq, sq)
        k2 = _rope_lanes(k_ref[0, :, cols], cq, sq) * (HEAD_DIM ** -0.5)
        q_hat = q2 * qs_ref[:, cols]
        k_hat_t = (k2 * ks_ref[:, cols]).T
        for e in range(2):
            h = 2 * p + e
            pick = (lambda x: jnp.where(lo_half, x, 0.0)) if e == 0 else (lambda x: jnp.where(lo_half, 0.0, x))
            vcols = slice(h * C_V_DIM, (h + 1) * C_V_DIM)
            v2 = v_ref[0, :, vcols].astype(BF16)
            s = lax.dot_general(pick(q2).astype(BF16), k2.astype(BF16), NT_DIMS,
                                preferred_element_type=F32) * e_diag[h]
            o = jnp.dot(s.astype(BF16), v2, preferred_element_type=F32)
            o = o + jnp.dot(pick(q_hat).astype(BF16), state[h].astype(BF16), preferred_element_type=F32)
            kv = jnp.dot(jnp.where(row_lo if e == 0 else ~row_lo, k_hat_t, 0.0).astype(BF16), v2,
                         preferred_element_type=F32)
            state[h] = dec_ref[h] * state[h] + kv
            mu = jnp.mean(o, axis=-1, keepdims=True)
            d = o - mu
            var = jnp.mean(d * d, axis=-1, keepdims=True)
            y = d * lax.rsqrt(var + EPS) * gret_ref[:, vcols]
            g = gate_ref[0, :, vcols]
            o_ref[0, :, vcols] = g * _sigmoid(g) * y


def _ret(z3, cos_a, sin_a, dec, q_scale, k_scale, e_diag, g_ret):
    qk_w = C_HEADS * HEAD_DIM
    v_w = C_HEADS * C_V_DIM
    tab = pl.BlockSpec((1, RET_BLK, LANES), lambda b, i: (b, i, 0))
    const2 = lambda a: pl.BlockSpec(a.shape, lambda b, i: (0, 0))
    return pl.pallas_call(
        _ret_kernel,
        grid=(BATCH, SEQ // RET_BLK),
        in_specs=[pl.BlockSpec(memory_space=pltpu.SMEM),
                  pl.BlockSpec((1, RET_BLK, qk_w), lambda b, i: (b, i, COL_QC // qk_w)),
                  pl.BlockSpec((1, RET_BLK, qk_w), lambda b, i: (b, i, COL_KC // qk_w)),
                  pl.BlockSpec((1, RET_BLK, v_w), lambda b, i: (b, i, COL_VC // v_w)),
                  pl.BlockSpec((1, RET_BLK, v_w), lambda b, i: (b, i, COL_GC // v_w)),
                  tab, tab, const2(q_scale), const2(k_scale),
                  pl.BlockSpec(e_diag.shape, lambda b, i: (0, 0, 0)),
                  pl.BlockSpec((1, v_w), lambda b, i: (0, 0))],
        out_specs=pl.BlockSpec((1, RET_BLK, v_w), lambda b, i: (b, i, 0)),
        out_shape=jax.ShapeDtypeStruct((BATCH, SEQ, v_w), F32),
        scratch_shapes=[pltpu.VMEM((C_HEADS, LANES, C_V_DIM), F32)],
        compiler_params=_params(("arbitrary", "arbitrary")),
        name="retention_mixer",
    )(dec, z3, z3, z3, z3, cos_a, sin_a, q_scale, k_scale, e_diag, g_ret.reshape(1, v_w))


MERGE_TM = 512
MOE_SUB = 256


def _rms(y, g):
    ms = jnp.mean(y * y, axis=-1, keepdims=True)
    return y * lax.rsqrt(ms + EPS) * g


def _route(logits_t, bias_col):
    n_tok = logits_t.shape[1]
    per_group = N_EXPERTS // N_GROUPS
    scores = _sigmoid(logits_t)
    sel = scores + bias_col
    sel3 = sel.reshape(N_GROUPS, per_group, n_tok)
    member = lax.broadcasted_iota(I32, (N_GROUPS, per_group, n_tok), 1)
    m1 = jnp.max(sel3, axis=1, keepdims=True)
    first = jnp.min(jnp.where(sel3 == m1, member, per_group), axis=1, keepdims=True)
    m2 = jnp.max(jnp.where(member == first, -jnp.inf, sel3), axis=1, keepdims=True)
    grp = (m1 + m2).reshape(N_GROUPS, n_tok)
    gid = lax.broadcasted_iota(I32, (N_GROUPS, n_tok), 0)
    rank = jnp.zeros((N_GROUPS, n_tok), I32)
    for g in range(N_GROUPS):
        other = grp[g:g + 1, :]
        rank = rank + ((other > grp) | ((other == grp) & (g < gid))).astype(I32)
    keep = jnp.where(rank < TOPK_GROUPS, 1.0, 0.0).reshape(N_GROUPS, 1, n_tok)
    live = jnp.where(keep > 0.0, sel3, -jnp.inf).reshape(N_EXPERTS, n_tok)
    eid = lax.broadcasted_iota(I32, (N_EXPERTS, n_tok), 0)
    chosen = jnp.zeros((N_EXPERTS, n_tok), jnp.bool_)
    for _ in range(TOP_K):
        m = jnp.max(live, axis=0, keepdims=True)
        pick = eid == jnp.min(jnp.where(live == m, eid, N_EXPERTS), axis=0, keepdims=True)
        chosen = chosen | pick
        live = jnp.where(pick, -jnp.inf, live)
    w = jnp.where(chosen, scores, 0.0)
    return w / jnp.sum(w, axis=0, keepdims=True) * ROUTED_SCALE, chosen


def _merge_kernel(ya_ref, yb_ref, yc_ref, g_ref, x_ref, gt_ref, gpost_ref, gpre_ref, sc_ref, sh_ref,
                  woa, wob, woc, wout, wr_t, rb_ref, xo_ref, h_ref, gate_ref, rank_ref):
    ya = jnp.dot(ya_ref[...].astype(BF16), woa[...], preferred_element_type=F32)
    yb = jnp.dot(yb_ref[...].astype(BF16), wob[...], preferred_element_type=F32)
    yc = jnp.dot(yc_ref[...].astype(BF16), woc[...], preferred_element_type=F32)
    merged = (_sigmoid(g_ref[:, 0:D_MODEL]) * ya
              + _sigmoid(g_ref[:, D_MODEL:2 * D_MODEL]) * yb
              + _sigmoid(g_ref[:, 2 * D_MODEL:3 * D_MODEL]) * yc)
    y = jnp.dot(merged.astype(BF16), wout[...], preferred_element_type=F32)
    x = x_ref[...] + gt_ref[0] * _rms(y, gpost_ref[...])
    xo_ref[...] = x
    h = (_rms(x, gpre_ref[...]) * (1.0 + sc_ref[0]) + sh_ref[0]).astype(BF16)
    h_ref[...] = h
    logits_t = lax.dot_general(wr_t[...], h, NT_DIMS, preferred_element_type=F32)
    gates_t, chosen = _route(logits_t, rb_ref[...])
    gate_ref[...] = gates_t
    earlier = (lax.broadcasted_iota(I32, (MOE_SUB, MOE_SUB), 0)
               < lax.broadcasted_iota(I32, (MOE_SUB, MOE_SUB), 1)).astype(BF16)
    routed = jnp.where(chosen, 1.0, 0.0).astype(BF16)
    before = jnp.concatenate(
        [jnp.dot(routed[:, s * MOE_SUB:(s + 1) * MOE_SUB], earlier, preferred_element_type=F32)
         for s in range(MERGE_TM // MOE_SUB)], axis=1)
    rank_ref[...] = jnp.where(chosen, before, -1.0)


def _merge(ya, yb, yc, z, x, gt1, g_post, g_pre, sc2, sh2, woa, wob, woc, wout, wr_t, rbias):
    tpb = SEQ // MERGE_TM
    row = lambda w: pl.BlockSpec((MERGE_TM, w), lambda i: (i, 0))
    vec = pl.BlockSpec((1, D_MODEL), lambda i: (0, 0))
    mod = pl.BlockSpec((1, 1, D_MODEL), lambda i: (i // tpb, 0, 0))
    whole = lambda a: pl.BlockSpec(a.shape, lambda i: (0, 0))
    return pl.pallas_call(
        _merge_kernel,
        grid=(N_TOK // MERGE_TM,),
        in_specs=[row(512), row(512), row(1024), pl.BlockSpec((MERGE_TM, 3 * D_MODEL), lambda i: (i, 0)),
                  row(D_MODEL), mod, vec, vec, mod, mod,
                  whole(woa), whole(wob), whole(woc), whole(wout), whole(wr_t), whole(rbias)],
        out_specs=[row(D_MODEL), row(D_MODEL),
                   pl.BlockSpec((N_EXPERTS, MERGE_TM), lambda i: (0, i)),
                   pl.BlockSpec((N_EXPERTS, MERGE_TM), lambda i: (0, i))],
        out_shape=[jax.ShapeDtypeStruct((N_TOK, D_MODEL), F32),
                   jax.ShapeDtypeStruct((N_TOK, D_MODEL), BF16),
                   jax.ShapeDtypeStruct((N_EXPERTS, N_TOK), F32),
                   jax.ShapeDtypeStruct((N_EXPERTS, N_TOK), F32)],
        compiler_params=_params(("arbitrary",)),
        name="merge_route",
    )(ya, yb, yc, z, x, gt1, g_post.reshape(1, D_MODEL), g_pre.reshape(1, D_MODEL), sc2, sh2,
      woa, wob, woc, wout, wr_t, rbias)


MOE_TM = 1024
MOE_EG = 4
MOE_CAP = 64
MOE_NSUB = MOE_TM // MOE_SUB
MOE_SLAB = MOE_EG * MOE_CAP


def _moe_kernel(h_ref, gate_ref, rank_ref, x_ref, gt_ref, gpost_ref, w1, w3, w2, ws1, ws3, ws2,
                o_ref, acc, xg_s, yw_s, pick_s, back_s):
    g_idx = pl.program_id(1)

    @pl.when(g_idx == 0)
    def _():
        acc[...] = jnp.zeros_like(acc)

    rank = rank_ref[0]
    gate = gate_ref[0]
    top_rank = jnp.max(rank)
    n_rounds = (top_rank.astype(I32) + MOE_CAP) // MOE_CAP
    slot = lax.broadcasted_iota(I32, (MOE_CAP, 1), 0).astype(F32)
    sub_top = [jnp.max(rank[:, s * MOE_SUB:(s + 1) * MOE_SUB]) for s in range(MOE_NSUB)]
    exp_top = [jnp.max(rank[e:e + 1, :]) for e in range(MOE_EG)]

    def make_pick(s, base):
        toks = slice(s * MOE_SUB, (s + 1) * MOE_SUB)
        return jnp.concatenate(
            [jnp.where(rank[e:e + 1, toks] == slot + base, 1.0, 0.0) for e in range(MOE_EG)], axis=0)

    def expert_rows(e, pick_of):
        rows = slice(e * MOE_CAP, (e + 1) * MOE_CAP)
        x_e = jnp.concatenate([xg_s[s, rows, :] for s in range(MOE_NSUB)], axis=0)
        a = jnp.dot(x_e, w1[e], preferred_element_type=F32)
        hid = a * _sigmoid(a) * jnp.dot(x_e, w3[e], preferred_element_type=F32)
        y = jnp.dot(hid.astype(BF16), w2[e], preferred_element_type=F32)
        for s in range(MOE_NSUB):
            toks = slice(s * MOE_SUB, (s + 1) * MOE_SUB)
            w_row = jnp.sum(pick_of(s)[rows, :] * gate[e:e + 1, toks], axis=1, keepdims=True)
            yw_s[s, rows, :] = (y[s * MOE_CAP:(s + 1) * MOE_CAP, :] * w_row).astype(BF16)

    def scatter(s, back):
        toks = slice(s * MOE_SUB, (s + 1) * MOE_SUB)
        acc[toks, :] += jnp.dot(back, yw_s[s], preferred_element_type=F32)

    picks = [make_pick(s, 0.0) for s in range(MOE_NSUB)]
    for s in range(MOE_NSUB):
        toks = slice(s * MOE_SUB, (s + 1) * MOE_SUB)
        xg_s[s] = jnp.dot(picks[s].astype(BF16), h_ref[toks, :], preferred_element_type=F32).astype(BF16)
    for e in range(MOE_EG):
        expert_rows(e, lambda s: picks[s])
    for s in range(MOE_NSUB):
        scatter(s, picks[s].T.astype(BF16))

    @pl.when(n_rounds > 1)
    def _():
        for s in range(MOE_NSUB):
            pick_s[s] = picks[s].astype(BF16)

    def overflow_round(r, carry):
        base = (r * MOE_CAP).astype(F32)
        for s in range(MOE_NSUB):
            @pl.when(sub_top[s] >= base)
            def _(s=s):
                pick = make_pick(s, base)
                pick_s[s] = pick.astype(BF16)
                back_s[s] = pick.T.astype(BF16)
                toks = slice(s * MOE_SUB, (s + 1) * MOE_SUB)
                xg_s[s] = jnp.dot(pick.astype(BF16), h_ref[toks, :], preferred_element_type=F32).astype(BF16)
        for e in range(MOE_EG):
            @pl.when(exp_top[e] >= base)
            def _(e=e):
                expert_rows(e, lambda s: pick_s[s].astype(F32))
        for s in range(MOE_NSUB):
            @pl.when(sub_top[s] >= base)
            def _(s=s):
                scatter(s, back_s[s])
        return carry

    lax.fori_loop(1, n_rounds, overflow_round, 0)

    @pl.when(g_idx == N_EXPERTS // MOE_EG - 1)
    def _():
        h = h_ref[...]
        s = jnp.dot(h, ws1[...], preferred_element_type=F32)
        shid = s * _sigmoid(s) * jnp.dot(h, ws3[...], preferred_element_type=F32)
        y_all = acc[...] + jnp.dot(shid.astype(BF16), ws2[...], preferred_element_type=F32)
        o_ref[...] = x_ref[...] + gt_ref[0] * _rms(y_all, gpost_ref[...])


def _moe(h, gates_t, ranks_t, x, gt2, g_post, w1, w3, w2, ws1, ws3, ws2, layer):
    tpb = SEQ // MOE_TM
    n_groups = N_EXPERTS // MOE_EG
    whole = lambda a: pl.BlockSpec(a.shape, lambda i, g: (0, 0))
    per_group = pl.BlockSpec((1, MOE_EG, MOE_TM), lambda i, g: (g, 0, i))
    return pl.pallas_call(
        _moe_kernel,
        grid=(N_TOK // MOE_TM, n_groups),
        in_specs=[pl.BlockSpec((MOE_TM, D_MODEL), lambda i, g: (i, 0)),
                  per_group, per_group,
                  pl.BlockSpec((MOE_TM, D_MODEL), lambda i, g: (i, 0)),
                  pl.BlockSpec((1, 1, D_MODEL), lambda i, g: (i // tpb, 0, 0)),
                  pl.BlockSpec((1, D_MODEL), lambda i, g: (0, 0)),
                  pl.BlockSpec((None, MOE_EG, D_MODEL, EXPERT_DIM), lambda i, g: (layer, g, 0, 0)),
                  pl.BlockSpec((None, MOE_EG, D_MODEL, EXPERT_DIM), lambda i, g: (layer, g, 0, 0)),
                  pl.BlockSpec((None, MOE_EG, EXPERT_DIM, D_MODEL), lambda i, g: (layer, g, 0, 0)),
                  whole(ws1), whole(ws3), whole(ws2)],
        out_specs=pl.BlockSpec((MOE_TM, D_MODEL), lambda i, g: (i, 0)),
        out_shape=jax.ShapeDtypeStruct((N_TOK, D_MODEL), F32),
        scratch_shapes=[pltpu.VMEM((MOE_TM, D_MODEL), F32),
                        pltpu.VMEM((MOE_NSUB, MOE_SLAB, D_MODEL), BF16),
                        pltpu.VMEM((MOE_NSUB, MOE_SLAB, D_MODEL), BF16),
                        pltpu.VMEM((MOE_NSUB, MOE_SLAB, MOE_SUB), BF16),
                        pltpu.VMEM((MOE_NSUB, MOE_SUB, MOE_SLAB), BF16)],
        compiler_params=_params(("arbitrary", "arbitrary")),
        name="moe_ffn",
    )(h, gates_t.reshape(n_groups, MOE_EG, N_TOK), ranks_t.reshape(n_groups, MOE_EG, N_TOK),
      x, gt2, g_post.reshape(1, D_MODEL), w1, w3, w2, ws1, ws3, ws2)


_IN_SIZES = (512, 256, 256, 64, 4, 512, 512, 512, 512, 512, 1024, 1024, 3072)
_IN_DEST = (COL_QA, COL_CKV, COL_IQ, COL_IKIW, COL_IKIW + 64, COL_QB, COL_KB, COL_VB, COL_QC, COL_KC,
            COL_VC, COL_GC, COL_GATES)
_IN_SRC = tuple(int(v) for v in np.cumsum((0,) + _IN_SIZES[:-1]))
N_IN = sum(_IN_SIZES)
PACK_TK = 128


def _pack_kernel(w_ref, o_ref, *, layer):
    for src, width, dst in zip(_IN_SRC, _IN_SIZES, _IN_DEST):
        o_ref[dst:dst + width, :] = w_ref[src:src + width, layer, :].astype(BF16)
    o_ref[COL_IKIW + 68:COL_IKIW + LANES, :] = jnp.zeros((LANES - 68, PACK_TK), BF16)


def _reorder_w_in(w_t, layer):
    return pl.pallas_call(
        functools.partial(_pack_kernel, layer=layer),
        grid=(D_MODEL // PACK_TK,),
        in_specs=[pl.BlockSpec((N_IN, DEPTH, PACK_TK), lambda i: (0, 0, i))],
        out_specs=pl.BlockSpec((N_PROJ, PACK_TK), lambda i: (0, i)),
        out_shape=jax.ShapeDtypeStruct((N_PROJ, D_MODEL), BF16),
        compiler_params=_params(("arbitrary",)),
        name="pack_w_in",
    )(w_t)


def _rope_tables(positions):
    half = HEAD_DIM // 2
    inv_freq = ROPE_THETA ** (-jnp.arange(half, dtype=F32) / half)
    ang = positions.astype(F32)[:, :, None] * inv_freq
    cos, sin = jnp.cos(ang), jnp.sin(ang)
    cos64 = jnp.concatenate([cos, cos], axis=-1)
    sin64 = jnp.concatenate([-sin, sin], axis=-1)
    cos_a = jnp.concatenate([cos64, cos64], axis=-1)
    sin_a = jnp.concatenate([sin64, sin64], axis=-1)
    return cos_a, sin_a


def _decay_tables():
    log_g = jnp.log(1.0 - 2.0 ** (-5.0 - jnp.arange(C_HEADS, dtype=F32)))
    pos = jnp.arange(RET_BLK, dtype=F32)
    diff = pos[:, None] - pos[None, :]
    chunk = jnp.arange(RET_BLK) // CHUNK
    visible = chunk[None, :] <= chunk[:, None]
    e_diag = jnp.where(visible, jnp.exp(log_g[:, None, None] * jnp.abs(diff)), 0.0)
    dec = jnp.exp(log_g * RET_BLK)
    per_lane = jnp.repeat(log_g, HEAD_DIM)[None, :]
    q_scale = jnp.exp(per_lane * pos[:, None])
    k_scale = jnp.exp(per_lane * (RET_BLK - pos)[:, None])
    return dec, q_scale, k_scale, e_diag


def kernel(x, c, positions, ada_w, ada_b, g_pre_mix, g_post_mix, g_pre_ffn, g_post_ffn, w_in, g_kv, w_uk, w_uv,
           g_ret, w_o_a, w_o_b, w_o_c, w_out, w_router, router_bias, w1, w3, w2, ws1, ws3, ws2):
    cos_a, sin_a = _rope_tables(positions)
    dec, q_scale, k_scale, e_diag = _decay_tables()
    xf = x.reshape(N_TOK, D_MODEL)
    w_in_t = jnp.transpose(w_in, (2, 0, 1))
    w1_b, w3_b, w2_b = w1.astype(BF16), w3.astype(BF16), w2.astype(BF16)
    for l in range(DEPTH):
        mod = _ada(c, ada_w, ada_b, l).reshape(BATCH, 1, N_ADA * D_MODEL)
        sh1, sc1, gt1, sh2, sc2, gt2 = (mod[:, :, k * D_MODEL:(k + 1) * D_MODEL] for k in range(N_ADA))
        z = _inproj(xf, g_pre_mix[l], sc1, sh1, _reorder_w_in(w_in_t, l))
        z3 = z.reshape(BATCH, SEQ, N_PROJ)
        w_ukv = jnp.concatenate([w_uk[l], w_uv[l]], axis=1).astype(BF16)
        ya = _dsa(z3, cos_a, sin_a, g_kv[l], w_ukv).reshape(N_TOK, -1)
        yb = _sb(z3).reshape(N_TOK, -1)
        yc = _ret(z3, cos_a, sin_a, dec, q_scale, k_scale, e_diag, g_ret[l]).reshape(N_TOK, -1)
        xf, h, gates, ranks = _merge(ya, yb, yc, z, xf, gt1, g_post_mix[l], g_pre_ffn[l], sc2, sh2,
                              w_o_a[l].astype(BF16), w_o_b[l].astype(BF16), w_o_c[l].astype(BF16),
                              w_out[l].astype(BF16), w_router[l].T.astype(BF16),
                              router_bias[l].reshape(N_EXPERTS, 1))
        xf = _moe(h, gates, ranks, xf, gt2, g_post_ffn[l], w1_b, w3_b, w2_b,
                  ws1[l].astype(BF16), ws3[l].astype(BF16), ws2[l].astype(BF16), l)
    return xf.reshape(BATCH, SEQ, D_MODEL)
```

```python
import functools
import math

import numpy as np
import jax
import jax.numpy as jnp
from jax import lax
from jax.experimental import pallas as pl
from jax.experimental.pallas import tpu as pltpu

F32 = jnp.float32
BF16 = jnp.bfloat16
I32 = jnp.int32

D_MODEL = 1024
BATCH = 8
SEQ = 2048
DEPTH = 2
N_TOK = BATCH * SEQ

CHUNK = 64
HEAD_DIM = 64
ROPE_THETA = 10000.0
EPS = 1e-6
A_HEADS = 8
A_KV_LORA = 256
IDX_HEADS = 4
IDX_DIM = 64
TOPK = 256
B_HEADS = 8
C_HEADS = 8
C_V_DIM = 128
N_EXPERTS = 64
TOP_K = 8
N_GROUPS = 8
TOPK_GROUPS = 4
EXPERT_DIM = 256
SHARED_DIM = 256
ROUTED_SCALE = 2.5
N_ADA = 6

LANES = 128
VMEM_LIMIT = 56 * 1024 * 1024

COL_GATES = 0
COL_VC = 3072
COL_GC = 4096
COL_QA = 5120
COL_QB = 5632
COL_KB = 6144
COL_VB = 6656
COL_QC = 7168
COL_KC = 7680
COL_CKV = 8192
COL_IQ = 8448
COL_IKIW = 8704
N_PROJ = 8832
PROJ_TN = 2944
PROJ_TM = 1024

INT_MIN = -2147483648
NEG_INF_KEY = -2139095041

NT_DIMS = (((1,), (1,)), ((), ()))
LOG2_E = math.log2(math.e)


def _params(sem):
    return pltpu.CompilerParams(dimension_semantics=sem, vmem_limit_bytes=VMEM_LIMIT)


def _sigmoid(x):
    return 1.0 / (1.0 + jnp.exp(-x))


def _rope_lanes(x, cos, sin_signed):
    n = x.shape[-1]
    lane = lax.broadcasted_iota(I32, (1, n), 1)
    first = (lane & 32) == 0
    ahead = pltpu.roll(x, n - 32, 1)
    behind = pltpu.roll(x, 32, 1)
    return x * cos + jnp.where(first, ahead, behind) * sin_signed


def _ada_kernel(c_ref, w_ref, b_ref, o_ref):
    c = c_ref[...]
    s = (c * _sigmoid(c)).astype(BF16)
    o_ref[...] = jnp.dot(s, w_ref[...].astype(BF16), preferred_element_type=F32) + b_ref[...]


def _ada(c, w, b, layer):
    n = w.shape[2]
    return pl.pallas_call(
        _ada_kernel,
        grid=(n // D_MODEL,),
        in_specs=[pl.BlockSpec((BATCH, D_MODEL), lambda j: (0, 0)),
                  pl.BlockSpec((None, D_MODEL, D_MODEL), lambda j: (layer, 0, j)),
                  pl.BlockSpec((None, 1, D_MODEL), lambda j: (layer, 0, j))],
        out_specs=pl.BlockSpec((BATCH, D_MODEL), lambda j: (0, j)),
        out_shape=jax.ShapeDtypeStruct((BATCH, n), F32),
        compiler_params=_params(("arbitrary",)),
        name="ada_mod",
    )(c, w, b.reshape(DEPTH, 1, n))


def _inproj_kernel(x_ref, g_ref, sc_ref, sh_ref, w_ref, o_ref, h_ref):
    @pl.when(pl.program_id(1) == 0)
    def _():
        x = x_ref[...]
        ms = jnp.mean(x * x, axis=-1, keepdims=True)
        y = x * lax.rsqrt(ms + EPS) * g_ref[...]
        h_ref[...] = (y * (1.0 + sc_ref[0]) + sh_ref[0]).astype(BF16)

    o_ref[...] = lax.dot_general(h_ref[...], w_ref[...], NT_DIMS, preferred_element_type=F32)


def _inproj(x, g, sc, sh, w_t_bf16):
    tiles_per_batch = SEQ // PROJ_TM
    return pl.pallas_call(
        _inproj_kernel,
        grid=(N_TOK // PROJ_TM, N_PROJ // PROJ_TN),
        in_specs=[pl.BlockSpec((PROJ_TM, D_MODEL), lambda i, j: (i, 0)),
                  pl.BlockSpec((1, D_MODEL), lambda i, j: (0, 0)),
                  pl.BlockSpec((1, 1, D_MODEL), lambda i, j: (i // tiles_per_batch, 0, 0)),
                  pl.BlockSpec((1, 1, D_MODEL), lambda i, j: (i // tiles_per_batch, 0, 0)),
                  pl.BlockSpec((PROJ_TN, D_MODEL), lambda i, j: (j, 0))],
        out_specs=pl.BlockSpec((PROJ_TM, PROJ_TN), lambda i, j: (i, j)),
        out_shape=jax.ShapeDtypeStruct((N_TOK, N_PROJ), F32),
        scratch_shapes=[pltpu.VMEM((PROJ_TM, D_MODEL), BF16)],
        compiler_params=_params(("arbitrary", "arbitrary")),
        name="in_proj",
    )(x, g.reshape(1, D_MODEL), sc, sh, w_t_bf16)


DSA_QB = 256
DSA_KC = 256


def _dsa_kernel(ckv_ref, ikiw_ref, q_ref, iq_ref, cos_a, sin_a, gkv_ref, wukv_ref,
                o_ref, k_lo, k_hi, v_one, ik_lo, ik_hi, keys_s, selt_s, acc_s, q_s):
    qi = pl.program_id(1)
    lane = lax.broadcasted_iota(I32, (1, LANES), 1)
    lo_half = lane < 64
    n_kc = SEQ // DSA_KC

    @pl.when(qi == 0)
    def _prologue():
        def chunk(c, carry):
            r0 = pl.multiple_of(c * DSA_KC, DSA_KC)
            rows = pl.ds(r0, DSA_KC)
            ckv = ckv_ref[0, rows, :]
            ms = jnp.mean(ckv * ckv, axis=-1, keepdims=True)
            cn = (ckv * lax.rsqrt(ms + EPS) * gkv_ref[...]).astype(BF16)
            kv = jnp.dot(cn, wukv_ref[...], preferred_element_type=F32)
            cb = jnp.where(lo_half, cos_a[0, rows, :], 1.0)
            sb = jnp.where(lo_half, sin_a[0, rows, :], 0.0)
            kv = _rope_lanes(kv, cb, sb)
            swapped = pltpu.roll(kv, 64, 1)
            k_lo[rows, :] = jnp.where(lo_half, kv, 0.0).astype(BF16)
            k_hi[rows, :] = jnp.where(lo_half, 0.0, swapped).astype(BF16)
            v_one[rows, :] = jnp.where(lo_half, swapped, 1.0).astype(BF16)
            ik = jnp.where(lo_half, _rope_lanes(ikiw_ref[0, rows, :], cb, sb), 0.0)
            ik_lo[rows, :] = ik.astype(BF16)
            ik_hi[rows, :] = pltpu.roll(ik, 64, 1).astype(BF16)
            return carry
        lax.fori_loop(0, n_kc, chunk, 0)

    q0 = pl.multiple_of(qi * DSA_QB, DSA_QB)
    qrows = pl.ds(q0, DSA_QB)
    cq, sq = cos_a[0, qrows, :], sin_a[0, qrows, :]

    iq = iq_ref[0]
    iq_heads = []
    for p in range(IDX_HEADS // 2):
        pair = _rope_lanes(iq[:, p * LANES:(p + 1) * LANES], cq, sq)
        iq_heads.append(jnp.where(lo_half, pair, 0.0).astype(BF16))
        iq_heads.append(jnp.where(lo_half, 0.0, pair).astype(BF16))
    iq_stacked = [jnp.concatenate(iq_heads[parity::2], axis=0) for parity in range(2)]
    iw_t = ikiw_ref[0, qrows, :].T * (IDX_HEADS ** -0.5)
    q_chunk = (q0 + lax.broadcasted_iota(I32, (1, DSA_QB), 1)) >> 6
    n_live = lax.shift_right_logical((qi + 1) * DSA_QB + (DSA_KC - 1), DSA_KC.bit_length() - 1)

    def chunk_rows(c):
        return pl.ds(pl.multiple_of(c * DSA_KC, DSA_KC), DSA_KC)

    def score_chunk(c, carry):
        rows = chunk_rows(c)
        k_chunk = (c * DSA_KC + lax.broadcasted_iota(I32, (DSA_KC, 1), 0)) >> 6
        score = jnp.zeros((DSA_KC, DSA_QB), F32)
        for parity, ik in enumerate((ik_lo, ik_hi)):
            logits = lax.dot_general(ik[rows, :], iq_stacked[parity], NT_DIMS,
                                     preferred_element_type=F32) * (IDX_DIM ** -0.5)
            for j in range(IDX_HEADS // 2):
                h = parity + 2 * j
                score = score + iw_t[64 + h:65 + h, :] * jnp.maximum(logits[:, j * DSA_QB:(j + 1) * DSA_QB], 0.0)
        score = jnp.where(score == 0.0, 0.0, score)
        score = jnp.where(k_chunk <= q_chunk, score, -jnp.inf)
        bits = lax.bitcast_convert_type(score, I32)
        keys_s[rows, :] = bits ^ ((bits >> 31) & 0x7FFFFFFF)
        return carry
    lax.fori_loop(0, n_live, score_chunk, 0)

    def count(preds):
        def chunk(c, accs):
            key = keys_s[chunk_rows(c), :]
            return tuple(acc + jnp.sum(pred(key).astype(I32).reshape(4, DSA_KC // 4, DSA_QB), axis=0)
                         for acc, pred in zip(accs, preds))
        zero = jnp.zeros((DSA_KC // 4, DSA_QB), I32)
        accs = lax.fori_loop(0, n_live, chunk, (zero,) * len(preds))
        return [jnp.sum(acc, axis=0, keepdims=True) for acc in accs]

    def descend(it, prefix):
        cand = prefix | lax.shift_left(jnp.int32(1), 31 - it)
        bound = cand ^ INT_MIN
        return jnp.where(count([lambda k: k >= bound])[0] >= TOPK, cand, prefix)
    thr = lax.fori_loop(0, 32, descend, jnp.zeros((1, DSA_QB), I32)) ^ INT_MIN

    n_gt, n_eq = count([lambda k: k > thr, lambda k: k == thr])
    need = (TOPK - n_gt).astype(F32)
    tie_break = jnp.max(n_gt + n_eq) > TOPK

    @pl.when(tie_break)
    def _():
        tri = (lax.broadcasted_iota(I32, (DSA_KC, DSA_KC), 0)
               >= lax.broadcasted_iota(I32, (DSA_KC, DSA_KC), 1)).astype(BF16)

        def select_chunk(c, seen):
            key = keys_s[chunk_rows(c), :]
            eq = key == thr
            rank = jnp.dot(tri, eq.astype(BF16), preferred_element_type=F32) + seen
            sel = ((key > thr) | (eq & (rank <= need))) & (key > NEG_INF_KEY)
            selt_s[c] = sel.astype(F32).T
            return rank[DSA_KC - 1:DSA_KC, :]
        lax.fori_loop(0, n_live, select_chunk, jnp.zeros((1, DSA_QB), F32))

    @pl.when(jnp.logical_not(tie_break))
    def _():
        def select_chunk(c, carry):
            key = keys_s[chunk_rows(c), :]
            selt_s[c] = ((key >= thr) & (key > NEG_INF_KEY)).astype(F32).T
            return carry
        lax.fori_loop(0, n_live, select_chunk, 0)

    q = q_ref[0]
    n_pairs = A_HEADS // 2
    for p in range(n_pairs):
        pair = _rope_lanes(q[:, p * LANES:(p + 1) * LANES], cq, sq) * (HEAD_DIM ** -0.5 * LOG2_E)
        q_s[p * DSA_QB:(p + 1) * DSA_QB, :] = jnp.where(lo_half, pair, 0.0).astype(BF16)
        q_s[(n_pairs + p) * DSA_QB:(n_pairs + p + 1) * DSA_QB, :] = jnp.where(lo_half, 0.0, pair).astype(BF16)
    masked = -1e30
    half = n_pairs * DSA_QB

    def attend(c, m):
        rows = chunk_rows(c)
        bias = jnp.where(selt_s[c] > 0.0, 0.0, masked)
        s = jnp.concatenate(
            [lax.dot_general(q_s[:half, :], k_lo[rows, :], NT_DIMS, preferred_element_type=F32),
             lax.dot_general(q_s[half:, :], k_hi[rows, :], NT_DIMS, preferred_element_type=F32)], axis=0)
        s = s.reshape(A_HEADS, DSA_QB, DSA_KC) + bias[None]
        m_new = jnp.maximum(m, jnp.max(s, axis=-1, keepdims=True))
        pe = jnp.exp2(s - m_new).astype(BF16)
        alpha = jnp.exp2(m - m_new)
        for part in range(2):
            slabs = slice(part * n_pairs, (part + 1) * n_pairs)
            pv = jnp.dot(pe[slabs].reshape(half, DSA_KC), v_one[rows, :], preferred_element_type=F32)
            acc_s[slabs] = alpha[slabs] * acc_s[slabs] + pv.reshape(n_pairs, DSA_QB, LANES)
        return m_new

    acc_s[...] = jnp.zeros_like(acc_s)
    lax.fori_loop(0, n_live, attend, jnp.full((A_HEADS, DSA_QB, 1), masked, F32))
    for p in range(n_pairs):
        even, odd = acc_s[p], acc_s[n_pairs + p]
        even_swapped = pltpu.roll(even, HEAD_DIM, 1)
        odd_swapped = pltpu.roll(odd, HEAD_DIM, 1)
        o_ref[0, :, p * LANES:(p + 1) * LANES] = jnp.where(lo_half, even / even_swapped, odd_swapped / odd)


def _dsa(z3, cos_a, sin_a, g_kv, w_ukv_bf16):
    full = lambda width, col: pl.BlockSpec((1, SEQ, width), lambda b, i: (b, 0, col // width))
    blk = lambda width, col: pl.BlockSpec((1, DSA_QB, width), lambda b, i: (b, i, col // width))
    tab = pl.BlockSpec((1, SEQ, LANES), lambda b, i: (b, 0, 0))
    return pl.pallas_call(
        _dsa_kernel,
        grid=(BATCH, SEQ // DSA_QB),
        in_specs=[full(A_KV_LORA, COL_CKV), full(LANES, COL_IKIW),
                  blk(A_HEADS * HEAD_DIM, COL_QA), blk(IDX_HEADS * IDX_DIM, COL_IQ),
                  tab, tab,
                  pl.BlockSpec((1, A_KV_LORA), lambda b, i: (0, 0)),
                  pl.BlockSpec((A_KV_LORA, LANES), lambda b, i: (0, 0))],
        out_specs=pl.BlockSpec((1, DSA_QB, A_HEADS * HEAD_DIM), lambda b, i: (b, i, 0)),
        out_shape=jax.ShapeDtypeStruct((BATCH, SEQ, A_HEADS * HEAD_DIM), F32),
        scratch_shapes=[pltpu.VMEM((SEQ, LANES), BF16)] * 5
                       + [pltpu.VMEM((SEQ, DSA_QB), I32),
                          pltpu.VMEM((SEQ // DSA_KC, DSA_QB, DSA_KC), F32),
                          pltpu.VMEM((A_HEADS, DSA_QB, LANES), F32),
                          pltpu.VMEM((A_HEADS * DSA_QB, LANES), BF16)],
        compiler_params=_params(("arbitrary", "arbitrary")),
        name="dsa_mixer",
    )(z3, z3, z3, z3, cos_a, sin_a, g_kv.reshape(1, A_KV_LORA), w_ukv_bf16)


SB_BLK = 128
SB_EXP_UNDERFLOW = -105.0


def _sb_kernel(q_ref, k_ref, v_ref, o_ref, q_s, tail_s, acc_s):
    qi = pl.program_id(1)
    n_pairs = B_HEADS // 2
    rows_all = B_HEADS * SB_BLK
    lane = lax.broadcasted_iota(I32, (1, LANES), 1)
    lo_half = lane < 64
    causal = (lax.broadcasted_iota(I32, (rows_all, SB_BLK), 1)
              < (lax.broadcasted_iota(I32, (rows_all, SB_BLK), 0) & (SB_BLK - 1)))
    r2 = lax.broadcasted_iota(I32, (2 * SB_BLK, 2 * SB_BLK), 0) & (SB_BLK - 1)
    c2 = lax.broadcasted_iota(I32, (2 * SB_BLK, 2 * SB_BLK), 1)
    suffix_mat = ((c2 >= SB_BLK) | (r2 > c2)).astype(BF16)

    for p in range(n_pairs):
        q2 = q_ref[0, :, p * LANES:(p + 1) * LANES] * (HEAD_DIM ** -0.5)
        q_s[2 * p * SB_BLK:(2 * p + 1) * SB_BLK, :] = jnp.where(lo_half, q2, 0.0).astype(BF16)
        q_s[(2 * p + 1) * SB_BLK:(2 * p + 2) * SB_BLK, :] = jnp.where(lo_half, 0.0, q2).astype(BF16)

    def key_block(kb, diagonal):
        r0 = pl.multiple_of(kb * SB_BLK, SB_BLK)
        pair_rows = lambda p: slice(2 * p * SB_BLK, (2 * p + 2) * SB_BLK)
        z = jnp.concatenate(
            [lax.dot_general(q_s[pair_rows(p), :], k_ref[0, pl.ds(r0, SB_BLK), p * LANES:(p + 1) * LANES].astype(BF16),
                             NT_DIMS, preferred_element_type=F32) for p in range(n_pairs)], axis=0)
        log_beta = jnp.minimum(z, 0.0) - jnp.log(1.0 + jnp.exp(-jnp.abs(z)))
        log_not = log_beta - z
        if diagonal:
            log_not = jnp.where(causal, log_not, 0.0)
        hi = log_not.astype(BF16)
        lo = (log_not - hi.astype(F32)).astype(BF16)
        hi_lo = jnp.concatenate([hi, lo], axis=1)
        t = jnp.concatenate([jnp.dot(hi_lo[pair_rows(p), :], suffix_mat, preferred_element_type=F32)
                             for p in range(n_pairs)], axis=0)
        if diagonal:
            a = jnp.where(causal, jnp.exp(log_beta + t[:, :SB_BLK]), 0.0)
            tail = t[:, SB_BLK:]
        else:
            a = jnp.exp(log_beta + t[:, :SB_BLK] + tail_s[...])
            tail = tail_s[...] + t[:, SB_BLK:]
        tail_s[...] = tail
        a = a.astype(BF16)
        for p in range(n_pairs):
            v2 = v_ref[0, pl.ds(r0, SB_BLK), p * LANES:(p + 1) * LANES].astype(BF16)
            pv = jnp.dot(a[pair_rows(p), :], v2, preferred_element_type=F32)
            if diagonal:
                acc_s[pair_rows(p), :] = pv
            else:
                acc_s[pair_rows(p), :] += pv
        return jnp.max(tail)

    first = key_block(qi, True)

    def more(state):
        jj, worst = state
        return (jj <= qi) & (worst > SB_EXP_UNDERFLOW)

    def step(state):
        jj, _ = state
        return jj + 1, key_block(qi - jj, False)

    lax.while_loop(more, step, (jnp.int32(1), first))
    for p in range(n_pairs):
        even = acc_s[2 * p * SB_BLK:(2 * p + 1) * SB_BLK, :]
        odd = acc_s[(2 * p + 1) * SB_BLK:(2 * p + 2) * SB_BLK, :]
        o_ref[0, :, p * LANES:(p + 1) * LANES] = jnp.where(lo_half, even, odd)


def _sb(z3):
    width = B_HEADS * HEAD_DIM
    return pl.pallas_call(
        _sb_kernel,
        grid=(BATCH, SEQ // SB_BLK),
        in_specs=[pl.BlockSpec((1, SB_BLK, width), lambda b, i: (b, i, COL_QB // width)),
                  pl.BlockSpec((1, SEQ, width), lambda b, i: (b, 0, COL_KB // width)),
                  pl.BlockSpec((1, SEQ, width), lambda b, i: (b, 0, COL_VB // width))],
        out_specs=pl.BlockSpec((1, SB_BLK, width), lambda b, i: (b, i, 0)),
        out_shape=jax.ShapeDtypeStruct((BATCH, SEQ, width), F32),
        scratch_shapes=[pltpu.VMEM((B_HEADS * SB_BLK, LANES), BF16),
                        pltpu.VMEM((B_HEADS * SB_BLK, SB_BLK), F32),
                        pltpu.VMEM((B_HEADS * SB_BLK, SB_BLK), F32)],
        compiler_params=_params(("arbitrary", "arbitrary")),
        name="stickbreak_mixer",
    )(z3, z3, z3)


RET_BLK = 256


def _ret_kernel(dec_ref, q_ref, k_ref, v_ref, gate_ref, cos_a, sin_a, qs_ref, ks_ref, e_diag, gret_ref,
                o_ref, state):
    lane = lax.broadcasted_iota(I32, (1, LANES), 1)
    lo_half = lane < 64
    row_lo = lax.broadcasted_iota(I32, (LANES, 1), 0) < 64

    @pl.when(pl.program_id(1) == 0)
    def _():
        state[...] = jnp.zeros_like(state)

    cq, sq = cos_a[0], sin_a[0]
    for p in range(C_HEADS // 2):
        cols = slice(p * LANES, (p + 1) * LANES)
        q2 = _rope_lanes(q_ref[0, :, cols], cq, sq)
        k2 = _rope_lanes(k_ref[0, :, cols], cq, sq) * (HEAD_DIM ** -0.5)
        q_hat = q2 * qs_ref[:, cols]
        k_hat_t = (k2 * ks_ref[:, cols]).T
        for e in range(2):
            h = 2 * p + e
            pick = (lambda x: jnp.where(lo_half, x, 0.0)) if e == 0 else (lambda x: jnp.where(lo_half, 0.0, x))
            vcols = slice(h * C_V_DIM, (h + 1) * C_V_DIM)
            v2 = v_ref[0, :, vcols].astype(BF16)
            s = lax.dot_general(pick(q2).astype(BF16), k2.astype(BF16), NT_DIMS,
                                preferred_element_type=F32) * e_diag[h]
            o = jnp.dot(s.astype(BF16), v2, preferred_element_type=F32)
            o = o + jnp.dot(pick(q_hat).astype(BF16), state[h].astype(BF16), preferred_element_type=F32)
            kv = jnp.dot(jnp.where(row_lo if e == 0 else ~row_lo, k_hat_t, 0.0).astype(BF16), v2,
                         preferred_element_type=F32)
            state[h] = dec_ref[h] * state[h] + kv
            mu = jnp.mean(o, axis=-1, keepdims=True)
            d = o - mu
            var = jnp.mean(d * d, axis=-1, keepdims=True)
            y = d * lax.rsqrt(var + EPS) * gret_ref[:, vcols]
            g = gate_ref[0, :, vcols]
            o_ref[0, :, vcols] = g * _sigmoid(g) * y


def _ret(z3, cos_a, sin_a, dec, q_scale, k_scale, e_diag, g_ret):
    qk_w = C_HEADS * HEAD_DIM
    v_w = C_HEADS * C_V_DIM
    tab = pl.BlockSpec((1, RET_BLK, LANES), lambda b, i: (b, i, 0))
    const2 = lambda a: pl.BlockSpec(a.shape, lambda b, i: (0, 0))
    return pl.pallas_call(
        _ret_kernel,
        grid=(BATCH, SEQ // RET_BLK),
        in_specs=[pl.BlockSpec(memory_space=pltpu.SMEM),
                  pl.BlockSpec((1, RET_BLK, qk_w), lambda b, i: (b, i, COL_QC // qk_w)),
                  pl.BlockSpec((1, RET_BLK, qk_w), lambda b, i: (b, i, COL_KC // qk_w)),
                  pl.BlockSpec((1, RET_BLK, v_w), lambda b, i: (b, i, COL_VC // v_w)),
                  pl.BlockSpec((1, RET_BLK, v_w), lambda b, i: (b, i, COL_GC // v_w)),
                  tab, tab, const2(q_scale), const2(k_scale),
                  pl.BlockSpec(e_diag.shape, lambda b, i: (0, 0, 0)),
                  pl.BlockSpec((1, v_w), lambda b, i: (0, 0))],
        out_specs=pl.BlockSpec((1, RET_BLK, v_w), lambda b, i: (b, i, 0)),
        out_shape=jax.ShapeDtypeStruct((BATCH, SEQ, v_w), F32),
        scratch_shapes=[pltpu.VMEM((C_HEADS, LANES, C_V_DIM), F32)],
        compiler_params=_params(("arbitrary", "arbitrary")),
        name="retention_mixer",
    )(dec, z3, z3, z3, z3, cos_a, sin_a, q_scale, k_scale, e_diag, g_ret.reshape(1, v_w))


MERGE_TM = 512
MOE_SUB = 256


def _rms(y, g):
    ms = jnp.mean(y * y, axis=-1, keepdims=True)
    return y * lax.rsqrt(ms + EPS) * g


def _route(logits_t, bias_col):
    n_tok = logits_t.shape[1]
    per_group = N_EXPERTS // N_GROUPS
    scores = _sigmoid(logits_t)
    sel = scores + bias_col
    sel3 = sel.reshape(N_GROUPS, per_group, n_tok)
    member = lax.broadcasted_iota(I32, (N_GROUPS, per_group, n_tok), 1)
    m1 = jnp.max(sel3, axis=1, keepdims=True)
    first = jnp.min(jnp.where(sel3 == m1, member, per_group), axis=1, keepdims=True)
    m2 = jnp.max(jnp.where(member == first, -jnp.inf, sel3), axis=1, keepdims=True)
    grp = (m1 + m2).reshape(N_GROUPS, n_tok)
    gid = lax.broadcasted_iota(I32, (N_GROUPS, n_tok), 0)
    rank = jnp.zeros((N_GROUPS, n_tok), I32)
    for g in range(N_GROUPS):
        other = grp[g:g + 1, :]
        rank = rank + ((other > grp) | ((other == grp) & (g < gid))).astype(I32)
    keep = jnp.where(rank < TOPK_GROUPS, 1.0, 0.0).reshape(N_GROUPS, 1, n_tok)
    live = jnp.where(keep > 0.0, sel3, -jnp.inf).reshape(N_EXPERTS, n_tok)
    eid = lax.broadcasted_iota(I32, (N_EXPERTS, n_tok), 0)
    chosen = jnp.zeros((N_EXPERTS, n_tok), jnp.bool_)
    for _ in range(TOP_K):
        m = jnp.max(live, axis=0, keepdims=True)
        pick = eid == jnp.min(jnp.where(live == m, eid, N_EXPERTS), axis=0, keepdims=True)
        chosen = chosen | pick
        live = jnp.where(pick, -jnp.inf, live)
    w = jnp.where(chosen, scores, 0.0)
    return w / jnp.sum(w, axis=0, keepdims=True) * ROUTED_SCALE, chosen


def _merge_kernel(ya_ref, yb_ref, yc_ref, g_ref, x_ref, gt_ref, gpost_ref, gpre_ref, sc_ref, sh_ref,
                  woa, wob, woc, wout, wr_t, rb_ref, xo_ref, h_ref, gate_ref, rank_ref):
    ya = jnp.dot(ya_ref[...].astype(BF16), woa[...], preferred_element_type=F32)
    yb = jnp.dot(yb_ref[...].astype(BF16), wob[...], preferred_element_type=F32)
    yc = jnp.dot(yc_ref[...].astype(BF16), woc[...], preferred_element_type=F32)
    merged = (_sigmoid(g_ref[:, 0:D_MODEL]) * ya
              + _sigmoid(g_ref[:, D_MODEL:2 * D_MODEL]) * yb
              + _sigmoid(g_ref[:, 2 * D_MODEL:3 * D_MODEL]) * yc)
    y = jnp.dot(merged.astype(BF16), wout[...], preferred_element_type=F32)
    x = x_ref[...] + gt_ref[0] * _rms(y, gpost_ref[...])
    xo_ref[...] = x
    h = (_rms(x, gpre_ref[...]) * (1.0 + sc_ref[0]) + sh_ref[0]).astype(BF16)
    h_ref[...] = h
    logits_t = lax.dot_general(wr_t[...], h, NT_DIMS, preferred_element_type=F32)
    gates_t, chosen = _route(logits_t, rb_ref[...])
    gate_ref[...] = gates_t
    earlier = (lax.broadcasted_iota(I32, (MOE_SUB, MOE_SUB), 0)
               < lax.broadcasted_iota(I32, (MOE_SUB, MOE_SUB), 1)).astype(BF16)
    routed = jnp.where(chosen, 1.0, 0.0).astype(BF16)
    before = jnp.concatenate(
        [jnp.dot(routed[:, s * MOE_SUB:(s + 1) * MOE_SUB], earlier, preferred_element_type=F32)
         for s in range(MERGE_TM // MOE_SUB)], axis=1)
    rank_ref[...] = jnp.where(chosen, before, -1.0)


def _merge(ya, yb, yc, z, x, gt1, g_post, g_pre, sc2, sh2, woa, wob, woc, wout, wr_t, rbias):
    tpb = SEQ // MERGE_TM
    row = lambda w: pl.BlockSpec((MERGE_TM, w), lambda i: (i, 0))
    vec = pl.BlockSpec((1, D_MODEL), lambda i: (0, 0))
    mod = pl.BlockSpec((1, 1, D_MODEL), lambda i: (i // tpb, 0, 0))
    whole = lambda a: pl.BlockSpec(a.shape, lambda i: (0, 0))
    return pl.pallas_call(
        _merge_kernel,
        grid=(N_TOK // MERGE_TM,),
        in_specs=[row(512), row(512), row(1024), pl.BlockSpec((MERGE_TM, 3 * D_MODEL), lambda i: (i, 0)),
                  row(D_MODEL), mod, vec, vec, mod, mod,
                  whole(woa), whole(wob), whole(woc), whole(wout), whole(wr_t), whole(rbias)],
        out_specs=[row(D_MODEL), row(D_MODEL),
                   pl.BlockSpec((N_EXPERTS, MERGE_TM), lambda i: (0, i)),
                   pl.BlockSpec((N_EXPERTS, MERGE_TM), lambda i: (0, i))],
        out_shape=[jax.ShapeDtypeStruct((N_TOK, D_MODEL), F32),
                   jax.ShapeDtypeStruct((N_TOK, D_MODEL), BF16),
                   jax.ShapeDtypeStruct((N_EXPERTS, N_TOK), F32),
                   jax.ShapeDtypeStruct((N_EXPERTS, N_TOK), F32)],
        compiler_params=_params(("arbitrary",)),
        name="merge_route",
    )(ya, yb, yc, z, x, gt1, g_post.reshape(1, D_MODEL), g_pre.reshape(1, D_MODEL), sc2, sh2,
      woa, wob, woc, wout, wr_t, rbias)


MOE_TM = 1024
MOE_EG = 4
MOE_CAP = 64
MOE_NSUB = MOE_TM // MOE_SUB
MOE_SLAB = MOE_EG * MOE_CAP


def _moe_kernel(h_ref, gate_ref, rank_ref, x_ref, gt_ref, gpost_ref, w1, w3, w2, ws1, ws3, ws2,
                o_ref, acc, xg_s, yw_s, pick_s, back_s):
    g_idx = pl.program_id(1)

    @pl.when(g_idx == 0)
    def _():
        acc[...] = jnp.zeros_like(acc)

    rank = rank_ref[0]
    gate = gate_ref[0]
    top_rank = jnp.max(rank)
    n_rounds = (top_rank.astype(I32) + MOE_CAP) // MOE_CAP
    slot = lax.broadcasted_iota(I32, (MOE_CAP, 1), 0).astype(F32)
    sub_top = [jnp.max(rank[:, s * MOE_SUB:(s + 1) * MOE_SUB]) for s in range(MOE_NSUB)]
    exp_top = [jnp.max(rank[e:e + 1, :]) for e in range(MOE_EG)]

    def make_pick(s, base):
        toks = slice(s * MOE_SUB, (s + 1) * MOE_SUB)
        return jnp.concatenate(
            [jnp.where(rank[e:e + 1, toks] == slot + base, 1.0, 0.0) for e in range(MOE_EG)], axis=0)

    def expert_rows(e, pick_of):
        rows = slice(e * MOE_CAP, (e + 1) * MOE_CAP)
        x_e = jnp.concatenate([xg_s[s, rows, :] for s in range(MOE_NSUB)], axis=0)
        a = jnp.dot(x_e, w1[e], preferred_element_type=F32)
        hid = a * _sigmoid(a) * jnp.dot(x_e, w3[e], preferred_element_type=F32)
        y = jnp.dot(hid.astype(BF16), w2[e], preferred_element_type=F32)
        for s in range(MOE_NSUB):
            toks = slice(s * MOE_SUB, (s + 1) * MOE_SUB)
            w_row = jnp.sum(pick_of(s)[rows, :] * gate[e:e + 1, toks], axis=1, keepdims=True)
            yw_s[s, rows, :] = (y[s * MOE_CAP:(s + 1) * MOE_CAP, :] * w_row).astype(BF16)

    def scatter(s, back):
        toks = slice(s * MOE_SUB, (s + 1) * MOE_SUB)
        acc[toks, :] += jnp.dot(back, yw_s[s], preferred_element_type=F32)

    picks = [make_pick(s, 0.0) for s in range(MOE_NSUB)]
    for s in range(MOE_NSUB):
        toks = slice(s * MOE_SUB, (s + 1) * MOE_SUB)
        xg_s[s] = jnp.dot(picks[s].astype(BF16), h_ref[toks, :], preferred_element_type=F32).astype(BF16)
    for e in range(MOE_EG):
        expert_rows(e, lambda s: picks[s])
    for s in range(MOE_NSUB):
        scatter(s, picks[s].T.astype(BF16))

    @pl.when(n_rounds > 1)
    def _():
        for s in range(MOE_NSUB):
            pick_s[s] = picks[s].astype(BF16)

    def overflow_round(r, carry):
        base = (r * MOE_CAP).astype(F32)
        for s in range(MOE_NSUB):
            @pl.when(sub_top[s] >= base)
            def _(s=s):
                pick = make_pick(s, base)
                pick_s[s] = pick.astype(BF16)
                back_s[s] = pick.T.astype(BF16)
                toks = slice(s * MOE_SUB, (s + 1) * MOE_SUB)
                xg_s[s] = jnp.dot(pick.astype(BF16), h_ref[toks, :], preferred_element_type=F32).astype(BF16)
        for e in range(MOE_EG):
            @pl.when(exp_top[e] >= base)
            def _(e=e):
                expert_rows(e, lambda s: pick_s[s].astype(F32))
        for s in range(MOE_NSUB):
            @pl.when(sub_top[s] >= base)
            def _(s=s):
                scatter(s, back_s[s])
        return carry

    lax.fori_loop(1, n_rounds, overflow_round, 0)

    @pl.when(g_idx == N_EXPERTS // MOE_EG - 1)
    def _():
        h = h_ref[...]
        s = jnp.dot(h, ws1[...], preferred_element_type=F32)
        shid = s * _sigmoid(s) * jnp.dot(h, ws3[...], preferred_element_type=F32)
        y_all = acc[...] + jnp.dot(shid.astype(BF16), ws2[...], preferred_element_type=F32)
        o_ref[...] = x_ref[...] + gt_ref[0] * _rms(y_all, gpost_ref[...])


def _moe(h, gates_t, ranks_t, x, gt2, g_post, w1, w3, w2, ws1, ws3, ws2, layer):
    tpb = SEQ // MOE_TM
    n_groups = N_EXPERTS // MOE_EG
    whole = lambda a: pl.BlockSpec(a.shape, lambda i, g: (0, 0))
    per_group = pl.BlockSpec((1, MOE_EG, MOE_TM), lambda i, g: (g, 0, i))
    return pl.pallas_call(
        _moe_kernel,
        grid=(N_TOK // MOE_TM, n_groups),
        in_specs=[pl.BlockSpec((MOE_TM, D_MODEL), lambda i, g: (i, 0)),
                  per_group, per_group,
                  pl.BlockSpec((MOE_TM, D_MODEL), lambda i, g: (i, 0)),
                  pl.BlockSpec((1, 1, D_MODEL), lambda i, g: (i // tpb, 0, 0)),
                  pl.BlockSpec((1, D_MODEL), lambda i, g: (0, 0)),
                  pl.BlockSpec((None, MOE_EG, D_MODEL, EXPERT_DIM), lambda i, g: (layer, g, 0, 0)),
                  pl.BlockSpec((None, MOE_EG, D_MODEL, EXPERT_DIM), lambda i, g: (layer, g, 0, 0)),
                  pl.BlockSpec((None, MOE_EG, EXPERT_DIM, D_MODEL), lambda i, g: (layer, g, 0, 0)),
                  whole(ws1), whole(ws3), whole(ws2)],
        out_specs=pl.BlockSpec((MOE_TM, D_MODEL), lambda i, g: (i, 0)),
        out_shape=jax.ShapeDtypeStruct((N_TOK, D_MODEL), F32),
        scratch_shapes=[pltpu.VMEM((MOE_TM, D_MODEL), F32),
                        pltpu.VMEM((MOE_NSUB, MOE_SLAB, D_MODEL), BF16),
                        pltpu.VMEM((MOE_NSUB, MOE_SLAB, D_MODEL), BF16),
                        pltpu.VMEM((MOE_NSUB, MOE_SLAB, MOE_SUB), BF16),
                        pltpu.VMEM((MOE_NSUB, MOE_SUB, MOE_SLAB), BF16)],
        compiler_params=_params(("arbitrary", "arbitrary")),
        name="moe_ffn",
    )(h, gates_t.reshape(n_groups, MOE_EG, N_TOK), ranks_t.reshape(n_groups, MOE_EG, N_TOK),
      x, gt2, g_post.reshape(1, D_MODEL), w1, w3, w2, ws1, ws3, ws2)


_IN_SIZES = (512, 256, 256, 64, 4, 512, 512, 512, 512, 512, 1024, 1024, 3072)
_IN_DEST = (COL_QA, COL_CKV, COL_IQ, COL_IKIW, COL_IKIW + 64, COL_QB, COL_KB, COL_VB, COL_QC, COL_KC,
            COL_VC, COL_GC, COL_GATES)
_IN_SRC = tuple(int(v) for v in np.cumsum((0,) + _IN_SIZES[:-1]))
N_IN = sum(_IN_SIZES)
PACK_TK = 128


def _pack_kernel(w_ref, o_ref, *, layer):
    for src, width, dst in zip(_IN_SRC, _IN_SIZES, _IN_DEST):
        o_ref[dst:dst + width, :] = w_ref[src:src + width, layer, :].astype(BF16)
    o_ref[COL_IKIW + 68:COL_IKIW + LANES, :] = jnp.zeros((LANES - 68, PACK_TK), BF16)


def _reorder_w_in(w_t, layer):
    return pl.pallas_call(
        functools.partial(_pack_kernel, layer=layer),
        grid=(D_MODEL // PACK_TK,),
        in_specs=[pl.BlockSpec((N_IN, DEPTH, PACK_TK), lambda i: (0, 0, i))],
        out_specs=pl.BlockSpec((N_PROJ, PACK_TK), lambda i: (0, i)),
        out_shape=jax.ShapeDtypeStruct((N_PROJ, D_MODEL), BF16),
        compiler_params=_params(("arbitrary",)),
        name="pack_w_in",
    )(w_t)


def _rope_tables(positions):
    half = HEAD_DIM // 2
    inv_freq = ROPE_THETA ** (-jnp.arange(half, dtype=F32) / half)
    ang = positions.astype(F32)[:, :, None] * inv_freq
    cos, sin = jnp.cos(ang), jnp.sin(ang)
    cos64 = jnp.concatenate([cos, cos], axis=-1)
    sin64 = jnp.concatenate([-sin, sin], axis=-1)
    cos_a = jnp.concatenate([cos64, cos64], axis=-1)
    sin_a = jnp.concatenate([sin64, sin64], axis=-1)
    return cos_a, sin_a


def _decay_tables():
    log_g = jnp.log(1.0 - 2.0 ** (-5.0 - jnp.arange(C_HEADS, dtype=F32)))
    pos = jnp.arange(RET_BLK, dtype=F32)
    diff = pos[:, None] - pos[None, :]
    chunk = jnp.arange(RET_BLK) // CHUNK
    visible = chunk[None, :] <= chunk[:, None]
    e_diag = jnp.where(visible, jnp.exp(log_g[:, None, None] * jnp.abs(diff)), 0.0)
    dec = jnp.exp(log_g * RET_BLK)
    per_lane = jnp.repeat(log_g, HEAD_DIM)[None, :]
    q_scale = jnp.exp(per_lane * pos[:, None])
    k_scale = jnp.exp(per_lane * (RET_BLK - pos)[:, None])
    return dec, q_scale, k_scale, e_diag


def kernel(x, c, positions, ada_w, ada_b, g_pre_mix, g_post_mix, g_pre_ffn, g_post_ffn, w_in, g_kv, w_uk, w_uv,
           g_ret, w_o_a, w_o_b, w_o_c, w_out, w_router, router_bias, w1, w3, w2, ws1, ws3, ws2):
    cos_a, sin_a = _rope_tables(positions)
    dec, q_scale, k_scale, e_diag = _decay_tables()
    xf = x.reshape(N_TOK, D_MODEL)
    w_in_t = jnp.transpose(w_in, (2, 0, 1))
    w1_b, w3_b, w2_b = w1.astype(BF16), w3.astype(BF16), w2.astype(BF16)
    for l in range(DEPTH):
        mod = _ada(c, ada_w, ada_b, l).reshape(BATCH, 1, N_ADA * D_MODEL)
        sh1, sc1, gt1, sh2, sc2, gt2 = (mod[:, :, k * D_MODEL:(k + 1) * D_MODEL] for k in range(N_ADA))
        z = _inproj(xf, g_pre_mix[l], sc1, sh1, _reorder_w_in(w_in_t, l))
        z3 = z.reshape(BATCH, SEQ, N_PROJ)
        w_ukv = jnp.concatenate([w_uk[l], w_uv[l]], axis=1).astype(BF16)
        ya = _dsa(z3, cos_a, sin_a, g_kv[l], w_ukv).reshape(N_TOK, -1)
        yb = _sb(z3).reshape(N_TOK, -1)
        yc = _ret(z3, cos_a, sin_a, dec, q_scale, k_scale, e_diag, g_ret[l]).reshape(N_TOK, -1)
        xf, h, gates, ranks = _merge(ya, yb, yc, z, xf, gt1, g_post_mix[l], g_pre_ffn[l], sc2, sh2,
                              w_o_a[l].astype(BF16), w_o_b[l].astype(BF16), w_o_c[l].astype(BF16),
                              w_out[l].astype(BF16), w_router[l].T.astype(BF16),
                              router_bias[l].reshape(N_EXPERTS, 1))
        xf = _moe(h, gates, ranks, xf, gt2, g_post_ffn[l], w1_b, w3_b, w2_b,
                  ws1[l].astype(BF16), ws3[l].astype(BF16), ws2[l].astype(BF16), l)
    return xf.reshape(BATCH, SEQ, D_MODEL)
```

```python
import math

import numpy as np
import jax
import jax.numpy as jnp
from jax import lax
from jax.experimental import pallas as pl
from jax.experimental.pallas import tpu as pltpu

F32 = jnp.float32
BF16 = jnp.bfloat16
I32 = jnp.int32

D_MODEL = 1024
BATCH = 8
SEQ = 2048
DEPTH = 2
N_TOK = BATCH * SEQ

CHUNK = 64
HEAD_DIM = 64
ROPE_THETA = 10000.0
EPS = 1e-6
A_HEADS = 8
A_KV_LORA = 256
IDX_HEADS = 4
IDX_DIM = 64
TOPK = 256
B_HEADS = 8
C_HEADS = 8
C_V_DIM = 128
N_EXPERTS = 64
TOP_K = 8
N_GROUPS = 8
TOPK_GROUPS = 4
EXPERT_DIM = 256
SHARED_DIM = 256
ROUTED_SCALE = 2.5
N_ADA = 6

LANES = 128
VMEM_LIMIT = 56 * 1024 * 1024

COL_GATES = 0
COL_VC = 3072
COL_GC = 4096
COL_QA = 5120
COL_QB = 5632
COL_KB = 6144
COL_VB = 6656
COL_QC = 7168
COL_KC = 7680
COL_CKV = 8192
COL_IQ = 8448
COL_IKIW = 8704
N_PROJ = 8832
PROJ_TN = 2944
PROJ_TM = 1024

INT_MIN = -2147483648
NEG_INF_KEY = -2139095041

NT_DIMS = (((1,), (1,)), ((), ()))
LOG2_E = math.log2(math.e)


def _params(sem):
    return pltpu.CompilerParams(dimension_semantics=sem, vmem_limit_bytes=VMEM_LIMIT)


def _sigmoid(x):
    return 1.0 / (1.0 + jnp.exp(-x))


def _rope_lanes(x, cos, sin_signed):
    n = x.shape[-1]
    lane = lax.broadcasted_iota(I32, (1, n), 1)
    first = (lane & 32) == 0
    ahead = pltpu.roll(x, n - 32, 1)
    behind = pltpu.roll(x, 32, 1)
    return x * cos + jnp.where(first, ahead, behind) * sin_signed


def _ada_kernel(c_ref, w_ref, b_ref, o_ref):
    c = c_ref[...]
    s = (c * _sigmoid(c)).astype(BF16)
    o_ref[...] = jnp.dot(s, w_ref[...].astype(BF16), preferred_element_type=F32) + b_ref[...]


def _ada(c, w, b, layer):
    n = w.shape[2]
    return pl.pallas_call(
        _ada_kernel,
        grid=(n // D_MODEL,),
        in_specs=[pl.BlockSpec((BATCH, D_MODEL), lambda j: (0, 0)),
                  pl.BlockSpec((None, D_MODEL, D_MODEL), lambda j: (layer, 0, j)),
                  pl.BlockSpec((None, 1, D_MODEL), lambda j: (layer, 0, j))],
        out_specs=pl.BlockSpec((BATCH, D_MODEL), lambda j: (0, j)),
        out_shape=jax.ShapeDtypeStruct((BATCH, n), F32),
        compiler_params=_params(("arbitrary",)),
        name="ada_mod",
    )(c, w, b.reshape(DEPTH, 1, n))


def _inproj_kernel(x_ref, g_ref, sc_ref, sh_ref, w_ref, o_ref, h_ref):
    @pl.when(pl.program_id(1) == 0)
    def _():
        x = x_ref[...]
        ms = jnp.mean(x * x, axis=-1, keepdims=True)
        y = x * lax.rsqrt(ms + EPS) * g_ref[...]
        h_ref[...] = (y * (1.0 + sc_ref[0]) + sh_ref[0]).astype(BF16)

    o_ref[...] = lax.dot_general(h_ref[...], w_ref[...], NT_DIMS, preferred_element_type=F32)


def _inproj(x, g, sc, sh, w_t_bf16, layer):
    tiles_per_batch = SEQ // PROJ_TM
    return pl.pallas_call(
        _inproj_kernel,
        grid=(N_TOK // PROJ_TM, N_PROJ // PROJ_TN),
        in_specs=[pl.BlockSpec((PROJ_TM, D_MODEL), lambda i, j: (i, 0)),
                  pl.BlockSpec((1, D_MODEL), lambda i, j: (0, 0)),
                  pl.BlockSpec((1, 1, D_MODEL), lambda i, j: (i // tiles_per_batch, 0, 0)),
                  pl.BlockSpec((1, 1, D_MODEL), lambda i, j: (i // tiles_per_batch, 0, 0)),
                  pl.BlockSpec((None, PROJ_TN, D_MODEL), lambda i, j: (layer, j, 0))],
        out_specs=pl.BlockSpec((PROJ_TM, PROJ_TN), lambda i, j: (i, j)),
        out_shape=jax.ShapeDtypeStruct((N_TOK, N_PROJ), F32),
        scratch_shapes=[pltpu.VMEM((PROJ_TM, D_MODEL), BF16)],
        compiler_params=_params(("arbitrary", "arbitrary")),
        name="in_proj",
    )(x, g.reshape(1, D_MODEL), sc, sh, w_t_bf16)


DSA_QB = 256
DSA_KC = 256


def _dsa_kernel(ckv_ref, ikiw_ref, q_ref, iq_ref, cos_a, sin_a, gkv_ref, wukv_ref,
                o_ref, k_lo, k_hi, v_one, ik_lo, ik_hi, keys_s, selt_s, acc_s, q_s):
    qi = pl.program_id(1)
    lane = lax.broadcasted_iota(I32, (1, LANES), 1)
    lo_half = lane < 64
    n_kc = SEQ // DSA_KC

    @pl.when(qi == 0)
    def _prologue():
        def chunk(c, carry):
            r0 = pl.multiple_of(c * DSA_KC, DSA_KC)
            rows = pl.ds(r0, DSA_KC)
            ckv = ckv_ref[0, rows, :]
            ms = jnp.mean(ckv * ckv, axis=-1, keepdims=True)
            cn = (ckv * lax.rsqrt(ms + EPS) * gkv_ref[...]).astype(BF16)
            kv = jnp.dot(cn, wukv_ref[...], preferred_element_type=F32)
            cb = jnp.where(lo_half, cos_a[0, rows, :], 1.0)
            sb = jnp.where(lo_half, sin_a[0, rows, :], 0.0)
            kv = _rope_lanes(kv, cb, sb)
            swapped = pltpu.roll(kv, 64, 1)
            k_lo[rows, :] = jnp.where(lo_half, kv, 0.0).astype(BF16)
            k_hi[rows, :] = jnp.where(lo_half, 0.0, swapped).astype(BF16)
            v_one[rows, :] = jnp.where(lo_half, swapped, 1.0).astype(BF16)
            ik = jnp.where(lo_half, _rope_lanes(ikiw_ref[0, rows, :], cb, sb), 0.0)
            ik_lo[rows, :] = ik.astype(BF16)
            ik_hi[rows, :] = pltpu.roll(ik, 64, 1).astype(BF16)
            return carry
        lax.fori_loop(0, n_kc, chunk, 0)

    q0 = pl.multiple_of(qi * DSA_QB, DSA_QB)
    qrows = pl.ds(q0, DSA_QB)
    cq, sq = cos_a[0, qrows, :], sin_a[0, qrows, :]

    iq = iq_ref[0]
    iq_heads = []
    for p in range(IDX_HEADS // 2):
        pair = _rope_lanes(iq[:, p * LANES:(p + 1) * LANES], cq, sq)
        iq_heads.append(jnp.where(lo_half, pair, 0.0).astype(BF16))
        iq_heads.append(jnp.where(lo_half, 0.0, pair).astype(BF16))
    iq_stacked = [jnp.concatenate(iq_heads[parity::2], axis=0) for parity in range(2)]
    iw_t = ikiw_ref[0, qrows, :].T * (IDX_HEADS ** -0.5)
    q_chunk = (q0 + lax.broadcasted_iota(I32, (1, DSA_QB), 1)) >> 6
    n_live = lax.shift_right_logical((qi + 1) * DSA_QB + (DSA_KC - 1), DSA_KC.bit_length() - 1)

    def chunk_rows(c):
        return pl.ds(pl.multiple_of(c * DSA_KC, DSA_KC), DSA_KC)

    def score_chunk(c, carry):
        rows = chunk_rows(c)
        k_chunk = (c * DSA_KC + lax.broadcasted_iota(I32, (DSA_KC, 1), 0)) >> 6
        score = jnp.zeros((DSA_KC, DSA_QB), F32)
        for parity, ik in enumerate((ik_lo, ik_hi)):
            logits = lax.dot_general(ik[rows, :], iq_stacked[parity], NT_DIMS,
                                     preferred_element_type=F32) * (IDX_DIM ** -0.5)
            for j in range(IDX_HEADS // 2):
                h = parity + 2 * j
                score = score + iw_t[64 + h:65 + h, :] * jnp.maximum(logits[:, j * DSA_QB:(j + 1) * DSA_QB], 0.0)
        score = jnp.where(score == 0.0, 0.0, score)
        score = jnp.where(k_chunk <= q_chunk, score, -jnp.inf)
        bits = lax.bitcast_convert_type(score, I32)
        keys_s[rows, :] = bits ^ ((bits >> 31) & 0x7FFFFFFF)
        return carry
    lax.fori_loop(0, n_live, score_chunk, 0)

    def count(preds):
        def chunk(c, accs):
            key = keys_s[chunk_rows(c), :]
            return tuple(acc + jnp.sum(pred(key).astype(I32).reshape(4, DSA_KC // 4, DSA_QB), axis=0)
                         for acc, pred in zip(accs, preds))
        zero = jnp.zeros((DSA_KC // 4, DSA_QB), I32)
        accs = lax.fori_loop(0, n_live, chunk, (zero,) * len(preds))
        return [jnp.sum(acc, axis=0, keepdims=True) for acc in accs]

    def descend(it, prefix):
        cand = prefix | lax.shift_left(jnp.int32(1), 31 - it)
        bound = cand ^ INT_MIN
        return jnp.where(count([lambda k: k >= bound])[0] >= TOPK, cand, prefix)
    thr = lax.fori_loop(0, 32, descend, jnp.zeros((1, DSA_QB), I32)) ^ INT_MIN

    n_gt, n_eq = count([lambda k: k > thr, lambda k: k == thr])
    need = (TOPK - n_gt).astype(F32)
    tie_break = jnp.max(n_gt + n_eq) > TOPK

    @pl.when(tie_break)
    def _():
        tri = (lax.broadcasted_iota(I32, (DSA_KC, DSA_KC), 0)
               >= lax.broadcasted_iota(I32, (DSA_KC, DSA_KC), 1)).astype(BF16)

        def select_chunk(c, seen):
            key = keys_s[chunk_rows(c), :]
            eq = key == thr
            rank = jnp.dot(tri, eq.astype(BF16), preferred_element_type=F32) + seen
            sel = ((key > thr) | (eq & (rank <= need))) & (key > NEG_INF_KEY)
            selt_s[c] = sel.astype(F32).T
            return rank[DSA_KC - 1:DSA_KC, :]
        lax.fori_loop(0, n_live, select_chunk, jnp.zeros((1, DSA_QB), F32))

    @pl.when(jnp.logical_not(tie_break))
    def _():
        def select_chunk(c, carry):
            key = keys_s[chunk_rows(c), :]
            selt_s[c] = ((key >= thr) & (key > NEG_INF_KEY)).astype(F32).T
            return carry
        lax.fori_loop(0, n_live, select_chunk, 0)

    q = q_ref[0]
    n_pairs = A_HEADS // 2
    for p in range(n_pairs):
        pair = _rope_lanes(q[:, p * LANES:(p + 1) * LANES], cq, sq) * (HEAD_DIM ** -0.5 * LOG2_E)
        q_s[p * DSA_QB:(p + 1) * DSA_QB, :] = jnp.where(lo_half, pair, 0.0).astype(BF16)
        q_s[(n_pairs + p) * DSA_QB:(n_pairs + p + 1) * DSA_QB, :] = jnp.where(lo_half, 0.0, pair).astype(BF16)
    masked = -1e30
    half = n_pairs * DSA_QB

    def attend(c, m):
        rows = chunk_rows(c)
        bias = jnp.where(selt_s[c] > 0.0, 0.0, masked)
        s = jnp.concatenate(
            [lax.dot_general(q_s[:half, :], k_lo[rows, :], NT_DIMS, preferred_element_type=F32),
             lax.dot_general(q_s[half:, :], k_hi[rows, :], NT_DIMS, preferred_element_type=F32)], axis=0)
        s = s.reshape(A_HEADS, DSA_QB, DSA_KC) + bias[None]
        m_new = jnp.maximum(m, jnp.max(s, axis=-1, keepdims=True))
        pe = jnp.exp2(s - m_new).astype(BF16)
        alpha = jnp.exp2(m - m_new)
        for part in range(2):
            slabs = slice(part * n_pairs, (part + 1) * n_pairs)
            pv = jnp.dot(pe[slabs].reshape(half, DSA_KC), v_one[rows, :], preferred_element_type=F32)
            acc_s[slabs] = alpha[slabs] * acc_s[slabs] + pv.reshape(n_pairs, DSA_QB, LANES)
        return m_new

    acc_s[...] = jnp.zeros_like(acc_s)
    lax.fori_loop(0, n_live, attend, jnp.full((A_HEADS, DSA_QB, 1), masked, F32))
    for p in range(n_pairs):
        even, odd = acc_s[p], acc_s[n_pairs + p]
        even_swapped = pltpu.roll(even, HEAD_DIM, 1)
        odd_swapped = pltpu.roll(odd, HEAD_DIM, 1)
        o_ref[0, :, p * LANES:(p + 1) * LANES] = jnp.where(lo_half, even / even_swapped, odd_swapped / odd)


def _dsa(z3, cos_a, sin_a, g_kv, w_ukv_bf16):
    full = lambda width, col: pl.BlockSpec((1, SEQ, width), lambda b, i: (b, 0, col // width))
    blk = lambda width, col: pl.BlockSpec((1, DSA_QB, width), lambda b, i: (b, i, col // width))
    tab = pl.BlockSpec((1, SEQ, LANES), lambda b, i: (b, 0, 0))
    return pl.pallas_call(
        _dsa_kernel,
        grid=(BATCH, SEQ // DSA_QB),
        in_specs=[full(A_KV_LORA, COL_CKV), full(LANES, COL_IKIW),
                  blk(A_HEADS * HEAD_DIM, COL_QA), blk(IDX_HEADS * IDX_DIM, COL_IQ),
                  tab, tab,
                  pl.BlockSpec((1, A_KV_LORA), lambda b, i: (0, 0)),
                  pl.BlockSpec((A_KV_LORA, LANES), lambda b, i: (0, 0))],
        out_specs=pl.BlockSpec((1, DSA_QB, A_HEADS * HEAD_DIM), lambda b, i: (b, i, 0)),
        out_shape=jax.ShapeDtypeStruct((BATCH, SEQ, A_HEADS * HEAD_DIM), F32),
        scratch_shapes=[pltpu.VMEM((SEQ, LANES), BF16)] * 5
                       + [pltpu.VMEM((SEQ, DSA_QB), I32),
                          pltpu.VMEM((SEQ // DSA_KC, DSA_QB, DSA_KC), F32),
                          pltpu.VMEM((A_HEADS, DSA_QB, LANES), F32),
                          pltpu.VMEM((A_HEADS * DSA_QB, LANES), BF16)],
        compiler_params=_params(("arbitrary", "arbitrary")),
        name="dsa_mixer",
    )(z3, z3, z3, z3, cos_a, sin_a, g_kv.reshape(1, A_KV_LORA), w_ukv_bf16)


SB_BLK = 128
SB_EXP_UNDERFLOW = -105.0


def _sb_kernel(q_ref, k_ref, v_ref, o_ref, q_s, tail_s, acc_s):
    qi = pl.program_id(1)
    n_pairs = B_HEADS // 2
    rows_all = B_HEADS * SB_BLK
    lane = lax.broadcasted_iota(I32, (1, LANES), 1)
    lo_half = lane < 64
    causal = (lax.broadcasted_iota(I32, (rows_all, SB_BLK), 1)
              < (lax.broadcasted_iota(I32, (rows_all, SB_BLK), 0) & (SB_BLK - 1)))
    r2 = lax.broadcasted_iota(I32, (2 * SB_BLK, 2 * SB_BLK), 0) & (SB_BLK - 1)
    c2 = lax.broadcasted_iota(I32, (2 * SB_BLK, 2 * SB_BLK), 1)
    suffix_mat = ((c2 >= SB_BLK) | (r2 > c2)).astype(BF16)

    for p in range(n_pairs):
        q2 = q_ref[0, :, p * LANES:(p + 1) * LANES] * (HEAD_DIM ** -0.5)
        q_s[2 * p * SB_BLK:(2 * p + 1) * SB_BLK, :] = jnp.where(lo_half, q2, 0.0).astype(BF16)
        q_s[(2 * p + 1) * SB_BLK:(2 * p + 2) * SB_BLK, :] = jnp.where(lo_half, 0.0, q2).astype(BF16)

    def key_block(kb, diagonal):
        r0 = pl.multiple_of(kb * SB_BLK, SB_BLK)
        pair_rows = lambda p: slice(2 * p * SB_BLK, (2 * p + 2) * SB_BLK)
        z = jnp.concatenate(
            [lax.dot_general(q_s[pair_rows(p), :], k_ref[0, pl.ds(r0, SB_BLK), p * LANES:(p + 1) * LANES].astype(BF16),
                             NT_DIMS, preferred_element_type=F32) for p in range(n_pairs)], axis=0)
        log_beta = jnp.minimum(z, 0.0) - jnp.log(1.0 + jnp.exp(-jnp.abs(z)))
        log_not = log_beta - z
        if diagonal:
            log_not = jnp.where(causal, log_not, 0.0)
        hi = log_not.astype(BF16)
        lo = (log_not - hi.astype(F32)).astype(BF16)
        hi_lo = jnp.concatenate([hi, lo], axis=1)
        t = jnp.concatenate([jnp.dot(hi_lo[pair_rows(p), :], suffix_mat, preferred_element_type=F32)
                             for p in range(n_pairs)], axis=0)
        if diagonal:
            a = jnp.where(causal, jnp.exp(log_beta + t[:, :SB_BLK]), 0.0)
            tail = t[:, SB_BLK:]
        else:
            a = jnp.exp(log_beta + t[:, :SB_BLK] + tail_s[...])
            tail = tail_s[...] + t[:, SB_BLK:]
        tail_s[...] = tail
        a = a.astype(BF16)
        for p in range(n_pairs):
            v2 = v_ref[0, pl.ds(r0, SB_BLK), p * LANES:(p + 1) * LANES].astype(BF16)
            pv = jnp.dot(a[pair_rows(p), :], v2, preferred_element_type=F32)
            if diagonal:
                acc_s[pair_rows(p), :] = pv
            else:
                acc_s[pair_rows(p), :] += pv
        return jnp.max(tail)

    first = key_block(qi, True)

    def more(state):
        jj, worst = state
        return (jj <= qi) & (worst > SB_EXP_UNDERFLOW)

    def step(state):
        jj, _ = state
        return jj + 1, key_block(qi - jj, False)

    lax.while_loop(more, step, (jnp.int32(1), first))
    for p in range(n_pairs):
        even = acc_s[2 * p * SB_BLK:(2 * p + 1) * SB_BLK, :]
        odd = acc_s[(2 * p + 1) * SB_BLK:(2 * p + 2) * SB_BLK, :]
        o_ref[0, :, p * LANES:(p + 1) * LANES] = jnp.where(lo_half, even, odd)


def _sb(z3):
    width = B_HEADS * HEAD_DIM
    return pl.pallas_call(
        _sb_kernel,
        grid=(BATCH, SEQ // SB_BLK),
        in_specs=[pl.BlockSpec((1, SB_BLK, width), lambda b, i: (b, i, COL_QB // width)),
                  pl.BlockSpec((1, SEQ, width), lambda b, i: (b, 0, COL_KB // width)),
                  pl.BlockSpec((1, SEQ, width), lambda b, i: (b, 0, COL_VB // width))],
        out_specs=pl.BlockSpec((1, SB_BLK, width), lambda b, i: (b, i, 0)),
        out_shape=jax.ShapeDtypeStruct((BATCH, SEQ, width), F32),
        scratch_shapes=[pltpu.VMEM((B_HEADS * SB_BLK, LANES), BF16),
                        pltpu.VMEM((B_HEADS * SB_BLK, SB_BLK), F32),
                        pltpu.VMEM((B_HEADS * SB_BLK, SB_BLK), F32)],
        compiler_params=_params(("arbitrary", "arbitrary")),
        name="stickbreak_mixer",
    )(z3, z3, z3)


RET_BLK = 256


def _ret_kernel(dec_ref, q_ref, k_ref, v_ref, gate_ref, cos_a, sin_a, qs_ref, ks_ref, e_diag, gret_ref,
                o_ref, state):
    lane = lax.broadcasted_iota(I32, (1, LANES), 1)
    lo_half = lane < 64
    row_lo = lax.broadcasted_iota(I32, (LANES, 1), 0) < 64

    @pl.when(pl.program_id(1) == 0)
    def _():
        state[...] = jnp.zeros_like(state)

    cq, sq = cos_a[0], sin_a[0]
    for p in range(C_HEADS // 2):
        cols = slice(p * LANES, (p + 1) * LANES)
        q2 = _rope_lanes(q_ref[0, :, cols], cq, sq)
        k2 = _rope_lanes(k_ref[0, :, cols], cq, sq) * (HEAD_DIM ** -0.5)
        q_hat = q2 * qs_ref[:, cols]
        k_hat_t = (k2 * ks_ref[:, cols]).T
        for e in range(2):
            h = 2 * p + e
            pick = (lambda x: jnp.where(lo_half, x, 0.0)) if e == 0 else (lambda x: jnp.where(lo_half, 0.0, x))
            vcols = slice(h * C_V_DIM, (h + 1) * C_V_DIM)
            v2 = v_ref[0, :, vcols].astype(BF16)
            s = lax.dot_general(pick(q2).astype(BF16), k2.astype(BF16), NT_DIMS,
                                preferred_element_type=F32) * e_diag[h]
            o = jnp.dot(s.astype(BF16), v2, preferred_element_type=F32)
            o = o + jnp.dot(pick(q_hat).astype(BF16), state[h].astype(BF16), preferred_element_type=F32)
            kv = jnp.dot(jnp.where(row_lo if e == 0 else ~row_lo, k_hat_t, 0.0).astype(BF16), v2,
                         preferred_element_type=F32)
            state[h] = dec_ref[h] * state[h] + kv
            mu = jnp.mean(o, axis=-1, keepdims=True)
            d = o - mu
            var = jnp.mean(d * d, axis=-1, keepdims=True)
            y = d * lax.rsqrt(var + EPS) * gret_ref[:, vcols]
            g = gate_ref[0, :, vcols]
            o_ref[0, :, vcols] = g * _sigmoid(g) * y


def _ret(z3, cos_a, sin_a, dec, q_scale, k_scale, e_diag, g_ret):
    qk_w = C_HEADS * HEAD_DIM
    v_w = C_HEADS * C_V_DIM
    tab = pl.BlockSpec((1, RET_BLK, LANES), lambda b, i: (b, i, 0))
    const2 = lambda a: pl.BlockSpec(a.shape, lambda b, i: (0, 0))
    return pl.pallas_call(
        _ret_kernel,
        grid=(BATCH, SEQ // RET_BLK),
        in_specs=[pl.BlockSpec(memory_space=pltpu.SMEM),
                  pl.BlockSpec((1, RET_BLK, qk_w), lambda b, i: (b, i, COL_QC // qk_w)),
                  pl.BlockSpec((1, RET_BLK, qk_w), lambda b, i: (b, i, COL_KC // qk_w)),
                  pl.BlockSpec((1, RET_BLK, v_w), lambda b, i: (b, i, COL_VC // v_w)),
                  pl.BlockSpec((1, RET_BLK, v_w), lambda b, i: (b, i, COL_GC // v_w)),
                  tab, tab, const2(q_scale), const2(k_scale),
                  pl.BlockSpec(e_diag.shape, lambda b, i: (0, 0, 0)),
                  pl.BlockSpec((1, v_w), lambda b, i: (0, 0))],
        out_specs=pl.BlockSpec((1, RET_BLK, v_w), lambda b, i: (b, i, 0)),
        out_shape=jax.ShapeDtypeStruct((BATCH, SEQ, v_w), F32),
        scratch_shapes=[pltpu.VMEM((C_HEADS, LANES, C_V_DIM), F32)],
        compiler_params=_params(("arbitrary", "arbitrary")),
        name="retention_mixer",
    )(dec, z3, z3, z3, z3, cos_a, sin_a, q_scale, k_scale, e_diag, g_ret.reshape(1, v_w))


MERGE_TM = 512
MOE_SUB = 256


def _rms(y, g):
    ms = jnp.mean(y * y, axis=-1, keepdims=True)
    return y * lax.rsqrt(ms + EPS) * g


def _route(logits_t, bias_col):
    n_tok = logits_t.shape[1]
    per_group = N_EXPERTS // N_GROUPS
    scores = _sigmoid(logits_t)
    sel = scores + bias_col
    sel3 = sel.reshape(N_GROUPS, per_group, n_tok)
    member = lax.broadcasted_iota(I32, (N_GROUPS, per_group, n_tok), 1)
    m1 = jnp.max(sel3, axis=1, keepdims=True)
    first = jnp.min(jnp.where(sel3 == m1, member, per_group), axis=1, keepdims=True)
    m2 = jnp.max(jnp.where(member == first, -jnp.inf, sel3), axis=1, keepdims=True)
    grp = (m1 + m2).reshape(N_GROUPS, n_tok)
    gid = lax.broadcasted_iota(I32, (N_GROUPS, n_tok), 0)
    rank = jnp.zeros((N_GROUPS, n_tok), I32)
    for g in range(N_GROUPS):
        other = grp[g:g + 1, :]
        rank = rank + ((other > grp) | ((other == grp) & (g < gid))).astype(I32)
    keep = jnp.where(rank < TOPK_GROUPS, 1.0, 0.0).reshape(N_GROUPS, 1, n_tok)
    live = jnp.where(keep > 0.0, sel3, -jnp.inf).reshape(N_EXPERTS, n_tok)
    eid = lax.broadcasted_iota(I32, (N_EXPERTS, n_tok), 0)
    chosen = jnp.zeros((N_EXPERTS, n_tok), jnp.bool_)
    for _ in range(TOP_K):
        m = jnp.max(live, axis=0, keepdims=True)
        pick = eid == jnp.min(jnp.where(live == m, eid, N_EXPERTS), axis=0, keepdims=True)
        chosen = chosen | pick
        live = jnp.where(pick, -jnp.inf, live)
    w = jnp.where(chosen, scores, 0.0)
    return w / jnp.sum(w, axis=0, keepdims=True) * ROUTED_SCALE, chosen


def _merge_kernel(ya_ref, yb_ref, yc_ref, g_ref, x_ref, gt_ref, gpost_ref, gpre_ref, sc_ref, sh_ref,
                  woa, wob, woc, wout, wr_t, rb_ref, xo_ref, h_ref, gate_ref, rank_ref):
    ya = jnp.dot(ya_ref[...].astype(BF16), woa[...], preferred_element_type=F32)
    yb = jnp.dot(yb_ref[...].astype(BF16), wob[...], preferred_element_type=F32)
    yc = jnp.dot(yc_ref[...].astype(BF16), woc[...], preferred_element_type=F32)
    merged = (_sigmoid(g_ref[:, 0:D_MODEL]) * ya
              + _sigmoid(g_ref[:, D_MODEL:2 * D_MODEL]) * yb
              + _sigmoid(g_ref[:, 2 * D_MODEL:3 * D_MODEL]) * yc)
    y = jnp.dot(merged.astype(BF16), wout[...], preferred_element_type=F32)
    x = x_ref[...] + gt_ref[0] * _rms(y, gpost_ref[...])
    xo_ref[...] = x
    h = (_rms(x, gpre_ref[...]) * (1.0 + sc_ref[0]) + sh_ref[0]).astype(BF16)
    h_ref[...] = h
    logits_t = lax.dot_general(wr_t[...], h, NT_DIMS, preferred_element_type=F32)
    gates_t, chosen = _route(logits_t, rb_ref[...])
    gate_ref[...] = gates_t
    earlier = (lax.broadcasted_iota(I32, (MOE_SUB, MOE_SUB), 0)
               < lax.broadcasted_iota(I32, (MOE_SUB, MOE_SUB), 1)).astype(BF16)
    routed = jnp.where(chosen, 1.0, 0.0).astype(BF16)
    before = jnp.concatenate(
        [jnp.dot(routed[:, s * MOE_SUB:(s + 1) * MOE_SUB], earlier, preferred_element_type=F32)
         for s in range(MERGE_TM // MOE_SUB)], axis=1)
    rank_ref[...] = jnp.where(chosen, before, -1.0)


def _merge(ya, yb, yc, z, x, gt1, g_post, g_pre, sc2, sh2, woa, wob, woc, wout, wr_t, rbias):
    tpb = SEQ // MERGE_TM
    row = lambda w: pl.BlockSpec((MERGE_TM, w), lambda i: (i, 0))
    vec = pl.BlockSpec((1, D_MODEL), lambda i: (0, 0))
    mod = pl.BlockSpec((1, 1, D_MODEL), lambda i: (i // tpb, 0, 0))
    whole = lambda a: pl.BlockSpec(a.shape, lambda i: (0, 0))
    return pl.pallas_call(
        _merge_kernel,
        grid=(N_TOK // MERGE_TM,),
        in_specs=[row(512), row(512), row(1024), pl.BlockSpec((MERGE_TM, 3 * D_MODEL), lambda i: (i, 0)),
                  row(D_MODEL), mod, vec, vec, mod, mod,
                  whole(woa), whole(wob), whole(woc), whole(wout), whole(wr_t), whole(rbias)],
        out_specs=[row(D_MODEL), row(D_MODEL),
                   pl.BlockSpec((N_EXPERTS, MERGE_TM), lambda i: (0, i)),
                   pl.BlockSpec((N_EXPERTS, MERGE_TM), lambda i: (0, i))],
        out_shape=[jax.ShapeDtypeStruct((N_TOK, D_MODEL), F32),
                   jax.ShapeDtypeStruct((N_TOK, D_MODEL), BF16),
                   jax.ShapeDtypeStruct((N_EXPERTS, N_TOK), F32),
                   jax.ShapeDtypeStruct((N_EXPERTS, N_TOK), F32)],
        compiler_params=_params(("arbitrary",)),
        name="merge_route",
    )(ya, yb, yc, z, x, gt1, g_post.reshape(1, D_MODEL), g_pre.reshape(1, D_MODEL), sc2, sh2,
      woa, wob, woc, wout, wr_t, rbias)


MOE_TM = 1024
MOE_EG = 4
MOE_CAP = 64
MOE_NSUB = MOE_TM // MOE_SUB
MOE_SLAB = MOE_EG * MOE_CAP


def _moe_kernel(h_ref, gate_ref, rank_ref, x_ref, gt_ref, gpost_ref, w1, w3, w2, ws1, ws3, ws2,
                o_ref, acc, xg_s, yw_s, pick_s, back_s):
    g_idx = pl.program_id(1)

    @pl.when(g_idx == 0)
    def _():
        acc[...] = jnp.zeros_like(acc)

    rank = rank_ref[0]
    gate = gate_ref[0]
    n_rounds = (jnp.max(rank).astype(I32) + MOE_CAP) // MOE_CAP
    slot = lax.broadcasted_iota(I32, (MOE_CAP, 1), 0).astype(F32)

    def make_pick(s, base):
        toks = slice(s * MOE_SUB, (s + 1) * MOE_SUB)
        return jnp.concatenate(
            [jnp.where(rank[e:e + 1, toks] == slot + base, 1.0, 0.0) for e in range(MOE_EG)], axis=0)

    def expert_rows(e, pick_of):
        rows = slice(e * MOE_CAP, (e + 1) * MOE_CAP)
        x_e = jnp.concatenate([xg_s[s, rows, :] for s in range(MOE_NSUB)], axis=0)
        a = jnp.dot(x_e, w1[e], preferred_element_type=F32)
        hid = a * _sigmoid(a) * jnp.dot(x_e, w3[e], preferred_element_type=F32)
        y = jnp.dot(hid.astype(BF16), w2[e], preferred_element_type=F32)
        for s in range(MOE_NSUB):
            toks = slice(s * MOE_SUB, (s + 1) * MOE_SUB)
            w_row = jnp.sum(pick_of(s)[rows, :] * gate[e:e + 1, toks], axis=1, keepdims=True)
            yw_s[s, rows, :] = (y[s * MOE_CAP:(s + 1) * MOE_CAP, :] * w_row).astype(BF16)

    def scatter(s, back):
        toks = slice(s * MOE_SUB, (s + 1) * MOE_SUB)
        acc[toks, :] += jnp.dot(back, yw_s[s], preferred_element_type=F32)

    picks = [make_pick(s, 0.0) for s in range(MOE_NSUB)]
    for s in range(MOE_NSUB):
        toks = slice(s * MOE_SUB, (s + 1) * MOE_SUB)
        xg_s[s] = jnp.dot(picks[s].astype(BF16), h_ref[toks, :], preferred_element_type=F32).astype(BF16)
    for e in range(MOE_EG):
        expert_rows(e, lambda s: picks[s])
    for s in range(MOE_NSUB):
        scatter(s, picks[s].T.astype(BF16))

    @pl.when(n_rounds > 1)
    def _():
        for s in range(MOE_NSUB):
            pick_s[s] = picks[s].astype(BF16)

    def overflow_round(r, carry):
        base = (r * MOE_CAP).astype(F32)
        sub_top = [jnp.max(rank[:, s * MOE_SUB:(s + 1) * MOE_SUB]) for s in range(MOE_NSUB)]
        exp_top = [jnp.max(rank[e:e + 1, :]) for e in range(MOE_EG)]
        for s in range(MOE_NSUB):
            @pl.when(sub_top[s] >= base)
            def _(s=s):
                pick = make_pick(s, base)
                pick_s[s] = pick.astype(BF16)
                back_s[s] = pick.T.astype(BF16)
                toks = slice(s * MOE_SUB, (s + 1) * MOE_SUB)
                xg_s[s] = jnp.dot(pick.astype(BF16), h_ref[toks, :], preferred_element_type=F32).astype(BF16)
        for e in range(MOE_EG):
            @pl.when(exp_top[e] >= base)
            def _(e=e):
                expert_rows(e, lambda s: pick_s[s].astype(F32))
        for s in range(MOE_NSUB):
            @pl.when(sub_top[s] >= base)
            def _(s=s):
                scatter(s, back_s[s])
        return carry

    lax.fori_loop(1, n_rounds, overflow_round, 0)

    @pl.when(g_idx == N_EXPERTS // MOE_EG - 1)
    def _():
        h = h_ref[...]
        s = jnp.dot(h, ws1[...], preferred_element_type=F32)
        shid = s * _sigmoid(s) * jnp.dot(h, ws3[...], preferred_element_type=F32)
        y_all = acc[...] + jnp.dot(shid.astype(BF16), ws2[...], preferred_element_type=F32)
        o_ref[...] = x_ref[...] + gt_ref[0] * _rms(y_all, gpost_ref[...])


def _moe(h, gates_t, ranks_t, x, gt2, g_post, w1, w3, w2, ws1, ws3, ws2, layer):
    tpb = SEQ // MOE_TM
    n_groups = N_EXPERTS // MOE_EG
    whole = lambda a: pl.BlockSpec(a.shape, lambda i, g: (0, 0))
    per_group = pl.BlockSpec((1, MOE_EG, MOE_TM), lambda i, g: (g, 0, i))
    return pl.pallas_call(
        _moe_kernel,
        grid=(N_TOK // MOE_TM, n_groups),
        in_specs=[pl.BlockSpec((MOE_TM, D_MODEL), lambda i, g: (i, 0)),
                  per_group, per_group,
                  pl.BlockSpec((MOE_TM, D_MODEL), lambda i, g: (i, 0)),
                  pl.BlockSpec((1, 1, D_MODEL), lambda i, g: (i // tpb, 0, 0)),
                  pl.BlockSpec((1, D_MODEL), lambda i, g: (0, 0)),
                  pl.BlockSpec((None, MOE_EG, D_MODEL, EXPERT_DIM), lambda i, g: (layer, g, 0, 0)),
                  pl.BlockSpec((None, MOE_EG, D_MODEL, EXPERT_DIM), lambda i, g: (layer, g, 0, 0)),
                  pl.BlockSpec((None, MOE_EG, EXPERT_DIM, D_MODEL), lambda i, g: (layer, g, 0, 0)),
                  whole(ws1), whole(ws3), whole(ws2)],
        out_specs=pl.BlockSpec((MOE_TM, D_MODEL), lambda i, g: (i, 0)),
        out_shape=jax.ShapeDtypeStruct((N_TOK, D_MODEL), F32),
        scratch_shapes=[pltpu.VMEM((MOE_TM, D_MODEL), F32),
                        pltpu.VMEM((MOE_NSUB, MOE_SLAB, D_MODEL), BF16),
                        pltpu.VMEM((MOE_NSUB, MOE_SLAB, D_MODEL), BF16),
                        pltpu.VMEM((MOE_NSUB, MOE_SLAB, MOE_SUB), BF16),
                        pltpu.VMEM((MOE_NSUB, MOE_SUB, MOE_SLAB), BF16)],
        compiler_params=_params(("arbitrary", "arbitrary")),
        name="moe_ffn",
    )(h, gates_t.reshape(n_groups, MOE_EG, N_TOK), ranks_t.reshape(n_groups, MOE_EG, N_TOK),
      x, gt2, g_post.reshape(1, D_MODEL), w1, w3, w2, ws1, ws3, ws2)


_IN_SIZES = (512, 256, 256, 64, 4, 512, 512, 512, 512, 512, 1024, 1024, 3072)
_IN_DEST = (COL_QA, COL_CKV, COL_IQ, COL_IKIW, COL_IKIW + 64, COL_QB, COL_KB, COL_VB, COL_QC, COL_KC,
            COL_VC, COL_GC, COL_GATES)
_IN_SRC = tuple(int(v) for v in np.cumsum((0,) + _IN_SIZES[:-1]))
N_IN = sum(_IN_SIZES)
PACK_TK = 128


def _pack_kernel(w_ref, o_ref):
    for layer in range(DEPTH):
        for src, width, dst in zip(_IN_SRC, _IN_SIZES, _IN_DEST):
            o_ref[layer, dst:dst + width, :] = w_ref[src:src + width, layer, :].astype(BF16)
        o_ref[layer, COL_IKIW + 68:COL_IKIW + LANES, :] = jnp.zeros((LANES - 68, PACK_TK), BF16)


def _reorder_w_in(w_t):
    return pl.pallas_call(
        _pack_kernel,
        grid=(D_MODEL // PACK_TK,),
        in_specs=[pl.BlockSpec((N_IN, DEPTH, PACK_TK), lambda i: (0, 0, i))],
        out_specs=pl.BlockSpec((DEPTH, N_PROJ, PACK_TK), lambda i: (0, 0, i)),
        out_shape=jax.ShapeDtypeStruct((DEPTH, N_PROJ, D_MODEL), BF16),
        compiler_params=_params(("arbitrary",)),
        name="pack_w_in",
    )(w_t)


def _rope_tables(positions):
    half = HEAD_DIM // 2
    inv_freq = ROPE_THETA ** (-jnp.arange(half, dtype=F32) / half)
    ang = positions.astype(F32)[:, :, None] * inv_freq
    cos, sin = jnp.cos(ang), jnp.sin(ang)
    cos64 = jnp.concatenate([cos, cos], axis=-1)
    sin64 = jnp.concatenate([-sin, sin], axis=-1)
    cos_a = jnp.concatenate([cos64, cos64], axis=-1)
    sin_a = jnp.concatenate([sin64, sin64], axis=-1)
    return cos_a, sin_a


def _decay_tables():
    log_g = jnp.log(1.0 - 2.0 ** (-5.0 - jnp.arange(C_HEADS, dtype=F32)))
    pos = jnp.arange(RET_BLK, dtype=F32)
    diff = pos[:, None] - pos[None, :]
    chunk = jnp.arange(RET_BLK) // CHUNK
    visible = chunk[None, :] <= chunk[:, None]
    e_diag = jnp.where(visible, jnp.exp(log_g[:, None, None] * jnp.abs(diff)), 0.0)
    dec = jnp.exp(log_g * RET_BLK)
    per_lane = jnp.repeat(log_g, HEAD_DIM)[None, :]
    q_scale = jnp.exp(per_lane * pos[:, None])
    k_scale = jnp.exp(per_lane * (RET_BLK - pos)[:, None])
    return dec, q_scale, k_scale, e_diag


def kernel(x, c, positions, ada_w, ada_b, g_pre_mix, g_post_mix, g_pre_ffn, g_post_ffn, w_in, g_kv, w_uk, w_uv,
           g_ret, w_o_a, w_o_b, w_o_c, w_out, w_router, router_bias, w1, w3, w2, ws1, ws3, ws2):
    cos_a, sin_a = _rope_tables(positions)
    dec, q_scale, k_scale, e_diag = _decay_tables()
    xf = x.reshape(N_TOK, D_MODEL)
    w_in_packed = _reorder_w_in(jnp.transpose(w_in, (2, 0, 1)))
    w1_b, w3_b, w2_b = w1.astype(BF16), w3.astype(BF16), w2.astype(BF16)
    for l in range(DEPTH):
        mod = _ada(c, ada_w, ada_b, l).reshape(BATCH, 1, N_ADA * D_MODEL)
        sh1, sc1, gt1, sh2, sc2, gt2 = (mod[:, :, k * D_MODEL:(k + 1) * D_MODEL] for k in range(N_ADA))
        z = _inproj(xf, g_pre_mix[l], sc1, sh1, w_in_packed, l)
        z3 = z.reshape(BATCH, SEQ, N_PROJ)
        w_ukv = jnp.concatenate([w_uk[l], w_uv[l]], axis=1).astype(BF16)
        ya = _dsa(z3, cos_a, sin_a, g_kv[l], w_ukv).reshape(N_TOK, -1)
        yb = _sb(z3).reshape(N_TOK, -1)
        yc = _ret(z3, cos_a, sin_a, dec, q_scale, k_scale, e_diag, g_ret[l]).reshape(N_TOK, -1)
        xf, h, gates, ranks = _merge(ya, yb, yc, z, xf, gt1, g_post_mix[l], g_pre_ffn[l], sc2, sh2,
                              w_o_a[l].astype(BF16), w_o_b[l].astype(BF16), w_o_c[l].astype(BF16),
                              w_out[l].astype(BF16), w_router[l].T.astype(BF16),
                              router_bias[l].reshape(N_EXPERTS, 1))
        xf = _moe(h, gates, ranks, xf, gt2, g_post_ffn[l], w1_b, w3_b, w2_b,
                  ws1[l].astype(BF16), ws3[l].astype(BF16), ws2[l].astype(BF16), l)
    return xf.reshape(BATCH, SEQ, D_MODEL)
```

```python
import math

import numpy as np
import jax
import jax.numpy as jnp
from jax import lax
from jax.experimental import pallas as pl
from jax.experimental.pallas import tpu as pltpu

F32 = jnp.float32
BF16 = jnp.bfloat16
I32 = jnp.int32

D_MODEL = 1024
BATCH = 8
SEQ = 2048
DEPTH = 2
N_TOK = BATCH * SEQ

CHUNK = 64
HEAD_DIM = 64
ROPE_THETA = 10000.0
EPS = 1e-6
A_HEADS = 8
A_KV_LORA = 256
IDX_HEADS = 4
IDX_DIM = 64
TOPK = 256
B_HEADS = 8
C_HEADS = 8
C_V_DIM = 128
N_EXPERTS = 64
TOP_K = 8
N_GROUPS = 8
TOPK_GROUPS = 4
EXPERT_DIM = 256
SHARED_DIM = 256
ROUTED_SCALE = 2.5
N_ADA = 6

LANES = 128
VMEM_LIMIT = 56 * 1024 * 1024

COL_GATES = 0
COL_VC = 3072
COL_GC = 4096
COL_QA = 5120
COL_QB = 5632
COL_KB = 6144
COL_VB = 6656
COL_QC = 7168
COL_KC = 7680
COL_CKV = 8192
COL_IQ = 8448
COL_IKIW = 8704
N_PROJ = 8832
PROJ_TN = 2944
PROJ_TM = 1024

INT_MIN = -2147483648
NEG_INF_KEY = -2139095041

NT_DIMS = (((1,), (1,)), ((), ()))
LOG2_E = math.log2(math.e)


def _params(sem):
    return pltpu.CompilerParams(dimension_semantics=sem, vmem_limit_bytes=VMEM_LIMIT)


def _sigmoid(x):
    return 1.0 / (1.0 + jnp.exp(-x))


def _rope_lanes(x, cos, sin_signed):
    n = x.shape[-1]
    lane = lax.broadcasted_iota(I32, (1, n), 1)
    first = (lane & 32) == 0
    ahead = pltpu.roll(x, n - 32, 1)
    behind = pltpu.roll(x, 32, 1)
    return x * cos + jnp.where(first, ahead, behind) * sin_signed


def _ada_kernel(c_ref, w_ref, b_ref, o_ref):
    c = c_ref[...]
    s = (c * _sigmoid(c)).astype(BF16)
    o_ref[...] = jnp.dot(s, w_ref[...].astype(BF16), preferred_element_type=F32) + b_ref[...]


def _ada(c, w, b, layer):
    n = w.shape[2]
    return pl.pallas_call(
        _ada_kernel,
        grid=(n // D_MODEL,),
        in_specs=[pl.BlockSpec((BATCH, D_MODEL), lambda j: (0, 0)),
                  pl.BlockSpec((None, D_MODEL, D_MODEL), lambda j: (layer, 0, j)),
                  pl.BlockSpec((None, 1, D_MODEL), lambda j: (layer, 0, j))],
        out_specs=pl.BlockSpec((BATCH, D_MODEL), lambda j: (0, j)),
        out_shape=jax.ShapeDtypeStruct((BATCH, n), F32),
        compiler_params=_params(("arbitrary",)),
        name="ada_mod",
    )(c, w, b.reshape(DEPTH, 1, n))


def _inproj_kernel(x_ref, g_ref, sc_ref, sh_ref, w_ref, o_ref, h_ref):
    @pl.when(pl.program_id(1) == 0)
    def _():
        x = x_ref[...]
        ms = jnp.mean(x * x, axis=-1, keepdims=True)
        y = x * lax.rsqrt(ms + EPS) * g_ref[...]
        h_ref[...] = (y * (1.0 + sc_ref[0]) + sh_ref[0]).astype(BF16)

    o_ref[...] = lax.dot_general(h_ref[...], w_ref[...], NT_DIMS, preferred_element_type=F32)


def _inproj(x, g, sc, sh, w_t_bf16, layer):
    tiles_per_batch = SEQ // PROJ_TM
    return pl.pallas_call(
        _inproj_kernel,
        grid=(N_TOK // PROJ_TM, N_PROJ // PROJ_TN),
        in_specs=[pl.BlockSpec((PROJ_TM, D_MODEL), lambda i, j: (i, 0)),
                  pl.BlockSpec((1, D_MODEL), lambda i, j: (0, 0)),
                  pl.BlockSpec((1, 1, D_MODEL), lambda i, j: (i // tiles_per_batch, 0, 0)),
                  pl.BlockSpec((1, 1, D_MODEL), lambda i, j: (i // tiles_per_batch, 0, 0)),
                  pl.BlockSpec((None, PROJ_TN, D_MODEL), lambda i, j: (layer, j, 0))],
        out_specs=pl.BlockSpec((PROJ_TM, PROJ_TN), lambda i, j: (i, j)),
        out_shape=jax.ShapeDtypeStruct((N_TOK, N_PROJ), F32),
        scratch_shapes=[pltpu.VMEM((PROJ_TM, D_MODEL), BF16)],
        compiler_params=_params(("arbitrary", "arbitrary")),
        name="in_proj",
    )(x, g.reshape(1, D_MODEL), sc, sh, w_t_bf16)


DSA_QB = 256
DSA_KC = 256


def _dsa_kernel(ckv_ref, ikiw_ref, q_ref, iq_ref, cos_a, sin_a, gkv_ref, wukv_ref,
                o_ref, k_dup, ik_dup, v_one_t, keys_s, bias_s, acc_s, q_s):
    qi = pl.program_id(1)
    lane = lax.broadcasted_iota(I32, (1, LANES), 1)
    lo_half = lane < 64
    n_kc = SEQ // DSA_KC
    masked = -1e30

    @pl.when(qi == 0)
    def _prologue():
        def chunk(c, carry):
            r0 = pl.multiple_of(c * DSA_KC, DSA_KC)
            rows = pl.ds(r0, DSA_KC)
            ckv = ckv_ref[0, rows, :]
            ms = jnp.mean(ckv * ckv, axis=-1, keepdims=True)
            cn = (ckv * lax.rsqrt(ms + EPS) * gkv_ref[...]).astype(BF16)
            kv = jnp.dot(cn, wukv_ref[...], preferred_element_type=F32)
            cb = jnp.where(lo_half, cos_a[0, rows, :], 1.0)
            sb = jnp.where(lo_half, sin_a[0, rows, :], 0.0)
            kv = _rope_lanes(kv, cb, sb)
            swapped = pltpu.roll(kv, 64, 1)
            k_dup[rows, :] = jnp.where(lo_half, kv, swapped).astype(BF16)
            v_one_t[c] = jnp.where(lo_half, swapped, 1.0).T.astype(BF16)
            ik = jnp.where(lo_half, _rope_lanes(ikiw_ref[0, rows, :], cb, sb), 0.0)
            ik_dup[rows, :] = (ik + pltpu.roll(ik, 64, 1)).astype(BF16)
            return carry
        lax.fori_loop(0, n_kc, chunk, 0)

    q0 = pl.multiple_of(qi * DSA_QB, DSA_QB)
    qrows = pl.ds(q0, DSA_QB)
    cq, sq = cos_a[0, qrows, :], sin_a[0, qrows, :]

    iq = iq_ref[0]
    iq_heads = []
    for p in range(IDX_HEADS // 2):
        pair = _rope_lanes(iq[:, p * LANES:(p + 1) * LANES], cq, sq) * (IDX_DIM ** -0.5)
        iq_heads.append(jnp.where(lo_half, pair, 0.0).astype(BF16))
        iq_heads.append(jnp.where(lo_half, 0.0, pair).astype(BF16))
    iq_stacked = jnp.concatenate(iq_heads, axis=0)
    iw_t = ikiw_ref[0, qrows, :].T * (IDX_HEADS ** -0.5)
    q_chunk = (q0 + lax.broadcasted_iota(I32, (1, DSA_QB), 1)) >> 6
    n_live = lax.shift_right_logical((qi + 1) * DSA_QB + (DSA_KC - 1), DSA_KC.bit_length() - 1)

    def chunk_rows(c):
        return pl.ds(pl.multiple_of(c * DSA_KC, DSA_KC), DSA_KC)

    def score_chunk(c, carry):
        rows = chunk_rows(c)
        k_chunk = (c * DSA_KC + lax.broadcasted_iota(I32, (DSA_KC, 1), 0)) >> 6
        score = jnp.zeros((DSA_KC, DSA_QB), F32)
        logits = lax.dot_general(ik_dup[rows, :], iq_stacked, NT_DIMS, preferred_element_type=F32)
        for h in range(IDX_HEADS):
            score = score + iw_t[64 + h:65 + h, :] * jnp.maximum(logits[:, h * DSA_QB:(h + 1) * DSA_QB], 0.0)
        score = jnp.where(score == 0.0, 0.0, score)
        score = jnp.where(k_chunk <= q_chunk, score, -jnp.inf)
        bits = lax.bitcast_convert_type(score, I32)
        keys_s[rows, :] = bits ^ ((bits >> 31) & 0x7FFFFFFF)
        return carry
    lax.fori_loop(0, n_live, score_chunk, 0)

    def count(preds):
        def chunk(c, accs):
            key = keys_s[chunk_rows(c), :]
            return tuple(acc + jnp.sum(pred(key).astype(I32).reshape(4, DSA_KC // 4, DSA_QB), axis=0)
                         for acc, pred in zip(accs, preds))
        zero = jnp.zeros((DSA_KC // 4, DSA_QB), I32)
        accs = lax.fori_loop(0, n_live, chunk, (zero,) * len(preds))
        return [jnp.sum(acc, axis=0, keepdims=True) for acc in accs]

    def descend(it, prefix):
        cand = prefix | lax.shift_left(jnp.int32(1), 31 - it)
        bound = cand ^ INT_MIN
        return jnp.where(count([lambda k: k >= bound])[0] >= TOPK, cand, prefix)
    thr = lax.fori_loop(0, 32, descend, jnp.zeros((1, DSA_QB), I32)) ^ INT_MIN

    n_gt, n_eq = count([lambda k: k > thr, lambda k: k == thr])
    need = (TOPK - n_gt).astype(F32)
    tie_break = jnp.max(n_gt + n_eq) > TOPK

    @pl.when(tie_break)
    def _():
        tri = (lax.broadcasted_iota(I32, (DSA_KC, DSA_KC), 0)
               >= lax.broadcasted_iota(I32, (DSA_KC, DSA_KC), 1)).astype(BF16)

        def select_chunk(c, seen):
            key = keys_s[chunk_rows(c), :]
            eq = key == thr
            rank = jnp.dot(tri, eq.astype(BF16), preferred_element_type=F32) + seen
            sel = ((key > thr) | (eq & (rank <= need))) & (key > NEG_INF_KEY)
            bias_s[c] = jnp.where(sel, 0.0, masked)
            return rank[DSA_KC - 1:DSA_KC, :]
        lax.fori_loop(0, n_live, select_chunk, jnp.zeros((1, DSA_QB), F32))

    @pl.when(jnp.logical_not(tie_break))
    def _():
        def select_chunk(c, carry):
            key = keys_s[chunk_rows(c), :]
            bias_s[c] = jnp.where((key >= thr) & (key > NEG_INF_KEY), 0.0, masked)
            return carry
        lax.fori_loop(0, n_live, select_chunk, 0)

    q = q_ref[0]
    n_pairs = A_HEADS // 2
    for p in range(n_pairs):
        pair = _rope_lanes(q[:, p * LANES:(p + 1) * LANES], cq, sq) * (HEAD_DIM ** -0.5 * LOG2_E)
        q_s[p * DSA_QB:(p + 1) * DSA_QB, :] = jnp.where(lo_half, pair, 0.0).astype(BF16)
        q_s[(n_pairs + p) * DSA_QB:(n_pairs + p + 1) * DSA_QB, :] = jnp.where(lo_half, 0.0, pair).astype(BF16)

    def attend(c, m):
        bias = jnp.concatenate([bias_s[c]] * A_HEADS, axis=1)
        s = lax.dot_general(k_dup[chunk_rows(c), :], q_s[...], NT_DIMS,
                            preferred_element_type=F32) + bias
        m_new = jnp.maximum(m, jnp.max(s, axis=0, keepdims=True))
        pe = jnp.exp2(s - m_new).astype(BF16)
        pv = jnp.dot(v_one_t[c], pe, preferred_element_type=F32)
        acc_s[...] = jnp.exp2(m - m_new) * acc_s[...] + pv
        return m_new

    acc_s[...] = jnp.zeros_like(acc_s)
    lax.fori_loop(0, n_live, attend, jnp.full((1, A_HEADS * DSA_QB), masked, F32))
    for p in range(n_pairs):
        both = []
        for slab in (p, n_pairs + p):
            a = acc_s[:, slab * DSA_QB:(slab + 1) * DSA_QB]
            both.append(a[:HEAD_DIM, :] / a[HEAD_DIM:HEAD_DIM + 1, :])
        o_ref[0, :, p * LANES:(p + 1) * LANES] = jnp.concatenate(both, axis=0).T


def _dsa(z3, cos_a, sin_a, g_kv, w_ukv_bf16):
    full = lambda width, col: pl.BlockSpec((1, SEQ, width), lambda b, i: (b, 0, col // width))
    blk = lambda width, col: pl.BlockSpec((1, DSA_QB, width), lambda b, i: (b, i, col // width))
    tab = pl.BlockSpec((1, SEQ, LANES), lambda b, i: (b, 0, 0))
    return pl.pallas_call(
        _dsa_kernel,
        grid=(BATCH, SEQ // DSA_QB),
        in_specs=[full(A_KV_LORA, COL_CKV), full(LANES, COL_IKIW),
                  blk(A_HEADS * HEAD_DIM, COL_QA), blk(IDX_HEADS * IDX_DIM, COL_IQ),
                  tab, tab,
                  pl.BlockSpec((1, A_KV_LORA), lambda b, i: (0, 0)),
                  pl.BlockSpec((A_KV_LORA, LANES), lambda b, i: (0, 0))],
        out_specs=pl.BlockSpec((1, DSA_QB, A_HEADS * HEAD_DIM), lambda b, i: (b, i, 0)),
        out_shape=jax.ShapeDtypeStruct((BATCH, SEQ, A_HEADS * HEAD_DIM), F32),
        scratch_shapes=[pltpu.VMEM((SEQ, LANES), BF16)] * 2
                       + [pltpu.VMEM((SEQ // DSA_KC, LANES, DSA_KC), BF16),
                          pltpu.VMEM((SEQ, DSA_QB), I32),
                          pltpu.VMEM((SEQ // DSA_KC, DSA_KC, DSA_QB), F32),
                          pltpu.VMEM((LANES, A_HEADS * DSA_QB), F32),
                          pltpu.VMEM((A_HEADS * DSA_QB, LANES), BF16)],
        compiler_params=_params(("arbitrary", "arbitrary")),
        name="dsa_mixer",
    )(z3, z3, z3, z3, cos_a, sin_a, g_kv.reshape(1, A_KV_LORA), w_ukv_bf16)


SB_BLK = 128
SB_EXP_UNDERFLOW = -105.0


def _sb_kernel(q_ref, k_ref, v_ref, o_ref, q_s, tail_s, acc_s):
    qi = pl.program_id(1)
    n_pairs = B_HEADS // 2
    rows_all = B_HEADS * SB_BLK
    lane = lax.broadcasted_iota(I32, (1, LANES), 1)
    lo_half = lane < 64
    causal = (lax.broadcasted_iota(I32, (rows_all, SB_BLK), 1)
              < (lax.broadcasted_iota(I32, (rows_all, SB_BLK), 0) & (SB_BLK - 1)))
    r2 = lax.broadcasted_iota(I32, (2 * SB_BLK, 2 * SB_BLK), 0) & (SB_BLK - 1)
    c2 = lax.broadcasted_iota(I32, (2 * SB_BLK, 2 * SB_BLK), 1)
    suffix_mat = ((c2 >= SB_BLK) | (r2 > c2)).astype(BF16)

    for p in range(n_pairs):
        q2 = q_ref[0, :, p * LANES:(p + 1) * LANES] * (HEAD_DIM ** -0.5)
        q_s[2 * p * SB_BLK:(2 * p + 1) * SB_BLK, :] = jnp.where(lo_half, q2, 0.0).astype(BF16)
        q_s[(2 * p + 1) * SB_BLK:(2 * p + 2) * SB_BLK, :] = jnp.where(lo_half, 0.0, q2).astype(BF16)

    def key_block(kb, diagonal):
        r0 = pl.multiple_of(kb * SB_BLK, SB_BLK)
        pair_rows = lambda p: slice(2 * p * SB_BLK, (2 * p + 2) * SB_BLK)
        z = jnp.concatenate(
            [lax.dot_general(q_s[pair_rows(p), :], k_ref[0, pl.ds(r0, SB_BLK), p * LANES:(p + 1) * LANES].astype(BF16),
                             NT_DIMS, preferred_element_type=F32) for p in range(n_pairs)], axis=0)
        log_beta = jnp.minimum(z, 0.0) - jnp.log(1.0 + jnp.exp(-jnp.abs(z)))
        log_not = log_beta - z
        if diagonal:
            log_not = jnp.where(causal, log_not, 0.0)
        hi = log_not.astype(BF16)
        lo = (log_not - hi.astype(F32)).astype(BF16)
        hi_lo = jnp.concatenate([hi, lo], axis=1)
        t = jnp.concatenate([jnp.dot(hi_lo[pair_rows(p), :], suffix_mat, preferred_element_type=F32)
                             for p in range(n_pairs)], axis=0)
        if diagonal:
            a = jnp.where(causal, jnp.exp(log_beta + t[:, :SB_BLK]), 0.0)
            tail = t[:, SB_BLK:]
        else:
            a = jnp.exp(log_beta + t[:, :SB_BLK] + tail_s[...])
            tail = tail_s[...] + t[:, SB_BLK:]
        tail_s[...] = tail
        a = a.astype(BF16)
        for p in range(n_pairs):
            v2 = v_ref[0, pl.ds(r0, SB_BLK), p * LANES:(p + 1) * LANES].astype(BF16)
            pv = jnp.dot(a[pair_rows(p), :], v2, preferred_element_type=F32)
            if diagonal:
                acc_s[pair_rows(p), :] = pv
            else:
                acc_s[pair_rows(p), :] += pv
        return jnp.max(tail)

    first = key_block(qi, True)

    def more(state):
        jj, worst = state
        return (jj <= qi) & (worst > SB_EXP_UNDERFLOW)

    def step(state):
        jj, _ = state
        return jj + 1, key_block(qi - jj, False)

    lax.while_loop(more, step, (jnp.int32(1), first))
    for p in range(n_pairs):
        even = acc_s[2 * p * SB_BLK:(2 * p + 1) * SB_BLK, :]
        odd = acc_s[(2 * p + 1) * SB_BLK:(2 * p + 2) * SB_BLK, :]
        o_ref[0, :, p * LANES:(p + 1) * LANES] = jnp.where(lo_half, even, odd)


def _sb(z3):
    width = B_HEADS * HEAD_DIM
    return pl.pallas_call(
        _sb_kernel,
        grid=(BATCH, SEQ // SB_BLK),
        in_specs=[pl.BlockSpec((1, SB_BLK, width), lambda b, i: (b, i, COL_QB // width)),
                  pl.BlockSpec((1, SEQ, width), lambda b, i: (b, 0, COL_KB // width)),
                  pl.BlockSpec((1, SEQ, width), lambda b, i: (b, 0, COL_VB // width))],
        out_specs=pl.BlockSpec((1, SB_BLK, width), lambda b, i: (b, i, 0)),
        out_shape=jax.ShapeDtypeStruct((BATCH, SEQ, width), F32),
        scratch_shapes=[pltpu.VMEM((B_HEADS * SB_BLK, LANES), BF16),
                        pltpu.VMEM((B_HEADS * SB_BLK, SB_BLK), F32),
                        pltpu.VMEM((B_HEADS * SB_BLK, SB_BLK), F32)],
        compiler_params=_params(("arbitrary", "arbitrary")),
        name="stickbreak_mixer",
    )(z3, z3, z3)


RET_BLK = 256


def _ret_kernel(dec_ref, q_ref, k_ref, v_ref, gate_ref, cos_a, sin_a, qs_ref, ks_ref, e_diag, gret_ref,
                o_ref, state):
    lane = lax.broadcasted_iota(I32, (1, LANES), 1)
    lo_half = lane < 64
    row_lo = lax.broadcasted_iota(I32, (LANES, 1), 0) < 64

    @pl.when(pl.program_id(1) == 0)
    def _():
        state[...] = jnp.zeros_like(state)

    cq, sq = cos_a[0], sin_a[0]
    for p in range(C_HEADS // 2):
        cols = slice(p * LANES, (p + 1) * LANES)
        q2 = _rope_lanes(q_ref[0, :, cols], cq, sq)
        k2 = _rope_lanes(k_ref[0, :, cols], cq, sq) * (HEAD_DIM ** -0.5)
        q_hat = q2 * qs_ref[:, cols]
        k_hat_t = (k2 * ks_ref[:, cols]).T
        for e in range(2):
            h = 2 * p + e
            pick = (lambda x: jnp.where(lo_half, x, 0.0)) if e == 0 else (lambda x: jnp.where(lo_half, 0.0, x))
            vcols = slice(h * C_V_DIM, (h + 1) * C_V_DIM)
            v2 = v_ref[0, :, vcols].astype(BF16)
            s = lax.dot_general(pick(q2).astype(BF16), k2.astype(BF16), NT_DIMS,
                                preferred_element_type=F32) * e_diag[h]
            o = jnp.dot(s.astype(BF16), v2, preferred_element_type=F32)
            o = o + jnp.dot(pick(q_hat).astype(BF16), state[h].astype(BF16), preferred_element_type=F32)
            kv = jnp.dot(jnp.where(row_lo if e == 0 else ~row_lo, k_hat_t, 0.0).astype(BF16), v2,
                         preferred_element_type=F32)
            state[h] = dec_ref[h] * state[h] + kv
            mu = jnp.mean(o, axis=-1, keepdims=True)
            d = o - mu
            var = jnp.mean(d * d, axis=-1, keepdims=True)
            y = d * lax.rsqrt(var + EPS) * gret_ref[:, vcols]
            g = gate_ref[0, :, vcols]
            o_ref[0, :, vcols] = g * _sigmoid(g) * y


def _ret(z3, cos_a, sin_a, dec, q_scale, k_scale, e_diag, g_ret):
    qk_w = C_HEADS * HEAD_DIM
    v_w = C_HEADS * C_V_DIM
    tab = pl.BlockSpec((1, RET_BLK, LANES), lambda b, i: (b, i, 0))
    const2 = lambda a: pl.BlockSpec(a.shape, lambda b, i: (0, 0))
    return pl.pallas_call(
        _ret_kernel,
        grid=(BATCH, SEQ // RET_BLK),
        in_specs=[pl.BlockSpec(memory_space=pltpu.SMEM),
                  pl.BlockSpec((1, RET_BLK, qk_w), lambda b, i: (b, i, COL_QC // qk_w)),
                  pl.BlockSpec((1, RET_BLK, qk_w), lambda b, i: (b, i, COL_KC // qk_w)),
                  pl.BlockSpec((1, RET_BLK, v_w), lambda b, i: (b, i, COL_VC // v_w)),
                  pl.BlockSpec((1, RET_BLK, v_w), lambda b, i: (b, i, COL_GC // v_w)),
                  tab, tab, const2(q_scale), const2(k_scale),
                  pl.BlockSpec(e_diag.shape, lambda b, i: (0, 0, 0)),
                  pl.BlockSpec((1, v_w), lambda b, i: (0, 0))],
        out_specs=pl.BlockSpec((1, RET_BLK, v_w), lambda b, i: (b, i, 0)),
        out_shape=jax.ShapeDtypeStruct((BATCH, SEQ, v_w), F32),
        scratch_shapes=[pltpu.VMEM((C_HEADS, LANES, C_V_DIM), F32)],
        compiler_params=_params(("arbitrary", "arbitrary")),
        name="retention_mixer",
    )(dec, z3, z3, z3, z3, cos_a, sin_a, q_scale, k_scale, e_diag, g_ret.reshape(1, v_w))


MERGE_TM = 512
MOE_SUB = 256


def _rms(y, g):
    ms = jnp.mean(y * y, axis=-1, keepdims=True)
    return y * lax.rsqrt(ms + EPS) * g


def _route(logits_t, bias_col):
    n_tok = logits_t.shape[1]
    per_group = N_EXPERTS // N_GROUPS
    scores = _sigmoid(logits_t)
    sel = scores + bias_col
    sel3 = sel.reshape(N_GROUPS, per_group, n_tok)
    member = lax.broadcasted_iota(I32, (N_GROUPS, per_group, n_tok), 1)
    m1 = jnp.max(sel3, axis=1, keepdims=True)
    first = jnp.min(jnp.where(sel3 == m1, member, per_group), axis=1, keepdims=True)
    m2 = jnp.max(jnp.where(member == first, -jnp.inf, sel3), axis=1, keepdims=True)
    grp = (m1 + m2).reshape(N_GROUPS, n_tok)
    gid = lax.broadcasted_iota(I32, (N_GROUPS, n_tok), 0)
    rank = jnp.zeros((N_GROUPS, n_tok), I32)
    for g in range(N_GROUPS):
        other = grp[g:g + 1, :]
        rank = rank + ((other > grp) | ((other == grp) & (g < gid))).astype(I32)
    keep = jnp.where(rank < TOPK_GROUPS, 1.0, 0.0).reshape(N_GROUPS, 1, n_tok)
    live = jnp.where(keep > 0.0, sel3, -jnp.inf).reshape(N_EXPERTS, n_tok)
    eid = lax.broadcasted_iota(I32, (N_EXPERTS, n_tok), 0)
    chosen = jnp.zeros((N_EXPERTS, n_tok), jnp.bool_)
    for _ in range(TOP_K):
        m = jnp.max(live, axis=0, keepdims=True)
        pick = eid == jnp.min(jnp.where(live == m, eid, N_EXPERTS), axis=0, keepdims=True)
        chosen = chosen | pick
        live = jnp.where(pick, -jnp.inf, live)
    w = jnp.where(chosen, scores, 0.0)
    return w / jnp.sum(w, axis=0, keepdims=True) * ROUTED_SCALE, chosen


def _merge_kernel(ya_ref, yb_ref, yc_ref, g_ref, x_ref, gt_ref, gpost_ref, gpre_ref, sc_ref, sh_ref,
                  woa, wob, woc, wout, wr_t, rb_ref, xo_ref, h_ref, gate_ref, rank_ref):
    ya = jnp.dot(ya_ref[...].astype(BF16), woa[...], preferred_element_type=F32)
    yb = jnp.dot(yb_ref[...].astype(BF16), wob[...], preferred_element_type=F32)
    yc = jnp.dot(yc_ref[...].astype(BF16), woc[...], preferred_element_type=F32)
    merged = (_sigmoid(g_ref[:, 0:D_MODEL]) * ya
              + _sigmoid(g_ref[:, D_MODEL:2 * D_MODEL]) * yb
              + _sigmoid(g_ref[:, 2 * D_MODEL:3 * D_MODEL]) * yc)
    y = jnp.dot(merged.astype(BF16), wout[...], preferred_element_type=F32)
    x = x_ref[...] + gt_ref[0] * _rms(y, gpost_ref[...])
    xo_ref[...] = x
    h = (_rms(x, gpre_ref[...]) * (1.0 + sc_ref[0]) + sh_ref[0]).astype(BF16)
    h_ref[...] = h
    logits_t = lax.dot_general(wr_t[...], h, NT_DIMS, preferred_element_type=F32)
    gates_t, chosen = _route(logits_t, rb_ref[...])
    gate_ref[...] = gates_t
    earlier = (lax.broadcasted_iota(I32, (MOE_SUB, MOE_SUB), 0)
               < lax.broadcasted_iota(I32, (MOE_SUB, MOE_SUB), 1)).astype(BF16)
    routed = jnp.where(chosen, 1.0, 0.0).astype(BF16)
    before = jnp.concatenate(
        [jnp.dot(routed[:, s * MOE_SUB:(s + 1) * MOE_SUB], earlier, preferred_element_type=F32)
         for s in range(MERGE_TM // MOE_SUB)], axis=1)
    rank_ref[...] = jnp.where(chosen, before, -1.0)


def _merge(ya, yb, yc, z, x, gt1, g_post, g_pre, sc2, sh2, woa, wob, woc, wout, wr_t, rbias):
    tpb = SEQ // MERGE_TM
    row = lambda w: pl.BlockSpec((MERGE_TM, w), lambda i: (i, 0))
    vec = pl.BlockSpec((1, D_MODEL), lambda i: (0, 0))
    mod = pl.BlockSpec((1, 1, D_MODEL), lambda i: (i // tpb, 0, 0))
    whole = lambda a: pl.BlockSpec(a.shape, lambda i: (0, 0))
    return pl.pallas_call(
        _merge_kernel,
        grid=(N_TOK // MERGE_TM,),
        in_specs=[row(512), row(512), row(1024), pl.BlockSpec((MERGE_TM, 3 * D_MODEL), lambda i: (i, 0)),
                  row(D_MODEL), mod, vec, vec, mod, mod,
                  whole(woa), whole(wob), whole(woc), whole(wout), whole(wr_t), whole(rbias)],
        out_specs=[row(D_MODEL), row(D_MODEL),
                   pl.BlockSpec((N_EXPERTS, MERGE_TM), lambda i: (0, i)),
                   pl.BlockSpec((N_EXPERTS, MERGE_TM), lambda i: (0, i))],
        out_shape=[jax.ShapeDtypeStruct((N_TOK, D_MODEL), F32),
                   jax.ShapeDtypeStruct((N_TOK, D_MODEL), BF16),
                   jax.ShapeDtypeStruct((N_EXPERTS, N_TOK), F32),
                   jax.ShapeDtypeStruct((N_EXPERTS, N_TOK), F32)],
        compiler_params=_params(("arbitrary",)),
        name="merge_route",
    )(ya, yb, yc, z, x, gt1, g_post.reshape(1, D_MODEL), g_pre.reshape(1, D_MODEL), sc2, sh2,
      woa, wob, woc, wout, wr_t, rbias)


MOE_TM = 1024
MOE_EG = 4
MOE_CAP = 64
MOE_NSUB = MOE_TM // MOE_SUB
MOE_SLAB = MOE_EG * MOE_CAP


def _moe_kernel(h_ref, gate_ref, rank_ref, x_ref, gt_ref, gpost_ref, w1, w3, w2, ws1, ws3, ws2,
                o_ref, acc, xg_s, yw_s, pick_s, back_s):
    g_idx = pl.program_id(1)

    @pl.when(g_idx == 0)
    def _():
        acc[...] = jnp.zeros_like(acc)

    rank = rank_ref[0]
    gate = gate_ref[0]
    n_rounds = (jnp.max(rank).astype(I32) + MOE_CAP) // MOE_CAP
    slot = lax.broadcasted_iota(I32, (MOE_CAP, 1), 0).astype(F32)

    def make_pick(s, base):
        toks = slice(s * MOE_SUB, (s + 1) * MOE_SUB)
        return jnp.concatenate(
            [jnp.where(rank[e:e + 1, toks] == slot + base, 1.0, 0.0) for e in range(MOE_EG)], axis=0)

    def expert_rows(e, pick_of):
        rows = slice(e * MOE_CAP, (e + 1) * MOE_CAP)
        x_e = jnp.concatenate([xg_s[s, rows, :] for s in range(MOE_NSUB)], axis=0)
        a = jnp.dot(x_e, w1[e], preferred_element_type=F32)
        hid = a * _sigmoid(a) * jnp.dot(x_e, w3[e], preferred_element_type=F32)
        y = jnp.dot(hid.astype(BF16), w2[e], preferred_element_type=F32)
        for s in range(MOE_NSUB):
            toks = slice(s * MOE_SUB, (s + 1) * MOE_SUB)
            w_row = jnp.sum(pick_of(s)[rows, :] * gate[e:e + 1, toks], axis=1, keepdims=True)
            yw_s[s, rows, :] = (y[s * MOE_CAP:(s + 1) * MOE_CAP, :] * w_row).astype(BF16)

    def scatter(s, back):
        toks = slice(s * MOE_SUB, (s + 1) * MOE_SUB)
        acc[toks, :] += jnp.dot(back, yw_s[s], preferred_element_type=F32)

    picks = [make_pick(s, 0.0) for s in range(MOE_NSUB)]
    for s in range(MOE_NSUB):
        toks = slice(s * MOE_SUB, (s + 1) * MOE_SUB)
        xg_s[s] = jnp.dot(picks[s].astype(BF16), h_ref[toks, :], preferred_element_type=F32).astype(BF16)
    for e in range(MOE_EG):
        expert_rows(e, lambda s: picks[s])
    for s in range(MOE_NSUB):
        scatter(s, picks[s].T.astype(BF16))

    @pl.when(n_rounds > 1)
    def _():
        for s in range(MOE_NSUB):
            pick_s[s] = picks[s].astype(BF16)

    def overflow_round(r, carry):
        base = (r * MOE_CAP).astype(F32)
        sub_top = [jnp.max(rank[:, s * MOE_SUB:(s + 1) * MOE_SUB]) for s in range(MOE_NSUB)]
        exp_top = [jnp.max(rank[e:e + 1, :]) for e in range(MOE_EG)]
        for s in range(MOE_NSUB):
            @pl.when(sub_top[s] >= base)
            def _(s=s):
                pick = make_pick(s, base)
                pick_s[s] = pick.astype(BF16)
                back_s[s] = pick.T.astype(BF16)
                toks = slice(s * MOE_SUB, (s + 1) * MOE_SUB)
                xg_s[s] = jnp.dot(pick.astype(BF16), h_ref[toks, :], preferred_element_type=F32).astype(BF16)
        for e in range(MOE_EG):
            @pl.when(exp_top[e] >= base)
            def _(e=e):
                expert_rows(e, lambda s: pick_s[s].astype(F32))
        for s in range(MOE_NSUB):
            @pl.when(sub_top[s] >= base)
            def _(s=s):
                scatter(s, back_s[s])
        return carry

    lax.fori_loop(1, n_rounds, overflow_round, 0)

    @pl.when(g_idx == N_EXPERTS // MOE_EG - 1)
    def _():
        h = h_ref[...]
        s = jnp.dot(h, ws1[...], preferred_element_type=F32)
        shid = s * _sigmoid(s) * jnp.dot(h, ws3[...], preferred_element_type=F32)
        y_all = acc[...] + jnp.dot(shid.astype(BF16), ws2[...], preferred_element_type=F32)
        o_ref[...] = x_ref[...] + gt_ref[0] * _rms(y_all, gpost_ref[...])


def _moe(h, gates_t, ranks_t, x, gt2, g_post, w1, w3, w2, ws1, ws3, ws2, layer):
    tpb = SEQ // MOE_TM
    n_groups = N_EXPERTS // MOE_EG
    whole = lambda a: pl.BlockSpec(a.shape, lambda i, g: (0, 0))
    per_group = pl.BlockSpec((1, MOE_EG, MOE_TM), lambda i, g: (g, 0, i))
    return pl.pallas_call(
        _moe_kernel,
        grid=(N_TOK // MOE_TM, n_groups),
        in_specs=[pl.BlockSpec((MOE_TM, D_MODEL), lambda i, g: (i, 0)),
                  per_group, per_group,
                  pl.BlockSpec((MOE_TM, D_MODEL), lambda i, g: (i, 0)),
                  pl.BlockSpec((1, 1, D_MODEL), lambda i, g: (i // tpb, 0, 0)),
                  pl.BlockSpec((1, D_MODEL), lambda i, g: (0, 0)),
                  pl.BlockSpec((None, MOE_EG, D_MODEL, EXPERT_DIM), lambda i, g: (layer, g, 0, 0)),
                  pl.BlockSpec((None, MOE_EG, D_MODEL, EXPERT_DIM), lambda i, g: (layer, g, 0, 0)),
                  pl.BlockSpec((None, MOE_EG, EXPERT_DIM, D_MODEL), lambda i, g: (layer, g, 0, 0)),
                  whole(ws1), whole(ws3), whole(ws2)],
        out_specs=pl.BlockSpec((MOE_TM, D_MODEL), lambda i, g: (i, 0)),
        out_shape=jax.ShapeDtypeStruct((N_TOK, D_MODEL), F32),
        scratch_shapes=[pltpu.VMEM((MOE_TM, D_MODEL), F32),
                        pltpu.VMEM((MOE_NSUB, MOE_SLAB, D_MODEL), BF16),
                        pltpu.VMEM((MOE_NSUB, MOE_SLAB, D_MODEL), BF16),
                        pltpu.VMEM((MOE_NSUB, MOE_SLAB, MOE_SUB), BF16),
                        pltpu.VMEM((MOE_NSUB, MOE_SUB, MOE_SLAB), BF16)],
        compiler_params=_params(("arbitrary", "arbitrary")),
        name="moe_ffn",
    )(h, gates_t.reshape(n_groups, MOE_EG, N_TOK), ranks_t.reshape(n_groups, MOE_EG, N_TOK),
      x, gt2, g_post.reshape(1, D_MODEL), w1, w3, w2, ws1, ws3, ws2)


_IN_SIZES = (512, 256, 256, 64, 4, 512, 512, 512, 512, 512, 1024, 1024, 3072)
_IN_DEST = (COL_QA, COL_CKV, COL_IQ, COL_IKIW, COL_IKIW + 64, COL_QB, COL_KB, COL_VB, COL_QC, COL_KC,
            COL_VC, COL_GC, COL_GATES)
_IN_SRC = tuple(int(v) for v in np.cumsum((0,) + _IN_SIZES[:-1]))
N_IN = sum(_IN_SIZES)
PACK_TK = 128


def _pack_kernel(w_ref, o_ref):
    for layer in range(DEPTH):
        for src, width, dst in zip(_IN_SRC, _IN_SIZES, _IN_DEST):
            o_ref[layer, dst:dst + width, :] = w_ref[src:src + width, layer, :].astype(BF16)
        o_ref[layer, COL_IKIW + 68:COL_IKIW + LANES, :] = jnp.zeros((LANES - 68, PACK_TK), BF16)


def _reorder_w_in(w_t):
    return pl.pallas_call(
        _pack_kernel,
        grid=(D_MODEL // PACK_TK,),
        in_specs=[pl.BlockSpec((N_IN, DEPTH, PACK_TK), lambda i: (0, 0, i))],
        out_specs=pl.BlockSpec((DEPTH, N_PROJ, PACK_TK), lambda i: (0, 0, i)),
        out_shape=jax.ShapeDtypeStruct((DEPTH, N_PROJ, D_MODEL), BF16),
        compiler_params=_params(("arbitrary",)),
        name="pack_w_in",
    )(w_t)


def _rope_tables(positions):
    half = HEAD_DIM // 2
    inv_freq = ROPE_THETA ** (-jnp.arange(half, dtype=F32) / half)
    ang = positions.astype(F32)[:, :, None] * inv_freq
    cos, sin = jnp.cos(ang), jnp.sin(ang)
    cos64 = jnp.concatenate([cos, cos], axis=-1)
    sin64 = jnp.concatenate([-sin, sin], axis=-1)
    cos_a = jnp.concatenate([cos64, cos64], axis=-1)
    sin_a = jnp.concatenate([sin64, sin64], axis=-1)
    return cos_a, sin_a


def _decay_tables():
    log_g = jnp.log(1.0 - 2.0 ** (-5.0 - jnp.arange(C_HEADS, dtype=F32)))
    pos = jnp.arange(RET_BLK, dtype=F32)
    diff = pos[:, None] - pos[None, :]
    chunk = jnp.arange(RET_BLK) // CHUNK
    visible = chunk[None, :] <= chunk[:, None]
    e_diag = jnp.where(visible, jnp.exp(log_g[:, None, None] * jnp.abs(diff)), 0.0)
    dec = jnp.exp(log_g * RET_BLK)
    per_lane = jnp.repeat(log_g, HEAD_DIM)[None, :]
    q_scale = jnp.exp(per_lane * pos[:, None])
    k_scale = jnp.exp(per_lane * (RET_BLK - pos)[:, None])
    return dec, q_scale, k_scale, e_diag


def kernel(x, c, positions, ada_w, ada_b, g_pre_mix, g_post_mix, g_pre_ffn, g_post_ffn, w_in, g_kv, w_uk, w_uv,
           g_ret, w_o_a, w_o_b, w_o_c, w_out, w_router, router_bias, w1, w3, w2, ws1, ws3, ws2):
    cos_a, sin_a = _rope_tables(positions)
    dec, q_scale, k_scale, e_diag = _decay_tables()
    xf = x.reshape(N_TOK, D_MODEL)
    w_in_packed = _reorder_w_in(jnp.transpose(w_in, (2, 0, 1)))
    w1_b, w3_b, w2_b = w1.astype(BF16), w3.astype(BF16), w2.astype(BF16)
    for l in range(DEPTH):
        mod = _ada(c, ada_w, ada_b, l).reshape(BATCH, 1, N_ADA * D_MODEL)
        sh1, sc1, gt1, sh2, sc2, gt2 = (mod[:, :, k * D_MODEL:(k + 1) * D_MODEL] for k in range(N_ADA))
        z = _inproj(xf, g_pre_mix[l], sc1, sh1, w_in_packed, l)
        z3 = z.reshape(BATCH, SEQ, N_PROJ)
        w_ukv = jnp.concatenate([w_uk[l], w_uv[l]], axis=1).astype(BF16)
        ya = _dsa(z3, cos_a, sin_a, g_kv[l], w_ukv).reshape(N_TOK, -1)
        yb = _sb(z3).reshape(N_TOK, -1)
        yc = _ret(z3, cos_a, sin_a, dec, q_scale, k_scale, e_diag, g_ret[l]).reshape(N_TOK, -1)
        xf, h, gates, ranks = _merge(ya, yb, yc, z, xf, gt1, g_post_mix[l], g_pre_ffn[l], sc2, sh2,
                              w_o_a[l].astype(BF16), w_o_b[l].astype(BF16), w_o_c[l].astype(BF16),
                              w_out[l].astype(BF16), w_router[l].T.astype(BF16),
                              router_bias[l].reshape(N_EXPERTS, 1))
        xf = _moe(h, gates, ranks, xf, gt2, g_post_ffn[l], w1_b, w3_b, w2_b,
                  ws1[l].astype(BF16), ws3[l].astype(BF16), ws2[l].astype(BF16), l)
    return xf.reshape(BATCH, SEQ, D_MODEL)
```

```python
import math

import numpy as np
import jax
import jax.numpy as jnp
from jax import lax
from jax.experimental import pallas as pl
from jax.experimental.pallas import tpu as pltpu

F32 = jnp.float32
BF16 = jnp.bfloat16
I32 = jnp.int32

D_MODEL = 1024
BATCH = 8
SEQ = 2048
DEPTH = 2
N_TOK = BATCH * SEQ

CHUNK = 64
HEAD_DIM = 64
ROPE_THETA = 10000.0
EPS = 1e-6
A_HEADS = 8
A_KV_LORA = 256
IDX_HEADS = 4
IDX_DIM = 64
TOPK = 256
B_HEADS = 8
C_HEADS = 8
C_V_DIM = 128
N_EXPERTS = 64
TOP_K = 8
N_GROUPS = 8
TOPK_GROUPS = 4
EXPERT_DIM = 256
SHARED_DIM = 256
ROUTED_SCALE = 2.5
N_ADA = 6

LANES = 128
VMEM_LIMIT = 56 * 1024 * 1024

COL_GATES = 0
COL_VC = 3072
COL_GC = 4096
COL_QA = 5120
COL_QB = 5632
COL_KB = 6144
COL_VB = 6656
COL_QC = 7168
COL_KC = 7680
COL_CKV = 8192
COL_IQ = 8448
COL_IKIW = 8704
N_PROJ = 8832
PROJ_TN = 2944
PROJ_TM = 1024

INT_MIN = -2147483648
NEG_INF_KEY = -2139095041

NT_DIMS = (((1,), (1,)), ((), ()))
LOG2_E = math.log2(math.e)


def _params(sem):
    return pltpu.CompilerParams(dimension_semantics=sem, vmem_limit_bytes=VMEM_LIMIT)


def _sigmoid(x):
    return 1.0 / (1.0 + jnp.exp(-x))


def _rope_lanes(x, cos, sin_signed):
    n = x.shape[-1]
    lane = lax.broadcasted_iota(I32, (1, n), 1)
    first = (lane & 32) == 0
    ahead = pltpu.roll(x, n - 32, 1)
    behind = pltpu.roll(x, 32, 1)
    return x * cos + jnp.where(first, ahead, behind) * sin_signed


def _ada_kernel(c_ref, w_ref, b_ref, o_ref):
    c = c_ref[...]
    s = (c * _sigmoid(c)).astype(BF16)
    o_ref[...] = jnp.dot(s, w_ref[...].astype(BF16), preferred_element_type=F32) + b_ref[...]


def _ada(c, w, b, layer):
    n = w.shape[2]
    return pl.pallas_call(
        _ada_kernel,
        grid=(n // D_MODEL,),
        in_specs=[pl.BlockSpec((BATCH, D_MODEL), lambda j: (0, 0)),
                  pl.BlockSpec((None, D_MODEL, D_MODEL), lambda j: (layer, 0, j)),
                  pl.BlockSpec((None, 1, D_MODEL), lambda j: (layer, 0, j))],
        out_specs=pl.BlockSpec((BATCH, D_MODEL), lambda j: (0, j)),
        out_shape=jax.ShapeDtypeStruct((BATCH, n), F32),
        compiler_params=_params(("arbitrary",)),
        name="ada_mod",
    )(c, w, b.reshape(DEPTH, 1, n))


def _inproj_kernel(x_ref, g_ref, sc_ref, sh_ref, w_ref, o_ref, h_ref):
    @pl.when(pl.program_id(1) == 0)
    def _():
        x = x_ref[...]
        ms = jnp.mean(x * x, axis=-1, keepdims=True)
        y = x * lax.rsqrt(ms + EPS) * g_ref[...]
        h_ref[...] = (y * (1.0 + sc_ref[0]) + sh_ref[0]).astype(BF16)

    o_ref[...] = lax.dot_general(h_ref[...], w_ref[...], NT_DIMS, preferred_element_type=F32)


def _inproj(x, g, sc, sh, w_t_bf16, layer):
    tiles_per_batch = SEQ // PROJ_TM
    return pl.pallas_call(
        _inproj_kernel,
        grid=(N_TOK // PROJ_TM, N_PROJ // PROJ_TN),
        in_specs=[pl.BlockSpec((PROJ_TM, D_MODEL), lambda i, j: (i, 0)),
                  pl.BlockSpec((1, D_MODEL), lambda i, j: (0, 0)),
                  pl.BlockSpec((1, 1, D_MODEL), lambda i, j: (i // tiles_per_batch, 0, 0)),
                  pl.BlockSpec((1, 1, D_MODEL), lambda i, j: (i // tiles_per_batch, 0, 0)),
                  pl.BlockSpec((None, PROJ_TN, D_MODEL), lambda i, j: (layer, j, 0))],
        out_specs=pl.BlockSpec((PROJ_TM, PROJ_TN), lambda i, j: (i, j)),
        out_shape=jax.ShapeDtypeStruct((N_TOK, N_PROJ), F32),
        scratch_shapes=[pltpu.VMEM((PROJ_TM, D_MODEL), BF16)],
        compiler_params=_params(("arbitrary", "arbitrary")),
        name="in_proj",
    )(x, g.reshape(1, D_MODEL), sc, sh, w_t_bf16)


DSA_QB = 256
DSA_KC = 256


def _dsa_kernel(ckv_ref, ikiw_ref, q_ref, iq_ref, cos_a, sin_a, gkv_ref, wukv_ref,
                o_ref, k_dup, ik_dup, v_one_t, keys_s, bias_s, acc_s, q_s):
    qi = pl.program_id(1)
    lane = lax.broadcasted_iota(I32, (1, LANES), 1)
    lo_half = lane < 64
    n_kc = SEQ // DSA_KC
    masked = -1e30

    @pl.when(qi == 0)
    def _prologue():
        def chunk(c, carry):
            r0 = pl.multiple_of(c * DSA_KC, DSA_KC)
            rows = pl.ds(r0, DSA_KC)
            ckv = ckv_ref[0, rows, :]
            ms = jnp.mean(ckv * ckv, axis=-1, keepdims=True)
            cn = (ckv * lax.rsqrt(ms + EPS) * gkv_ref[...]).astype(BF16)
            kv = jnp.dot(cn, wukv_ref[...], preferred_element_type=F32)
            cb = jnp.where(lo_half, cos_a[0, rows, :], 1.0)
            sb = jnp.where(lo_half, sin_a[0, rows, :], 0.0)
            kv = _rope_lanes(kv, cb, sb)
            swapped = pltpu.roll(kv, 64, 1)
            k_dup[rows, :] = jnp.where(lo_half, kv, swapped).astype(BF16)
            v_one_t[c] = jnp.where(lo_half, swapped, 1.0).T.astype(BF16)
            ik = jnp.where(lo_half, _rope_lanes(ikiw_ref[0, rows, :], cb, sb), 0.0)
            ik_dup[rows, :] = (ik + pltpu.roll(ik, 64, 1)).astype(BF16)
            return carry
        lax.fori_loop(0, n_kc, chunk, 0)

    q0 = pl.multiple_of(qi * DSA_QB, DSA_QB)
    qrows = pl.ds(q0, DSA_QB)
    cq, sq = cos_a[0, qrows, :], sin_a[0, qrows, :]

    iq = iq_ref[0]
    iq_heads = []
    for p in range(IDX_HEADS // 2):
        pair = _rope_lanes(iq[:, p * LANES:(p + 1) * LANES], cq, sq) * (IDX_DIM ** -0.5)
        iq_heads.append(jnp.where(lo_half, pair, 0.0).astype(BF16))
        iq_heads.append(jnp.where(lo_half, 0.0, pair).astype(BF16))
    iq_stacked = jnp.concatenate(iq_heads, axis=0)
    iw_t = ikiw_ref[0, qrows, :].T * (IDX_HEADS ** -0.5)
    q_chunk = (q0 + lax.broadcasted_iota(I32, (1, DSA_QB), 1)) >> 6
    n_live = lax.shift_right_logical((qi + 1) * DSA_QB + (DSA_KC - 1), DSA_KC.bit_length() - 1)

    def chunk_rows(c):
        return pl.ds(pl.multiple_of(c * DSA_KC, DSA_KC), DSA_KC)

    def score_chunk(c, carry):
        rows = chunk_rows(c)
        k_chunk = (c * DSA_KC + lax.broadcasted_iota(I32, (DSA_KC, 1), 0)) >> 6
        score = jnp.zeros((DSA_KC, DSA_QB), F32)
        logits = lax.dot_general(ik_dup[rows, :], iq_stacked, NT_DIMS, preferred_element_type=F32)
        for h in range(IDX_HEADS):
            score = score + iw_t[64 + h:65 + h, :] * jnp.maximum(logits[:, h * DSA_QB:(h + 1) * DSA_QB], 0.0)
        score = jnp.where(score == 0.0, 0.0, score)
        score = jnp.where(k_chunk <= q_chunk, score, -jnp.inf)
        bits = lax.bitcast_convert_type(score, I32)
        keys_s[rows, :] = bits ^ ((bits >> 31) & 0x7FFFFFFF)
        return carry
    lax.fori_loop(0, n_live, score_chunk, 0)

    def count(preds):
        def chunk(c, accs):
            key = keys_s[chunk_rows(c), :]
            return tuple(acc + jnp.sum(pred(key).astype(I32).reshape(4, DSA_KC // 4, DSA_QB), axis=0)
                         for acc, pred in zip(accs, preds))
        zero = jnp.zeros((DSA_KC // 4, DSA_QB), I32)
        accs = lax.fori_loop(0, n_live, chunk, (zero,) * len(preds))
        return [jnp.sum(acc, axis=0, keepdims=True) for acc in accs]

    def descend(it, prefix):
        cand = prefix | lax.shift_left(jnp.int32(1), 31 - it)
        bound = cand ^ INT_MIN
        return jnp.where(count([lambda k: k >= bound])[0] >= TOPK, cand, prefix)
    thr = lax.fori_loop(0, 32, descend, jnp.zeros((1, DSA_QB), I32)) ^ INT_MIN

    n_gt, n_eq = count([lambda k: k > thr, lambda k: k == thr])
    need = (TOPK - n_gt).astype(F32)
    tie_break = jnp.max(n_gt + n_eq) > TOPK

    @pl.when(tie_break)
    def _():
        tri = (lax.broadcasted_iota(I32, (DSA_KC, DSA_KC), 0)
               >= lax.broadcasted_iota(I32, (DSA_KC, DSA_KC), 1)).astype(BF16)

        def select_chunk(c, seen):
            key = keys_s[chunk_rows(c), :]
            eq = key == thr
            rank = jnp.dot(tri, eq.astype(BF16), preferred_element_type=F32) + seen
            sel = ((key > thr) | (eq & (rank <= need))) & (key > NEG_INF_KEY)
            bias_s[c] = jnp.where(sel, 0.0, masked)
            return rank[DSA_KC - 1:DSA_KC, :]
        lax.fori_loop(0, n_live, select_chunk, jnp.zeros((1, DSA_QB), F32))

    @pl.when(jnp.logical_not(tie_break))
    def _():
        def select_chunk(c, carry):
            key = keys_s[chunk_rows(c), :]
            bias_s[c] = jnp.where((key >= thr) & (key > NEG_INF_KEY), 0.0, masked)
            return carry
        lax.fori_loop(0, n_live, select_chunk, 0)

    q = q_ref[0]
    n_pairs = A_HEADS // 2
    for p in range(n_pairs):
        pair = _rope_lanes(q[:, p * LANES:(p + 1) * LANES], cq, sq) * (HEAD_DIM ** -0.5 * LOG2_E)
        q_s[p * DSA_QB:(p + 1) * DSA_QB, :] = jnp.where(lo_half, pair, 0.0).astype(BF16)
        q_s[(n_pairs + p) * DSA_QB:(n_pairs + p + 1) * DSA_QB, :] = jnp.where(lo_half, 0.0, pair).astype(BF16)

    def attend(c, m):
        bias = jnp.concatenate([bias_s[c]] * A_HEADS, axis=1)
        s = lax.dot_general(k_dup[chunk_rows(c), :], q_s[...], NT_DIMS,
                            preferred_element_type=F32) + bias
        m_new = jnp.maximum(m, jnp.max(s, axis=0, keepdims=True))
        pe = jnp.exp2(s - m_new).astype(BF16)
        pv = jnp.dot(v_one_t[c], pe, preferred_element_type=F32)
        acc_s[...] = jnp.exp2(m - m_new) * acc_s[...] + pv
        return m_new

    acc_s[...] = jnp.zeros_like(acc_s)
    lax.fori_loop(0, n_live, attend, jnp.full((1, A_HEADS * DSA_QB), masked, F32))
    for p in range(n_pairs):
        both = []
        for slab in (p, n_pairs + p):
            a = acc_s[:, slab * DSA_QB:(slab + 1) * DSA_QB]
            both.append(a[:HEAD_DIM, :] / a[HEAD_DIM:HEAD_DIM + 1, :])
        o_ref[0, :, p * LANES:(p + 1) * LANES] = jnp.concatenate(both, axis=0).T


def _dsa(z3, cos_a, sin_a, g_kv, w_ukv_bf16):
    full = lambda width, col: pl.BlockSpec((1, SEQ, width), lambda b, i: (b, 0, col // width))
    blk = lambda width, col: pl.BlockSpec((1, DSA_QB, width), lambda b, i: (b, i, col // width))
    tab = pl.BlockSpec((1, SEQ, LANES), lambda b, i: (b, 0, 0))
    return pl.pallas_call(
        _dsa_kernel,
        grid=(BATCH, SEQ // DSA_QB),
        in_specs=[full(A_KV_LORA, COL_CKV), full(LANES, COL_IKIW),
                  blk(A_HEADS * HEAD_DIM, COL_QA), blk(IDX_HEADS * IDX_DIM, COL_IQ),
                  tab, tab,
                  pl.BlockSpec((1, A_KV_LORA), lambda b, i: (0, 0)),
                  pl.BlockSpec((A_KV_LORA, LANES), lambda b, i: (0, 0))],
        out_specs=pl.BlockSpec((1, DSA_QB, A_HEADS * HEAD_DIM), lambda b, i: (b, i, 0)),
        out_shape=jax.ShapeDtypeStruct((BATCH, SEQ, A_HEADS * HEAD_DIM), F32),
        scratch_shapes=[pltpu.VMEM((SEQ, LANES), BF16)] * 2
                       + [pltpu.VMEM((SEQ // DSA_KC, LANES, DSA_KC), BF16),
                          pltpu.VMEM((SEQ, DSA_QB), I32),
                          pltpu.VMEM((SEQ // DSA_KC, DSA_KC, DSA_QB), F32),
                          pltpu.VMEM((LANES, A_HEADS * DSA_QB), F32),
                          pltpu.VMEM((A_HEADS * DSA_QB, LANES), BF16)],
        compiler_params=_params(("arbitrary", "arbitrary")),
        name="dsa_mixer",
    )(z3, z3, z3, z3, cos_a, sin_a, g_kv.reshape(1, A_KV_LORA), w_ukv_bf16)


SB_BLK = 128
SB_EXP_UNDERFLOW = -105.0


def _sb_kernel(q_ref, k_ref, v_ref, o_ref, q_s, tail_s, acc_s):
    qi = pl.program_id(1)
    n_pairs = B_HEADS // 2
    rows_all = B_HEADS * SB_BLK
    lane = lax.broadcasted_iota(I32, (1, LANES), 1)
    lo_half = lane < 64
    causal = (lax.broadcasted_iota(I32, (rows_all, SB_BLK), 1)
              < (lax.broadcasted_iota(I32, (rows_all, SB_BLK), 0) & (SB_BLK - 1)))
    r2 = lax.broadcasted_iota(I32, (2 * SB_BLK, 2 * SB_BLK), 0) & (SB_BLK - 1)
    c2 = lax.broadcasted_iota(I32, (2 * SB_BLK, 2 * SB_BLK), 1)
    suffix_mat = ((c2 >= SB_BLK) | (r2 > c2)).astype(BF16)

    for p in range(n_pairs):
        q2 = q_ref[0, :, p * LANES:(p + 1) * LANES] * (HEAD_DIM ** -0.5)
        q_s[2 * p * SB_BLK:(2 * p + 1) * SB_BLK, :] = jnp.where(lo_half, q2, 0.0).astype(BF16)
        q_s[(2 * p + 1) * SB_BLK:(2 * p + 2) * SB_BLK, :] = jnp.where(lo_half, 0.0, q2).astype(BF16)

    def key_block(kb, diagonal):
        r0 = pl.multiple_of(kb * SB_BLK, SB_BLK)
        pair_rows = lambda p: slice(2 * p * SB_BLK, (2 * p + 2) * SB_BLK)
        z = jnp.concatenate(
            [lax.dot_general(q_s[pair_rows(p), :], k_ref[0, pl.ds(r0, SB_BLK), p * LANES:(p + 1) * LANES].astype(BF16),
                             NT_DIMS, preferred_element_type=F32) for p in range(n_pairs)], axis=0)
        log_beta = jnp.minimum(z, 0.0) - jnp.log(1.0 + jnp.exp(-jnp.abs(z)))
        log_not = log_beta - z
        if diagonal:
            log_not = jnp.where(causal, log_not, 0.0)
        hi = log_not.astype(BF16)
        lo = (log_not - hi.astype(F32)).astype(BF16)
        hi_lo = jnp.concatenate([hi, lo], axis=1)
        t = jnp.concatenate([jnp.dot(hi_lo[pair_rows(p), :], suffix_mat, preferred_element_type=F32)
                             for p in range(n_pairs)], axis=0)
        if diagonal:
            a = jnp.where(causal, jnp.exp(log_beta + t[:, :SB_BLK]), 0.0)
            tail = t[:, SB_BLK:]
        else:
            a = jnp.exp(log_beta + t[:, :SB_BLK] + tail_s[...])
            tail = tail_s[...] + t[:, SB_BLK:]
        tail_s[...] = tail
        a = a.astype(BF16)
        for p in range(n_pairs):
            v2 = v_ref[0, pl.ds(r0, SB_BLK), p * LANES:(p + 1) * LANES].astype(BF16)
            pv = jnp.dot(a[pair_rows(p), :], v2, preferred_element_type=F32)
            if diagonal:
                acc_s[pair_rows(p), :] = pv
            else:
                acc_s[pair_rows(p), :] += pv
        return jnp.max(tail)

    first = key_block(qi, True)

    def more(state):
        jj, worst = state
        return (jj <= qi) & (worst > SB_EXP_UNDERFLOW)

    def step(state):
        jj, _ = state
        return jj + 1, key_block(qi - jj, False)

    lax.while_loop(more, step, (jnp.int32(1), first))
    for p in range(n_pairs):
        even = acc_s[2 * p * SB_BLK:(2 * p + 1) * SB_BLK, :]
        odd = acc_s[(2 * p + 1) * SB_BLK:(2 * p + 2) * SB_BLK, :]
        o_ref[0, :, p * LANES:(p + 1) * LANES] = jnp.where(lo_half, even, odd)


def _sb(z3):
    width = B_HEADS * HEAD_DIM
    return pl.pallas_call(
        _sb_kernel,
        grid=(BATCH, SEQ // SB_BLK),
        in_specs=[pl.BlockSpec((1, SB_BLK, width), lambda b, i: (b, i, COL_QB // width)),
                  pl.BlockSpec((1, SEQ, width), lambda b, i: (b, 0, COL_KB // width)),
                  pl.BlockSpec((1, SEQ, width), lambda b, i: (b, 0, COL_VB // width))],
        out_specs=pl.BlockSpec((1, SB_BLK, width), lambda b, i: (b, i, 0)),
        out_shape=jax.ShapeDtypeStruct((BATCH, SEQ, width), F32),
        scratch_shapes=[pltpu.VMEM((B_HEADS * SB_BLK, LANES), BF16),
                        pltpu.VMEM((B_HEADS * SB_BLK, SB_BLK), F32),
                        pltpu.VMEM((B_HEADS * SB_BLK, SB_BLK), F32)],
        compiler_params=_params(("arbitrary", "arbitrary")),
        name="stickbreak_mixer",
    )(z3, z3, z3)


RET_BLK = 256


def _ret_kernel(dec_ref, q_ref, k_ref, v_ref, gate_ref, cos_a, sin_a, qs_ref, ks_ref, e_diag, gret_ref,
                o_ref, state):
    lane = lax.broadcasted_iota(I32, (1, LANES), 1)
    lo_half = lane < 64
    row_lo = lax.broadcasted_iota(I32, (LANES, 1), 0) < 64

    @pl.when(pl.program_id(1) == 0)
    def _():
        state[...] = jnp.zeros_like(state)

    cq, sq = cos_a[0], sin_a[0]
    for p in range(C_HEADS // 2):
        cols = slice(p * LANES, (p + 1) * LANES)
        q2 = _rope_lanes(q_ref[0, :, cols], cq, sq)
        k2 = _rope_lanes(k_ref[0, :, cols], cq, sq) * (HEAD_DIM ** -0.5)
        q_hat = q2 * qs_ref[:, cols]
        k_hat_t = (k2 * ks_ref[:, cols]).T
        for e in range(2):
            h = 2 * p + e
            pick = (lambda x: jnp.where(lo_half, x, 0.0)) if e == 0 else (lambda x: jnp.where(lo_half, 0.0, x))
            vcols = slice(h * C_V_DIM, (h + 1) * C_V_DIM)
            v2 = v_ref[0, :, vcols].astype(BF16)
            s = lax.dot_general(pick(q2).astype(BF16), k2.astype(BF16), NT_DIMS,
                                preferred_element_type=F32) * e_diag[h]
            o = jnp.dot(s.astype(BF16), v2, preferred_element_type=F32)
            o = o + jnp.dot(pick(q_hat).astype(BF16), state[h].astype(BF16), preferred_element_type=F32)
            kv = jnp.dot(jnp.where(row_lo if e == 0 else ~row_lo, k_hat_t, 0.0).astype(BF16), v2,
                         preferred_element_type=F32)
            state[h] = dec_ref[h] * state[h] + kv
            mu = jnp.mean(o, axis=-1, keepdims=True)
            d = o - mu
            var = jnp.mean(d * d, axis=-1, keepdims=True)
            y = d * lax.rsqrt(var + EPS) * gret_ref[:, vcols]
            g = gate_ref[0, :, vcols]
            o_ref[0, :, vcols] = g * _sigmoid(g) * y


def _ret(z3, cos_a, sin_a, dec, q_scale, k_scale, e_diag, g_ret):
    qk_w = C_HEADS * HEAD_DIM
    v_w = C_HEADS * C_V_DIM
    tab = pl.BlockSpec((1, RET_BLK, LANES), lambda b, i: (b, i, 0))
    const2 = lambda a: pl.BlockSpec(a.shape, lambda b, i: (0, 0))
    return pl.pallas_call(
        _ret_kernel,
        grid=(BATCH, SEQ // RET_BLK),
        in_specs=[pl.BlockSpec(memory_space=pltpu.SMEM),
                  pl.BlockSpec((1, RET_BLK, qk_w), lambda b, i: (b, i, COL_QC // qk_w)),
                  pl.BlockSpec((1, RET_BLK, qk_w), lambda b, i: (b, i, COL_KC // qk_w)),
                  pl.BlockSpec((1, RET_BLK, v_w), lambda b, i: (b, i, COL_VC // v_w)),
                  pl.BlockSpec((1, RET_BLK, v_w), lambda b, i: (b, i, COL_GC // v_w)),
                  tab, tab, const2(q_scale), const2(k_scale),
                  pl.BlockSpec(e_diag.shape, lambda b, i: (0, 0, 0)),
                  pl.BlockSpec((1, v_w), lambda b, i: (0, 0))],
        out_specs=pl.BlockSpec((1, RET_BLK, v_w), lambda b, i: (b, i, 0)),
        out_shape=jax.ShapeDtypeStruct((BATCH, SEQ, v_w), F32),
        scratch_shapes=[pltpu.VMEM((C_HEADS, LANES, C_V_DIM), F32)],
        compiler_params=_params(("arbitrary", "arbitrary")),
        name="retention_mixer",
    )(dec, z3, z3, z3, z3, cos_a, sin_a, q_scale, k_scale, e_diag, g_ret.reshape(1, v_w))


MERGE_TM = 512
MOE_PIECE = 128
MOE_SUB = 256


def _rms(y, g):
    ms = jnp.mean(y * y, axis=-1, keepdims=True)
    return y * lax.rsqrt(ms + EPS) * g


def _route(logits_t, bias_col):
    n_tok = logits_t.shape[1]
    per_group = N_EXPERTS // N_GROUPS
    scores = _sigmoid(logits_t)
    sel = scores + bias_col
    sel3 = sel.reshape(N_GROUPS, per_group, n_tok)
    member = lax.broadcasted_iota(I32, (N_GROUPS, per_group, n_tok), 1)
    m1 = jnp.max(sel3, axis=1, keepdims=True)
    first = jnp.min(jnp.where(sel3 == m1, member, per_group), axis=1, keepdims=True)
    m2 = jnp.max(jnp.where(member == first, -jnp.inf, sel3), axis=1, keepdims=True)
    grp = (m1 + m2).reshape(N_GROUPS, n_tok)
    gid = lax.broadcasted_iota(I32, (N_GROUPS, n_tok), 0)
    rank = jnp.zeros((N_GROUPS, n_tok), I32)
    for g in range(N_GROUPS):
        other = grp[g:g + 1, :]
        rank = rank + ((other > grp) | ((other == grp) & (g < gid))).astype(I32)
    keep = jnp.where(rank < TOPK_GROUPS, 1.0, 0.0).reshape(N_GROUPS, 1, n_tok)
    live = jnp.where(keep > 0.0, sel3, -jnp.inf).reshape(N_EXPERTS, n_tok)
    eid = lax.broadcasted_iota(I32, (N_EXPERTS, n_tok), 0)
    chosen = jnp.zeros((N_EXPERTS, n_tok), jnp.bool_)
    for _ in range(TOP_K):
        m = jnp.max(live, axis=0, keepdims=True)
        pick = eid == jnp.min(jnp.where(live == m, eid, N_EXPERTS), axis=0, keepdims=True)
        chosen = chosen | pick
        live = jnp.where(pick, -jnp.inf, live)
    w = jnp.where(chosen, scores, 0.0)
    return w / jnp.sum(w, axis=0, keepdims=True) * ROUTED_SCALE, chosen


def _merge_kernel(ya_ref, yb_ref, yc_ref, g_ref, x_ref, gt_ref, gpost_ref, gpre_ref, sc_ref, sh_ref,
                  woa, wob, woc, wout, wr_t, rb_ref, xo_ref, h_ref, gate_ref):
    ya = jnp.dot(ya_ref[...].astype(BF16), woa[...], preferred_element_type=F32)
    yb = jnp.dot(yb_ref[...].astype(BF16), wob[...], preferred_element_type=F32)
    yc = jnp.dot(yc_ref[...].astype(BF16), woc[...], preferred_element_type=F32)
    merged = (_sigmoid(g_ref[:, 0:D_MODEL]) * ya
              + _sigmoid(g_ref[:, D_MODEL:2 * D_MODEL]) * yb
              + _sigmoid(g_ref[:, 2 * D_MODEL:3 * D_MODEL]) * yc)
    y = jnp.dot(merged.astype(BF16), wout[...], preferred_element_type=F32)
    x = x_ref[...] + gt_ref[0] * _rms(y, gpost_ref[...])
    xo_ref[...] = x
    h = (_rms(x, gpre_ref[...]) * (1.0 + sc_ref[0]) + sh_ref[0]).astype(BF16)
    h_ref[...] = h
    logits_t = lax.dot_general(wr_t[...], h, NT_DIMS, preferred_element_type=F32)
    gates_t, chosen = _route(logits_t, rb_ref[...])
    marked = jnp.where(chosen, gates_t, -1.0)
    for piece in range(MERGE_TM // MOE_PIECE):
        gate_ref[piece] = marked[:, piece * MOE_PIECE:(piece + 1) * MOE_PIECE]


def _merge(ya, yb, yc, z, x, gt1, g_post, g_pre, sc2, sh2, woa, wob, woc, wout, wr_t, rbias):
    tpb = SEQ // MERGE_TM
    row = lambda w: pl.BlockSpec((MERGE_TM, w), lambda i: (i, 0))
    vec = pl.BlockSpec((1, D_MODEL), lambda i: (0, 0))
    mod = pl.BlockSpec((1, 1, D_MODEL), lambda i: (i // tpb, 0, 0))
    whole = lambda a: pl.BlockSpec(a.shape, lambda i: (0, 0))
    return pl.pallas_call(
        _merge_kernel,
        grid=(N_TOK // MERGE_TM,),
        in_specs=[row(512), row(512), row(1024), pl.BlockSpec((MERGE_TM, 3 * D_MODEL), lambda i: (i, 0)),
                  row(D_MODEL), mod, vec, vec, mod, mod,
                  whole(woa), whole(wob), whole(woc), whole(wout), whole(wr_t), whole(rbias)],
        out_specs=[row(D_MODEL), row(D_MODEL),
                   pl.BlockSpec((MERGE_TM // MOE_PIECE, N_EXPERTS, MOE_PIECE), lambda i: (i % tpb, 0, i // tpb))],
        out_shape=[jax.ShapeDtypeStruct((N_TOK, D_MODEL), F32),
                   jax.ShapeDtypeStruct((N_TOK, D_MODEL), BF16),
                   jax.ShapeDtypeStruct((SEQ // MOE_PIECE, N_EXPERTS, BATCH * MOE_PIECE), F32)],
        compiler_params=_params(("arbitrary",)),
        name="merge_route",
    )(ya, yb, yc, z, x, gt1, g_post.reshape(1, D_MODEL), g_pre.reshape(1, D_MODEL), sc2, sh2,
      woa, wob, woc, wout, wr_t, rbias)


MOE_TM = BATCH * MOE_PIECE
MOE_EG = 4
MOE_CAP = 64
MOE_NSUB = MOE_TM // MOE_SUB
MOE_SLAB = MOE_EG * MOE_CAP


def _moe_kernel(h_ref, gate_ref, x_ref, gt_ref, gpost_ref, w1, w3, w2, ws1, ws3, ws2,
                o_ref, acc, xg_s, yw_s, pick_s, back_s, rank_s, gate_s):
    g_idx = pl.program_id(1)
    n_groups = N_EXPERTS // MOE_EG
    pieces_per_sub = MOE_SUB // MOE_PIECE

    @pl.when(g_idx == 0)
    def _():
        acc[...] = jnp.zeros_like(acc)
        marked = gate_ref[...]
        routed = marked >= 0.0
        earlier = (lax.broadcasted_iota(I32, (MOE_SUB, MOE_SUB), 0)
                   < lax.broadcasted_iota(I32, (MOE_SUB, MOE_SUB), 1)).astype(BF16)
        ones = jnp.where(routed, 1.0, 0.0).astype(BF16)
        before = jnp.concatenate(
            [jnp.dot(ones[:, s * MOE_SUB:(s + 1) * MOE_SUB], earlier, preferred_element_type=F32)
             for s in range(MOE_NSUB)], axis=1)
        ranks = jnp.where(routed, before, -1.0)
        gates = jnp.maximum(marked, 0.0)
        for gg in range(n_groups):
            rank_s[gg] = ranks[gg * MOE_EG:(gg + 1) * MOE_EG, :]
            gate_s[gg] = gates[gg * MOE_EG:(gg + 1) * MOE_EG, :]

    def h_rows(s):
        return h_ref[s * pieces_per_sub:(s + 1) * pieces_per_sub].reshape(MOE_SUB, D_MODEL)

    rank = rank_s[g_idx]
    gate = gate_s[g_idx]
    n_rounds = (jnp.max(rank).astype(I32) + MOE_CAP) // MOE_CAP
    slot = lax.broadcasted_iota(I32, (MOE_CAP, 1), 0).astype(F32)

    def make_pick(s, base):
        toks = slice(s * MOE_SUB, (s + 1) * MOE_SUB)
        return jnp.concatenate(
            [jnp.where(rank[e:e + 1, toks] == slot + base, 1.0, 0.0) for e in range(MOE_EG)], axis=0)

    def expert_rows(e, pick_of):
        rows = slice(e * MOE_CAP, (e + 1) * MOE_CAP)
        x_e = jnp.concatenate([xg_s[s, rows, :] for s in range(MOE_NSUB)], axis=0)
        a = jnp.dot(x_e, w1[e], preferred_element_type=F32)
        hid = a * _sigmoid(a) * jnp.dot(x_e, w3[e], preferred_element_type=F32)
        y = jnp.dot(hid.astype(BF16), w2[e], preferred_element_type=F32)
        for s in range(MOE_NSUB):
            toks = slice(s * MOE_SUB, (s + 1) * MOE_SUB)
            w_row = jnp.sum(pick_of(s)[rows, :] * gate[e:e + 1, toks], axis=1, keepdims=True)
            yw_s[s, rows, :] = (y[s * MOE_CAP:(s + 1) * MOE_CAP, :] * w_row).astype(BF16)

    def scatter(s, back):
        toks = slice(s * MOE_SUB, (s + 1) * MOE_SUB)
        acc[toks, :] += jnp.dot(back, yw_s[s], preferred_element_type=F32)

    picks = [make_pick(s, 0.0) for s in range(MOE_NSUB)]
    for s in range(MOE_NSUB):
        xg_s[s] = jnp.dot(picks[s].astype(BF16), h_rows(s), preferred_element_type=F32).astype(BF16)
    for e in range(MOE_EG):
        expert_rows(e, lambda s: picks[s])
    for s in range(MOE_NSUB):
        scatter(s, picks[s].T.astype(BF16))

    @pl.when(n_rounds > 1)
    def _():
        for s in range(MOE_NSUB):
            pick_s[s] = picks[s].astype(BF16)

    def overflow_round(r, carry):
        base = (r * MOE_CAP).astype(F32)
        sub_top = [jnp.max(rank[:, s * MOE_SUB:(s + 1) * MOE_SUB]) for s in range(MOE_NSUB)]
        exp_top = [jnp.max(rank[e:e + 1, :]) for e in range(MOE_EG)]
        for s in range(MOE_NSUB):
            @pl.when(sub_top[s] >= base)
            def _(s=s):
                pick = make_pick(s, base)
                pick_s[s] = pick.astype(BF16)
                back_s[s] = pick.T.astype(BF16)
                xg_s[s] = jnp.dot(pick.astype(BF16), h_rows(s), preferred_element_type=F32).astype(BF16)
        for e in range(MOE_EG):
            @pl.when(exp_top[e] >= base)
            def _(e=e):
                expert_rows(e, lambda s: pick_s[s].astype(F32))
        for s in range(MOE_NSUB):
            @pl.when(sub_top[s] >= base)
            def _(s=s):
                scatter(s, back_s[s])
        return carry

    lax.fori_loop(1, n_rounds, overflow_round, 0)

    @pl.when(g_idx == n_groups - 1)
    def _():
        h = h_ref[...].reshape(MOE_TM, D_MODEL)
        s = jnp.dot(h, ws1[...], preferred_element_type=F32)
        shid = s * _sigmoid(s) * jnp.dot(h, ws3[...], preferred_element_type=F32)
        y_all = acc[...] + jnp.dot(shid.astype(BF16), ws2[...], preferred_element_type=F32)
        normed = _rms(y_all, gpost_ref[...]).reshape(BATCH, MOE_PIECE, D_MODEL)
        o_ref[...] = x_ref[...] + gt_ref[...] * normed


def _moe(h, gates, x, gt2, g_post, w1, w3, w2, ws1, ws3, ws2, layer):
    n_groups = N_EXPERTS // MOE_EG
    n_tiles = SEQ // MOE_PIECE
    whole = lambda a: pl.BlockSpec(a.shape, lambda i, g: (0, 0))
    tile = pl.BlockSpec((BATCH, None, MOE_PIECE, D_MODEL), lambda i, g: (0, i, 0, 0))
    by_tile = lambda a: a.reshape(BATCH, n_tiles, MOE_PIECE, D_MODEL)
    out = pl.pallas_call(
        _moe_kernel,
        grid=(n_tiles, n_groups),
        in_specs=[tile,
                  pl.BlockSpec((None, N_EXPERTS, MOE_TM), lambda i, g: (i, 0, 0)),
                  tile,
                  pl.BlockSpec((BATCH, 1, D_MODEL), lambda i, g: (0, 0, 0)),
                  pl.BlockSpec((1, D_MODEL), lambda i, g: (0, 0)),
                  pl.BlockSpec((None, MOE_EG, D_MODEL, EXPERT_DIM), lambda i, g: (layer, g, 0, 0)),
                  pl.BlockSpec((None, MOE_EG, D_MODEL, EXPERT_DIM), lambda i, g: (layer, g, 0, 0)),
                  pl.BlockSpec((None, MOE_EG, EXPERT_DIM, D_MODEL), lambda i, g: (layer, g, 0, 0)),
                  whole(ws1), whole(ws3), whole(ws2)],
        out_specs=tile,
        out_shape=jax.ShapeDtypeStruct((BATCH, n_tiles, MOE_PIECE, D_MODEL), F32),
        scratch_shapes=[pltpu.VMEM((MOE_TM, D_MODEL), F32),
                        pltpu.VMEM((MOE_NSUB, MOE_SLAB, D_MODEL), BF16),
                        pltpu.VMEM((MOE_NSUB, MOE_SLAB, D_MODEL), BF16),
                        pltpu.VMEM((MOE_NSUB, MOE_SLAB, MOE_SUB), BF16),
                        pltpu.VMEM((MOE_NSUB, MOE_SUB, MOE_SLAB), BF16),
                        pltpu.VMEM((n_groups, MOE_EG, MOE_TM), F32),
                        pltpu.VMEM((n_groups, MOE_EG, MOE_TM), F32)],
        compiler_params=_params(("arbitrary", "arbitrary")),
        name="moe_ffn",
    )(by_tile(h), gates, by_tile(x), gt2, g_post.reshape(1, D_MODEL), w1, w3, w2, ws1, ws3, ws2)
    return out.reshape(N_TOK, D_MODEL)


_IN_SIZES = (512, 256, 256, 64, 4, 512, 512, 512, 512, 512, 1024, 1024, 3072)
_IN_DEST = (COL_QA, COL_CKV, COL_IQ, COL_IKIW, COL_IKIW + 64, COL_QB, COL_KB, COL_VB, COL_QC, COL_KC,
            COL_VC, COL_GC, COL_GATES)
_IN_SRC = tuple(int(v) for v in np.cumsum((0,) + _IN_SIZES[:-1]))
N_IN = sum(_IN_SIZES)
PACK_TK = 128


def _pack_kernel(w_ref, o_ref):
    for layer in range(DEPTH):
        for src, width, dst in zip(_IN_SRC, _IN_SIZES, _IN_DEST):
            o_ref[layer, dst:dst + width, :] = w_ref[src:src + width, layer, :].astype(BF16)
        o_ref[layer, COL_IKIW + 68:COL_IKIW + LANES, :] = jnp.zeros((LANES - 68, PACK_TK), BF16)


def _reorder_w_in(w_t):
    return pl.pallas_call(
        _pack_kernel,
        grid=(D_MODEL // PACK_TK,),
        in_specs=[pl.BlockSpec((N_IN, DEPTH, PACK_TK), lambda i: (0, 0, i))],
        out_specs=pl.BlockSpec((DEPTH, N_PROJ, PACK_TK), lambda i: (0, 0, i)),
        out_shape=jax.ShapeDtypeStruct((DEPTH, N_PROJ, D_MODEL), BF16),
        compiler_params=_params(("arbitrary",)),
        name="pack_w_in",
    )(w_t)


def _rope_tables(positions):
    half = HEAD_DIM // 2
    inv_freq = ROPE_THETA ** (-jnp.arange(half, dtype=F32) / half)
    ang = positions.astype(F32)[:, :, None] * inv_freq
    cos, sin = jnp.cos(ang), jnp.sin(ang)
    cos64 = jnp.concatenate([cos, cos], axis=-1)
    sin64 = jnp.concatenate([-sin, sin], axis=-1)
    cos_a = jnp.concatenate([cos64, cos64], axis=-1)
    sin_a = jnp.concatenate([sin64, sin64], axis=-1)
    return cos_a, sin_a


def _decay_tables():
    log_g = jnp.log(1.0 - 2.0 ** (-5.0 - jnp.arange(C_HEADS, dtype=F32)))
    pos = jnp.arange(RET_BLK, dtype=F32)
    diff = pos[:, None] - pos[None, :]
    chunk = jnp.arange(RET_BLK) // CHUNK
    visible = chunk[None, :] <= chunk[:, None]
    e_diag = jnp.where(visible, jnp.exp(log_g[:, None, None] * jnp.abs(diff)), 0.0)
    dec = jnp.exp(log_g * RET_BLK)
    per_lane = jnp.repeat(log_g, HEAD_DIM)[None, :]
    q_scale = jnp.exp(per_lane * pos[:, None])
    k_scale = jnp.exp(per_lane * (RET_BLK - pos)[:, None])
    return dec, q_scale, k_scale, e_diag


def kernel(x, c, positions, ada_w, ada_b, g_pre_mix, g_post_mix, g_pre_ffn, g_post_ffn, w_in, g_kv, w_uk, w_uv,
           g_ret, w_o_a, w_o_b, w_o_c, w_out, w_router, router_bias, w1, w3, w2, ws1, ws3, ws2):
    cos_a, sin_a = _rope_tables(positions)
    dec, q_scale, k_scale, e_diag = _decay_tables()
    xf = x.reshape(N_TOK, D_MODEL)
    w_in_packed = _reorder_w_in(jnp.transpose(w_in, (2, 0, 1)))
    w1_b, w3_b, w2_b = w1.astype(BF16), w3.astype(BF16), w2.astype(BF16)
    for l in range(DEPTH):
        mod = _ada(c, ada_w, ada_b, l).reshape(BATCH, 1, N_ADA * D_MODEL)
        sh1, sc1, gt1, sh2, sc2, gt2 = (mod[:, :, k * D_MODEL:(k + 1) * D_MODEL] for k in range(N_ADA))
        z = _inproj(xf, g_pre_mix[l], sc1, sh1, w_in_packed, l)
        z3 = z.reshape(BATCH, SEQ, N_PROJ)
        w_ukv = jnp.concatenate([w_uk[l], w_uv[l]], axis=1).astype(BF16)
        ya = _dsa(z3, cos_a, sin_a, g_kv[l], w_ukv).reshape(N_TOK, -1)
        yb = _sb(z3).reshape(N_TOK, -1)
        yc = _ret(z3, cos_a, sin_a, dec, q_scale, k_scale, e_diag, g_ret[l]).reshape(N_TOK, -1)
        xf, h, gates = _merge(ya, yb, yc, z, xf, gt1, g_post_mix[l], g_pre_ffn[l], sc2, sh2,
                              w_o_a[l].astype(BF16), w_o_b[l].astype(BF16), w_o_c[l].astype(BF16),
                              w_out[l].astype(BF16), w_router[l].T.astype(BF16),
                              router_bias[l].reshape(N_EXPERTS, 1))
        xf = _moe(h, gates, xf, gt2, g_post_ffn[l], w1_b, w3_b, w2_b,
                  ws1[l].astype(BF16), ws3[l].astype(BF16), ws2[l].astype(BF16), l)
    return xf.reshape(BATCH, SEQ, D_MODEL)
```

```python
import math

import numpy as np
import jax
import jax.numpy as jnp
from jax import lax
from jax.experimental import pallas as pl
from jax.experimental.pallas import tpu as pltpu

F32 = jnp.float32
BF16 = jnp.bfloat16
I32 = jnp.int32

D_MODEL = 1024
BATCH = 8
SEQ = 2048
DEPTH = 2
N_TOK = BATCH * SEQ

CHUNK = 64
HEAD_DIM = 64
ROPE_THETA = 10000.0
EPS = 1e-6
A_HEADS = 8
A_KV_LORA = 256
IDX_HEADS = 4
IDX_DIM = 64
TOPK = 256
B_HEADS = 8
C_HEADS = 8
C_V_DIM = 128
N_EXPERTS = 64
TOP_K = 8
N_GROUPS = 8
TOPK_GROUPS = 4
EXPERT_DIM = 256
SHARED_DIM = 256
ROUTED_SCALE = 2.5
N_ADA = 6

LANES = 128
VMEM_LIMIT = 56 * 1024 * 1024

COL_GATES = 0
COL_VC = 3072
COL_GC = 4096
COL_QA = 5120
COL_QB = 5632
COL_KB = 6144
COL_VB = 6656
COL_QC = 7168
COL_KC = 7680
COL_CKV = 8192
COL_IQ = 8448
COL_IKIW = 8704
N_PROJ = 8832
PROJ_TN = 2944
PROJ_TM = 1024

INT_MIN = -2147483648
NEG_INF_KEY = -2139095041

NT_DIMS = (((1,), (1,)), ((), ()))
LOG2_E = math.log2(math.e)


def _params(sem):
    return pltpu.CompilerParams(dimension_semantics=sem, vmem_limit_bytes=VMEM_LIMIT)


def _sigmoid(x):
    return 1.0 / (1.0 + jnp.exp(-x))


def _rope_lanes(x, cos, sin_signed):
    n = x.shape[-1]
    lane = lax.broadcasted_iota(I32, (1, n), 1)
    first = (lane & 32) == 0
    ahead = pltpu.roll(x, n - 32, 1)
    behind = pltpu.roll(x, 32, 1)
    return x * cos + jnp.where(first, ahead, behind) * sin_signed


def _ada_kernel(c_ref, w_ref, b_ref, o_ref):
    c = c_ref[...]
    s = (c * _sigmoid(c)).astype(BF16)
    o_ref[...] = jnp.dot(s, w_ref[...].astype(BF16), preferred_element_type=F32) + b_ref[...]


def _ada(c, w, b, layer):
    n = w.shape[2]
    return pl.pallas_call(
        _ada_kernel,
        grid=(n // D_MODEL,),
        in_specs=[pl.BlockSpec((BATCH, D_MODEL), lambda j: (0, 0)),
                  pl.BlockSpec((None, D_MODEL, D_MODEL), lambda j: (layer, 0, j)),
                  pl.BlockSpec((None, 1, D_MODEL), lambda j: (layer, 0, j))],
        out_specs=pl.BlockSpec((BATCH, D_MODEL), lambda j: (0, j)),
        out_shape=jax.ShapeDtypeStruct((BATCH, n), F32),
        compiler_params=_params(("arbitrary",)),
        name="ada_mod",
    )(c, w, b.reshape(DEPTH, 1, n))


def _inproj_kernel(x_ref, g_ref, sc_ref, sh_ref, w_ref, o_ref, h_ref):
    @pl.when(pl.program_id(1) == 0)
    def _():
        x = x_ref[...]
        ms = jnp.mean(x * x, axis=-1, keepdims=True)
        y = x * lax.rsqrt(ms + EPS) * g_ref[...]
        h_ref[...] = (y * (1.0 + sc_ref[0]) + sh_ref[0]).astype(BF16)

    o_ref[...] = lax.dot_general(h_ref[...], w_ref[...], NT_DIMS, preferred_element_type=F32)


def _inproj(x, g, sc, sh, w_t_bf16, layer):
    tiles_per_batch = SEQ // PROJ_TM
    return pl.pallas_call(
        _inproj_kernel,
        grid=(N_TOK // PROJ_TM, N_PROJ // PROJ_TN),
        in_specs=[pl.BlockSpec((PROJ_TM, D_MODEL), lambda i, j: (i, 0)),
                  pl.BlockSpec((1, D_MODEL), lambda i, j: (0, 0)),
                  pl.BlockSpec((1, 1, D_MODEL), lambda i, j: (i // tiles_per_batch, 0, 0)),
                  pl.BlockSpec((1, 1, D_MODEL), lambda i, j: (i // tiles_per_batch, 0, 0)),
                  pl.BlockSpec((None, PROJ_TN, D_MODEL), lambda i, j: (layer, j, 0))],
        out_specs=pl.BlockSpec((PROJ_TM, PROJ_TN), lambda i, j: (i, j)),
        out_shape=jax.ShapeDtypeStruct((N_TOK, N_PROJ), F32),
        scratch_shapes=[pltpu.VMEM((PROJ_TM, D_MODEL), BF16)],
        compiler_params=_params(("arbitrary", "arbitrary")),
        name="in_proj",
    )(x, g.reshape(1, D_MODEL), sc, sh, w_t_bf16)


DSA_QB = 256
DSA_KC = 256


def _dsa_kernel(ckv_ref, ikiw_ref, q_ref, iq_ref, cos_a, sin_a, gkv_ref, wukv_ref,
                o_ref, k_dup, ik_dup, v_one_t, keys_s, bias_s, acc_s, q_s):
    qi = pl.program_id(1)
    lane = lax.broadcasted_iota(I32, (1, LANES), 1)
    lo_half = lane < 64
    n_kc = SEQ // DSA_KC
    masked = -1e30

    @pl.when(qi == 0)
    def _prologue():
        def chunk(c, carry):
            r0 = pl.multiple_of(c * DSA_KC, DSA_KC)
            rows = pl.ds(r0, DSA_KC)
            ckv = ckv_ref[0, rows, :]
            ms = jnp.mean(ckv * ckv, axis=-1, keepdims=True)
            cn = (ckv * lax.rsqrt(ms + EPS) * gkv_ref[...]).astype(BF16)
            kv = jnp.dot(cn, wukv_ref[...], preferred_element_type=F32)
            cb = jnp.where(lo_half, cos_a[0, rows, :], 1.0)
            sb = jnp.where(lo_half, sin_a[0, rows, :], 0.0)
            kv = _rope_lanes(kv, cb, sb)
            swapped = pltpu.roll(kv, 64, 1)
            k_dup[rows, :] = jnp.where(lo_half, kv, swapped).astype(BF16)
            v_one_t[c] = jnp.where(lo_half, swapped, 1.0).T.astype(BF16)
            ik = jnp.where(lo_half, _rope_lanes(ikiw_ref[0, rows, :], cb, sb), 0.0)
            ik_dup[rows, :] = (ik + pltpu.roll(ik, 64, 1)).astype(BF16)
            return carry
        lax.fori_loop(0, n_kc, chunk, 0)

    q0 = pl.multiple_of(qi * DSA_QB, DSA_QB)
    qrows = pl.ds(q0, DSA_QB)
    cq, sq = cos_a[0, qrows, :], sin_a[0, qrows, :]

    iq = iq_ref[0]
    iq_heads = []
    for p in range(IDX_HEADS // 2):
        pair = _rope_lanes(iq[:, p * LANES:(p + 1) * LANES], cq, sq) * (IDX_DIM ** -0.5)
        iq_heads.append(jnp.where(lo_half, pair, 0.0).astype(BF16))
        iq_heads.append(jnp.where(lo_half, 0.0, pair).astype(BF16))
    iq_stacked = jnp.concatenate(iq_heads, axis=0)
    iw_t = ikiw_ref[0, qrows, :].T * (IDX_HEADS ** -0.5)
    q_chunk = (q0 + lax.broadcasted_iota(I32, (1, DSA_QB), 1)) >> 6
    n_live = lax.shift_right_logical((qi + 1) * DSA_QB + (DSA_KC - 1), DSA_KC.bit_length() - 1)

    def chunk_rows(c):
        return pl.ds(pl.multiple_of(c * DSA_KC, DSA_KC), DSA_KC)

    def score_chunk(c, carry):
        rows = chunk_rows(c)
        k_chunk = (c * DSA_KC + lax.broadcasted_iota(I32, (DSA_KC, 1), 0)) >> 6
        score = jnp.zeros((DSA_KC, DSA_QB), F32)
        logits = lax.dot_general(ik_dup[rows, :], iq_stacked, NT_DIMS, preferred_element_type=F32)
        for h in range(IDX_HEADS):
            score = score + iw_t[64 + h:65 + h, :] * jnp.maximum(logits[:, h * DSA_QB:(h + 1) * DSA_QB], 0.0)
        score = jnp.where(score == 0.0, 0.0, score)
        score = jnp.where(k_chunk <= q_chunk, score, -jnp.inf)
        bits = lax.bitcast_convert_type(score, I32)
        keys_s[rows, :] = bits ^ ((bits >> 31) & 0x7FFFFFFF)
        return carry
    lax.fori_loop(0, n_live, score_chunk, 0)

    def count(preds):
        def chunk(c, accs):
            key = keys_s[chunk_rows(c), :]
            return tuple(acc + jnp.sum(pred(key).astype(I32).reshape(4, DSA_KC // 4, DSA_QB), axis=0)
                         for acc, pred in zip(accs, preds))
        zero = jnp.zeros((DSA_KC // 4, DSA_QB), I32)
        accs = lax.fori_loop(0, n_live, chunk, (zero,) * len(preds))
        return [jnp.sum(acc, axis=0, keepdims=True) for acc in accs]

    def descend(it, prefix):
        cand = prefix | lax.shift_left(jnp.int32(1), 31 - it)
        bound = cand ^ INT_MIN
        return jnp.where(count([lambda k: k >= bound])[0] >= TOPK, cand, prefix)
    thr = lax.fori_loop(0, 32, descend, jnp.zeros((1, DSA_QB), I32)) ^ INT_MIN

    tie_break = jnp.max(count([lambda k: k >= thr])[0]) > TOPK

    @pl.when(tie_break)
    def _():
        need = (TOPK - count([lambda k: k > thr])[0]).astype(F32)
        tri = (lax.broadcasted_iota(I32, (DSA_KC, DSA_KC), 0)
               >= lax.broadcasted_iota(I32, (DSA_KC, DSA_KC), 1)).astype(BF16)

        def select_chunk(c, seen):
            key = keys_s[chunk_rows(c), :]
            eq = key == thr
            rank = jnp.dot(tri, eq.astype(BF16), preferred_element_type=F32) + seen
            sel = ((key > thr) | (eq & (rank <= need))) & (key > NEG_INF_KEY)
            bias_s[c] = jnp.where(sel, 0.0, masked)
            return rank[DSA_KC - 1:DSA_KC, :]
        lax.fori_loop(0, n_live, select_chunk, jnp.zeros((1, DSA_QB), F32))

    @pl.when(jnp.logical_not(tie_break))
    def _():
        def select_chunk(c, carry):
            key = keys_s[chunk_rows(c), :]
            bias_s[c] = jnp.where((key >= thr) & (key > NEG_INF_KEY), 0.0, masked)
            return carry
        lax.fori_loop(0, n_live, select_chunk, 0)

    q = q_ref[0]
    n_pairs = A_HEADS // 2
    for p in range(n_pairs):
        pair = _rope_lanes(q[:, p * LANES:(p + 1) * LANES], cq, sq) * (HEAD_DIM ** -0.5 * LOG2_E)
        q_s[p * DSA_QB:(p + 1) * DSA_QB, :] = jnp.where(lo_half, pair, 0.0).astype(BF16)
        q_s[(n_pairs + p) * DSA_QB:(n_pairs + p + 1) * DSA_QB, :] = jnp.where(lo_half, 0.0, pair).astype(BF16)

    def attend(c, m):
        bias = jnp.concatenate([bias_s[c]] * A_HEADS, axis=1)
        s = lax.dot_general(k_dup[chunk_rows(c), :], q_s[...], NT_DIMS,
                            preferred_element_type=F32) + bias
        m_new = jnp.maximum(m, jnp.max(s, axis=0, keepdims=True))
        pe = jnp.exp2(s - m_new).astype(BF16)
        pv = jnp.dot(v_one_t[c], pe, preferred_element_type=F32)
        acc_s[...] = jnp.exp2(m - m_new) * acc_s[...] + pv
        return m_new

    acc_s[...] = jnp.zeros_like(acc_s)
    lax.fori_loop(0, n_live, attend, jnp.full((1, A_HEADS * DSA_QB), masked, F32))
    for p in range(n_pairs):
        both = []
        for slab in (p, n_pairs + p):
            a = acc_s[:, slab * DSA_QB:(slab + 1) * DSA_QB]
            both.append(a[:HEAD_DIM, :] / a[HEAD_DIM:HEAD_DIM + 1, :])
        o_ref[0, :, p * LANES:(p + 1) * LANES] = jnp.concatenate(both, axis=0).T


def _dsa(z3, cos_a, sin_a, g_kv, w_ukv_bf16):
    full = lambda width, col: pl.BlockSpec((1, SEQ, width), lambda b, i: (b, 0, col // width))
    blk = lambda width, col: pl.BlockSpec((1, DSA_QB, width), lambda b, i: (b, i, col // width))
    tab = pl.BlockSpec((1, SEQ, LANES), lambda b, i: (b, 0, 0))
    return pl.pallas_call(
        _dsa_kernel,
        grid=(BATCH, SEQ // DSA_QB),
        in_specs=[full(A_KV_LORA, COL_CKV), full(LANES, COL_IKIW),
                  blk(A_HEADS * HEAD_DIM, COL_QA), blk(IDX_HEADS * IDX_DIM, COL_IQ),
                  tab, tab,
                  pl.BlockSpec((1, A_KV_LORA), lambda b, i: (0, 0)),
                  pl.BlockSpec((A_KV_LORA, LANES), lambda b, i: (0, 0))],
        out_specs=pl.BlockSpec((1, DSA_QB, A_HEADS * HEAD_DIM), lambda b, i: (b, i, 0)),
        out_shape=jax.ShapeDtypeStruct((BATCH, SEQ, A_HEADS * HEAD_DIM), F32),
        scratch_shapes=[pltpu.VMEM((SEQ, LANES), BF16)] * 2
                       + [pltpu.VMEM((SEQ // DSA_KC, LANES, DSA_KC), BF16),
                          pltpu.VMEM((SEQ, DSA_QB), I32),
                          pltpu.VMEM((SEQ // DSA_KC, DSA_KC, DSA_QB), F32),
                          pltpu.VMEM((LANES, A_HEADS * DSA_QB), F32),
                          pltpu.VMEM((A_HEADS * DSA_QB, LANES), BF16)],
        compiler_params=_params(("arbitrary", "arbitrary")),
        name="dsa_mixer",
    )(z3, z3, z3, z3, cos_a, sin_a, g_kv.reshape(1, A_KV_LORA), w_ukv_bf16)


SB_BLK = 128
SB_EXP_UNDERFLOW = -105.0


def _sb_kernel(q_ref, k_ref, v_ref, o_ref, q_s, tail_s, acc_s):
    qi = pl.program_id(1)
    n_pairs = B_HEADS // 2
    rows_all = B_HEADS * SB_BLK
    lane = lax.broadcasted_iota(I32, (1, LANES), 1)
    lo_half = lane < 64
    causal = (lax.broadcasted_iota(I32, (rows_all, SB_BLK), 1)
              < (lax.broadcasted_iota(I32, (rows_all, SB_BLK), 0) & (SB_BLK - 1)))
    r2 = lax.broadcasted_iota(I32, (2 * SB_BLK, 2 * SB_BLK), 0) & (SB_BLK - 1)
    c2 = lax.broadcasted_iota(I32, (2 * SB_BLK, 2 * SB_BLK), 1)
    suffix_mat = ((c2 >= SB_BLK) | (r2 > c2)).astype(BF16)

    for p in range(n_pairs):
        q2 = q_ref[0, :, p * LANES:(p + 1) * LANES] * (HEAD_DIM ** -0.5)
        q_s[2 * p * SB_BLK:(2 * p + 1) * SB_BLK, :] = jnp.where(lo_half, q2, 0.0).astype(BF16)
        q_s[(2 * p + 1) * SB_BLK:(2 * p + 2) * SB_BLK, :] = jnp.where(lo_half, 0.0, q2).astype(BF16)

    def key_block(kb, diagonal):
        r0 = pl.multiple_of(kb * SB_BLK, SB_BLK)
        pair_rows = lambda p: slice(2 * p * SB_BLK, (2 * p + 2) * SB_BLK)
        z = jnp.concatenate(
            [lax.dot_general(q_s[pair_rows(p), :], k_ref[0, pl.ds(r0, SB_BLK), p * LANES:(p + 1) * LANES].astype(BF16),
                             NT_DIMS, preferred_element_type=F32) for p in range(n_pairs)], axis=0)
        log_beta = jnp.minimum(z, 0.0) - jnp.log(1.0 + jnp.exp(-jnp.abs(z)))
        log_not = log_beta - z
        if diagonal:
            log_not = jnp.where(causal, log_not, 0.0)
        hi = log_not.astype(BF16)
        lo = (log_not - hi.astype(F32)).astype(BF16)
        hi_lo = jnp.concatenate([hi, lo], axis=1)
        t = jnp.concatenate([jnp.dot(hi_lo[pair_rows(p), :], suffix_mat, preferred_element_type=F32)
                             for p in range(n_pairs)], axis=0)
        if diagonal:
            a = jnp.where(causal, jnp.exp(log_beta + t[:, :SB_BLK]), 0.0)
            tail = t[:, SB_BLK:]
        else:
            a = jnp.exp(log_beta + t[:, :SB_BLK] + tail_s[...])
            tail = tail_s[...] + t[:, SB_BLK:]
        tail_s[...] = tail
        a = a.astype(BF16)
        for p in range(n_pairs):
            v2 = v_ref[0, pl.ds(r0, SB_BLK), p * LANES:(p + 1) * LANES].astype(BF16)
            pv = jnp.dot(a[pair_rows(p), :], v2, preferred_element_type=F32)
            if diagonal:
                acc_s[pair_rows(p), :] = pv
            else:
                acc_s[pair_rows(p), :] += pv
        return jnp.max(tail)

    first = key_block(qi, True)

    def more(state):
        jj, worst = state
        return (jj <= qi) & (worst > SB_EXP_UNDERFLOW)

    def step(state):
        jj, _ = state
        return jj + 1, key_block(qi - jj, False)

    lax.while_loop(more, step, (jnp.int32(1), first))
    for p in range(n_pairs):
        even = acc_s[2 * p * SB_BLK:(2 * p + 1) * SB_BLK, :]
        odd = acc_s[(2 * p + 1) * SB_BLK:(2 * p + 2) * SB_BLK, :]
        o_ref[0, :, p * LANES:(p + 1) * LANES] = jnp.where(lo_half, even, odd)


def _sb(z3):
    width = B_HEADS * HEAD_DIM
    return pl.pallas_call(
        _sb_kernel,
        grid=(BATCH, SEQ // SB_BLK),
        in_specs=[pl.BlockSpec((1, SB_BLK, width), lambda b, i: (b, i, COL_QB // width)),
                  pl.BlockSpec((1, SEQ, width), lambda b, i: (b, 0, COL_KB // width)),
                  pl.BlockSpec((1, SEQ, width), lambda b, i: (b, 0, COL_VB // width))],
        out_specs=pl.BlockSpec((1, SB_BLK, width), lambda b, i: (b, i, 0)),
        out_shape=jax.ShapeDtypeStruct((BATCH, SEQ, width), F32),
        scratch_shapes=[pltpu.VMEM((B_HEADS * SB_BLK, LANES), BF16),
                        pltpu.VMEM((B_HEADS * SB_BLK, SB_BLK), F32),
                        pltpu.VMEM((B_HEADS * SB_BLK, SB_BLK), F32)],
        compiler_params=_params(("arbitrary", "arbitrary")),
        name="stickbreak_mixer",
    )(z3, z3, z3)


RET_BLK = 256


def _ret_kernel(dec_ref, q_ref, k_ref, v_ref, gate_ref, cos_a, sin_a, qs_ref, ks_ref, e_diag, gret_ref,
                o_ref, state):
    lane = lax.broadcasted_iota(I32, (1, LANES), 1)
    lo_half = lane < 64
    row_lo = lax.broadcasted_iota(I32, (LANES, 1), 0) < 64

    @pl.when(pl.program_id(1) == 0)
    def _():
        state[...] = jnp.zeros_like(state)

    cq, sq = cos_a[0], sin_a[0]
    for p in range(C_HEADS // 2):
        cols = slice(p * LANES, (p + 1) * LANES)
        q2 = _rope_lanes(q_ref[0, :, cols], cq, sq)
        k2 = _rope_lanes(k_ref[0, :, cols], cq, sq) * (HEAD_DIM ** -0.5)
        q_hat = q2 * qs_ref[:, cols]
        k_hat_t = (k2 * ks_ref[:, cols]).T
        for e in range(2):
            h = 2 * p + e
            pick = (lambda x: jnp.where(lo_half, x, 0.0)) if e == 0 else (lambda x: jnp.where(lo_half, 0.0, x))
            vcols = slice(h * C_V_DIM, (h + 1) * C_V_DIM)
            v2 = v_ref[0, :, vcols].astype(BF16)
            s = lax.dot_general(pick(q2).astype(BF16), k2.astype(BF16), NT_DIMS,
                                preferred_element_type=F32) * e_diag[h]
            o = jnp.dot(s.astype(BF16), v2, preferred_element_type=F32)
            o = o + jnp.dot(pick(q_hat).astype(BF16), state[h].astype(BF16), preferred_element_type=F32)
            kv = jnp.dot(jnp.where(row_lo if e == 0 else ~row_lo, k_hat_t, 0.0).astype(BF16), v2,
                         preferred_element_type=F32)
            state[h] = dec_ref[h] * state[h] + kv
            mu = jnp.mean(o, axis=-1, keepdims=True)
            d = o - mu
            var = jnp.mean(d * d, axis=-1, keepdims=True)
            y = d * lax.rsqrt(var + EPS) * gret_ref[:, vcols]
            g = gate_ref[0, :, vcols]
            o_ref[0, :, vcols] = g * _sigmoid(g) * y


def _ret(z3, cos_a, sin_a, dec, q_scale, k_scale, e_diag, g_ret):
    qk_w = C_HEADS * HEAD_DIM
    v_w = C_HEADS * C_V_DIM
    tab = pl.BlockSpec((1, RET_BLK, LANES), lambda b, i: (b, i, 0))
    const2 = lambda a: pl.BlockSpec(a.shape, lambda b, i: (0, 0))
    return pl.pallas_call(
        _ret_kernel,
        grid=(BATCH, SEQ // RET_BLK),
        in_specs=[pl.BlockSpec(memory_space=pltpu.SMEM),
                  pl.BlockSpec((1, RET_BLK, qk_w), lambda b, i: (b, i, COL_QC // qk_w)),
                  pl.BlockSpec((1, RET_BLK, qk_w), lambda b, i: (b, i, COL_KC // qk_w)),
                  pl.BlockSpec((1, RET_BLK, v_w), lambda b, i: (b, i, COL_VC // v_w)),
                  pl.BlockSpec((1, RET_BLK, v_w), lambda b, i: (b, i, COL_GC // v_w)),
                  tab, tab, const2(q_scale), const2(k_scale),
                  pl.BlockSpec(e_diag.shape, lambda b, i: (0, 0, 0)),
                  pl.BlockSpec((1, v_w), lambda b, i: (0, 0))],
        out_specs=pl.BlockSpec((1, RET_BLK, v_w), lambda b, i: (b, i, 0)),
        out_shape=jax.ShapeDtypeStruct((BATCH, SEQ, v_w), F32),
        scratch_shapes=[pltpu.VMEM((C_HEADS, LANES, C_V_DIM), F32)],
        compiler_params=_params(("arbitrary", "arbitrary")),
        name="retention_mixer",
    )(dec, z3, z3, z3, z3, cos_a, sin_a, q_scale, k_scale, e_diag, g_ret.reshape(1, v_w))


MERGE_TM = 512
MOE_PIECE = 128
MOE_SUB = 256


def _rms(y, g):
    ms = jnp.mean(y * y, axis=-1, keepdims=True)
    return y * lax.rsqrt(ms + EPS) * g


def _route(logits_t, bias_col):
    n_tok = logits_t.shape[1]
    per_group = N_EXPERTS // N_GROUPS
    scores = _sigmoid(logits_t)
    sel = scores + bias_col
    sel3 = sel.reshape(N_GROUPS, per_group, n_tok)
    member = lax.broadcasted_iota(I32, (N_GROUPS, per_group, n_tok), 1)
    m1 = jnp.max(sel3, axis=1, keepdims=True)
    first = jnp.min(jnp.where(sel3 == m1, member, per_group), axis=1, keepdims=True)
    m2 = jnp.max(jnp.where(member == first, -jnp.inf, sel3), axis=1, keepdims=True)
    grp = (m1 + m2).reshape(N_GROUPS, n_tok)
    gid = lax.broadcasted_iota(I32, (N_GROUPS, n_tok), 0)
    rank = jnp.zeros((N_GROUPS, n_tok), I32)
    for g in range(N_GROUPS):
        other = grp[g:g + 1, :]
        rank = rank + ((other > grp) | ((other == grp) & (g < gid))).astype(I32)
    keep = jnp.where(rank < TOPK_GROUPS, 1.0, 0.0).reshape(N_GROUPS, 1, n_tok)
    live = jnp.where(keep > 0.0, sel3, -jnp.inf).reshape(N_EXPERTS, n_tok)
    eid = lax.broadcasted_iota(I32, (N_EXPERTS, n_tok), 0)
    chosen = jnp.zeros((N_EXPERTS, n_tok), jnp.bool_)
    for _ in range(TOP_K):
        m = jnp.max(live, axis=0, keepdims=True)
        pick = eid == jnp.min(jnp.where(live == m, eid, N_EXPERTS), axis=0, keepdims=True)
        chosen = chosen | pick
        live = jnp.where(pick, -jnp.inf, live)
    w = jnp.where(chosen, scores, 0.0)
    return w / jnp.sum(w, axis=0, keepdims=True) * ROUTED_SCALE, chosen


def _merge_kernel(ya_ref, yb_ref, yc_ref, g_ref, x_ref, gt_ref, gpost_ref, gpre_ref, sc_ref, sh_ref,
                  woa, wob, woc, wout, wr_t, rb_ref, xo_ref, h_ref, gate_ref):
    ya = jnp.dot(ya_ref[...].astype(BF16), woa[...], preferred_element_type=F32)
    yb = jnp.dot(yb_ref[...].astype(BF16), wob[...], preferred_element_type=F32)
    yc = jnp.dot(yc_ref[...].astype(BF16), woc[...], preferred_element_type=F32)
    merged = (_sigmoid(g_ref[:, 0:D_MODEL]) * ya
              + _sigmoid(g_ref[:, D_MODEL:2 * D_MODEL]) * yb
              + _sigmoid(g_ref[:, 2 * D_MODEL:3 * D_MODEL]) * yc)
    y = jnp.dot(merged.astype(BF16), wout[...], preferred_element_type=F32)
    x = x_ref[...] + gt_ref[0] * _rms(y, gpost_ref[...])
    xo_ref[...] = x
    h = (_rms(x, gpre_ref[...]) * (1.0 + sc_ref[0]) + sh_ref[0]).astype(BF16)
    h_ref[...] = h
    logits_t = lax.dot_general(wr_t[...], h, NT_DIMS, preferred_element_type=F32)
    gates_t, chosen = _route(logits_t, rb_ref[...])
    marked = jnp.where(chosen, gates_t, -1.0)
    for piece in range(MERGE_TM // MOE_PIECE):
        gate_ref[piece] = marked[:, piece * MOE_PIECE:(piece + 1) * MOE_PIECE]


def _merge(ya, yb, yc, z, x, gt1, g_post, g_pre, sc2, sh2, woa, wob, woc, wout, wr_t, rbias):
    tpb = SEQ // MERGE_TM
    row = lambda w: pl.BlockSpec((MERGE_TM, w), lambda i: (i, 0))
    vec = pl.BlockSpec((1, D_MODEL), lambda i: (0, 0))
    mod = pl.BlockSpec((1, 1, D_MODEL), lambda i: (i // tpb, 0, 0))
    whole = lambda a: pl.BlockSpec(a.shape, lambda i: (0, 0))
    return pl.pallas_call(
        _merge_kernel,
        grid=(N_TOK // MERGE_TM,),
        in_specs=[row(512), row(512), row(1024), pl.BlockSpec((MERGE_TM, 3 * D_MODEL), lambda i: (i, 0)),
                  row(D_MODEL), mod, vec, vec, mod, mod,
                  whole(woa), whole(wob), whole(woc), whole(wout), whole(wr_t), whole(rbias)],
        out_specs=[row(D_MODEL), row(D_MODEL),
                   pl.BlockSpec((MERGE_TM // MOE_PIECE, N_EXPERTS, MOE_PIECE), lambda i: (i % tpb, 0, i // tpb))],
        out_shape=[jax.ShapeDtypeStruct((N_TOK, D_MODEL), F32),
                   jax.ShapeDtypeStruct((N_TOK, D_MODEL), BF16),
                   jax.ShapeDtypeStruct((SEQ // MOE_PIECE, N_EXPERTS, BATCH * MOE_PIECE), F32)],
        compiler_params=_params(("arbitrary",)),
        name="merge_route",
    )(ya, yb, yc, z, x, gt1, g_post.reshape(1, D_MODEL), g_pre.reshape(1, D_MODEL), sc2, sh2,
      woa, wob, woc, wout, wr_t, rbias)


MOE_TM = BATCH * MOE_PIECE
MOE_EG = 4
MOE_CAP = 64
MOE_NSUB = MOE_TM // MOE_SUB
MOE_SLAB = MOE_EG * MOE_CAP


def _moe_kernel(h_ref, gate_ref, x_ref, gt_ref, gpost_ref, w1, w3, w2, ws1, ws3, ws2,
                o_ref, acc, xg_s, yw_s, pick_s, back_s, rank_s, gate_s):
    g_idx = pl.program_id(1)
    n_groups = N_EXPERTS // MOE_EG
    pieces_per_sub = MOE_SUB // MOE_PIECE

    @pl.when(g_idx == 0)
    def _():
        acc[...] = jnp.zeros_like(acc)
        marked = gate_ref[...]
        routed = marked >= 0.0
        earlier = (lax.broadcasted_iota(I32, (MOE_SUB, MOE_SUB), 0)
                   < lax.broadcasted_iota(I32, (MOE_SUB, MOE_SUB), 1)).astype(BF16)
        ones = jnp.where(routed, 1.0, 0.0).astype(BF16)
        before = jnp.concatenate(
            [jnp.dot(ones[:, s * MOE_SUB:(s + 1) * MOE_SUB], earlier, preferred_element_type=F32)
             for s in range(MOE_NSUB)], axis=1)
        ranks = jnp.where(routed, before, -1.0)
        gates = jnp.maximum(marked, 0.0)
        for gg in range(n_groups):
            rank_s[gg] = ranks[gg * MOE_EG:(gg + 1) * MOE_EG, :]
            gate_s[gg] = gates[gg * MOE_EG:(gg + 1) * MOE_EG, :]

    def h_rows(s):
        return h_ref[s * pieces_per_sub:(s + 1) * pieces_per_sub].reshape(MOE_SUB, D_MODEL)

    rank = rank_s[g_idx]
    gate = gate_s[g_idx]
    n_rounds = (jnp.max(rank).astype(I32) + MOE_CAP) // MOE_CAP
    slot = lax.broadcasted_iota(I32, (MOE_CAP, 1), 0).astype(F32)

    def make_pick(s, base):
        toks = slice(s * MOE_SUB, (s + 1) * MOE_SUB)
        return jnp.concatenate(
            [jnp.where(rank[e:e + 1, toks] == slot + base, 1.0, 0.0) for e in range(MOE_EG)], axis=0)

    def expert_rows(e, pick_of):
        rows = slice(e * MOE_CAP, (e + 1) * MOE_CAP)
        x_e = jnp.concatenate([xg_s[s, rows, :] for s in range(MOE_NSUB)], axis=0)
        a = jnp.dot(x_e, w1[e], preferred_element_type=F32)
        hid = a * _sigmoid(a) * jnp.dot(x_e, w3[e], preferred_element_type=F32)
        y = jnp.dot(hid.astype(BF16), w2[e], preferred_element_type=F32)
        for s in range(MOE_NSUB):
            toks = slice(s * MOE_SUB, (s + 1) * MOE_SUB)
            w_row = jnp.sum(pick_of(s)[rows, :] * gate[e:e + 1, toks], axis=1, keepdims=True)
            yw_s[s, rows, :] = (y[s * MOE_CAP:(s + 1) * MOE_CAP, :] * w_row).astype(BF16)

    def scatter(s, back):
        toks = slice(s * MOE_SUB, (s + 1) * MOE_SUB)
        acc[toks, :] += jnp.dot(back, yw_s[s], preferred_element_type=F32)

    picks = [make_pick(s, 0.0) for s in range(MOE_NSUB)]
    for s in range(MOE_NSUB):
        xg_s[s] = jnp.dot(picks[s].astype(BF16), h_rows(s), preferred_element_type=F32).astype(BF16)
    for e in range(MOE_EG):
        expert_rows(e, lambda s: picks[s])
    for s in range(MOE_NSUB):
        scatter(s, picks[s].T.astype(BF16))

    def overflow_round(r, carry):
        base = (r * MOE_CAP).astype(F32)
        sub_top = [jnp.max(rank[:, s * MOE_SUB:(s + 1) * MOE_SUB]) for s in range(MOE_NSUB)]
        exp_top = [jnp.max(rank[e:e + 1, :]) for e in range(MOE_EG)]
        for s in range(MOE_NSUB):
            @pl.when(sub_top[s] >= base)
            def _(s=s):
                pick = make_pick(s, base)
                pick_s[s] = pick.astype(BF16)
                back_s[s] = pick.T.astype(BF16)
                xg_s[s] = jnp.dot(pick.astype(BF16), h_rows(s), preferred_element_type=F32).astype(BF16)
        for e in range(MOE_EG):
            @pl.when(exp_top[e] >= base)
            def _(e=e):
                expert_rows(e, lambda s: pick_s[s].astype(F32))
        for s in range(MOE_NSUB):
            @pl.when(sub_top[s] >= base)
            def _(s=s):
                scatter(s, back_s[s])
        return carry

    @pl.when(n_rounds > 1)
    def _():
        for s in range(MOE_NSUB):
            pick_s[s] = picks[s].astype(BF16)
        lax.fori_loop(1, n_rounds, overflow_round, 0)

    @pl.when(g_idx == n_groups - 1)
    def _():
        h = h_ref[...].reshape(MOE_TM, D_MODEL)
        s = jnp.dot(h, ws1[...], preferred_element_type=F32)
        shid = s * _sigmoid(s) * jnp.dot(h, ws3[...], preferred_element_type=F32)
        y_all = acc[...] + jnp.dot(shid.astype(BF16), ws2[...], preferred_element_type=F32)
        normed = _rms(y_all, gpost_ref[...]).reshape(BATCH, MOE_PIECE, D_MODEL)
        o_ref[...] = x_ref[...] + gt_ref[...] * normed


def _moe(h, gates, x, gt2, g_post, w1, w3, w2, ws1, ws3, ws2, layer):
    n_groups = N_EXPERTS // MOE_EG
    n_tiles = SEQ // MOE_PIECE
    whole = lambda a: pl.BlockSpec(a.shape, lambda i, g: (0, 0))
    tile = pl.BlockSpec((BATCH, None, MOE_PIECE, D_MODEL), lambda i, g: (0, i, 0, 0))
    by_tile = lambda a: a.reshape(BATCH, n_tiles, MOE_PIECE, D_MODEL)
    out = pl.pallas_call(
        _moe_kernel,
        grid=(n_tiles, n_groups),
        in_specs=[tile,
                  pl.BlockSpec((None, N_EXPERTS, MOE_TM), lambda i, g: (i, 0, 0)),
                  tile,
                  pl.BlockSpec((BATCH, 1, D_MODEL), lambda i, g: (0, 0, 0)),
                  pl.BlockSpec((1, D_MODEL), lambda i, g: (0, 0)),
                  pl.BlockSpec((None, MOE_EG, D_MODEL, EXPERT_DIM), lambda i, g: (layer, g, 0, 0)),
                  pl.BlockSpec((None, MOE_EG, D_MODEL, EXPERT_DIM), lambda i, g: (layer, g, 0, 0)),
                  pl.BlockSpec((None, MOE_EG, EXPERT_DIM, D_MODEL), lambda i, g: (layer, g, 0, 0)),
                  whole(ws1), whole(ws3), whole(ws2)],
        out_specs=tile,
        out_shape=jax.ShapeDtypeStruct((BATCH, n_tiles, MOE_PIECE, D_MODEL), F32),
        scratch_shapes=[pltpu.VMEM((MOE_TM, D_MODEL), F32),
                        pltpu.VMEM((MOE_NSUB, MOE_SLAB, D_MODEL), BF16),
                        pltpu.VMEM((MOE_NSUB, MOE_SLAB, D_MODEL), BF16),
                        pltpu.VMEM((MOE_NSUB, MOE_SLAB, MOE_SUB), BF16),
                        pltpu.VMEM((MOE_NSUB, MOE_SUB, MOE_SLAB), BF16),
                        pltpu.VMEM((n_groups, MOE_EG, MOE_TM), F32),
                        pltpu.VMEM((n_groups, MOE_EG, MOE_TM), F32)],
        compiler_params=_params(("arbitrary", "arbitrary")),
        name="moe_ffn",
    )(by_tile(h), gates, by_tile(x), gt2, g_post.reshape(1, D_MODEL), w1, w3, w2, ws1, ws3, ws2)
    return out.reshape(N_TOK, D_MODEL)


_IN_SIZES = (512, 256, 256, 64, 4, 512, 512, 512, 512, 512, 1024, 1024, 3072)
_IN_DEST = (COL_QA, COL_CKV, COL_IQ, COL_IKIW, COL_IKIW + 64, COL_QB, COL_KB, COL_VB, COL_QC, COL_KC,
            COL_VC, COL_GC, COL_GATES)
_IN_SRC = tuple(int(v) for v in np.cumsum((0,) + _IN_SIZES[:-1]))
N_IN = sum(_IN_SIZES)
PACK_TK = 128


def _pack_kernel(w_ref, o_ref):
    for layer in range(DEPTH):
        for src, width, dst in zip(_IN_SRC, _IN_SIZES, _IN_DEST):
            o_ref[layer, dst:dst + width, :] = w_ref[src:src + width, layer, :].astype(BF16)
        o_ref[layer, COL_IKIW + 68:COL_IKIW + LANES, :] = jnp.zeros((LANES - 68, PACK_TK), BF16)


def _reorder_w_in(w_t):
    return pl.pallas_call(
        _pack_kernel,
        grid=(D_MODEL // PACK_TK,),
        in_specs=[pl.BlockSpec((N_IN, DEPTH, PACK_TK), lambda i: (0, 0, i))],
        out_specs=pl.BlockSpec((DEPTH, N_PROJ, PACK_TK), lambda i: (0, 0, i)),
        out_shape=jax.ShapeDtypeStruct((DEPTH, N_PROJ, D_MODEL), BF16),
        compiler_params=_params(("arbitrary",)),
        name="pack_w_in",
    )(w_t)


def _rope_tables(positions):
    half = HEAD_DIM // 2
    inv_freq = ROPE_THETA ** (-jnp.arange(half, dtype=F32) / half)
    ang = positions.astype(F32)[:, :, None] * inv_freq
    cos, sin = jnp.cos(ang), jnp.sin(ang)
    cos64 = jnp.concatenate([cos, cos], axis=-1)
    sin64 = jnp.concatenate([-sin, sin], axis=-1)
    cos_a = jnp.concatenate([cos64, cos64], axis=-1)
    sin_a = jnp.concatenate([sin64, sin64], axis=-1)
    return cos_a, sin_a


def _decay_tables():
    log_g = jnp.log(1.0 - 2.0 ** (-5.0 - jnp.arange(C_HEADS, dtype=F32)))
    pos = jnp.arange(RET_BLK, dtype=F32)
    diff = pos[:, None] - pos[None, :]
    chunk = jnp.arange(RET_BLK) // CHUNK
    visible = chunk[None, :] <= chunk[:, None]
    e_diag = jnp.where(visible, jnp.exp(log_g[:, None, None] * jnp.abs(diff)), 0.0)
    dec = jnp.exp(log_g * RET_BLK)
    per_lane = jnp.repeat(log_g, HEAD_DIM)[None, :]
    q_scale = jnp.exp(per_lane * pos[:, None])
    k_scale = jnp.exp(per_lane * (RET_BLK - pos)[:, None])
    return dec, q_scale, k_scale, e_diag


def kernel(x, c, positions, ada_w, ada_b, g_pre_mix, g_post_mix, g_pre_ffn, g_post_ffn, w_in, g_kv, w_uk, w_uv,
           g_ret, w_o_a, w_o_b, w_o_c, w_out, w_router, router_bias, w1, w3, w2, ws1, ws3, ws2):
    cos_a, sin_a = _rope_tables(positions)
    dec, q_scale, k_scale, e_diag = _decay_tables()
    xf = x.reshape(N_TOK, D_MODEL)
    w_in_packed = _reorder_w_in(jnp.transpose(w_in, (2, 0, 1)))
    w1_b, w3_b, w2_b = w1.astype(BF16), w3.astype(BF16), w2.astype(BF16)
    for l in range(DEPTH):
        mod = _ada(c, ada_w, ada_b, l).reshape(BATCH, 1, N_ADA * D_MODEL)
        sh1, sc1, gt1, sh2, sc2, gt2 = (mod[:, :, k * D_MODEL:(k + 1) * D_MODEL] for k in range(N_ADA))
        z = _inproj(xf, g_pre_mix[l], sc1, sh1, w_in_packed, l)
        z3 = z.reshape(BATCH, SEQ, N_PROJ)
        w_ukv = jnp.concatenate([w_uk[l], w_uv[l]], axis=1).astype(BF16)
        ya = _dsa(z3, cos_a, sin_a, g_kv[l], w_ukv).reshape(N_TOK, -1)
        yb = _sb(z3).reshape(N_TOK, -1)
        yc = _ret(z3, cos_a, sin_a, dec, q_scale, k_scale, e_diag, g_ret[l]).reshape(N_TOK, -1)
        xf, h, gates = _merge(ya, yb, yc, z, xf, gt1, g_post_mix[l], g_pre_ffn[l], sc2, sh2,
                              w_o_a[l].astype(BF16), w_o_b[l].astype(BF16), w_o_c[l].astype(BF16),
                              w_out[l].astype(BF16), w_router[l].T.astype(BF16),
                              router_bias[l].reshape(N_EXPERTS, 1))
        xf = _moe(h, gates, xf, gt2, g_post_ffn[l], w1_b, w3_b, w2_b,
                  ws1[l].astype(BF16), ws3[l].astype(BF16), ws2[l].astype(BF16), l)
    return xf.reshape(BATCH, SEQ, D_MODEL)
```
